```python
import jax
import jax.numpy as jnp
from jax import lax
import numpy as np

D_MODEL = 1024
BATCH = 8
SEQ = 4096
DEPTH = 1

GRID_W = 64
CTX_LEN = 256
MIX_WIDTH = D_MODEL
HG_WIDTH = MIX_WIDTH // 2
HG_HEADS = 4
HG_KDIM = 128
HG_KEYS = HG_HEADS * HG_KDIM
HG_VDIM = HG_WIDTH // HG_HEADS
SC_WIDTH = MIX_WIDTH - HG_WIDTH
CONV_W = 3
CHUNK = 32
N_EXPERTS = 16
EC_CAPACITY = 2
EXPERT_DFF = 2 * D_MODEL
N_MOD = 6
EPS = 1e-6
IN_COLS = 3 * HG_KEYS + 2 * HG_WIDTH + 3 * SC_WIDTH

kernel_name = 'hybrid_hgrn2_shortconv_ec_moe_dit'


def rms_norm(x, gain):
    xf = x.astype(jnp.float32)
    y = xf * lax.rsqrt(jnp.mean(xf * xf, axis=-1, keepdims=True) + EPS)
    return (y * gain.astype(jnp.float32)).astype(x.dtype)


def modulate(h, shift, scale):
    return h * (1 + scale) + shift


def adaln(cond, w, b):
    return jnp.split(jax.nn.silu(cond) @ w + b, N_MOD, axis=-1)


def dwconv3(u, w):
    pad = [(0, 0)] * (u.ndim - 2) + [(1, 1), (0, 0)]
    up = jnp.pad(u, pad)
    return w[0] * up[..., :-2, :] + w[1] * up[..., 1:-1, :] + w[2] * up[..., 2:, :]


def split_proj(z):
    cuts = [HG_KEYS, 2 * HG_KEYS, 3 * HG_KEYS, 3 * HG_KEYS + HG_WIDTH, 3 * HG_KEYS + 2 * HG_WIDTH,
            3 * HG_KEYS + 2 * HG_WIDTH + SC_WIDTH, 3 * HG_KEYS + 2 * HG_WIDTH + 2 * SC_WIDTH]
    return jnp.split(z, cuts, axis=-1)


def hgrn2_gates(zf, lb):
    b, t, _ = zf.shape
    f = lb + (1.0 - lb) * jax.nn.sigmoid(zf.astype(jnp.float32))
    return jnp.log(f).reshape(b, t, HG_HEADS, HG_KDIM), (1.0 - f).reshape(b, t, HG_HEADS, HG_KDIM)


def gla_chunked(q, k, v, log_f, s0):
    b, t, h, _ = q.shape
    dv = v.shape[-1]
    n = t // CHUNK
    q, k, v, log_f = [a.astype(jnp.float32).reshape(b, n, CHUNK, h, a.shape[-1]) for a in (q, k, v, log_f)]
    cum = jnp.cumsum(log_f, axis=2)
    last = cum[:, :, -1:]
    q_dec = q * jnp.exp(cum)
    k_inv = k * jnp.exp(-cum)
    k_end = k * jnp.exp(last - cum)
    lower = jnp.tril(jnp.ones((CHUNK, CHUNK), jnp.bool_))
    scores = jnp.where(lower, jnp.einsum('bnihd,bnjhd->bnhij', q_dec, k_inv), 0.0)
    o_intra = jnp.einsum('bnhij,bnjhv->bnihv', scores, v)
    kv = jnp.einsum('bnjhd,bnjhv->bnhdv', k_end, v)
    decay = jnp.exp(last[:, :, 0])

    def step(s, inp):
        dec, kv_n = inp
        return dec[..., None] * s + kv_n, s

    s_final, s_prev = lax.scan(step, s0.astype(jnp.float32),
                               (jnp.moveaxis(decay, 1, 0), jnp.moveaxis(kv, 1, 0)))
    o_inter = jnp.einsum('bnihd,nbhdv->bnihv', q_dec, s_prev)
    return (o_intra + o_inter).reshape(b, t, h, dv), s_final


def gla_final_state(k, v, log_f):
    cum = jnp.cumsum(log_f, axis=1)
    return jnp.einsum('bthd,bthv->bhdv', k * jnp.exp(cum[:, -1:] - cum), v)


def hgrn2_mix(q, i, zf_f, zf_b, g, lb_f, lb_b, s0_f, s0_b, gain):
    b, t, _ = q.shape
    qh = q.reshape(b, t, HG_HEADS, HG_KDIM)
    vh = i.reshape(b, t, HG_HEADS, HG_VDIM)
    logf_f, k_f = hgrn2_gates(zf_f, lb_f)
    logf_b, k_b = hgrn2_gates(zf_b, lb_b)
    rev = lambda a: jnp.flip(a, axis=1)
    o_f, s_f = gla_chunked(qh, k_f, vh, logf_f, s0_f)
    o_b, s_b = gla_chunked(rev(qh), rev(k_b), rev(vh), rev(logf_b), s0_b)
    o = (o_f + rev(o_b)).astype(q.dtype)
    o = rms_norm(o, gain) * jax.nn.silu(g.reshape(b, t, HG_HEADS, HG_VDIM))
    return o.reshape(b, t, HG_WIDTH), s_f, s_b


def hgrn2_context_states(zf_f, zf_b, i, lb_f, lb_b):
    b, t, _ = i.shape
    vh = i.astype(jnp.float32).reshape(b, t, HG_HEADS, HG_VDIM)
    logf_f, k_f = hgrn2_gates(zf_f, lb_f)
    logf_b, k_b = hgrn2_gates(zf_b, lb_b)
    s_f = gla_final_state(k_f, vh, logf_f)
    s_b = gla_final_state(jnp.flip(k_b, 1), jnp.flip(vh, 1), jnp.flip(logf_b, 1))
    return s_f, s_b


def token_mix(h, w_in_l, w_out_l, lb_f, lb_b, hg_gain, conv_l, s0_f, s0_b, grid_rows):
    b, t, _ = h.shape
    q, zf_f, zf_b, i, g, gate_b, gate_c, v = split_proj(h @ w_in_l)
    o_hg, s_f, s_b = hgrn2_mix(q, i, zf_f, zf_b, g, lb_f, lb_b, s0_f, s0_b, hg_gain)
    u = gate_c * v
    if grid_rows is None:
        y = dwconv3(u, conv_l)
    else:
        y = dwconv3(u.reshape(b, grid_rows, GRID_W, SC_WIDTH), conv_l).reshape(b, t, SC_WIDTH)
    y = gate_b * y
    return jnp.concatenate([o_hg, y], axis=-1) @ w_out_l, s_f, s_b


def expert_choice_ffn(h, w_router, w_gate, w_up, w_down):
    b, t, d = h.shape
    cap = EC_CAPACITY * t // N_EXPERTS
    aff = jax.nn.softmax(jnp.einsum('btd,de->bte', h, w_router).astype(jnp.float32), axis=-1)
    gate, idx = lax.top_k(jnp.swapaxes(aff, 1, 2), cap)
    xs = jax.vmap(lambda hb, ib: hb[ib])(h, idx)
    hid = jax.nn.silu(jnp.einsum('becd,edf->becf', xs, w_gate)) * jnp.einsum('becd,edf->becf', xs, w_up)
    ys = jnp.einsum('becf,efd->becd', hid, w_down) * gate[..., None].astype(h.dtype)
    return jax.vmap(lambda ib, yb: jnp.zeros((t, d), yb.dtype).at[ib.reshape(-1)].add(yb.reshape(-1, d)))(idx, ys)


def setup_inputs(seed: int = 0) -> dict:
    key = jax.random.key(seed)
    ks = jax.random.split(key, 20)
    nrm = lambda k, shape, s: jax.random.normal(k, shape, jnp.float32) * s
    return {
        'x': nrm(ks[0], (BATCH, SEQ, D_MODEL), 1.0),
        'c': nrm(ks[1], (BATCH, D_MODEL), 1.0),
        'ctx': nrm(ks[2], (BATCH, CTX_LEN, D_MODEL), 1.0),
        'c_ctx': nrm(ks[3], (D_MODEL,), 1.0),
        'w_ada': nrm(ks[4], (DEPTH, D_MODEL, N_MOD * D_MODEL), 0.5 * D_MODEL ** -0.5),
        'b_ada': nrm(ks[5], (DEPTH, N_MOD * D_MODEL), 0.02),
        'norm_mix': 1.0 + nrm(ks[6], (DEPTH, D_MODEL), 0.02),
        'norm_ffn': 1.0 + nrm(ks[7], (DEPTH, D_MODEL), 0.02),
        'w_in': nrm(ks[8], (DEPTH, D_MODEL, IN_COLS), D_MODEL ** -0.5),
        'lb_logits': nrm(ks[9], (DEPTH + 1, 2, HG_KEYS), 0.3),
        'hg_norm': 1.0 + nrm(ks[10], (DEPTH, HG_VDIM), 0.02),
        'conv_w': nrm(ks[11], (DEPTH, CONV_W, SC_WIDTH), CONV_W ** -0.5),
        'w_out': nrm(ks[12], (DEPTH, MIX_WIDTH, D_MODEL), MIX_WIDTH ** -0.5),
        'w_router': nrm(ks[13], (DEPTH, D_MODEL, N_EXPERTS), D_MODEL ** -0.5),
        'w_gate': nrm(ks[14], (DEPTH, N_EXPERTS, D_MODEL, EXPERT_DFF), D_MODEL ** -0.5),
        'w_up': nrm(ks[15], (DEPTH, N_EXPERTS, D_MODEL, EXPERT_DFF), D_MODEL ** -0.5),
        'w_down': nrm(ks[16], (DEPTH, N_EXPERTS, EXPERT_DFF, D_MODEL), EXPERT_DFF ** -0.5),
        'norm_final': 1.0 + nrm(ks[17], (D_MODEL,), 0.02),
    }


def reference(x, c, ctx, c_ctx, w_ada, b_ada, norm_mix, norm_ffn, w_in, lb_logits, hg_norm, conv_w,
              w_out, w_router, w_gate, w_up, w_down, norm_final):
    b = x.shape[0]
    rows = x.shape[1] // GRID_W
    lower_bounds = jnp.cumsum(jax.nn.softmax(lb_logits.astype(jnp.float32), axis=0), axis=0)
    cx = ctx
    for l in range(DEPTH):
        sh1, sc1, g1, sh2, sc2, g2 = [m[:, None] for m in adaln(c, w_ada[l], b_ada[l])]
        csh1, csc1, cg1, csh2, csc2, cg2 = adaln(c_ctx, w_ada[l], b_ada[l])
        lb_f, lb_b = lower_bounds[l, 0], lower_bounds[l, 1]
        hc = modulate(rms_norm(cx, norm_mix[l]), csh1, csc1)
        if l == DEPTH - 1:
            zf_f, zf_b, i_c = jnp.split(hc @ w_in[l][:, HG_KEYS:3 * HG_KEYS + HG_WIDTH],
                                        [HG_KEYS, 2 * HG_KEYS], axis=-1)
            s_f, s_b = hgrn2_context_states(zf_f, zf_b, i_c, lb_f, lb_b)
        else:
            zero = jnp.zeros((b, HG_HEADS, HG_KDIM, HG_VDIM), jnp.float32)
            mc, s_f, s_b = token_mix(hc, w_in[l], w_out[l], lb_f, lb_b, hg_norm[l], conv_w[l],
                                     zero, zero, None)
            cx = cx + cg1 * mc
            hc2 = modulate(rms_norm(cx, norm_ffn[l]), csh2, csc2)
            cx = cx + cg2 * expert_choice_ffn(hc2, w_router[l], w_gate[l], w_up[l], w_down[l])
        hx = modulate(rms_norm(x, norm_mix[l]), sh1, sc1)
        mx, _, _ = token_mix(hx, w_in[l], w_out[l], lb_f, lb_b, hg_norm[l], conv_w[l], s_f, s_b, rows)
        x = x + g1 * mx
        hx2 = modulate(rms_norm(x, norm_ffn[l]), sh2, sc2)
        x = x + g2 * expert_choice_ffn(hx2, w_router[l], w_gate[l], w_up[l], w_down[l])
    return rms_norm(x, norm_final)
```

```python
import functools

import jax
import jax.numpy as jnp
from jax import lax
from jax.experimental import pallas as pl
from jax.experimental.pallas import tpu as pltpu

F32 = jnp.float32
BF16 = jnp.bfloat16
I32 = jnp.int32

D_MODEL = 1024
N_MOD = 6
HG_HEADS = 4
HG_KDIM = 128
HG_VDIM = 128
HG_KEYS = HG_HEADS * HG_KDIM
HG_WIDTH = HG_HEADS * HG_VDIM
SC_WIDTH = D_MODEL - HG_WIDTH
GRID_W = 64
CHUNK = 32
N_EXPERTS = 16
EC_CAPACITY = 2
EPS = 1e-6

LANES = 128
SUBLANES = 8
VMEM_LIMIT = 60 * 1024 * 1024

TT = 256
NCH = TT // CHUNK
MOD_ROWS = 16
DFF_TILE = 512
EPI_TOK = 256
ACC_ROWS = D_MODEL // LANES


def _dot(a, b):
    return jnp.dot(a, b, preferred_element_type=F32)


def _dot_nt(a, b):
    return lax.dot_general(a, b, (((1,), (1,)), ((), ())), preferred_element_type=F32)


def _dot_tn(a, b):
    return lax.dot_general(a, b, (((0,), (0,)), ((), ())), preferred_element_type=F32)


def _split(x):
    hi = x.astype(BF16)
    lo = (x - hi.astype(F32)).astype(BF16)
    return hi, lo


def _rms(x, gain):
    return x * lax.rsqrt(jnp.mean(x * x, axis=-1, keepdims=True) + EPS) * gain


def _silu(x):
    return x * jax.nn.sigmoid(x)


def _lower_bounds(lbl_ref):
    rows = [lbl_ref[l] for l in range(lbl_ref.shape[0])]
    m = functools.reduce(jnp.maximum, rows)
    es = [jnp.exp(r - m) for r in rows]
    sm0 = es[0] / functools.reduce(lambda a, b: a + b, es)
    return sm0[0:1, :], sm0[1:2, :]


def _ada_body(c_ref, w_ref, b_ref, o_ref):
    s_hi, s_lo = _split(_silu(c_ref[...]))
    w_hi, w_lo = _split(w_ref[...])
    o_ref[...] = _dot(s_hi, w_hi) + _dot(s_hi, w_lo) + _dot(s_lo, w_hi) + b_ref[...]


def _adaln(cc, w, b):
    n = w.shape[1]
    tn = n // 4
    return pl.pallas_call(
        _ada_body,
        grid=(n // tn,),
        in_specs=[pl.BlockSpec((MOD_ROWS, D_MODEL), lambda i: (0, 0)),
                  pl.BlockSpec((D_MODEL, tn), lambda i: (0, i)),
                  pl.BlockSpec((1, tn), lambda i: (0, i))],
        out_specs=pl.BlockSpec((MOD_ROWS, tn), lambda i: (0, i)),
        out_shape=jax.ShapeDtypeStruct((MOD_ROWS, n), F32),
        compiler_params=pltpu.CompilerParams(vmem_limit_bytes=VMEM_LIMIT),
        name="adaln",
    )(cc, w, b)


def _ctx_body(ctx_ref, mods_ref, nm_ref, wf_ref, wb_ref, wi_ref, lbl_ref, sf_ref, sb_ref, *, ctx_row):
    x = ctx_ref[0]
    n = x.shape[0]
    csh = mods_ref[ctx_row:ctx_row + 1, 0:D_MODEL]
    csc = mods_ref[ctx_row:ctx_row + 1, D_MODEL:2 * D_MODEL]
    hb = (_rms(x, nm_ref[...]) * (1.0 + csc) + csh).astype(BF16)
    zf = _dot(hb, wf_ref[...])
    zb = _dot(hb, wb_ref[...])
    vi = _dot(hb, wi_ref[...]).astype(BF16)
    lb_f, lb_b = _lower_bounds(lbl_ref)
    f_f = lb_f + (1.0 - lb_f) * jax.nn.sigmoid(zf)
    f_b = lb_b + (1.0 - lb_b) * jax.nn.sigmoid(zb)
    ii = lax.broadcasted_iota(I32, (n, n), 0)
    jj = lax.broadcasted_iota(I32, (n, n), 1)
    after = jnp.where(jj > ii, 1.0, 0.0).astype(BF16)
    before = jnp.where(jj < ii, 1.0, 0.0).astype(BF16)
    hi, lo = _split(jnp.log(f_f))
    kd_f = ((1.0 - f_f) * jnp.exp(_dot(after, hi) + _dot(after, lo))).astype(BF16)
    hi, lo = _split(jnp.log(f_b))
    kd_b = ((1.0 - f_b) * jnp.exp(_dot(before, hi) + _dot(before, lo))).astype(BF16)
    for h in range(HG_HEADS):
        ks = slice(h * HG_KDIM, (h + 1) * HG_KDIM)
        vs = slice(h * HG_VDIM, (h + 1) * HG_VDIM)
        sf_ref[0, h] = _dot_tn(vi[:, vs], kd_f[:, ks])
        sb_ref[0, h] = _dot_tn(vi[:, vs], kd_b[:, ks])


def _ctx_states(ctx, mods, nm, w_in_bf, lbl, ctx_row):
    nb, n, _ = ctx.shape
    st = jax.ShapeDtypeStruct((nb, HG_HEADS, HG_VDIM, HG_KDIM), F32)
    st_spec = pl.BlockSpec((1, HG_HEADS, HG_VDIM, HG_KDIM), lambda b: (b, 0, 0, 0))
    wcol = lambda k: pl.BlockSpec((D_MODEL, HG_KEYS), lambda b, k=k: (0, k))
    return pl.pallas_call(
        functools.partial(_ctx_body, ctx_row=ctx_row),
        grid=(nb,),
        in_specs=[pl.BlockSpec((1, n, D_MODEL), lambda b: (b, 0, 0)),
                  pl.BlockSpec(mods.shape, lambda b: (0, 0)),
                  pl.BlockSpec((1, D_MODEL), lambda b: (0, 0)),
                  wcol(1), wcol(2), wcol(3),
                  pl.BlockSpec(lbl.shape, lambda b: (0, 0, 0))],
        out_specs=[st_spec, st_spec],
        out_shape=[st, st],
        compiler_params=pltpu.CompilerParams(vmem_limit_bytes=VMEM_LIMIT),
        name="ctx_states",
    )(ctx, mods, nm, w_in_bf, w_in_bf, w_in_bf, lbl)


def _level_map(reverse):
    ii = lax.broadcasted_iota(I32, (TT, TT), 0)
    jj = lax.broadcasted_iota(I32, (TT, TT), 1)
    if reverse:
        ii = TT - 1 - ii
        jj = TT - 1 - jj
    sh0 = CHUNK.bit_length() - 1
    ci = ii >> sh0
    cj = jj >> sh0
    lm = jnp.where((ci == cj) & (jj <= ii), 1, 0)
    for lvl in range(1, NCH.bit_length()):
        m = ((ci >> lvl) == (cj >> lvl)) & (((ci >> (lvl - 1)) & 1) == 1) & (((cj >> (lvl - 1)) & 1) == 0)
        lm = jnp.where(m, lvl + 1, lm)
    return lm


def _gla_tile(q, zf, v, lb, st_ref, lm, reverse):
    n_lvl = NCH.bit_length() - 1
    f = lb + (1.0 - lb) * jax.nn.sigmoid(zf)
    k = 1.0 - f
    tri = jnp.where(lm == 1, 1.0, 0.0).astype(BF16)
    hi, lo = _split(jnp.log(f))
    cum = _dot(tri, hi) + _dot(tri, lo)
    phys = list(range(NCH - 1, -1, -1)) if reverse else list(range(NCH))
    last = 0 if reverse else CHUNK - 1
    tot = [cum[a * CHUNK + last:a * CHUNK + last + 1, :] for a in range(NCH)]
    bnd = [jnp.zeros_like(tot[0])]
    for c in range(NCH):
        bnd.append(bnd[-1] + tot[phys[c]])
    q0 = q * jnp.exp(cum)
    kinv = (k * jnp.exp(-cum)).astype(BF16)
    qs = [[None] * NCH for _ in range(n_lvl + 1)]
    ks = [[None] * NCH for _ in range(n_lvl + 1)]
    for c in range(NCH):
        a = phys[c]
        rows = slice(a * CHUNK, (a + 1) * CHUNK)
        q0a = q0[rows]
        kend = k[rows] * jnp.exp(tot[a] - cum[rows])
        zero = jnp.zeros_like(q0a)
        for lvl in range(1, n_lvl + 1):
            s = 1 << (lvl - 1)
            mid = (c // (2 * s)) * 2 * s + s
            right = (c // s) % 2 == 1
            qs[lvl - 1][a] = q0a * jnp.exp(bnd[c] - bnd[mid]) if right else zero
            ks[lvl - 1][a] = zero if right else kend * jnp.exp(bnd[mid] - bnd[c + 1])
        qs[n_lvl][a] = q0a * jnp.exp(bnd[c])
        ks[n_lvl][a] = kend * jnp.exp(bnd[NCH] - bnd[c + 1])
    qs = [jnp.concatenate(p, axis=0).astype(BF16) for p in qs]
    ks = [jnp.concatenate(p, axis=0).astype(BF16) for p in ks]
    q0 = q0.astype(BF16)
    vb = v.astype(BF16)
    dec = jnp.exp(bnd[NCH])
    outs = []
    for h in range(HG_HEADS):
        kk = slice(h * HG_KDIM, (h + 1) * HG_KDIM)
        vv = slice(h * HG_VDIM, (h + 1) * HG_VDIM)
        sc = jnp.where(lm == 1, _dot_nt(q0[:, kk], kinv[:, kk]), 0.0)
        for lvl in range(n_lvl):
            sc = jnp.where(lm == lvl + 2, _dot_nt(qs[lvl][:, kk], ks[lvl][:, kk]), sc)
        st = st_ref[h]
        outs.append(_dot(sc.astype(BF16), vb[:, vv]) + _dot_nt(qs[n_lvl][:, kk], st.astype(BF16)))
        st_ref[h] = st * dec[:, kk] + _dot_tn(vb[:, vv], ks[n_lvl][:, kk])
    return jnp.concatenate(outs, axis=1)


def _mix_fwd_body(x_ref, mods_ref, nm_ref, w_ref, lbl_ref, cw_ref, s0_ref,
                  q_ref, zb_ref, i_ref, g_ref, y_ref, of_ref, st_ref, lm_ref):
    b = pl.program_id(0)
    j = pl.program_id(1)

    @pl.when((b == 0) & (j == 0))
    def _():
        lm_ref[...] = _level_map(False)

    @pl.when(j == 0)
    def _():
        st_ref[...] = s0_ref[0]

    sh = mods_ref[pl.ds(b, 1), 0:D_MODEL]
    sc = mods_ref[pl.ds(b, 1), D_MODEL:2 * D_MODEL]
    hx = (_rms(x_ref[0], nm_ref[...]) * (1.0 + sc) + sh).astype(BF16)
    w = HG_KEYS
    proj = lambda k: _dot(hx, w_ref[:, k * w:(k + 1) * w])
    q = proj(0)
    zff = proj(1)
    zb_ref[0] = proj(2)
    vi = proj(3)
    g_ref[0] = proj(4).astype(BF16)
    gate_b = proj(5)
    u = proj(6) * proj(7)
    col = lax.broadcasted_iota(I32, (TT, 1), 0) & (GRID_W - 1)
    up = jnp.where(col != 0, pltpu.roll(u, 1, axis=0), 0.0)
    un = jnp.where(col != GRID_W - 1, pltpu.roll(u, TT - 1, axis=0), 0.0)
    cw = cw_ref[...]
    y_ref[0] = (gate_b * (cw[0:1] * up + cw[1:2] * u + cw[2:3] * un)).astype(BF16)
    lb_f, _ = _lower_bounds(lbl_ref)
    of_ref[0] = _gla_tile(q, zff, vi, lb_f, st_ref, lm_ref[...], False)
    q_ref[0] = q.astype(BF16)
    i_ref[0] = vi.astype(BF16)


def _mix_fwd(x, mods, nm, w_in_bf, lbl, cw, s0_f):
    nb, t, _ = x.shape
    nt = t // TT
    tok = lambda dt: jax.ShapeDtypeStruct((nb, t, HG_WIDTH), dt)
    tspec = pl.BlockSpec((1, TT, HG_WIDTH), lambda b, j: (b, j, 0))
    return pl.pallas_call(
        _mix_fwd_body,
        grid=(nb, nt),
        in_specs=[pl.BlockSpec((1, TT, D_MODEL), lambda b, j: (b, j, 0)),
                  pl.BlockSpec(mods.shape, lambda b, j: (0, 0)),
                  pl.BlockSpec((1, D_MODEL), lambda b, j: (0, 0)),
                  pl.BlockSpec(w_in_bf.shape, lambda b, j: (0, 0)),
                  pl.BlockSpec(lbl.shape, lambda b, j: (0, 0, 0)),
                  pl.BlockSpec(cw.shape, lambda b, j: (0, 0)),
                  pl.BlockSpec((1, HG_HEADS, HG_VDIM, HG_KDIM), lambda b, j: (b, 0, 0, 0))],
        out_specs=[tspec] * 6,
        out_shape=[tok(BF16), tok(F32), tok(BF16), tok(BF16), tok(BF16), tok(F32)],
        scratch_shapes=[pltpu.VMEM((HG_HEADS, HG_VDIM, HG_KDIM), F32), pltpu.VMEM((TT, TT), I32)],
        compiler_params=pltpu.CompilerParams(dimension_semantics=("arbitrary", "arbitrary"),
                                             vmem_limit_bytes=VMEM_LIMIT),
        name="mix_fwd",
    )(x, mods, nm, w_in_bf, lbl, cw, s0_f)


def _mix_bwd_body(x_ref, q_ref, zb_ref, i_ref, g_ref, y_ref, of_ref, mods_ref, lbl_ref, hgn_ref,
                  wo_ref, nf_ref, wr_ref, s0_ref,
                  x1_ref, hr_ref, at_ref, ak_ref, st_ref, lm_ref):
    b = pl.program_id(0)
    j = pl.program_id(1)

    @pl.when((b == 0) & (j == 0))
    def _():
        lm_ref[...] = _level_map(True)

    @pl.when(j == 0)
    def _():
        st_ref[...] = s0_ref[0]

    _, lb_b = _lower_bounds(lbl_ref)
    o_b = _gla_tile(q_ref[0].astype(F32), zb_ref[0], i_ref[0].astype(F32), lb_b, st_ref, lm_ref[...], True)
    o = of_ref[0] + o_b
    hgn = hgn_ref[...]
    heads = [_rms(o[:, h * HG_VDIM:(h + 1) * HG_VDIM], hgn) for h in range(HG_HEADS)]
    g = g_ref[0].astype(F32)
    ohg = jnp.concatenate(heads, axis=1) * _silu(g)
    cat = jnp.concatenate([ohg.astype(BF16), y_ref[0]], axis=1)
    mx = _dot(cat, wo_ref[...])
    mod = lambda k: mods_ref[pl.ds(b, 1), k * D_MODEL:(k + 1) * D_MODEL]
    x1 = x_ref[0] + mod(2) * mx
    x1_ref[0] = x1
    h2 = _rms(x1, nf_ref[...]) * (1.0 + mod(4)) + mod(3)
    for c in range(ACC_ROWS):
        hr_ref[0, pl.ds(c, TT, stride=ACC_ROWS), :] = h2[:, c * LANES:(c + 1) * LANES]
    hh, hl = _split(h2)
    wh, wl = _split(wr_ref[...])
    lg = _dot(hh, wh) + _dot(hh, wl) + _dot(hl, wh)
    lane = lax.broadcasted_iota(I32, lg.shape, 1)
    lg = jnp.where(lane < N_EXPERTS, lg, -jnp.inf)
    e = jnp.exp(lg - jnp.max(lg, axis=-1, keepdims=True))
    aff = e / jnp.sum(e, axis=-1, keepdims=True)
    ak_ref[0] = aff
    at_ref[0] = aff.T[0:N_EXPERTS, :]


def _mix_bwd(x, q, zb, vi, g, y, of, mods, lbl, hgn, w_out_bf, nf, wr_pad, s0_b):
    nb, t, _ = x.shape
    nt = t // TT
    rev = lambda b, j: (b, nt - 1 - j, 0)
    tspec = pl.BlockSpec((1, TT, HG_WIDTH), rev)
    xspec = pl.BlockSpec((1, TT, D_MODEL), rev)
    full2 = lambda a: pl.BlockSpec(a.shape, lambda b, j: (0, 0))
    return pl.pallas_call(
        _mix_bwd_body,
        grid=(nb, nt),
        in_specs=[xspec, tspec, tspec, tspec, tspec, tspec, tspec,
                  full2(mods), pl.BlockSpec(lbl.shape, lambda b, j: (0, 0, 0)), full2(hgn),
                  full2(w_out_bf), full2(nf), full2(wr_pad),
                  pl.BlockSpec((1, HG_HEADS, HG_VDIM, HG_KDIM), lambda b, j: (b, 0, 0, 0))],
        out_specs=[xspec,
                   pl.BlockSpec((1, TT * ACC_ROWS, LANES), rev),
                   pl.BlockSpec((1, N_EXPERTS, TT), lambda b, j: (b, 0, nt - 1 - j)),
                   pl.BlockSpec((1, TT, LANES), rev)],
        out_shape=[jax.ShapeDtypeStruct((nb, t, D_MODEL), F32),
                   jax.ShapeDtypeStruct((nb, t * ACC_ROWS, LANES), F32),
                   jax.ShapeDtypeStruct((nb, N_EXPERTS, t), F32),
                   jax.ShapeDtypeStruct((nb, t, LANES), F32)],
        scratch_shapes=[pltpu.VMEM((HG_HEADS, HG_VDIM, HG_KDIM), F32), pltpu.VMEM((TT, TT), I32)],
        compiler_params=pltpu.CompilerParams(dimension_semantics=("arbitrary", "arbitrary"),
                                             vmem_limit_bytes=VMEM_LIMIT),
        name="mix_bwd",
    )(x, q, zb, vi, g, y, of, mods, lbl, hgn, w_out_bf, nf, wr_pad, s0_b)


def _route_body(at_ref, idx_ref, inc_ref, col_ref, *, cap):
    ne, t = at_ref.shape[1], at_ref.shape[2]
    blk = 2 * LANES
    aff = at_ref[0]

    def count(m):
        return jnp.sum(jnp.where(m, 1.0, 0.0), axis=1, keepdims=True)

    def bit_step(i, thr):
        cand = thr | jnp.left_shift(jnp.int32(1), 30 - i)
        return jnp.where(count(aff >= lax.bitcast_convert_type(cand, F32)) >= cap, cand, thr)

    thr = lax.fori_loop(0, 31, bit_step, jnp.zeros((ne, 1), I32))
    gt = aff >= lax.bitcast_convert_type(thr + 1, F32)
    eq = (aff >= lax.bitcast_convert_type(thr, F32)) & jnp.logical_not(gt)
    need = cap - count(gt)

    ii = lax.broadcasted_iota(I32, (blk, blk), 0)
    jj = lax.broadcasted_iota(I32, (blk, blk), 1)
    upper = jnp.where(ii <= jj, 1.0, 0.0).astype(BF16)

    def prefix_blocks(m):
        mb = jnp.where(m, 1.0, 0.0).astype(BF16)
        carry = jnp.zeros((ne, 1), F32)
        out = []
        for kb in range(t // blk):
            p = _dot(mb[:, kb * blk:(kb + 1) * blk], upper) + carry
            out.append(p)
            carry = p[:, blk - 1:blk]
        return out

    eq_rank = jnp.concatenate(prefix_blocks(eq), axis=1)
    sel = gt | (eq & (eq_rank <= need))
    for kb, p in enumerate(prefix_blocks(sel)):
        for half in range(blk // LANES):
            inc_ref[kb * (blk // LANES) + half] = p[:, half * LANES:(half + 1) * LANES]

    n_lb = t // LANES
    lane = lax.broadcasted_iota(I32, (LANES, LANES), 1)
    col_ref[...] = jnp.zeros_like(col_ref)

    def expert_step(e, carry):
        for jg in range(cap // LANES):
            slot = (lax.broadcasted_iota(I32, (LANES, LANES), 0) + jg * LANES).astype(F32)

            def lane_block(lb, acc):
                r = inc_ref[lb, pl.ds(e, 1), :]
                return acc + jnp.where(r <= slot, 1.0, 0.0)

            acc = lax.fori_loop(0, n_lb, lane_block, jnp.zeros((LANES, LANES), F32), unroll=4)
            cnt = jnp.sum(acc, axis=1, keepdims=True)
            rows = slice(jg * LANES, (jg + 1) * LANES)
            col_ref[rows, :] = jnp.where(lane == e, cnt, col_ref[rows, :])
        return carry

    lax.fori_loop(0, ne, expert_step, 0)
    idx_ref[0] = col_ref[...].T[0:ne, :].astype(I32)


def _route(aff_t, cap):
    nb, ne, t = aff_t.shape
    return pl.pallas_call(
        functools.partial(_route_body, cap=cap),
        grid=(nb,),
        in_specs=[pl.BlockSpec((1, ne, t), lambda b: (b, 0, 0))],
        out_specs=pl.BlockSpec((1, ne, cap), lambda b: (b, 0, 0)),
        out_shape=jax.ShapeDtypeStruct((nb, ne, cap), I32),
        scratch_shapes=[pltpu.VMEM((t // LANES, ne, LANES), F32), pltpu.VMEM((cap, LANES), F32)],
        compiler_params=pltpu.CompilerParams(vmem_limit_bytes=VMEM_LIMIT),
        name="route",
    )(aff_t)


def _moe_body(idx_ref, hr_ref, ak_ref, wg_ref, wu_ref, wd_ref, x1_ref, mods_ref, nfin_ref,
              out_ref, acc_ref, row_ref, aff_ref, xs_ref, ys_ref, *, cap, n_ff):
    b = pl.program_id(0)
    s = pl.program_id(1)
    n_work = N_EXPERTS * n_ff
    e = s // n_ff
    ff = s % n_ff
    base = (b * N_EXPERTS + e) * cap
    unroll = SUBLANES

    @pl.when(s == 0)
    def _():
        acc_ref[...] = jnp.zeros_like(acc_ref)

    @pl.when((s < n_work) & (ff == 0))
    def _():
        def gather(jo, carry):
            for u in range(unroll):
                jr = jo * unroll + u
                tkn = idx_ref[base + jr]
                row_ref[pl.ds(pl.multiple_of(jr * ACC_ROWS, ACC_ROWS), ACC_ROWS), :] = (
                    hr_ref[0, pl.ds(pl.multiple_of(tkn * ACC_ROWS, ACC_ROWS), ACC_ROWS), :])
                aff_ref[pl.ds(jr, 1), :] = ak_ref[0, pl.ds(tkn, 1), :]
            return carry

        lax.fori_loop(0, cap // unroll, gather, 0)
        for c in range(ACC_ROWS):
            xs_ref[:, c * LANES:(c + 1) * LANES] = row_ref[pl.ds(c, cap, stride=ACC_ROWS), :].astype(BF16)

    @pl.when(s < n_work)
    def _():
        xs = xs_ref[...]
        hid = _silu(_dot(xs, wg_ref[0])) * _dot(xs, wu_ref[0])
        part = _dot(hid.astype(BF16), wd_ref[0])

        @pl.when(ff == 0)
        def _():
            ys_ref[...] = part

        @pl.when(ff != 0)
        def _():
            ys_ref[...] += part

    @pl.when((s < n_work) & (ff == n_ff - 1))
    def _():
        lane = lax.broadcasted_iota(I32, aff_ref.shape, 1)
        gate = jnp.sum(jnp.where(lane == e, aff_ref[...], 0.0), axis=1, keepdims=True)
        for c in range(ACC_ROWS):
            row_ref[pl.ds(c, cap, stride=ACC_ROWS), :] = ys_ref[:, c * LANES:(c + 1) * LANES] * gate

        def scatter(jo, carry):
            dst = []
            val = []
            for u in range(unroll):
                jr = jo * unroll + u
                r = pl.multiple_of(idx_ref[base + jr] * ACC_ROWS, ACC_ROWS)
                dst.append(r)
                val.append(acc_ref[pl.ds(r, ACC_ROWS), :]
                           + row_ref[pl.ds(pl.multiple_of(jr * ACC_ROWS, ACC_ROWS), ACC_ROWS), :])
            for u in range(unroll):
                acc_ref[pl.ds(dst[u], ACC_ROWS), :] = val[u]
            return carry

        lax.fori_loop(0, cap // unroll, scatter, 0)

    @pl.when(s >= n_work)
    def _():
        row0 = (s - n_work) * (EPI_TOK * ACC_ROWS)
        moe = jnp.concatenate([acc_ref[pl.ds(row0 + c, EPI_TOK, stride=ACC_ROWS), :] for c in range(ACC_ROWS)], axis=1)
        g2 = mods_ref[pl.ds(b, 1), 5 * D_MODEL:6 * D_MODEL]
        out_ref[0] = _rms(x1_ref[0] + g2 * moe, nfin_ref[...])


def _moe(idx_flat, hr, ak, wg, wu, wd, x1, mods, nfin, cap):
    nb, t, _ = x1.shape
    dff = wg.shape[2]
    n_ff = dff // DFF_TILE
    n_work = N_EXPERTS * n_ff
    n_epi = t // EPI_TOK

    def w_idx(s):
        sc = jnp.minimum(s, n_work - 1)
        return sc // n_ff, sc % n_ff

    def epi(b, s, idx):
        return (b, jnp.maximum(s - n_work, 0), 0)

    one = pl.Buffered(1)
    grid_spec = pltpu.PrefetchScalarGridSpec(
        num_scalar_prefetch=1,
        grid=(nb, n_work + n_epi),
        in_specs=[pl.BlockSpec((1, t * ACC_ROWS, LANES), lambda b, s, idx: (b, 0, 0), pipeline_mode=one),
                  pl.BlockSpec((1, t, LANES), lambda b, s, idx: (b, 0, 0), pipeline_mode=one),
                  pl.BlockSpec((1, D_MODEL, DFF_TILE), lambda b, s, idx: (w_idx(s)[0], 0, w_idx(s)[1])),
                  pl.BlockSpec((1, D_MODEL, DFF_TILE), lambda b, s, idx: (w_idx(s)[0], 0, w_idx(s)[1])),
                  pl.BlockSpec((1, DFF_TILE, D_MODEL), lambda b, s, idx: (w_idx(s)[0], w_idx(s)[1], 0)),
                  pl.BlockSpec((1, EPI_TOK, D_MODEL), epi),
                  pl.BlockSpec(mods.shape, lambda b, s, idx: (0, 0)),
                  pl.BlockSpec((1, D_MODEL), lambda b, s, idx: (0, 0))],
        out_specs=pl.BlockSpec((1, EPI_TOK, D_MODEL), epi),
        scratch_shapes=[pltpu.VMEM((t * ACC_ROWS, LANES), F32),
                        pltpu.VMEM((cap * ACC_ROWS, LANES), F32),
                        pltpu.VMEM((cap, LANES), F32),
                        pltpu.VMEM((cap, D_MODEL), BF16),
                        pltpu.VMEM((cap, D_MODEL), F32)])
    return pl.pallas_call(
        functools.partial(_moe_body, cap=cap, n_ff=n_ff),
        grid_spec=grid_spec,
        out_shape=jax.ShapeDtypeStruct((nb, t, D_MODEL), F32),
        compiler_params=pltpu.CompilerParams(dimension_semantics=("arbitrary", "arbitrary"),
                                             vmem_limit_bytes=VMEM_LIMIT),
        name="moe",
    )(idx_flat, hr, ak, wg, wu, wd, x1, mods, nfin)


def kernel(x, c, ctx, c_ctx, w_ada, b_ada, norm_mix, norm_ffn, w_in, lb_logits, hg_norm, conv_w,
           w_out, w_router, w_gate, w_up, w_down, norm_final):
    nb, t, d = x.shape
    assert d == D_MODEL and w_ada.shape[0] == 1 and nb < MOD_ROWS and t % TT == 0 and TT % GRID_W == 0
    assert w_in.shape[2] == 3 * HG_KEYS + 2 * HG_WIDTH + 3 * SC_WIDTH and w_router.shape[2] == N_EXPERTS
    cap = EC_CAPACITY * t // N_EXPERTS
    assert cap % LANES == 0 and w_gate.shape[3] % DFF_TILE == 0

    cc = jnp.concatenate([c, c_ctx[None, :], jnp.zeros((MOD_ROWS - nb - 1, d), F32)], axis=0)
    w_in_bf = w_in[0].astype(BF16)
    w_out_bf = w_out[0].astype(BF16)
    wg, wu, wd = w_gate[0].astype(BF16), w_up[0].astype(BF16), w_down[0].astype(BF16)
    wr_pad = jnp.pad(w_router[0], ((0, 0), (0, LANES - N_EXPERTS)))
    nm, nf, nfin = norm_mix[0][None, :], norm_ffn[0][None, :], norm_final[None, :]
    hgn = hg_norm[0][None, :]

    mods = _adaln(cc, w_ada[0], b_ada[0][None, :])
    s0_f, s0_b = _ctx_states(ctx, mods, nm, w_in_bf, lb_logits, ctx_row=nb)
    q, zb, vi, g, y, of = _mix_fwd(x, mods, nm, w_in_bf, lb_logits, conv_w[0], s0_f)
    x1, hr, aff_t, aff_k = _mix_bwd(x, q, zb, vi, g, y, of, mods, lb_logits, hgn, w_out_bf, nf, wr_pad, s0_b)
    idx = _route(aff_t, cap)
    return _moe(idx.reshape(-1), hr, aff_k, wg, wu, wd, x1, mods, nfin, cap)
```

```python
import functools

import jax
import jax.numpy as jnp
from jax import lax
from jax.experimental import pallas as pl
from jax.experimental.pallas import tpu as pltpu

F32 = jnp.float32
BF16 = jnp.bfloat16
I32 = jnp.int32

D_MODEL = 1024
N_MOD = 6
HG_HEADS = 4
HG_KDIM = 128
HG_VDIM = 128
HG_KEYS = HG_HEADS * HG_KDIM
HG_WIDTH = HG_HEADS * HG_VDIM
SC_WIDTH = D_MODEL - HG_WIDTH
GRID_W = 64
CHUNK = 32
N_EXPERTS = 16
EC_CAPACITY = 2
EPS = 1e-6

LANES = 128
SUBLANES = 8
VMEM_LIMIT = 60 * 1024 * 1024

TT = 256
NCH = TT // CHUNK
MOD_ROWS = 16
DFF_TILE = 512
EPI_TOK = 256
ACC_ROWS = D_MODEL // LANES


def _dot(a, b):
    return jnp.dot(a, b, preferred_element_type=F32)


def _dot_nt(a, b):
    return lax.dot_general(a, b, (((1,), (1,)), ((), ())), preferred_element_type=F32)


def _dot_tn(a, b):
    return lax.dot_general(a, b, (((0,), (0,)), ((), ())), preferred_element_type=F32)


def _split(x):
    hi = x.astype(BF16)
    lo = (x - hi.astype(F32)).astype(BF16)
    return hi, lo


def _rms(x, gain):
    return x * lax.rsqrt(jnp.mean(x * x, axis=-1, keepdims=True) + EPS) * gain


def _silu(x):
    return x * jax.nn.sigmoid(x)


def _lower_bounds(lbl_ref):
    rows = [lbl_ref[l] for l in range(lbl_ref.shape[0])]
    m = functools.reduce(jnp.maximum, rows)
    es = [jnp.exp(r - m) for r in rows]
    sm0 = es[0] / functools.reduce(lambda a, b: a + b, es)
    return sm0[0:1, :], sm0[1:2, :]


def _ada_body(c_ref, w_ref, b_ref, o_ref):
    s_hi, s_lo = _split(_silu(c_ref[...]))
    w_hi, w_lo = _split(w_ref[...])
    o_ref[...] = _dot(s_hi, w_hi) + _dot(s_hi, w_lo) + _dot(s_lo, w_hi) + b_ref[...]


def _adaln(cc, w, b):
    n = w.shape[1]
    tn = n // 4
    return pl.pallas_call(
        _ada_body,
        grid=(n // tn,),
        in_specs=[pl.BlockSpec((MOD_ROWS, D_MODEL), lambda i: (0, 0)),
                  pl.BlockSpec((D_MODEL, tn), lambda i: (0, i)),
                  pl.BlockSpec((1, tn), lambda i: (0, i))],
        out_specs=pl.BlockSpec((MOD_ROWS, tn), lambda i: (0, i)),
        out_shape=jax.ShapeDtypeStruct((MOD_ROWS, n), F32),
        compiler_params=pltpu.CompilerParams(vmem_limit_bytes=VMEM_LIMIT),
        name="adaln",
    )(cc, w, b)


def _ctx_body(ctx_ref, mods_ref, nm_ref, wf_ref, wb_ref, wi_ref, lbl_ref, sf_ref, sb_ref, *, ctx_row):
    x = ctx_ref[0]
    n = x.shape[0]
    csh = mods_ref[ctx_row:ctx_row + 1, 0:D_MODEL]
    csc = mods_ref[ctx_row:ctx_row + 1, D_MODEL:2 * D_MODEL]
    hb = (_rms(x, nm_ref[...]) * (1.0 + csc) + csh).astype(BF16)
    zf = _dot(hb, wf_ref[...])
    zb = _dot(hb, wb_ref[...])
    vi = _dot(hb, wi_ref[...]).astype(BF16)
    lb_f, lb_b = _lower_bounds(lbl_ref)
    f_f = lb_f + (1.0 - lb_f) * jax.nn.sigmoid(zf)
    f_b = lb_b + (1.0 - lb_b) * jax.nn.sigmoid(zb)
    ii = lax.broadcasted_iota(I32, (n, n), 0)
    jj = lax.broadcasted_iota(I32, (n, n), 1)
    after = jnp.where(jj > ii, 1.0, 0.0).astype(BF16)
    before = jnp.where(jj < ii, 1.0, 0.0).astype(BF16)
    hi, lo = _split(jnp.log(f_f))
    kd_f = ((1.0 - f_f) * jnp.exp(_dot(after, hi) + _dot(after, lo))).astype(BF16)
    hi, lo = _split(jnp.log(f_b))
    kd_b = ((1.0 - f_b) * jnp.exp(_dot(before, hi) + _dot(before, lo))).astype(BF16)
    for h in range(HG_HEADS):
        ks = slice(h * HG_KDIM, (h + 1) * HG_KDIM)
        vs = slice(h * HG_VDIM, (h + 1) * HG_VDIM)
        sf_ref[0, h] = _dot_tn(vi[:, vs], kd_f[:, ks])
        sb_ref[0, h] = _dot_tn(vi[:, vs], kd_b[:, ks])


def _ctx_states(ctx, mods, nm, w_in_bf, lbl, ctx_row):
    nb, n, _ = ctx.shape
    st = jax.ShapeDtypeStruct((nb, HG_HEADS, HG_VDIM, HG_KDIM), F32)
    st_spec = pl.BlockSpec((1, HG_HEADS, HG_VDIM, HG_KDIM), lambda b: (b, 0, 0, 0))
    wcol = lambda k: pl.BlockSpec((D_MODEL, HG_KEYS), lambda b, k=k: (0, k))
    return pl.pallas_call(
        functools.partial(_ctx_body, ctx_row=ctx_row),
        grid=(nb,),
        in_specs=[pl.BlockSpec((1, n, D_MODEL), lambda b: (b, 0, 0)),
                  pl.BlockSpec(mods.shape, lambda b: (0, 0)),
                  pl.BlockSpec((1, D_MODEL), lambda b: (0, 0)),
                  wcol(1), wcol(2), wcol(3),
                  pl.BlockSpec(lbl.shape, lambda b: (0, 0, 0))],
        out_specs=[st_spec, st_spec],
        out_shape=[st, st],
        compiler_params=pltpu.CompilerParams(vmem_limit_bytes=VMEM_LIMIT),
        name="ctx_states",
    )(ctx, mods, nm, w_in_bf, w_in_bf, w_in_bf, lbl)


def _level_map(reverse):
    ii = lax.broadcasted_iota(I32, (TT, TT), 0)
    jj = lax.broadcasted_iota(I32, (TT, TT), 1)
    if reverse:
        ii = TT - 1 - ii
        jj = TT - 1 - jj
    sh0 = CHUNK.bit_length() - 1
    ci = ii >> sh0
    cj = jj >> sh0
    lm = jnp.where((ci == cj) & (jj <= ii), 1, 0)
    for lvl in range(1, NCH.bit_length()):
        m = ((ci >> lvl) == (cj >> lvl)) & (((ci >> (lvl - 1)) & 1) == 1) & (((cj >> (lvl - 1)) & 1) == 0)
        lm = jnp.where(m, lvl + 1, lm)
    return lm


def _gla_tile(q, zf, v, lb, st_ref, lm, reverse):
    n_lvl = NCH.bit_length() - 1
    f = lb + (1.0 - lb) * jax.nn.sigmoid(zf)
    k = 1.0 - f
    tri = jnp.where(lm == 1, 1.0, 0.0).astype(BF16)
    hi, lo = _split(jnp.log(f))
    cum = _dot(tri, hi) + _dot(tri, lo)
    phys = list(range(NCH - 1, -1, -1)) if reverse else list(range(NCH))
    last = 0 if reverse else CHUNK - 1
    tot = [cum[a * CHUNK + last:a * CHUNK + last + 1, :] for a in range(NCH)]
    bnd = [jnp.zeros_like(tot[0])]
    for c in range(NCH):
        bnd.append(bnd[-1] + tot[phys[c]])
    q0 = q * jnp.exp(cum)
    kinv = (k * jnp.exp(-cum)).astype(BF16)
    qs = [[None] * NCH for _ in range(n_lvl + 1)]
    ks = [[None] * NCH for _ in range(n_lvl + 1)]
    for c in range(NCH):
        a = phys[c]
        rows = slice(a * CHUNK, (a + 1) * CHUNK)
        q0a = q0[rows]
        kend = k[rows] * jnp.exp(tot[a] - cum[rows])
        zero = jnp.zeros_like(q0a)
        for lvl in range(1, n_lvl + 1):
            s = 1 << (lvl - 1)
            mid = (c // (2 * s)) * 2 * s + s
            right = (c // s) % 2 == 1
            qs[lvl - 1][a] = q0a * jnp.exp(bnd[c] - bnd[mid]) if right else zero
            ks[lvl - 1][a] = zero if right else kend * jnp.exp(bnd[mid] - bnd[c + 1])
        qs[n_lvl][a] = q0a * jnp.exp(bnd[c])
        ks[n_lvl][a] = kend * jnp.exp(bnd[NCH] - bnd[c + 1])
    qs = [jnp.concatenate(p, axis=0).astype(BF16) for p in qs]
    ks = [jnp.concatenate(p, axis=0).astype(BF16) for p in ks]
    q0 = q0.astype(BF16)
    vb = v.astype(BF16)
    dec = jnp.exp(bnd[NCH])
    outs = []
    for h in range(HG_HEADS):
        kk = slice(h * HG_KDIM, (h + 1) * HG_KDIM)
        vv = slice(h * HG_VDIM, (h + 1) * HG_VDIM)
        sc = jnp.where(lm == 1, _dot_nt(q0[:, kk], kinv[:, kk]), 0.0)
        for lvl in range(n_lvl):
            sc = jnp.where(lm == lvl + 2, _dot_nt(qs[lvl][:, kk], ks[lvl][:, kk]), sc)
        st = st_ref[h]
        outs.append(_dot(sc.astype(BF16), vb[:, vv]) + _dot_nt(qs[n_lvl][:, kk], st.astype(BF16)))
        st_ref[h] = st * dec[:, kk] + _dot_tn(vb[:, vv], ks[n_lvl][:, kk])
    return jnp.concatenate(outs, axis=1)


def _mix_fwd_body(x_ref, mods_ref, nm_ref, w_ref, lbl_ref, cw_ref, s0_ref,
                  q_ref, zb_ref, i_ref, g_ref, y_ref, of_ref, st_ref, lm_ref):
    b = pl.program_id(0)
    j = pl.program_id(1)

    @pl.when((b == 0) & (j == 0))
    def _():
        lm_ref[...] = _level_map(False)

    @pl.when(j == 0)
    def _():
        st_ref[...] = s0_ref[0]

    sh = mods_ref[pl.ds(b, 1), 0:D_MODEL]
    sc = mods_ref[pl.ds(b, 1), D_MODEL:2 * D_MODEL]
    hx = (_rms(x_ref[0], nm_ref[...]) * (1.0 + sc) + sh).astype(BF16)
    w = HG_KEYS
    proj = lambda k: _dot(hx, w_ref[:, k * w:(k + 1) * w])
    q = proj(0)
    zff = proj(1)
    zb_ref[0] = proj(2)
    vi = proj(3)
    g_ref[0] = proj(4).astype(BF16)
    gate_b = proj(5)
    u = proj(6) * proj(7)
    col = lax.broadcasted_iota(I32, (TT, 1), 0) & (GRID_W - 1)
    up = jnp.where(col != 0, pltpu.roll(u, 1, axis=0), 0.0)
    un = jnp.where(col != GRID_W - 1, pltpu.roll(u, TT - 1, axis=0), 0.0)
    cw = cw_ref[...]
    y_ref[0] = (gate_b * (cw[0:1] * up + cw[1:2] * u + cw[2:3] * un)).astype(BF16)
    lb_f, _ = _lower_bounds(lbl_ref)
    of_ref[0] = _gla_tile(q, zff, vi, lb_f, st_ref, lm_ref[...], False)
    q_ref[0] = q.astype(BF16)
    i_ref[0] = vi.astype(BF16)


def _mix_fwd(x, mods, nm, w_in_bf, lbl, cw, s0_f):
    nb, t, _ = x.shape
    nt = t // TT
    tok = lambda dt: jax.ShapeDtypeStruct((nb, t, HG_WIDTH), dt)
    tspec = pl.BlockSpec((1, TT, HG_WIDTH), lambda b, j: (b, j, 0))
    return pl.pallas_call(
        _mix_fwd_body,
        grid=(nb, nt),
        in_specs=[pl.BlockSpec((1, TT, D_MODEL), lambda b, j: (b, j, 0)),
                  pl.BlockSpec(mods.shape, lambda b, j: (0, 0)),
                  pl.BlockSpec((1, D_MODEL), lambda b, j: (0, 0)),
                  pl.BlockSpec(w_in_bf.shape, lambda b, j: (0, 0)),
                  pl.BlockSpec(lbl.shape, lambda b, j: (0, 0, 0)),
                  pl.BlockSpec(cw.shape, lambda b, j: (0, 0)),
                  pl.BlockSpec((1, HG_HEADS, HG_VDIM, HG_KDIM), lambda b, j: (b, 0, 0, 0))],
        out_specs=[tspec] * 6,
        out_shape=[tok(BF16), tok(F32), tok(BF16), tok(BF16), tok(BF16), tok(F32)],
        scratch_shapes=[pltpu.VMEM((HG_HEADS, HG_VDIM, HG_KDIM), F32), pltpu.VMEM((TT, TT), I32)],
        compiler_params=pltpu.CompilerParams(dimension_semantics=("arbitrary", "arbitrary"),
                                             vmem_limit_bytes=VMEM_LIMIT),
        name="mix_fwd",
    )(x, mods, nm, w_in_bf, lbl, cw, s0_f)


def _mix_bwd_body(x_ref, q_ref, zb_ref, i_ref, g_ref, y_ref, of_ref, mods_ref, lbl_ref, hgn_ref,
                  wo_ref, nf_ref, wr_ref, s0_ref,
                  x1_ref, hr_ref, at_ref, ak_ref, st_ref, lm_ref):
    b = pl.program_id(0)
    j = pl.program_id(1)

    @pl.when((b == 0) & (j == 0))
    def _():
        lm_ref[...] = _level_map(True)

    @pl.when(j == 0)
    def _():
        st_ref[...] = s0_ref[0]

    _, lb_b = _lower_bounds(lbl_ref)
    o_b = _gla_tile(q_ref[0].astype(F32), zb_ref[0], i_ref[0].astype(F32), lb_b, st_ref, lm_ref[...], True)
    o = of_ref[0] + o_b
    hgn = hgn_ref[...]
    heads = [_rms(o[:, h * HG_VDIM:(h + 1) * HG_VDIM], hgn) for h in range(HG_HEADS)]
    g = g_ref[0].astype(F32)
    ohg = jnp.concatenate(heads, axis=1) * _silu(g)
    cat = jnp.concatenate([ohg.astype(BF16), y_ref[0]], axis=1)
    mx = _dot(cat, wo_ref[...])
    mod = lambda k: mods_ref[pl.ds(b, 1), k * D_MODEL:(k + 1) * D_MODEL]
    x1 = x_ref[0] + mod(2) * mx
    x1_ref[0] = x1
    h2 = _rms(x1, nf_ref[...]) * (1.0 + mod(4)) + mod(3)
    for c in range(ACC_ROWS):
        hr_ref[0, pl.ds(c, TT, stride=ACC_ROWS), :] = h2[:, c * LANES:(c + 1) * LANES]
    hh, hl = _split(h2)
    wh, wl = _split(wr_ref[...])
    lg = _dot(hh, wh) + _dot(hh, wl) + _dot(hl, wh)
    lane = lax.broadcasted_iota(I32, lg.shape, 1)
    lg = jnp.where(lane < N_EXPERTS, lg, -jnp.inf)
    e = jnp.exp(lg - jnp.max(lg, axis=-1, keepdims=True))
    aff = e / jnp.sum(e, axis=-1, keepdims=True)
    ak_ref[0] = aff
    at_ref[0] = aff.T[0:N_EXPERTS, :]


def _mix_bwd(x, q, zb, vi, g, y, of, mods, lbl, hgn, w_out_bf, nf, wr_pad, s0_b):
    nb, t, _ = x.shape
    nt = t // TT
    rev = lambda b, j: (b, nt - 1 - j, 0)
    tspec = pl.BlockSpec((1, TT, HG_WIDTH), rev)
    xspec = pl.BlockSpec((1, TT, D_MODEL), rev)
    full2 = lambda a: pl.BlockSpec(a.shape, lambda b, j: (0, 0))
    return pl.pallas_call(
        _mix_bwd_body,
        grid=(nb, nt),
        in_specs=[xspec, tspec, tspec, tspec, tspec, tspec, tspec,
                  full2(mods), pl.BlockSpec(lbl.shape, lambda b, j: (0, 0, 0)), full2(hgn),
                  full2(w_out_bf), full2(nf), full2(wr_pad),
                  pl.BlockSpec((1, HG_HEADS, HG_VDIM, HG_KDIM), lambda b, j: (b, 0, 0, 0))],
        out_specs=[xspec,
                   pl.BlockSpec((1, TT * ACC_ROWS, LANES), rev),
                   pl.BlockSpec((1, N_EXPERTS, TT), lambda b, j: (b, 0, nt - 1 - j)),
                   pl.BlockSpec((1, TT, LANES), rev)],
        out_shape=[jax.ShapeDtypeStruct((nb, t, D_MODEL), F32),
                   jax.ShapeDtypeStruct((nb, t * ACC_ROWS, LANES), F32),
                   jax.ShapeDtypeStruct((nb, N_EXPERTS, t), F32),
                   jax.ShapeDtypeStruct((nb, t, LANES), F32)],
        scratch_shapes=[pltpu.VMEM((HG_HEADS, HG_VDIM, HG_KDIM), F32), pltpu.VMEM((TT, TT), I32)],
        compiler_params=pltpu.CompilerParams(dimension_semantics=("arbitrary", "arbitrary"),
                                             vmem_limit_bytes=VMEM_LIMIT),
        name="mix_bwd",
    )(x, q, zb, vi, g, y, of, mods, lbl, hgn, w_out_bf, nf, wr_pad, s0_b)


def _route_body(at_ref, idx_ref, inc_ref, col_ref, *, cap):
    ne, t = at_ref.shape[1], at_ref.shape[2]
    blk = 2 * LANES
    aff = at_ref[0]

    def count(m):
        return jnp.sum(jnp.where(m, 1.0, 0.0), axis=1, keepdims=True)

    def bit_step(i, thr):
        cand = thr | jnp.left_shift(jnp.int32(1), 30 - i)
        return jnp.where(count(aff >= lax.bitcast_convert_type(cand, F32)) >= cap, cand, thr)

    thr = lax.fori_loop(0, 31, bit_step, jnp.zeros((ne, 1), I32))
    gt = aff >= lax.bitcast_convert_type(thr + 1, F32)
    eq = (aff >= lax.bitcast_convert_type(thr, F32)) & jnp.logical_not(gt)
    need = cap - count(gt)

    ii = lax.broadcasted_iota(I32, (blk, blk), 0)
    jj = lax.broadcasted_iota(I32, (blk, blk), 1)
    upper = jnp.where(ii <= jj, 1.0, 0.0).astype(BF16)

    def prefix_blocks(m):
        mb = jnp.where(m, 1.0, 0.0).astype(BF16)
        carry = jnp.zeros((ne, 1), F32)
        out = []
        for kb in range(t // blk):
            p = _dot(mb[:, kb * blk:(kb + 1) * blk], upper) + carry
            out.append(p)
            carry = p[:, blk - 1:blk]
        return out

    eq_rank = jnp.concatenate(prefix_blocks(eq), axis=1)
    sel = gt | (eq & (eq_rank <= need))
    for kb, p in enumerate(prefix_blocks(sel)):
        for half in range(blk // LANES):
            inc_ref[kb * (blk // LANES) + half] = p[:, half * LANES:(half + 1) * LANES]

    n_lb = t // LANES
    lane = lax.broadcasted_iota(I32, (LANES, LANES), 1)
    col_ref[...] = jnp.zeros_like(col_ref)

    def expert_step(e, carry):
        for jg in range(cap // LANES):
            slot = (lax.broadcasted_iota(I32, (LANES, LANES), 0) + jg * LANES).astype(F32)

            def lane_block(lb, acc):
                r = inc_ref[lb, pl.ds(e, 1), :]
                return acc + jnp.where(r <= slot, 1.0, 0.0)

            acc = lax.fori_loop(0, n_lb, lane_block, jnp.zeros((LANES, LANES), F32), unroll=4)
            cnt = jnp.sum(acc, axis=1, keepdims=True)
            rows = slice(jg * LANES, (jg + 1) * LANES)
            col_ref[rows, :] = jnp.where(lane == e, cnt, col_ref[rows, :])
        return carry

    lax.fori_loop(0, ne, expert_step, 0)
    idx_ref[0] = col_ref[...].T[0:ne, :].astype(I32)


def _route(aff_t, cap):
    nb, ne, t = aff_t.shape
    return pl.pallas_call(
        functools.partial(_route_body, cap=cap),
        grid=(nb,),
        in_specs=[pl.BlockSpec((1, ne, t), lambda b: (b, 0, 0))],
        out_specs=pl.BlockSpec((1, ne, cap), lambda b: (b, 0, 0)),
        out_shape=jax.ShapeDtypeStruct((nb, ne, cap), I32),
        scratch_shapes=[pltpu.VMEM((t // LANES, ne, LANES), F32), pltpu.VMEM((cap, LANES), F32)],
        compiler_params=pltpu.CompilerParams(vmem_limit_bytes=VMEM_LIMIT),
        name="route",
    )(aff_t)


def _moe_body(idx_ref, hr_ref, ak_ref, wg_ref, wu_ref, wd_ref, x1_ref, mods_ref, nfin_ref,
              out_ref, acc_ref, rowg_ref, rows_ref, affg_ref, gate_ref, xs_ref, ys_ref, *, cap, n_ff):
    b = pl.program_id(0)
    s = pl.program_id(1)
    n_work = N_EXPERTS * n_ff
    e = s // n_ff
    ff = s % n_ff
    group = SUBLANES
    part_tok = cap // n_ff

    def list_base(expert):
        return (b * N_EXPERTS + expert) * cap

    def gather(base, j0, n):
        for u in range(n):
            tkn = idx_ref[base + j0 + u]
            rowg_ref[pl.ds(pl.multiple_of((j0 + u) * ACC_ROWS, ACC_ROWS), ACC_ROWS), :] = (
                hr_ref[0, pl.ds(pl.multiple_of(tkn * ACC_ROWS, ACC_ROWS), ACC_ROWS), :])
            affg_ref[pl.ds(j0 + u, 1), :] = ak_ref[0, pl.ds(tkn, 1), :]

    def scatter(base, j0, n):
        for g0 in range(0, n, group):
            dst = []
            val = []
            for u in range(g0, g0 + group):
                r = pl.multiple_of(idx_ref[base + j0 + u] * ACC_ROWS, ACC_ROWS)
                dst.append(r)
                val.append(acc_ref[pl.ds(r, ACC_ROWS), :]
                           + rows_ref[pl.ds(pl.multiple_of((j0 + u) * ACC_ROWS, ACC_ROWS), ACC_ROWS), :])
            for r, v in zip(dst, val):
                acc_ref[pl.ds(r, ACC_ROWS), :] = v

    def in_groups(fn, base):
        def step(jo, carry):
            fn(base, jo * group, group)
            return carry
        lax.fori_loop(0, cap // group, step, 0)

    @pl.when(s == 0)
    def _():
        acc_ref[...] = jnp.zeros_like(acc_ref)
        rows_ref[...] = jnp.zeros_like(rows_ref)
        in_groups(gather, list_base(0))

    @pl.when((s < n_work) & (ff == 0))
    def _():
        for c in range(ACC_ROWS):
            xs_ref[:, c * LANES:(c + 1) * LANES] = rowg_ref[pl.ds(c, cap, stride=ACC_ROWS), :].astype(BF16)
        lane = lax.broadcasted_iota(I32, affg_ref.shape, 1)
        gate_ref[...] = jnp.sum(jnp.where(lane == e, affg_ref[...], 0.0), axis=1, keepdims=True)

    @pl.when(s < n_work)
    def _():
        gather(list_base(jnp.minimum(e + 1, N_EXPERTS - 1)), ff * part_tok, part_tok)
        scatter(list_base(jnp.maximum(e - 1, 0)), ff * part_tok, part_tok)
        xs = xs_ref[...]
        hid = _silu(_dot(xs, wg_ref[0])) * _dot(xs, wu_ref[0])
        part = _dot(hid.astype(BF16), wd_ref[0])

        @pl.when(ff == 0)
        def _():
            ys_ref[...] = part

        @pl.when(ff != 0)
        def _():
            ys_ref[...] += part

    @pl.when((s < n_work) & (ff == n_ff - 1))
    def _():
        gate = gate_ref[...]
        for c in range(ACC_ROWS):
            rows_ref[pl.ds(c, cap, stride=ACC_ROWS), :] = ys_ref[:, c * LANES:(c + 1) * LANES] * gate

    @pl.when(s == n_work)
    def _():
        in_groups(scatter, list_base(N_EXPERTS - 1))

    @pl.when(s >= n_work)
    def _():
        row0 = (s - n_work) * (EPI_TOK * ACC_ROWS)
        moe = jnp.concatenate([acc_ref[pl.ds(row0 + c, EPI_TOK, stride=ACC_ROWS), :] for c in range(ACC_ROWS)], axis=1)
        g2 = mods_ref[pl.ds(b, 1), 5 * D_MODEL:6 * D_MODEL]
        out_ref[0] = _rms(x1_ref[0] + g2 * moe, nfin_ref[...])


def _moe(idx_flat, hr, ak, wg, wu, wd, x1, mods, nfin, cap):
    nb, t, _ = x1.shape
    dff = wg.shape[2]
    n_ff = dff // DFF_TILE
    n_work = N_EXPERTS * n_ff
    n_epi = t // EPI_TOK

    def w_idx(s):
        sc = jnp.minimum(s, n_work - 1)
        return sc // n_ff, sc % n_ff

    def epi(b, s, idx):
        return (b, jnp.maximum(s - n_work, 0), 0)

    one = pl.Buffered(1)
    grid_spec = pltpu.PrefetchScalarGridSpec(
        num_scalar_prefetch=1,
        grid=(nb, n_work + n_epi),
        in_specs=[pl.BlockSpec((1, t * ACC_ROWS, LANES), lambda b, s, idx: (b, 0, 0), pipeline_mode=one),
                  pl.BlockSpec((1, t, LANES), lambda b, s, idx: (b, 0, 0), pipeline_mode=one),
                  pl.BlockSpec((1, D_MODEL, DFF_TILE), lambda b, s, idx: (w_idx(s)[0], 0, w_idx(s)[1])),
                  pl.BlockSpec((1, D_MODEL, DFF_TILE), lambda b, s, idx: (w_idx(s)[0], 0, w_idx(s)[1])),
                  pl.BlockSpec((1, DFF_TILE, D_MODEL), lambda b, s, idx: (w_idx(s)[0], w_idx(s)[1], 0)),
                  pl.BlockSpec((1, EPI_TOK, D_MODEL), epi),
                  pl.BlockSpec(mods.shape, lambda b, s, idx: (0, 0)),
                  pl.BlockSpec((1, D_MODEL), lambda b, s, idx: (0, 0))],
        out_specs=pl.BlockSpec((1, EPI_TOK, D_MODEL), epi),
        scratch_shapes=[pltpu.VMEM((t * ACC_ROWS, LANES), F32),
                        pltpu.VMEM((cap * ACC_ROWS, LANES), F32),
                        pltpu.VMEM((cap * ACC_ROWS, LANES), F32),
                        pltpu.VMEM((cap, LANES), F32),
                        pltpu.VMEM((cap, 1), F32),
                        pltpu.VMEM((cap, D_MODEL), BF16),
                        pltpu.VMEM((cap, D_MODEL), F32)])
    return pl.pallas_call(
        functools.partial(_moe_body, cap=cap, n_ff=n_ff),
        grid_spec=grid_spec,
        out_shape=jax.ShapeDtypeStruct((nb, t, D_MODEL), F32),
        compiler_params=pltpu.CompilerParams(dimension_semantics=("arbitrary", "arbitrary"),
                                             vmem_limit_bytes=VMEM_LIMIT),
        name="moe",
    )(idx_flat, hr, ak, wg, wu, wd, x1, mods, nfin)


def kernel(x, c, ctx, c_ctx, w_ada, b_ada, norm_mix, norm_ffn, w_in, lb_logits, hg_norm, conv_w,
           w_out, w_router, w_gate, w_up, w_down, norm_final):
    nb, t, d = x.shape
    assert d == D_MODEL and w_ada.shape[0] == 1 and nb < MOD_ROWS and t % TT == 0 and TT % GRID_W == 0
    assert w_in.shape[2] == 3 * HG_KEYS + 2 * HG_WIDTH + 3 * SC_WIDTH and w_router.shape[2] == N_EXPERTS
    cap = EC_CAPACITY * t // N_EXPERTS
    assert cap % LANES == 0 and w_gate.shape[3] % DFF_TILE == 0

    cc = jnp.concatenate([c, c_ctx[None, :], jnp.zeros((MOD_ROWS - nb - 1, d), F32)], axis=0)
    w_in_bf = w_in[0].astype(BF16)
    w_out_bf = w_out[0].astype(BF16)
    wg, wu, wd = w_gate[0].astype(BF16), w_up[0].astype(BF16), w_down[0].astype(BF16)
    wr_pad = jnp.pad(w_router[0], ((0, 0), (0, LANES - N_EXPERTS)))
    nm, nf, nfin = norm_mix[0][None, :], norm_ffn[0][None, :], norm_final[None, :]
    hgn = hg_norm[0][None, :]

    mods = _adaln(cc, w_ada[0], b_ada[0][None, :])
    s0_f, s0_b = _ctx_states(ctx, mods, nm, w_in_bf, lb_logits, ctx_row=nb)
    q, zb, vi, g, y, of = _mix_fwd(x, mods, nm, w_in_bf, lb_logits, conv_w[0], s0_f)
    x1, hr, aff_t, aff_k = _mix_bwd(x, q, zb, vi, g, y, of, mods, lb_logits, hgn, w_out_bf, nf, wr_pad, s0_b)
    idx = _route(aff_t, cap)
    return _moe(idx.reshape(-1), hr, aff_k, wg, wu, wd, x1, mods, nfin, cap)
```

```python
import functools

import jax
import jax.numpy as jnp
from jax import lax
from jax.experimental import pallas as pl
from jax.experimental.pallas import tpu as pltpu

F32 = jnp.float32
BF16 = jnp.bfloat16
I32 = jnp.int32

D_MODEL = 1024
N_MOD = 6
HG_HEADS = 4
HG_KDIM = 128
HG_VDIM = 128
HG_KEYS = HG_HEADS * HG_KDIM
HG_WIDTH = HG_HEADS * HG_VDIM
SC_WIDTH = D_MODEL - HG_WIDTH
GRID_W = 64
CHUNK = 32
N_EXPERTS = 16
EC_CAPACITY = 2
EPS = 1e-6

LANES = 128
SUBLANES = 8
VMEM_LIMIT = 60 * 1024 * 1024

TT = 256
NCH = TT // CHUNK
MOD_ROWS = 16
DFF_TILE = 512
EPI_TOK = 256
ACC_ROWS = D_MODEL // LANES


def _dot(a, b):
    return jnp.dot(a, b, preferred_element_type=F32)


def _dot_nt(a, b):
    return lax.dot_general(a, b, (((1,), (1,)), ((), ())), preferred_element_type=F32)


def _dot_tn(a, b):
    return lax.dot_general(a, b, (((0,), (0,)), ((), ())), preferred_element_type=F32)


def _split(x):
    hi = x.astype(BF16)
    lo = (x - hi.astype(F32)).astype(BF16)
    return hi, lo


def _rms(x, gain):
    return x * lax.rsqrt(jnp.mean(x * x, axis=-1, keepdims=True) + EPS) * gain


def _silu(x):
    return x * jax.nn.sigmoid(x)


def _lower_bounds(lbl_ref):
    rows = [lbl_ref[l] for l in range(lbl_ref.shape[0])]
    m = functools.reduce(jnp.maximum, rows)
    es = [jnp.exp(r - m) for r in rows]
    sm0 = es[0] / functools.reduce(lambda a, b: a + b, es)
    return sm0[0:1, :], sm0[1:2, :]


def _ada_body(c_ref, w_ref, b_ref, o_ref):
    s_hi, s_lo = _split(_silu(c_ref[...]))
    w_hi, w_lo = _split(w_ref[...])
    o_ref[...] = _dot(s_hi, w_hi) + _dot(s_hi, w_lo) + _dot(s_lo, w_hi) + b_ref[...]


def _adaln(cc, w, b):
    n = w.shape[1]
    tn = n // 4
    return pl.pallas_call(
        _ada_body,
        grid=(n // tn,),
        in_specs=[pl.BlockSpec((MOD_ROWS, D_MODEL), lambda i: (0, 0)),
                  pl.BlockSpec((D_MODEL, tn), lambda i: (0, i)),
                  pl.BlockSpec((1, tn), lambda i: (0, i))],
        out_specs=pl.BlockSpec((MOD_ROWS, tn), lambda i: (0, i)),
        out_shape=jax.ShapeDtypeStruct((MOD_ROWS, n), F32),
        compiler_params=pltpu.CompilerParams(vmem_limit_bytes=VMEM_LIMIT),
        name="adaln",
    )(cc, w, b)


def _ctx_body(ctx_ref, mods_ref, nm_ref, wf_ref, wb_ref, wi_ref, lbl_ref, sf_ref, sb_ref, *, ctx_row):
    x = ctx_ref[0]
    n = x.shape[0]
    csh = mods_ref[ctx_row:ctx_row + 1, 0:D_MODEL]
    csc = mods_ref[ctx_row:ctx_row + 1, D_MODEL:2 * D_MODEL]
    hb = (_rms(x, nm_ref[...]) * (1.0 + csc) + csh).astype(BF16)
    zf = _dot(hb, wf_ref[...])
    zb = _dot(hb, wb_ref[...])
    vi = _dot(hb, wi_ref[...]).astype(BF16)
    lb_f, lb_b = _lower_bounds(lbl_ref)
    f_f = lb_f + (1.0 - lb_f) * jax.nn.sigmoid(zf)
    f_b = lb_b + (1.0 - lb_b) * jax.nn.sigmoid(zb)
    ii = lax.broadcasted_iota(I32, (n, n), 0)
    jj = lax.broadcasted_iota(I32, (n, n), 1)
    after = jnp.where(jj > ii, 1.0, 0.0).astype(BF16)
    before = jnp.where(jj < ii, 1.0, 0.0).astype(BF16)
    hi, lo = _split(jnp.log(f_f))
    kd_f = ((1.0 - f_f) * jnp.exp(_dot(after, hi) + _dot(after, lo))).astype(BF16)
    hi, lo = _split(jnp.log(f_b))
    kd_b = ((1.0 - f_b) * jnp.exp(_dot(before, hi) + _dot(before, lo))).astype(BF16)
    for h in range(HG_HEADS):
        ks = slice(h * HG_KDIM, (h + 1) * HG_KDIM)
        vs = slice(h * HG_VDIM, (h + 1) * HG_VDIM)
        sf_ref[0, h] = _dot_tn(vi[:, vs], kd_f[:, ks])
        sb_ref[0, h] = _dot_tn(vi[:, vs], kd_b[:, ks])


def _ctx_states(ctx, mods, nm, w_in_bf, lbl, ctx_row):
    nb, n, _ = ctx.shape
    st = jax.ShapeDtypeStruct((nb, HG_HEADS, HG_VDIM, HG_KDIM), F32)
    st_spec = pl.BlockSpec((1, HG_HEADS, HG_VDIM, HG_KDIM), lambda b: (b, 0, 0, 0))
    wcol = lambda k: pl.BlockSpec((D_MODEL, HG_KEYS), lambda b, k=k: (0, k))
    return pl.pallas_call(
        functools.partial(_ctx_body, ctx_row=ctx_row),
        grid=(nb,),
        in_specs=[pl.BlockSpec((1, n, D_MODEL), lambda b: (b, 0, 0)),
                  pl.BlockSpec(mods.shape, lambda b: (0, 0)),
                  pl.BlockSpec((1, D_MODEL), lambda b: (0, 0)),
                  wcol(1), wcol(2), wcol(3),
                  pl.BlockSpec(lbl.shape, lambda b: (0, 0, 0))],
        out_specs=[st_spec, st_spec],
        out_shape=[st, st],
        compiler_params=pltpu.CompilerParams(vmem_limit_bytes=VMEM_LIMIT),
        name="ctx_states",
    )(ctx, mods, nm, w_in_bf, w_in_bf, w_in_bf, lbl)


def _level_map(reverse):
    ii = lax.broadcasted_iota(I32, (TT, TT), 0)
    jj = lax.broadcasted_iota(I32, (TT, TT), 1)
    if reverse:
        ii = TT - 1 - ii
        jj = TT - 1 - jj
    sh0 = CHUNK.bit_length() - 1
    ci = ii >> sh0
    cj = jj >> sh0
    lm = jnp.where((ci == cj) & (jj <= ii), 1, 0)
    for lvl in range(1, NCH.bit_length()):
        m = ((ci >> lvl) == (cj >> lvl)) & (((ci >> (lvl - 1)) & 1) == 1) & (((cj >> (lvl - 1)) & 1) == 0)
        lm = jnp.where(m, lvl + 1, lm)
    return lm


def _gla_tile(q, zf, v, lb, st_ref, lm, reverse):
    n_lvl = NCH.bit_length() - 1
    f = lb + (1.0 - lb) * jax.nn.sigmoid(zf)
    k = 1.0 - f
    tri = jnp.where(lm == 1, 1.0, 0.0).astype(BF16)
    hi, lo = _split(jnp.log(f))
    cum = _dot(tri, hi) + _dot(tri, lo)
    phys = list(range(NCH - 1, -1, -1)) if reverse else list(range(NCH))
    last = 0 if reverse else CHUNK - 1
    tot = [cum[a * CHUNK + last:a * CHUNK + last + 1, :] for a in range(NCH)]
    bnd = [jnp.zeros_like(tot[0])]
    for c in range(NCH):
        bnd.append(bnd[-1] + tot[phys[c]])
    q0 = q * jnp.exp(cum)
    kinv = (k * jnp.exp(-cum)).astype(BF16)
    qs = [[None] * NCH for _ in range(n_lvl + 1)]
    ks = [[None] * NCH for _ in range(n_lvl + 1)]
    for c in range(NCH):
        a = phys[c]
        rows = slice(a * CHUNK, (a + 1) * CHUNK)
        q0a = q0[rows]
        kend = k[rows] * jnp.exp(tot[a] - cum[rows])
        zero = jnp.zeros_like(q0a)
        for lvl in range(1, n_lvl + 1):
            s = 1 << (lvl - 1)
            mid = (c // (2 * s)) * 2 * s + s
            right = (c // s) % 2 == 1
            qs[lvl - 1][a] = q0a * jnp.exp(bnd[c] - bnd[mid]) if right else zero
            ks[lvl - 1][a] = zero if right else kend * jnp.exp(bnd[mid] - bnd[c + 1])
        qs[n_lvl][a] = q0a * jnp.exp(bnd[c])
        ks[n_lvl][a] = kend * jnp.exp(bnd[NCH] - bnd[c + 1])
    qs = [jnp.concatenate(p, axis=0).astype(BF16) for p in qs]
    ks = [jnp.concatenate(p, axis=0).astype(BF16) for p in ks]
    q0 = q0.astype(BF16)
    vb = v.astype(BF16)
    dec = jnp.exp(bnd[NCH])
    outs = []
    for h in range(HG_HEADS):
        kk = slice(h * HG_KDIM, (h + 1) * HG_KDIM)
        vv = slice(h * HG_VDIM, (h + 1) * HG_VDIM)
        sc = jnp.where(lm == 1, _dot_nt(q0[:, kk], kinv[:, kk]), 0.0)
        for lvl in range(n_lvl):
            sc = jnp.where(lm == lvl + 2, _dot_nt(qs[lvl][:, kk], ks[lvl][:, kk]), sc)
        st = st_ref[h]
        outs.append(_dot(sc.astype(BF16), vb[:, vv]) + _dot_nt(qs[n_lvl][:, kk], st.astype(BF16)))
        st_ref[h] = st * dec[:, kk] + _dot_tn(vb[:, vv], ks[n_lvl][:, kk])
    return jnp.concatenate(outs, axis=1)


def _mix_fwd_body(x_ref, mods_ref, nm_ref, w_ref, lbl_ref, cw_ref, s0_ref,
                  q_ref, zb_ref, i_ref, g_ref, y_ref, of_ref, st_ref, lm_ref):
    b = pl.program_id(0)
    j = pl.program_id(1)

    @pl.when((b == 0) & (j == 0))
    def _():
        lm_ref[...] = _level_map(False)

    @pl.when(j == 0)
    def _():
        st_ref[...] = s0_ref[0]

    sh = mods_ref[pl.ds(b, 1), 0:D_MODEL]
    sc = mods_ref[pl.ds(b, 1), D_MODEL:2 * D_MODEL]
    hx = (_rms(x_ref[0], nm_ref[...]) * (1.0 + sc) + sh).astype(BF16)
    w = HG_KEYS
    proj = lambda k: _dot(hx, w_ref[:, k * w:(k + 1) * w])
    q = proj(0)
    zff = proj(1)
    zb_ref[0] = proj(2)
    vi = proj(3)
    g_ref[0] = proj(4).astype(BF16)
    gate_b = proj(5)
    u = proj(6) * proj(7)
    col = lax.broadcasted_iota(I32, (TT, 1), 0) & (GRID_W - 1)
    up = jnp.where(col != 0, pltpu.roll(u, 1, axis=0), 0.0)
    un = jnp.where(col != GRID_W - 1, pltpu.roll(u, TT - 1, axis=0), 0.0)
    cw = cw_ref[...]
    y_ref[0] = (gate_b * (cw[0:1] * up + cw[1:2] * u + cw[2:3] * un)).astype(BF16)
    lb_f, _ = _lower_bounds(lbl_ref)
    of_ref[0] = _gla_tile(q, zff, vi, lb_f, st_ref, lm_ref[...], False)
    q_ref[0] = q.astype(BF16)
    i_ref[0] = vi.astype(BF16)


def _mix_fwd(x, mods, nm, w_in_bf, lbl, cw, s0_f):
    nb, t, _ = x.shape
    nt = t // TT
    tok = lambda dt: jax.ShapeDtypeStruct((nb, t, HG_WIDTH), dt)
    tspec = pl.BlockSpec((1, TT, HG_WIDTH), lambda b, j: (b, j, 0))
    return pl.pallas_call(
        _mix_fwd_body,
        grid=(nb, nt),
        in_specs=[pl.BlockSpec((1, TT, D_MODEL), lambda b, j: (b, j, 0)),
                  pl.BlockSpec(mods.shape, lambda b, j: (0, 0)),
                  pl.BlockSpec((1, D_MODEL), lambda b, j: (0, 0)),
                  pl.BlockSpec(w_in_bf.shape, lambda b, j: (0, 0)),
                  pl.BlockSpec(lbl.shape, lambda b, j: (0, 0, 0)),
                  pl.BlockSpec(cw.shape, lambda b, j: (0, 0)),
                  pl.BlockSpec((1, HG_HEADS, HG_VDIM, HG_KDIM), lambda b, j: (b, 0, 0, 0))],
        out_specs=[tspec] * 6,
        out_shape=[tok(BF16), tok(F32), tok(BF16), tok(BF16), tok(BF16), tok(F32)],
        scratch_shapes=[pltpu.VMEM((HG_HEADS, HG_VDIM, HG_KDIM), F32), pltpu.VMEM((TT, TT), I32)],
        compiler_params=pltpu.CompilerParams(dimension_semantics=("arbitrary", "arbitrary"),
                                             vmem_limit_bytes=VMEM_LIMIT),
        name="mix_fwd",
    )(x, mods, nm, w_in_bf, lbl, cw, s0_f)


def _mix_bwd_body(x_ref, q_ref, zb_ref, i_ref, g_ref, y_ref, of_ref, mods_ref, lbl_ref, hgn_ref,
                  wo_ref, nf_ref, wr_ref, s0_ref,
                  x1_ref, hr_ref, at_ref, ak_ref, st_ref, lm_ref):
    b = pl.program_id(0)
    j = pl.program_id(1)

    @pl.when((b == 0) & (j == 0))
    def _():
        lm_ref[...] = _level_map(True)

    @pl.when(j == 0)
    def _():
        st_ref[...] = s0_ref[0]

    _, lb_b = _lower_bounds(lbl_ref)
    o_b = _gla_tile(q_ref[0].astype(F32), zb_ref[0], i_ref[0].astype(F32), lb_b, st_ref, lm_ref[...], True)
    o = of_ref[0] + o_b
    hgn = hgn_ref[...]
    heads = [_rms(o[:, h * HG_VDIM:(h + 1) * HG_VDIM], hgn) for h in range(HG_HEADS)]
    g = g_ref[0].astype(F32)
    ohg = jnp.concatenate(heads, axis=1) * _silu(g)
    cat = jnp.concatenate([ohg.astype(BF16), y_ref[0]], axis=1)
    mx = _dot(cat, wo_ref[...])
    mod = lambda k: mods_ref[pl.ds(b, 1), k * D_MODEL:(k + 1) * D_MODEL]
    x1 = x_ref[0] + mod(2) * mx
    x1_ref[0] = x1
    h2 = _rms(x1, nf_ref[...]) * (1.0 + mod(4)) + mod(3)
    for c in range(ACC_ROWS):
        hr_ref[0, pl.ds(c, TT, stride=ACC_ROWS), :] = h2[:, c * LANES:(c + 1) * LANES]
    hh, hl = _split(h2)
    wh, wl = _split(wr_ref[...])
    lg = _dot(hh, wh) + _dot(hh, wl) + _dot(hl, wh)
    lane = lax.broadcasted_iota(I32, lg.shape, 1)
    lg = jnp.where(lane < N_EXPERTS, lg, -jnp.inf)
    e = jnp.exp(lg - jnp.max(lg, axis=-1, keepdims=True))
    aff = e / jnp.sum(e, axis=-1, keepdims=True)
    ak_ref[0] = aff
    at_ref[0] = aff.T[0:N_EXPERTS, :]


def _mix_bwd(x, q, zb, vi, g, y, of, mods, lbl, hgn, w_out_bf, nf, wr_pad, s0_b):
    nb, t, _ = x.shape
    nt = t // TT
    rev = lambda b, j: (b, nt - 1 - j, 0)
    tspec = pl.BlockSpec((1, TT, HG_WIDTH), rev)
    xspec = pl.BlockSpec((1, TT, D_MODEL), rev)
    full2 = lambda a: pl.BlockSpec(a.shape, lambda b, j: (0, 0))
    return pl.pallas_call(
        _mix_bwd_body,
        grid=(nb, nt),
        in_specs=[xspec, tspec, tspec, tspec, tspec, tspec, tspec,
                  full2(mods), pl.BlockSpec(lbl.shape, lambda b, j: (0, 0, 0)), full2(hgn),
                  full2(w_out_bf), full2(nf), full2(wr_pad),
                  pl.BlockSpec((1, HG_HEADS, HG_VDIM, HG_KDIM), lambda b, j: (b, 0, 0, 0))],
        out_specs=[xspec,
                   pl.BlockSpec((1, TT * ACC_ROWS, LANES), rev),
                   pl.BlockSpec((1, N_EXPERTS, TT), lambda b, j: (b, 0, nt - 1 - j)),
                   pl.BlockSpec((1, TT, LANES), rev)],
        out_shape=[jax.ShapeDtypeStruct((nb, t, D_MODEL), F32),
                   jax.ShapeDtypeStruct((nb, t * ACC_ROWS, LANES), F32),
                   jax.ShapeDtypeStruct((nb, N_EXPERTS, t), F32),
                   jax.ShapeDtypeStruct((nb, t, LANES), F32)],
        scratch_shapes=[pltpu.VMEM((HG_HEADS, HG_VDIM, HG_KDIM), F32), pltpu.VMEM((TT, TT), I32)],
        compiler_params=pltpu.CompilerParams(dimension_semantics=("arbitrary", "arbitrary"),
                                             vmem_limit_bytes=VMEM_LIMIT),
        name="mix_bwd",
    )(x, q, zb, vi, g, y, of, mods, lbl, hgn, w_out_bf, nf, wr_pad, s0_b)


def _route_body(at_ref, idx_ref, inc_ref, col_ref, *, cap):
    ne, t = at_ref.shape[1], at_ref.shape[2]
    blk = 2 * LANES
    aff = at_ref[0]

    def count(m):
        return jnp.sum(jnp.where(m, 1.0, 0.0), axis=1, keepdims=True)

    def bit_step(i, thr):
        cand = thr | jnp.left_shift(jnp.int32(1), 30 - i)
        return jnp.where(count(aff >= lax.bitcast_convert_type(cand, F32)) >= cap, cand, thr)

    thr = lax.fori_loop(0, 31, bit_step, jnp.zeros((ne, 1), I32))
    gt = aff >= lax.bitcast_convert_type(thr + 1, F32)
    eq = (aff >= lax.bitcast_convert_type(thr, F32)) & jnp.logical_not(gt)
    need = cap - count(gt)

    ii = lax.broadcasted_iota(I32, (blk, blk), 0)
    jj = lax.broadcasted_iota(I32, (blk, blk), 1)
    upper = jnp.where(ii <= jj, 1.0, 0.0).astype(BF16)

    def prefix_blocks(m):
        mb = jnp.where(m, 1.0, 0.0).astype(BF16)
        carry = jnp.zeros((ne, 1), F32)
        out = []
        for kb in range(t // blk):
            p = _dot(mb[:, kb * blk:(kb + 1) * blk], upper) + carry
            out.append(p)
            carry = p[:, blk - 1:blk]
        return out

    eq_rank = jnp.concatenate(prefix_blocks(eq), axis=1)
    sel = gt | (eq & (eq_rank <= need))
    for kb, p in enumerate(prefix_blocks(sel)):
        for half in range(blk // LANES):
            inc_ref[kb * (blk // LANES) + half] = p[:, half * LANES:(half + 1) * LANES]

    n_lb = t // LANES
    lane = lax.broadcasted_iota(I32, (LANES, LANES), 1)
    col_ref[...] = jnp.zeros_like(col_ref)

    def expert_step(e, carry):
        for jg in range(cap // LANES):
            slot = (lax.broadcasted_iota(I32, (LANES, LANES), 0) + jg * LANES).astype(F32)

            def lane_block(lb, acc):
                r = inc_ref[lb, pl.ds(e, 1), :]
                return acc + jnp.where(r <= slot, 1.0, 0.0)

            acc = lax.fori_loop(0, n_lb, lane_block, jnp.zeros((LANES, LANES), F32), unroll=4)
            cnt = jnp.sum(acc, axis=1, keepdims=True)
            rows = slice(jg * LANES, (jg + 1) * LANES)
            col_ref[rows, :] = jnp.where(lane == e, cnt, col_ref[rows, :])
        return carry

    lax.fori_loop(0, ne, expert_step, 0)
    idx_ref[0] = col_ref[...].T[0:ne, :].astype(I32)


def _route(aff_t, cap):
    nb, ne, t = aff_t.shape
    return pl.pallas_call(
        functools.partial(_route_body, cap=cap),
        grid=(nb,),
        in_specs=[pl.BlockSpec((1, ne, t), lambda b: (b, 0, 0))],
        out_specs=pl.BlockSpec((1, ne, cap), lambda b: (b, 0, 0)),
        out_shape=jax.ShapeDtypeStruct((nb, ne, cap), I32),
        scratch_shapes=[pltpu.VMEM((t // LANES, ne, LANES), F32), pltpu.VMEM((cap, LANES), F32)],
        compiler_params=pltpu.CompilerParams(vmem_limit_bytes=VMEM_LIMIT),
        name="route",
    )(aff_t)


def _moe_body(idx_ref, hr_ref, ak_ref, wg_ref, wu_ref, wd_ref, x1_ref, mods_ref, nfin_ref,
              out_ref, acc_ref, rowg_ref, rows_ref, affg_ref, gate_ref, xs_ref, ys_ref, *, cap, n_ff):
    b = pl.program_id(0)
    s = pl.program_id(1)
    n_work = N_EXPERTS * n_ff
    e = s // n_ff
    ff = s % n_ff
    group = SUBLANES
    part_tok = cap // n_ff

    def list_base(expert):
        return (b * N_EXPERTS + expert) * cap

    def gather(base, j0, n):
        for u in range(n):
            tkn = idx_ref[base + j0 + u]
            rowg_ref[pl.ds(pl.multiple_of((j0 + u) * ACC_ROWS, ACC_ROWS), ACC_ROWS), :] = (
                hr_ref[0, pl.ds(pl.multiple_of(tkn * ACC_ROWS, ACC_ROWS), ACC_ROWS), :])
            affg_ref[pl.ds(j0 + u, 1), :] = ak_ref[0, pl.ds(tkn, 1), :]

    def scatter(base, j0, n):
        for g0 in range(0, n, group):
            dst = []
            val = []
            for u in range(g0, g0 + group):
                r = pl.multiple_of(idx_ref[base + j0 + u] * ACC_ROWS, ACC_ROWS)
                dst.append(r)
                val.append(acc_ref[pl.ds(r, ACC_ROWS), :]
                           + rows_ref[pl.ds(pl.multiple_of((j0 + u) * ACC_ROWS, ACC_ROWS), ACC_ROWS), :])
            for r, v in zip(dst, val):
                acc_ref[pl.ds(r, ACC_ROWS), :] = v

    def in_groups(fn, base):
        def step(jo, carry):
            fn(base, jo * group, group)
            return carry
        lax.fori_loop(0, cap // group, step, 0)

    @pl.when(s == 0)
    def _():
        acc_ref[...] = jnp.zeros_like(acc_ref)
        rows_ref[...] = jnp.zeros_like(rows_ref)
        ys_ref[...] = jnp.zeros_like(ys_ref)
        in_groups(gather, list_base(0))

    @pl.when((s < n_work) & (ff == 0))
    def _():
        for c in range(ACC_ROWS):
            xs_ref[:, c * LANES:(c + 1) * LANES] = rowg_ref[pl.ds(c, cap, stride=ACC_ROWS), :].astype(BF16)
        lane = lax.broadcasted_iota(I32, affg_ref.shape, 1)
        gate_ref[...] = jnp.sum(jnp.where(lane == e, affg_ref[...], 0.0), axis=1, keepdims=True)

    @pl.when(s < n_work)
    def _():
        gather(list_base(jnp.minimum(e + 1, N_EXPERTS - 1)), ff * part_tok, part_tok)
        scatter(list_base(jnp.maximum(e - 1, 0)), ff * part_tok, part_tok)
        xs = xs_ref[...]
        hid = _silu(_dot(xs, wg_ref[0])) * _dot(xs, wu_ref[0])
        ys_ref[...] = _dot(hid.astype(BF16), wd_ref[0]) + jnp.where(ff == 0, 0.0, ys_ref[...])

    @pl.when((s < n_work) & (ff == n_ff - 1))
    def _():
        gate = gate_ref[...]
        for c in range(ACC_ROWS):
            rows_ref[pl.ds(c, cap, stride=ACC_ROWS), :] = ys_ref[:, c * LANES:(c + 1) * LANES] * gate

    @pl.when(s == n_work)
    def _():
        in_groups(scatter, list_base(N_EXPERTS - 1))

    @pl.when(s >= n_work)
    def _():
        row0 = (s - n_work) * (EPI_TOK * ACC_ROWS)
        moe = jnp.concatenate([acc_ref[pl.ds(row0 + c, EPI_TOK, stride=ACC_ROWS), :] for c in range(ACC_ROWS)], axis=1)
        g2 = mods_ref[pl.ds(b, 1), 5 * D_MODEL:6 * D_MODEL]
        out_ref[0] = _rms(x1_ref[0] + g2 * moe, nfin_ref[...])


def _moe(idx_flat, hr, ak, wg, wu, wd, x1, mods, nfin, cap):
    nb, t, _ = x1.shape
    dff = wg.shape[2]
    n_ff = dff // DFF_TILE
    n_work = N_EXPERTS * n_ff
    n_epi = t // EPI_TOK

    def w_idx(s):
        sc = jnp.minimum(s, n_work - 1)
        return sc // n_ff, sc % n_ff

    def epi(b, s, idx):
        return (b, jnp.maximum(s - n_work, 0), 0)

    one = pl.Buffered(1)
    grid_spec = pltpu.PrefetchScalarGridSpec(
        num_scalar_prefetch=1,
        grid=(nb, n_work + n_epi),
        in_specs=[pl.BlockSpec((1, t * ACC_ROWS, LANES), lambda b, s, idx: (b, 0, 0), pipeline_mode=one),
                  pl.BlockSpec((1, t, LANES), lambda b, s, idx: (b, 0, 0), pipeline_mode=one),
                  pl.BlockSpec((1, D_MODEL, DFF_TILE), lambda b, s, idx: (w_idx(s)[0], 0, w_idx(s)[1])),
                  pl.BlockSpec((1, D_MODEL, DFF_TILE), lambda b, s, idx: (w_idx(s)[0], 0, w_idx(s)[1])),
                  pl.BlockSpec((1, DFF_TILE, D_MODEL), lambda b, s, idx: (w_idx(s)[0], w_idx(s)[1], 0)),
                  pl.BlockSpec((1, EPI_TOK, D_MODEL), epi),
                  pl.BlockSpec(mods.shape, lambda b, s, idx: (0, 0)),
                  pl.BlockSpec((1, D_MODEL), lambda b, s, idx: (0, 0))],
        out_specs=pl.BlockSpec((1, EPI_TOK, D_MODEL), epi),
        scratch_shapes=[pltpu.VMEM((t * ACC_ROWS, LANES), F32),
                        pltpu.VMEM((cap * ACC_ROWS, LANES), F32),
                        pltpu.VMEM((cap * ACC_ROWS, LANES), F32),
                        pltpu.VMEM((cap, LANES), F32),
                        pltpu.VMEM((cap, 1), F32),
                        pltpu.VMEM((cap, D_MODEL), BF16),
                        pltpu.VMEM((cap, D_MODEL), F32)])
    return pl.pallas_call(
        functools.partial(_moe_body, cap=cap, n_ff=n_ff),
        grid_spec=grid_spec,
        out_shape=jax.ShapeDtypeStruct((nb, t, D_MODEL), F32),
        compiler_params=pltpu.CompilerParams(dimension_semantics=("arbitrary", "arbitrary"),
                                             vmem_limit_bytes=VMEM_LIMIT),
        name="moe",
    )(idx_flat, hr, ak, wg, wu, wd, x1, mods, nfin)


def kernel(x, c, ctx, c_ctx, w_ada, b_ada, norm_mix, norm_ffn, w_in, lb_logits, hg_norm, conv_w,
           w_out, w_router, w_gate, w_up, w_down, norm_final):
    nb, t, d = x.shape
    assert d == D_MODEL and w_ada.shape[0] == 1 and nb < MOD_ROWS and t % TT == 0 and TT % GRID_W == 0
    assert w_in.shape[2] == 3 * HG_KEYS + 2 * HG_WIDTH + 3 * SC_WIDTH and w_router.shape[2] == N_EXPERTS
    cap = EC_CAPACITY * t // N_EXPERTS
    assert cap % LANES == 0 and w_gate.shape[3] % DFF_TILE == 0

    cc = jnp.concatenate([c, c_ctx[None, :], jnp.zeros((MOD_ROWS - nb - 1, d), F32)], axis=0)
    w_in_bf = w_in[0].astype(BF16)
    w_out_bf = w_out[0].astype(BF16)
    wg, wu, wd = w_gate[0].astype(BF16), w_up[0].astype(BF16), w_down[0].astype(BF16)
    wr_pad = jnp.pad(w_router[0], ((0, 0), (0, LANES - N_EXPERTS)))
    nm, nf, nfin = norm_mix[0][None, :], norm_ffn[0][None, :], norm_final[None, :]
    hgn = hg_norm[0][None, :]

    mods = _adaln(cc, w_ada[0], b_ada[0][None, :])
    s0_f, s0_b = _ctx_states(ctx, mods, nm, w_in_bf, lb_logits, ctx_row=nb)
    q, zb, vi, g, y, of = _mix_fwd(x, mods, nm, w_in_bf, lb_logits, conv_w[0], s0_f)
    x1, hr, aff_t, aff_k = _mix_bwd(x, q, zb, vi, g, y, of, mods, lb_logits, hgn, w_out_bf, nf, wr_pad, s0_b)
    idx = _route(aff_t, cap)
    return _moe(idx.reshape(-1), hr, aff_k, wg, wu, wd, x1, mods, nfin, cap)
```

```python
import functools

import jax
import jax.numpy as jnp
from jax import lax
from jax.experimental import pallas as pl
from jax.experimental.pallas import tpu as pltpu

F32 = jnp.float32
BF16 = jnp.bfloat16
I32 = jnp.int32

D_MODEL = 1024
N_MOD = 6
HG_HEADS = 4
HG_KDIM = 128
HG_VDIM = 128
HG_KEYS = HG_HEADS * HG_KDIM
HG_WIDTH = HG_HEADS * HG_VDIM
SC_WIDTH = D_MODEL - HG_WIDTH
GRID_W = 64
CHUNK = 32
N_EXPERTS = 16
EC_CAPACITY = 2
EPS = 1e-6

LANES = 128
SUBLANES = 8
VMEM_LIMIT = 62 * 1024 * 1024

TT = 256
NCH = TT // CHUNK
MOD_ROWS = 16
DFF_TILE = 1024
EPI_TOK = 256
ACC_ROWS = D_MODEL // LANES


def _dot(a, b):
    return jnp.dot(a, b, preferred_element_type=F32)


def _dot_nt(a, b):
    return lax.dot_general(a, b, (((1,), (1,)), ((), ())), preferred_element_type=F32)


def _dot_tn(a, b):
    return lax.dot_general(a, b, (((0,), (0,)), ((), ())), preferred_element_type=F32)


def _split(x):
    hi = x.astype(BF16)
    lo = (x - hi.astype(F32)).astype(BF16)
    return hi, lo


def _rms(x, gain):
    return x * lax.rsqrt(jnp.mean(x * x, axis=-1, keepdims=True) + EPS) * gain


def _silu(x):
    return x * jax.nn.sigmoid(x)


def _lower_bounds(lbl_ref):
    rows = [lbl_ref[l] for l in range(lbl_ref.shape[0])]
    m = functools.reduce(jnp.maximum, rows)
    es = [jnp.exp(r - m) for r in rows]
    sm0 = es[0] / functools.reduce(lambda a, b: a + b, es)
    return sm0[0:1, :], sm0[1:2, :]


def _ada_body(c_ref, w_ref, b_ref, o_ref):
    s_hi, s_lo = _split(_silu(c_ref[...]))
    w_hi, w_lo = _split(w_ref[...])
    o_ref[...] = _dot(s_hi, w_hi) + _dot(s_hi, w_lo) + _dot(s_lo, w_hi) + b_ref[...]


def _adaln(cc, w, b):
    n = w.shape[1]
    tn = n // 4
    return pl.pallas_call(
        _ada_body,
        grid=(n // tn,),
        in_specs=[pl.BlockSpec((MOD_ROWS, D_MODEL), lambda i: (0, 0)),
                  pl.BlockSpec((D_MODEL, tn), lambda i: (0, i)),
                  pl.BlockSpec((1, tn), lambda i: (0, i))],
        out_specs=pl.BlockSpec((MOD_ROWS, tn), lambda i: (0, i)),
        out_shape=jax.ShapeDtypeStruct((MOD_ROWS, n), F32),
        compiler_params=pltpu.CompilerParams(vmem_limit_bytes=VMEM_LIMIT),
        name="adaln",
    )(cc, w, b)


def _ctx_body(ctx_ref, mods_ref, nm_ref, wf_ref, wb_ref, wi_ref, lbl_ref, sf_ref, sb_ref, *, ctx_row):
    x = ctx_ref[0]
    n = x.shape[0]
    csh = mods_ref[ctx_row:ctx_row + 1, 0:D_MODEL]
    csc = mods_ref[ctx_row:ctx_row + 1, D_MODEL:2 * D_MODEL]
    hb = (_rms(x, nm_ref[...]) * (1.0 + csc) + csh).astype(BF16)
    zf = _dot(hb, wf_ref[...])
    zb = _dot(hb, wb_ref[...])
    vi = _dot(hb, wi_ref[...]).astype(BF16)
    lb_f, lb_b = _lower_bounds(lbl_ref)
    f_f = lb_f + (1.0 - lb_f) * jax.nn.sigmoid(zf)
    f_b = lb_b + (1.0 - lb_b) * jax.nn.sigmoid(zb)
    ii = lax.broadcasted_iota(I32, (n, n), 0)
    jj = lax.broadcasted_iota(I32, (n, n), 1)
    after = jnp.where(jj > ii, 1.0, 0.0).astype(BF16)
    before = jnp.where(jj < ii, 1.0, 0.0).astype(BF16)
    hi, lo = _split(jnp.log(f_f))
    kd_f = ((1.0 - f_f) * jnp.exp(_dot(after, hi) + _dot(after, lo))).astype(BF16)
    hi, lo = _split(jnp.log(f_b))
    kd_b = ((1.0 - f_b) * jnp.exp(_dot(before, hi) + _dot(before, lo))).astype(BF16)
    for h in range(HG_HEADS):
        ks = slice(h * HG_KDIM, (h + 1) * HG_KDIM)
        vs = slice(h * HG_VDIM, (h + 1) * HG_VDIM)
        sf_ref[0, h] = _dot_tn(vi[:, vs], kd_f[:, ks])
        sb_ref[0, h] = _dot_tn(vi[:, vs], kd_b[:, ks])


def _ctx_states(ctx, mods, nm, w_in_bf, lbl, ctx_row):
    nb, n, _ = ctx.shape
    st = jax.ShapeDtypeStruct((nb, HG_HEADS, HG_VDIM, HG_KDIM), F32)
    st_spec = pl.BlockSpec((1, HG_HEADS, HG_VDIM, HG_KDIM), lambda b: (b, 0, 0, 0))
    wcol = lambda k: pl.BlockSpec((D_MODEL, HG_KEYS), lambda b, k=k: (0, k))
    return pl.pallas_call(
        functools.partial(_ctx_body, ctx_row=ctx_row),
        grid=(nb,),
        in_specs=[pl.BlockSpec((1, n, D_MODEL), lambda b: (b, 0, 0)),
                  pl.BlockSpec(mods.shape, lambda b: (0, 0)),
                  pl.BlockSpec((1, D_MODEL), lambda b: (0, 0)),
                  wcol(1), wcol(2), wcol(3),
                  pl.BlockSpec(lbl.shape, lambda b: (0, 0, 0))],
        out_specs=[st_spec, st_spec],
        out_shape=[st, st],
        compiler_params=pltpu.CompilerParams(vmem_limit_bytes=VMEM_LIMIT),
        name="ctx_states",
    )(ctx, mods, nm, w_in_bf, w_in_bf, w_in_bf, lbl)


def _level_map(reverse):
    ii = lax.broadcasted_iota(I32, (TT, TT), 0)
    jj = lax.broadcasted_iota(I32, (TT, TT), 1)
    if reverse:
        ii = TT - 1 - ii
        jj = TT - 1 - jj
    sh0 = CHUNK.bit_length() - 1
    ci = ii >> sh0
    cj = jj >> sh0
    lm = jnp.where((ci == cj) & (jj <= ii), 1, 0)
    for lvl in range(1, NCH.bit_length()):
        m = ((ci >> lvl) == (cj >> lvl)) & (((ci >> (lvl - 1)) & 1) == 1) & (((cj >> (lvl - 1)) & 1) == 0)
        lm = jnp.where(m, lvl + 1, lm)
    return lm


def _gla_tile(q, zf, v, lb, st_ref, lm, reverse):
    n_lvl = NCH.bit_length() - 1
    f = lb + (1.0 - lb) * jax.nn.sigmoid(zf)
    k = 1.0 - f
    tri = jnp.where(lm == 1, 1.0, 0.0).astype(BF16)
    hi, lo = _split(jnp.log(f))
    cum = _dot(tri, hi) + _dot(tri, lo)
    phys = list(range(NCH - 1, -1, -1)) if reverse else list(range(NCH))
    last = 0 if reverse else CHUNK - 1
    tot = [cum[a * CHUNK + last:a * CHUNK + last + 1, :] for a in range(NCH)]
    bnd = [jnp.zeros_like(tot[0])]
    for c in range(NCH):
        bnd.append(bnd[-1] + tot[phys[c]])
    q0 = q * jnp.exp(cum)
    kinv = (k * jnp.exp(-cum)).astype(BF16)
    qs = [[None] * NCH for _ in range(n_lvl + 1)]
    ks = [[None] * NCH for _ in range(n_lvl + 1)]
    for c in range(NCH):
        a = phys[c]
        rows = slice(a * CHUNK, (a + 1) * CHUNK)
        q0a = q0[rows]
        kend = k[rows] * jnp.exp(tot[a] - cum[rows])
        zero = jnp.zeros_like(q0a)
        for lvl in range(1, n_lvl + 1):
            s = 1 << (lvl - 1)
            mid = (c // (2 * s)) * 2 * s + s
            right = (c // s) % 2 == 1
            qs[lvl - 1][a] = q0a * jnp.exp(bnd[c] - bnd[mid]) if right else zero
            ks[lvl - 1][a] = zero if right else kend * jnp.exp(bnd[mid] - bnd[c + 1])
        qs[n_lvl][a] = q0a * jnp.exp(bnd[c])
        ks[n_lvl][a] = kend * jnp.exp(bnd[NCH] - bnd[c + 1])
    qs = [jnp.concatenate(p, axis=0).astype(BF16) for p in qs]
    ks = [jnp.concatenate(p, axis=0).astype(BF16) for p in ks]
    q0 = q0.astype(BF16)
    vb = v.astype(BF16)
    dec = jnp.exp(bnd[NCH])
    outs = []
    for h in range(HG_HEADS):
        kk = slice(h * HG_KDIM, (h + 1) * HG_KDIM)
        vv = slice(h * HG_VDIM, (h + 1) * HG_VDIM)
        sc = jnp.where(lm == 1, _dot_nt(q0[:, kk], kinv[:, kk]), 0.0)
        for lvl in range(n_lvl):
            sc = jnp.where(lm == lvl + 2, _dot_nt(qs[lvl][:, kk], ks[lvl][:, kk]), sc)
        st = st_ref[h]
        outs.append(_dot(sc.astype(BF16), vb[:, vv]) + _dot_nt(qs[n_lvl][:, kk], st.astype(BF16)))
        st_ref[h] = st * dec[:, kk] + _dot_tn(vb[:, vv], ks[n_lvl][:, kk])
    return jnp.concatenate(outs, axis=1)


def _mix_fwd_body(x_ref, mods_ref, nm_ref, w_ref, lbl_ref, cw_ref, s0_ref,
                  q_ref, zb_ref, i_ref, g_ref, y_ref, of_ref, st_ref, lm_ref):
    b = pl.program_id(0)
    j = pl.program_id(1)

    @pl.when((b == 0) & (j == 0))
    def _():
        lm_ref[...] = _level_map(False)

    @pl.when(j == 0)
    def _():
        st_ref[...] = s0_ref[0]

    sh = mods_ref[pl.ds(b, 1), 0:D_MODEL]
    sc = mods_ref[pl.ds(b, 1), D_MODEL:2 * D_MODEL]
    hx = (_rms(x_ref[0], nm_ref[...]) * (1.0 + sc) + sh).astype(BF16)
    w = HG_KEYS
    proj = lambda k: _dot(hx, w_ref[:, k * w:(k + 1) * w])
    q = proj(0)
    zff = proj(1)
    zb_ref[0] = proj(2)
    vi = proj(3)
    g_ref[0] = proj(4).astype(BF16)
    gate_b = proj(5)
    u = proj(6) * proj(7)
    col = lax.broadcasted_iota(I32, (TT, 1), 0) & (GRID_W - 1)
    up = jnp.where(col != 0, pltpu.roll(u, 1, axis=0), 0.0)
    un = jnp.where(col != GRID_W - 1, pltpu.roll(u, TT - 1, axis=0), 0.0)
    cw = cw_ref[...]
    y_ref[0] = (gate_b * (cw[0:1] * up + cw[1:2] * u + cw[2:3] * un)).astype(BF16)
    lb_f, _ = _lower_bounds(lbl_ref)
    of_ref[0] = _gla_tile(q, zff, vi, lb_f, st_ref, lm_ref[...], False)
    q_ref[0] = q.astype(BF16)
    i_ref[0] = vi.astype(BF16)


def _mix_fwd(x, mods, nm, w_in_bf, lbl, cw, s0_f):
    nb, t, _ = x.shape
    nt = t // TT
    tok = lambda dt: jax.ShapeDtypeStruct((nb, t, HG_WIDTH), dt)
    tspec = pl.BlockSpec((1, TT, HG_WIDTH), lambda b, j: (b, j, 0))
    return pl.pallas_call(
        _mix_fwd_body,
        grid=(nb, nt),
        in_specs=[pl.BlockSpec((1, TT, D_MODEL), lambda b, j: (b, j, 0)),
                  pl.BlockSpec(mods.shape, lambda b, j: (0, 0)),
                  pl.BlockSpec((1, D_MODEL), lambda b, j: (0, 0)),
                  pl.BlockSpec(w_in_bf.shape, lambda b, j: (0, 0)),
                  pl.BlockSpec(lbl.shape, lambda b, j: (0, 0, 0)),
                  pl.BlockSpec(cw.shape, lambda b, j: (0, 0)),
                  pl.BlockSpec((1, HG_HEADS, HG_VDIM, HG_KDIM), lambda b, j: (b, 0, 0, 0))],
        out_specs=[tspec] * 6,
        out_shape=[tok(BF16), tok(F32), tok(BF16), tok(BF16), tok(BF16), tok(F32)],
        scratch_shapes=[pltpu.VMEM((HG_HEADS, HG_VDIM, HG_KDIM), F32), pltpu.VMEM((TT, TT), I32)],
        compiler_params=pltpu.CompilerParams(dimension_semantics=("arbitrary", "arbitrary"),
                                             vmem_limit_bytes=VMEM_LIMIT),
        name="mix_fwd",
    )(x, mods, nm, w_in_bf, lbl, cw, s0_f)


def _mix_bwd_body(x_ref, q_ref, zb_ref, i_ref, g_ref, y_ref, of_ref, mods_ref, lbl_ref, hgn_ref,
                  wo_ref, nf_ref, wr_ref, s0_ref,
                  x1_ref, hr_ref, at_ref, ak_ref, st_ref, lm_ref):
    b = pl.program_id(0)
    j = pl.program_id(1)

    @pl.when((b == 0) & (j == 0))
    def _():
        lm_ref[...] = _level_map(True)

    @pl.when(j == 0)
    def _():
        st_ref[...] = s0_ref[0]

    _, lb_b = _lower_bounds(lbl_ref)
    o_b = _gla_tile(q_ref[0].astype(F32), zb_ref[0], i_ref[0].astype(F32), lb_b, st_ref, lm_ref[...], True)
    o = of_ref[0] + o_b
    hgn = hgn_ref[...]
    heads = [_rms(o[:, h * HG_VDIM:(h + 1) * HG_VDIM], hgn) for h in range(HG_HEADS)]
    g = g_ref[0].astype(F32)
    ohg = jnp.concatenate(heads, axis=1) * _silu(g)
    cat = jnp.concatenate([ohg.astype(BF16), y_ref[0]], axis=1)
    mx = _dot(cat, wo_ref[...])
    mod = lambda k: mods_ref[pl.ds(b, 1), k * D_MODEL:(k + 1) * D_MODEL]
    x1 = x_ref[0] + mod(2) * mx
    x1_ref[0] = x1
    h2 = _rms(x1, nf_ref[...]) * (1.0 + mod(4)) + mod(3)
    for c in range(ACC_ROWS):
        hr_ref[0, pl.ds(c, TT, stride=ACC_ROWS), :] = h2[:, c * LANES:(c + 1) * LANES]
    hh, hl = _split(h2)
    wh, wl = _split(wr_ref[...])
    lg = _dot(hh, wh) + _dot(hh, wl) + _dot(hl, wh)
    lane = lax.broadcasted_iota(I32, lg.shape, 1)
    lg = jnp.where(lane < N_EXPERTS, lg, -jnp.inf)
    e = jnp.exp(lg - jnp.max(lg, axis=-1, keepdims=True))
    aff = e / jnp.sum(e, axis=-1, keepdims=True)
    ak_ref[0] = aff
    at_ref[0] = aff.T[0:N_EXPERTS, :]


def _mix_bwd(x, q, zb, vi, g, y, of, mods, lbl, hgn, w_out_bf, nf, wr_pad, s0_b):
    nb, t, _ = x.shape
    nt = t // TT
    rev = lambda b, j: (b, nt - 1 - j, 0)
    tspec = pl.BlockSpec((1, TT, HG_WIDTH), rev)
    xspec = pl.BlockSpec((1, TT, D_MODEL), rev)
    full2 = lambda a: pl.BlockSpec(a.shape, lambda b, j: (0, 0))
    return pl.pallas_call(
        _mix_bwd_body,
        grid=(nb, nt),
        in_specs=[xspec, tspec, tspec, tspec, tspec, tspec, tspec,
                  full2(mods), pl.BlockSpec(lbl.shape, lambda b, j: (0, 0, 0)), full2(hgn),
                  full2(w_out_bf), full2(nf), full2(wr_pad),
                  pl.BlockSpec((1, HG_HEADS, HG_VDIM, HG_KDIM), lambda b, j: (b, 0, 0, 0))],
        out_specs=[xspec,
                   pl.BlockSpec((1, TT * ACC_ROWS, LANES), rev),
                   pl.BlockSpec((1, N_EXPERTS, TT), lambda b, j: (b, 0, nt - 1 - j)),
                   pl.BlockSpec((1, TT, LANES), rev)],
        out_shape=[jax.ShapeDtypeStruct((nb, t, D_MODEL), F32),
                   jax.ShapeDtypeStruct((nb, t * ACC_ROWS, LANES), F32),
                   jax.ShapeDtypeStruct((nb, N_EXPERTS, t), F32),
                   jax.ShapeDtypeStruct((nb, t, LANES), F32)],
        scratch_shapes=[pltpu.VMEM((HG_HEADS, HG_VDIM, HG_KDIM), F32), pltpu.VMEM((TT, TT), I32)],
        compiler_params=pltpu.CompilerParams(dimension_semantics=("arbitrary", "arbitrary"),
                                             vmem_limit_bytes=VMEM_LIMIT),
        name="mix_bwd",
    )(x, q, zb, vi, g, y, of, mods, lbl, hgn, w_out_bf, nf, wr_pad, s0_b)


def _route_body(at_ref, idx_ref, inc_ref, col_ref, *, cap):
    ne, t = at_ref.shape[1], at_ref.shape[2]
    blk = 2 * LANES
    aff = at_ref[0]

    def count(m):
        return jnp.sum(jnp.where(m, 1.0, 0.0), axis=1, keepdims=True)

    def bit_step(i, thr):
        cand = thr | jnp.left_shift(jnp.int32(1), 30 - i)
        return jnp.where(count(aff >= lax.bitcast_convert_type(cand, F32)) >= cap, cand, thr)

    thr = lax.fori_loop(0, 31, bit_step, jnp.zeros((ne, 1), I32))
    gt = aff >= lax.bitcast_convert_type(thr + 1, F32)
    eq = (aff >= lax.bitcast_convert_type(thr, F32)) & jnp.logical_not(gt)
    need = cap - count(gt)

    ii = lax.broadcasted_iota(I32, (blk, blk), 0)
    jj = lax.broadcasted_iota(I32, (blk, blk), 1)
    upper = jnp.where(ii <= jj, 1.0, 0.0).astype(BF16)

    def prefix_blocks(m):
        mb = jnp.where(m, 1.0, 0.0).astype(BF16)
        carry = jnp.zeros((ne, 1), F32)
        out = []
        for kb in range(t // blk):
            p = _dot(mb[:, kb * blk:(kb + 1) * blk], upper) + carry
            out.append(p)
            carry = p[:, blk - 1:blk]
        return out

    eq_rank = jnp.concatenate(prefix_blocks(eq), axis=1)
    sel = gt | (eq & (eq_rank <= need))
    for kb, p in enumerate(prefix_blocks(sel)):
        for half in range(blk // LANES):
            inc_ref[kb * (blk // LANES) + half] = p[:, half * LANES:(half + 1) * LANES]

    n_lb = t // LANES
    lane = lax.broadcasted_iota(I32, (LANES, LANES), 1)
    col_ref[...] = jnp.zeros_like(col_ref)

    def expert_step(e, carry):
        for jg in range(cap // LANES):
            slot = (lax.broadcasted_iota(I32, (LANES, LANES), 0) + jg * LANES).astype(F32)

            def lane_block(lb, acc):
                r = inc_ref[lb, pl.ds(e, 1), :]
                return acc + jnp.where(r <= slot, 1.0, 0.0)

            acc = lax.fori_loop(0, n_lb, lane_block, jnp.zeros((LANES, LANES), F32), unroll=4)
            cnt = jnp.sum(acc, axis=1, keepdims=True)
            rows = slice(jg * LANES, (jg + 1) * LANES)
            col_ref[rows, :] = jnp.where(lane == e, cnt, col_ref[rows, :])
        return carry

    lax.fori_loop(0, ne, expert_step, 0)
    idx_ref[0] = col_ref[...].T[0:ne, :].astype(I32)


def _route(aff_t, cap):
    nb, ne, t = aff_t.shape
    return pl.pallas_call(
        functools.partial(_route_body, cap=cap),
        grid=(nb,),
        in_specs=[pl.BlockSpec((1, ne, t), lambda b: (b, 0, 0))],
        out_specs=pl.BlockSpec((1, ne, cap), lambda b: (b, 0, 0)),
        out_shape=jax.ShapeDtypeStruct((nb, ne, cap), I32),
        scratch_shapes=[pltpu.VMEM((t // LANES, ne, LANES), F32), pltpu.VMEM((cap, LANES), F32)],
        compiler_params=pltpu.CompilerParams(vmem_limit_bytes=VMEM_LIMIT),
        name="route",
    )(aff_t)


def _moe_body(idx_ref, hr_ref, ak_ref, wg_ref, wu_ref, wd_ref, x1_ref, mods_ref, nfin_ref,
              out_ref, acc_ref, rowg_ref, rows_ref, affg_ref, gate_ref, xs_ref, ys_ref, *, cap, n_ff):
    b = pl.program_id(0)
    s = pl.program_id(1)
    n_work = N_EXPERTS * n_ff
    e = s // n_ff
    ff = s % n_ff
    group = SUBLANES
    part_tok = cap // n_ff

    def list_base(expert):
        return (b * N_EXPERTS + expert) * cap

    def gather(base, j0, n):
        for u in range(n):
            tkn = idx_ref[base + j0 + u]
            rowg_ref[pl.ds(pl.multiple_of((j0 + u) * ACC_ROWS, ACC_ROWS), ACC_ROWS), :] = (
                hr_ref[0, pl.ds(pl.multiple_of(tkn * ACC_ROWS, ACC_ROWS), ACC_ROWS), :])
            affg_ref[pl.ds(j0 + u, 1), :] = ak_ref[0, pl.ds(tkn, 1), :]

    def scatter(base, j0, n):
        for g0 in range(0, n, group):
            dst = []
            val = []
            for u in range(g0, g0 + group):
                r = pl.multiple_of(idx_ref[base + j0 + u] * ACC_ROWS, ACC_ROWS)
                dst.append(r)
                val.append(acc_ref[pl.ds(r, ACC_ROWS), :]
                           + rows_ref[pl.ds(pl.multiple_of((j0 + u) * ACC_ROWS, ACC_ROWS), ACC_ROWS), :])
            for r, v in zip(dst, val):
                acc_ref[pl.ds(r, ACC_ROWS), :] = v

    def in_groups(fn, base):
        def step(jo, carry):
            fn(base, jo * group, group)
            return carry
        lax.fori_loop(0, cap // group, step, 0)

    @pl.when(s == 0)
    def _():
        acc_ref[...] = jnp.zeros_like(acc_ref)
        rows_ref[...] = jnp.zeros_like(rows_ref)
        ys_ref[...] = jnp.zeros_like(ys_ref)
        in_groups(gather, list_base(0))

    @pl.when((s < n_work) & (ff == 0))
    def _():
        for c in range(ACC_ROWS):
            xs_ref[:, c * LANES:(c + 1) * LANES] = rowg_ref[pl.ds(c, cap, stride=ACC_ROWS), :].astype(BF16)
        lane = lax.broadcasted_iota(I32, affg_ref.shape, 1)
        gate_ref[...] = jnp.sum(jnp.where(lane == e, affg_ref[...], 0.0), axis=1, keepdims=True)

    @pl.when(s < n_work)
    def _():
        gather(list_base(jnp.minimum(e + 1, N_EXPERTS - 1)), ff * part_tok, part_tok)
        scatter(list_base(jnp.maximum(e - 1, 0)), ff * part_tok, part_tok)
        xs = xs_ref[...]
        hid = _silu(_dot(xs, wg_ref[0])) * _dot(xs, wu_ref[0])
        ys_ref[...] = _dot(hid.astype(BF16), wd_ref[0]) + jnp.where(ff == 0, 0.0, ys_ref[...])

    @pl.when((s < n_work) & (ff == n_ff - 1))
    def _():
        gate = gate_ref[...]
        for c in range(ACC_ROWS):
            rows_ref[pl.ds(c, cap, stride=ACC_ROWS), :] = ys_ref[:, c * LANES:(c + 1) * LANES] * gate

    @pl.when(s == n_work)
    def _():
        in_groups(scatter, list_base(N_EXPERTS - 1))

    @pl.when(s >= n_work)
    def _():
        row0 = (s - n_work) * (EPI_TOK * ACC_ROWS)
        moe = jnp.concatenate([acc_ref[pl.ds(row0 + c, EPI_TOK, stride=ACC_ROWS), :] for c in range(ACC_ROWS)], axis=1)
        g2 = mods_ref[pl.ds(b, 1), 5 * D_MODEL:6 * D_MODEL]
        out_ref[0] = _rms(x1_ref[0] + g2 * moe, nfin_ref[...])


def _moe(idx_flat, hr, ak, wg, wu, wd, x1, mods, nfin, cap):
    nb, t, _ = x1.shape
    dff = wg.shape[2]
    n_ff = dff // DFF_TILE
    n_work = N_EXPERTS * n_ff
    n_epi = t // EPI_TOK

    def w_idx(s):
        sc = jnp.minimum(s, n_work - 1)
        return sc // n_ff, sc % n_ff

    def epi(b, s, idx):
        return (b, jnp.maximum(s - n_work, 0), 0)

    one = pl.Buffered(1)
    grid_spec = pltpu.PrefetchScalarGridSpec(
        num_scalar_prefetch=1,
        grid=(nb, n_work + n_epi),
        in_specs=[pl.BlockSpec((1, t * ACC_ROWS, LANES), lambda b, s, idx: (b, 0, 0), pipeline_mode=one),
                  pl.BlockSpec((1, t, LANES), lambda b, s, idx: (b, 0, 0), pipeline_mode=one),
                  pl.BlockSpec((1, D_MODEL, DFF_TILE), lambda b, s, idx: (w_idx(s)[0], 0, w_idx(s)[1])),
                  pl.BlockSpec((1, D_MODEL, DFF_TILE), lambda b, s, idx: (w_idx(s)[0], 0, w_idx(s)[1])),
                  pl.BlockSpec((1, DFF_TILE, D_MODEL), lambda b, s, idx: (w_idx(s)[0], w_idx(s)[1], 0)),
                  pl.BlockSpec((1, EPI_TOK, D_MODEL), epi),
                  pl.BlockSpec(mods.shape, lambda b, s, idx: (0, 0)),
                  pl.BlockSpec((1, D_MODEL), lambda b, s, idx: (0, 0))],
        out_specs=pl.BlockSpec((1, EPI_TOK, D_MODEL), epi),
        scratch_shapes=[pltpu.VMEM((t * ACC_ROWS, LANES), F32),
                        pltpu.VMEM((cap * ACC_ROWS, LANES), F32),
                        pltpu.VMEM((cap * ACC_ROWS, LANES), F32),
                        pltpu.VMEM((cap, LANES), F32),
                        pltpu.VMEM((cap, 1), F32),
                        pltpu.VMEM((cap, D_MODEL), BF16),
                        pltpu.VMEM((cap, D_MODEL), F32)])
    return pl.pallas_call(
        functools.partial(_moe_body, cap=cap, n_ff=n_ff),
        grid_spec=grid_spec,
        out_shape=jax.ShapeDtypeStruct((nb, t, D_MODEL), F32),
        compiler_params=pltpu.CompilerParams(dimension_semantics=("arbitrary", "arbitrary"),
                                             vmem_limit_bytes=VMEM_LIMIT),
        name="moe",
    )(idx_flat, hr, ak, wg, wu, wd, x1, mods, nfin)


def kernel(x, c, ctx, c_ctx, w_ada, b_ada, norm_mix, norm_ffn, w_in, lb_logits, hg_norm, conv_w,
           w_out, w_router, w_gate, w_up, w_down, norm_final):
    nb, t, d = x.shape
    assert d == D_MODEL and w_ada.shape[0] == 1 and nb < MOD_ROWS and t % TT == 0 and TT % GRID_W == 0
    assert w_in.shape[2] == 3 * HG_KEYS + 2 * HG_WIDTH + 3 * SC_WIDTH and w_router.shape[2] == N_EXPERTS
    cap = EC_CAPACITY * t // N_EXPERTS
    assert cap % LANES == 0 and w_gate.shape[3] % DFF_TILE == 0

    cc = jnp.concatenate([c, c_ctx[None, :], jnp.zeros((MOD_ROWS - nb - 1, d), F32)], axis=0)
    w_in_bf = w_in[0].astype(BF16)
    w_out_bf = w_out[0].astype(BF16)
    wg, wu, wd = w_gate[0].astype(BF16), w_up[0].astype(BF16), w_down[0].astype(BF16)
    wr_pad = jnp.pad(w_router[0], ((0, 0), (0, LANES - N_EXPERTS)))
    nm, nf, nfin = norm_mix[0][None, :], norm_ffn[0][None, :], norm_final[None, :]
    hgn = hg_norm[0][None, :]

    mods = _adaln(cc, w_ada[0], b_ada[0][None, :])
    s0_f, s0_b = _ctx_states(ctx, mods, nm, w_in_bf, lb_logits, ctx_row=nb)
    q, zb, vi, g, y, of = _mix_fwd(x, mods, nm, w_in_bf, lb_logits, conv_w[0], s0_f)
    x1, hr, aff_t, aff_k = _mix_bwd(x, q, zb, vi, g, y, of, mods, lb_logits, hgn, w_out_bf, nf, wr_pad, s0_b)
    idx = _route(aff_t, cap)
    return _moe(idx.reshape(-1), hr, aff_k, wg, wu, wd, x1, mods, nfin, cap)
```

```python
import functools

import jax
import jax.numpy as jnp
from jax import lax
from jax.experimental import pallas as pl
from jax.experimental.pallas import tpu as pltpu

F32 = jnp.float32
BF16 = jnp.bfloat16
I32 = jnp.int32

D_MODEL = 1024
N_MOD = 6
HG_HEADS = 4
HG_KDIM = 128
HG_VDIM = 128
HG_KEYS = HG_HEADS * HG_KDIM
HG_WIDTH = HG_HEADS * HG_VDIM
SC_WIDTH = D_MODEL - HG_WIDTH
GRID_W = 64
CHUNK = 32
N_EXPERTS = 16
EC_CAPACITY = 2
EPS = 1e-6

LANES = 128
SUBLANES = 8
VMEM_LIMIT = 62 * 1024 * 1024

TT = 256
NCH = TT // CHUNK
MOD_ROWS = 16
DFF_TILE = 1024
EPI_TOK = 256
ACC_ROWS = D_MODEL // LANES


def _dot(a, b):
    return jnp.dot(a, b, preferred_element_type=F32)


def _dot_nt(a, b):
    return lax.dot_general(a, b, (((1,), (1,)), ((), ())), preferred_element_type=F32)


def _dot_tn(a, b):
    return lax.dot_general(a, b, (((0,), (0,)), ((), ())), preferred_element_type=F32)


def _split(x):
    hi = x.astype(BF16)
    lo = (x - hi.astype(F32)).astype(BF16)
    return hi, lo


def _rms(x, gain):
    return x * lax.rsqrt(jnp.mean(x * x, axis=-1, keepdims=True) + EPS) * gain


def _silu(x):
    return x * jax.nn.sigmoid(x)


def _lower_bounds(lbl_ref):
    rows = [lbl_ref[l] for l in range(lbl_ref.shape[0])]
    m = functools.reduce(jnp.maximum, rows)
    es = [jnp.exp(r - m) for r in rows]
    sm0 = es[0] / functools.reduce(lambda a, b: a + b, es)
    return sm0[0:1, :], sm0[1:2, :]


def _ada_body(c_ref, w_ref, b_ref, o_ref):
    s_hi, s_lo = _split(_silu(c_ref[...]))
    w_hi, w_lo = _split(w_ref[...])
    o_ref[...] = _dot(s_hi, w_hi) + _dot(s_hi, w_lo) + _dot(s_lo, w_hi) + b_ref[...]


def _adaln(cc, w, b):
    n = w.shape[1]
    tn = n // 4
    return pl.pallas_call(
        _ada_body,
        grid=(n // tn,),
        in_specs=[pl.BlockSpec((MOD_ROWS, D_MODEL), lambda i: (0, 0)),
                  pl.BlockSpec((D_MODEL, tn), lambda i: (0, i)),
                  pl.BlockSpec((1, tn), lambda i: (0, i))],
        out_specs=pl.BlockSpec((MOD_ROWS, tn), lambda i: (0, i)),
        out_shape=jax.ShapeDtypeStruct((MOD_ROWS, n), F32),
        compiler_params=pltpu.CompilerParams(vmem_limit_bytes=VMEM_LIMIT),
        name="adaln",
    )(cc, w, b)


def _ctx_body(ctx_ref, mods_ref, nm_ref, wf_ref, wb_ref, wi_ref, lbl_ref, sf_ref, sb_ref, *, ctx_row):
    x = ctx_ref[0]
    n = x.shape[0]
    csh = mods_ref[ctx_row:ctx_row + 1, 0:D_MODEL]
    csc = mods_ref[ctx_row:ctx_row + 1, D_MODEL:2 * D_MODEL]
    hb = (_rms(x, nm_ref[...]) * (1.0 + csc) + csh).astype(BF16)
    zf = _dot(hb, wf_ref[...])
    zb = _dot(hb, wb_ref[...])
    vi = _dot(hb, wi_ref[...]).astype(BF16)
    lb_f, lb_b = _lower_bounds(lbl_ref)
    f_f = lb_f + (1.0 - lb_f) * jax.nn.sigmoid(zf)
    f_b = lb_b + (1.0 - lb_b) * jax.nn.sigmoid(zb)
    ii = lax.broadcasted_iota(I32, (n, n), 0)
    jj = lax.broadcasted_iota(I32, (n, n), 1)
    after = jnp.where(jj > ii, 1.0, 0.0).astype(BF16)
    before = jnp.where(jj < ii, 1.0, 0.0).astype(BF16)
    hi, lo = _split(jnp.log(f_f))
    kd_f = ((1.0 - f_f) * jnp.exp(_dot(after, hi) + _dot(after, lo))).astype(BF16)
    hi, lo = _split(jnp.log(f_b))
    kd_b = ((1.0 - f_b) * jnp.exp(_dot(before, hi) + _dot(before, lo))).astype(BF16)
    for h in range(HG_HEADS):
        ks = slice(h * HG_KDIM, (h + 1) * HG_KDIM)
        vs = slice(h * HG_VDIM, (h + 1) * HG_VDIM)
        sf_ref[0, h] = _dot_tn(vi[:, vs], kd_f[:, ks])
        sb_ref[0, h] = _dot_tn(vi[:, vs], kd_b[:, ks])


def _ctx_states(ctx, mods, nm, w_in_bf, lbl, ctx_row):
    nb, n, _ = ctx.shape
    st = jax.ShapeDtypeStruct((nb, HG_HEADS, HG_VDIM, HG_KDIM), F32)
    st_spec = pl.BlockSpec((1, HG_HEADS, HG_VDIM, HG_KDIM), lambda b: (b, 0, 0, 0))
    wcol = lambda k: pl.BlockSpec((D_MODEL, HG_KEYS), lambda b, k=k: (0, k))
    return pl.pallas_call(
        functools.partial(_ctx_body, ctx_row=ctx_row),
        grid=(nb,),
        in_specs=[pl.BlockSpec((1, n, D_MODEL), lambda b: (b, 0, 0)),
                  pl.BlockSpec(mods.shape, lambda b: (0, 0)),
                  pl.BlockSpec((1, D_MODEL), lambda b: (0, 0)),
                  wcol(1), wcol(2), wcol(3),
                  pl.BlockSpec(lbl.shape, lambda b: (0, 0, 0))],
        out_specs=[st_spec, st_spec],
        out_shape=[st, st],
        compiler_params=pltpu.CompilerParams(vmem_limit_bytes=VMEM_LIMIT),
        name="ctx_states",
    )(ctx, mods, nm, w_in_bf, w_in_bf, w_in_bf, lbl)


def _level_map(reverse):
    ii = lax.broadcasted_iota(I32, (TT, TT), 0)
    jj = lax.broadcasted_iota(I32, (TT, TT), 1)
    if reverse:
        ii = TT - 1 - ii
        jj = TT - 1 - jj
    sh0 = CHUNK.bit_length() - 1
    ci = ii >> sh0
    cj = jj >> sh0
    lm = jnp.where((ci == cj) & (jj <= ii), 1, 0)
    for lvl in range(1, NCH.bit_length()):
        m = ((ci >> lvl) == (cj >> lvl)) & (((ci >> (lvl - 1)) & 1) == 1) & (((cj >> (lvl - 1)) & 1) == 0)
        lm = jnp.where(m, lvl + 1, lm)
    return lm


def _gla_tile(q, zf, v, lb, st_ref, lm, reverse):
    n_lvl = NCH.bit_length() - 1
    f = lb + (1.0 - lb) * jax.nn.sigmoid(zf)
    k = 1.0 - f
    tri = jnp.where(lm == 1, 1.0, 0.0).astype(BF16)
    hi, lo = _split(jnp.log(f))
    cum = _dot(tri, hi) + _dot(tri, lo)
    phys = list(range(NCH - 1, -1, -1)) if reverse else list(range(NCH))
    last = 0 if reverse else CHUNK - 1
    tot = [cum[a * CHUNK + last:a * CHUNK + last + 1, :] for a in range(NCH)]
    bnd = [jnp.zeros_like(tot[0])]
    for c in range(NCH):
        bnd.append(bnd[-1] + tot[phys[c]])
    q0 = q * jnp.exp(cum)
    kinv = (k * jnp.exp(-cum)).astype(BF16)
    qs = [[None] * NCH for _ in range(n_lvl + 1)]
    ks = [[None] * NCH for _ in range(n_lvl + 1)]
    for c in range(NCH):
        a = phys[c]
        rows = slice(a * CHUNK, (a + 1) * CHUNK)
        q0a = q0[rows]
        kend = k[rows] * jnp.exp(tot[a] - cum[rows])
        zero = jnp.zeros_like(q0a)
        for lvl in range(1, n_lvl + 1):
            s = 1 << (lvl - 1)
            mid = (c // (2 * s)) * 2 * s + s
            right = (c // s) % 2 == 1
            qs[lvl - 1][a] = q0a * jnp.exp(bnd[c] - bnd[mid]) if right else zero
            ks[lvl - 1][a] = zero if right else kend * jnp.exp(bnd[mid] - bnd[c + 1])
        qs[n_lvl][a] = q0a * jnp.exp(bnd[c])
        ks[n_lvl][a] = kend * jnp.exp(bnd[NCH] - bnd[c + 1])
    qs = [jnp.concatenate(p, axis=0).astype(BF16) for p in qs]
    ks = [jnp.concatenate(p, axis=0).astype(BF16) for p in ks]
    q0 = q0.astype(BF16)
    vb = v.astype(BF16)
    dec = jnp.exp(bnd[NCH])
    outs = []
    for h in range(HG_HEADS):
        kk = slice(h * HG_KDIM, (h + 1) * HG_KDIM)
        vv = slice(h * HG_VDIM, (h + 1) * HG_VDIM)
        sc = jnp.where(lm == 1, _dot_nt(q0[:, kk], kinv[:, kk]), 0.0)
        for lvl in range(n_lvl):
            sc = jnp.where(lm == lvl + 2, _dot_nt(qs[lvl][:, kk], ks[lvl][:, kk]), sc)
        st = st_ref[h]
        outs.append(_dot(sc.astype(BF16), vb[:, vv]) + _dot_nt(qs[n_lvl][:, kk], st.astype(BF16)))
        st_ref[h] = st * dec[:, kk] + _dot_tn(vb[:, vv], ks[n_lvl][:, kk])
    return jnp.concatenate(outs, axis=1)


def _mix_fwd_body(x_ref, mods_ref, nm_ref, w_ref, lbl_ref, cw_ref, s0_ref, wa_ref, wb_ref,
                  q_ref, zb_ref, i_ref, g_ref, y_ref, of_ref, wa_bf_ref, wb_bf_ref, st_ref, lm_ref):
    b = pl.program_id(0)
    j = pl.program_id(1)
    wa_bf_ref[...] = wa_ref[...].astype(BF16)
    wb_bf_ref[...] = wb_ref[...].astype(BF16)

    @pl.when((b == 0) & (j == 0))
    def _():
        lm_ref[...] = _level_map(False)

    @pl.when(j == 0)
    def _():
        st_ref[...] = s0_ref[0]

    sh = mods_ref[pl.ds(b, 1), 0:D_MODEL]
    sc = mods_ref[pl.ds(b, 1), D_MODEL:2 * D_MODEL]
    hx = (_rms(x_ref[0], nm_ref[...]) * (1.0 + sc) + sh).astype(BF16)
    w = HG_KEYS
    proj = lambda k: _dot(hx, w_ref[:, k * w:(k + 1) * w])
    q = proj(0)
    zff = proj(1)
    zb_ref[0] = proj(2)
    vi = proj(3)
    g_ref[0] = proj(4).astype(BF16)
    gate_b = proj(5)
    u = proj(6) * proj(7)
    col = lax.broadcasted_iota(I32, (TT, 1), 0) & (GRID_W - 1)
    up = jnp.where(col != 0, pltpu.roll(u, 1, axis=0), 0.0)
    un = jnp.where(col != GRID_W - 1, pltpu.roll(u, TT - 1, axis=0), 0.0)
    cw = cw_ref[...]
    y_ref[0] = (gate_b * (cw[0:1] * up + cw[1:2] * u + cw[2:3] * un)).astype(BF16)
    lb_f, _ = _lower_bounds(lbl_ref)
    of_ref[0] = _gla_tile(q, zff, vi, lb_f, st_ref, lm_ref[...], False)
    q_ref[0] = q.astype(BF16)
    i_ref[0] = vi.astype(BF16)


def _mix_fwd(x, mods, nm, w_in_bf, lbl, cw, s0_f, wa, wb):
    nb, t, _ = x.shape
    nt = t // TT
    tok = lambda dt: jax.ShapeDtypeStruct((nb, t, HG_WIDTH), dt)
    tspec = pl.BlockSpec((1, TT, HG_WIDTH), lambda b, j: (b, j, 0))
    cast_spec = lambda a: pl.BlockSpec((1,) + a.shape[1:], lambda b, j: (b * nt + j, 0, 0))
    cast_out = lambda a: jax.ShapeDtypeStruct(a.shape, BF16)
    return pl.pallas_call(
        _mix_fwd_body,
        grid=(nb, nt),
        in_specs=[pl.BlockSpec((1, TT, D_MODEL), lambda b, j: (b, j, 0)),
                  pl.BlockSpec(mods.shape, lambda b, j: (0, 0)),
                  pl.BlockSpec((1, D_MODEL), lambda b, j: (0, 0)),
                  pl.BlockSpec(w_in_bf.shape, lambda b, j: (0, 0)),
                  pl.BlockSpec(lbl.shape, lambda b, j: (0, 0, 0)),
                  pl.BlockSpec(cw.shape, lambda b, j: (0, 0)),
                  pl.BlockSpec((1, HG_HEADS, HG_VDIM, HG_KDIM), lambda b, j: (b, 0, 0, 0)),
                  cast_spec(wa), cast_spec(wb)],
        out_specs=[tspec] * 6 + [cast_spec(wa), cast_spec(wb)],
        out_shape=[tok(BF16), tok(F32), tok(BF16), tok(BF16), tok(BF16), tok(F32), cast_out(wa), cast_out(wb)],
        scratch_shapes=[pltpu.VMEM((HG_HEADS, HG_VDIM, HG_KDIM), F32), pltpu.VMEM((TT, TT), I32)],
        compiler_params=pltpu.CompilerParams(dimension_semantics=("arbitrary", "arbitrary"),
                                             vmem_limit_bytes=VMEM_LIMIT),
        name="mix_fwd",
    )(x, mods, nm, w_in_bf, lbl, cw, s0_f, wa, wb)


def _mix_bwd_body(x_ref, q_ref, zb_ref, i_ref, g_ref, y_ref, of_ref, mods_ref, lbl_ref, hgn_ref,
                  wo_ref, nf_ref, wr_ref, s0_ref, wc_ref,
                  x1_ref, hr_ref, at_ref, ak_ref, wc_bf_ref, st_ref, lm_ref):
    b = pl.program_id(0)
    j = pl.program_id(1)
    wc_bf_ref[...] = wc_ref[...].astype(BF16)

    @pl.when((b == 0) & (j == 0))
    def _():
        lm_ref[...] = _level_map(True)

    @pl.when(j == 0)
    def _():
        st_ref[...] = s0_ref[0]

    _, lb_b = _lower_bounds(lbl_ref)
    o_b = _gla_tile(q_ref[0].astype(F32), zb_ref[0], i_ref[0].astype(F32), lb_b, st_ref, lm_ref[...], True)
    o = of_ref[0] + o_b
    hgn = hgn_ref[...]
    heads = [_rms(o[:, h * HG_VDIM:(h + 1) * HG_VDIM], hgn) for h in range(HG_HEADS)]
    g = g_ref[0].astype(F32)
    ohg = jnp.concatenate(heads, axis=1) * _silu(g)
    cat = jnp.concatenate([ohg.astype(BF16), y_ref[0]], axis=1)
    mx = _dot(cat, wo_ref[...])
    mod = lambda k: mods_ref[pl.ds(b, 1), k * D_MODEL:(k + 1) * D_MODEL]
    x1 = x_ref[0] + mod(2) * mx
    x1_ref[0] = x1
    h2 = _rms(x1, nf_ref[...]) * (1.0 + mod(4)) + mod(3)
    for c in range(ACC_ROWS):
        hr_ref[0, pl.ds(c, TT, stride=ACC_ROWS), :] = h2[:, c * LANES:(c + 1) * LANES]
    hh, hl = _split(h2)
    wh, wl = _split(wr_ref[...])
    lg = _dot(hh, wh) + _dot(hh, wl) + _dot(hl, wh)
    lane = lax.broadcasted_iota(I32, lg.shape, 1)
    lg = jnp.where(lane < N_EXPERTS, lg, -jnp.inf)
    e = jnp.exp(lg - jnp.max(lg, axis=-1, keepdims=True))
    aff = e / jnp.sum(e, axis=-1, keepdims=True)
    ak_ref[0] = aff
    at_ref[0] = aff.T[0:N_EXPERTS, :]


def _mix_bwd(x, q, zb, vi, g, y, of, mods, lbl, hgn, w_out_bf, nf, wr_pad, s0_b, wc):
    nb, t, _ = x.shape
    nt = t // TT
    cast_spec = pl.BlockSpec((1,) + wc.shape[1:], lambda b, j: (b * nt + j, 0, 0))
    rev = lambda b, j: (b, nt - 1 - j, 0)
    tspec = pl.BlockSpec((1, TT, HG_WIDTH), rev)
    xspec = pl.BlockSpec((1, TT, D_MODEL), rev)
    full2 = lambda a: pl.BlockSpec(a.shape, lambda b, j: (0, 0))
    return pl.pallas_call(
        _mix_bwd_body,
        grid=(nb, nt),
        in_specs=[xspec, tspec, tspec, tspec, tspec, tspec, tspec,
                  full2(mods), pl.BlockSpec(lbl.shape, lambda b, j: (0, 0, 0)), full2(hgn),
                  full2(w_out_bf), full2(nf), full2(wr_pad),
                  pl.BlockSpec((1, HG_HEADS, HG_VDIM, HG_KDIM), lambda b, j: (b, 0, 0, 0)),
                  cast_spec],
        out_specs=[xspec,
                   pl.BlockSpec((1, TT * ACC_ROWS, LANES), rev),
                   pl.BlockSpec((1, N_EXPERTS, TT), lambda b, j: (b, 0, nt - 1 - j)),
                   pl.BlockSpec((1, TT, LANES), rev),
                   cast_spec],
        out_shape=[jax.ShapeDtypeStruct((nb, t, D_MODEL), F32),
                   jax.ShapeDtypeStruct((nb, t * ACC_ROWS, LANES), F32),
                   jax.ShapeDtypeStruct((nb, N_EXPERTS, t), F32),
                   jax.ShapeDtypeStruct((nb, t, LANES), F32),
                   jax.ShapeDtypeStruct(wc.shape, BF16)],
        scratch_shapes=[pltpu.VMEM((HG_HEADS, HG_VDIM, HG_KDIM), F32), pltpu.VMEM((TT, TT), I32)],
        compiler_params=pltpu.CompilerParams(dimension_semantics=("arbitrary", "arbitrary"),
                                             vmem_limit_bytes=VMEM_LIMIT),
        name="mix_bwd",
    )(x, q, zb, vi, g, y, of, mods, lbl, hgn, w_out_bf, nf, wr_pad, s0_b, wc)


def _route_body(at_ref, idx_ref, inc_ref, col_ref, *, cap):
    ne, t = at_ref.shape[1], at_ref.shape[2]
    blk = 2 * LANES
    aff = at_ref[0]

    def count(m):
        return jnp.sum(jnp.where(m, 1.0, 0.0), axis=1, keepdims=True)

    def bit_step(i, thr):
        cand = thr | jnp.left_shift(jnp.int32(1), 30 - i)
        return jnp.where(count(aff >= lax.bitcast_convert_type(cand, F32)) >= cap, cand, thr)

    thr = lax.fori_loop(0, 31, bit_step, jnp.zeros((ne, 1), I32))
    gt = aff >= lax.bitcast_convert_type(thr + 1, F32)
    eq = (aff >= lax.bitcast_convert_type(thr, F32)) & jnp.logical_not(gt)
    need = cap - count(gt)

    ii = lax.broadcasted_iota(I32, (blk, blk), 0)
    jj = lax.broadcasted_iota(I32, (blk, blk), 1)
    upper = jnp.where(ii <= jj, 1.0, 0.0).astype(BF16)

    def prefix_blocks(m):
        mb = jnp.where(m, 1.0, 0.0).astype(BF16)
        carry = jnp.zeros((ne, 1), F32)
        out = []
        for kb in range(t // blk):
            p = _dot(mb[:, kb * blk:(kb + 1) * blk], upper) + carry
            out.append(p)
            carry = p[:, blk - 1:blk]
        return out

    eq_rank = jnp.concatenate(prefix_blocks(eq), axis=1)
    sel = gt | (eq & (eq_rank <= need))
    for kb, p in enumerate(prefix_blocks(sel)):
        for half in range(blk // LANES):
            inc_ref[kb * (blk // LANES) + half] = p[:, half * LANES:(half + 1) * LANES]

    n_lb = t // LANES
    lane = lax.broadcasted_iota(I32, (LANES, LANES), 1)
    col_ref[...] = jnp.zeros_like(col_ref)

    def expert_step(e, carry):
        for jg in range(cap // LANES):
            slot = (lax.broadcasted_iota(I32, (LANES, LANES), 0) + jg * LANES).astype(F32)

            def lane_block(lb, acc):
                r = inc_ref[lb, pl.ds(e, 1), :]
                return acc + jnp.where(r <= slot, 1.0, 0.0)

            acc = lax.fori_loop(0, n_lb, lane_block, jnp.zeros((LANES, LANES), F32), unroll=4)
            cnt = jnp.sum(acc, axis=1, keepdims=True)
            rows = slice(jg * LANES, (jg + 1) * LANES)
            col_ref[rows, :] = jnp.where(lane == e, cnt, col_ref[rows, :])
        return carry

    lax.fori_loop(0, ne, expert_step, 0)
    idx_ref[0] = col_ref[...].T[0:ne, :].astype(I32)


def _route(aff_t, cap):
    nb, ne, t = aff_t.shape
    return pl.pallas_call(
        functools.partial(_route_body, cap=cap),
        grid=(nb,),
        in_specs=[pl.BlockSpec((1, ne, t), lambda b: (b, 0, 0))],
        out_specs=pl.BlockSpec((1, ne, cap), lambda b: (b, 0, 0)),
        out_shape=jax.ShapeDtypeStruct((nb, ne, cap), I32),
        scratch_shapes=[pltpu.VMEM((t // LANES, ne, LANES), F32), pltpu.VMEM((cap, LANES), F32)],
        compiler_params=pltpu.CompilerParams(vmem_limit_bytes=VMEM_LIMIT),
        name="route",
    )(aff_t)


def _moe_body(idx_ref, hr_ref, ak_ref, wg_ref, wu_ref, wd_ref, x1_ref, mods_ref, nfin_ref,
              out_ref, acc_ref, rowg_ref, rows_ref, affg_ref, gate_ref, xs_ref, ys_ref, *, cap, n_ff):
    b = pl.program_id(0)
    s = pl.program_id(1)
    n_work = N_EXPERTS * n_ff
    e = s // n_ff
    ff = s % n_ff
    group = SUBLANES
    part_tok = cap // n_ff

    def list_base(expert):
        return (b * N_EXPERTS + expert) * cap

    def gather(base, j0, n):
        for u in range(n):
            tkn = idx_ref[base + j0 + u]
            rowg_ref[pl.ds(pl.multiple_of((j0 + u) * ACC_ROWS, ACC_ROWS), ACC_ROWS), :] = (
                hr_ref[0, pl.ds(pl.multiple_of(tkn * ACC_ROWS, ACC_ROWS), ACC_ROWS), :])
            affg_ref[pl.ds(j0 + u, 1), :] = ak_ref[0, pl.ds(tkn, 1), :]

    def scatter(base, j0, n):
        for g0 in range(0, n, group):
            dst = []
            val = []
            for u in range(g0, g0 + group):
                r = pl.multiple_of(idx_ref[base + j0 + u] * ACC_ROWS, ACC_ROWS)
                dst.append(r)
                val.append(acc_ref[pl.ds(r, ACC_ROWS), :]
                           + rows_ref[pl.ds(pl.multiple_of((j0 + u) * ACC_ROWS, ACC_ROWS), ACC_ROWS), :])
            for r, v in zip(dst, val):
                acc_ref[pl.ds(r, ACC_ROWS), :] = v

    def in_groups(fn, base):
        def step(jo, carry):
            fn(base, jo * group, group)
            return carry
        lax.fori_loop(0, cap // group, step, 0)

    @pl.when(s == 0)
    def _():
        acc_ref[...] = jnp.zeros_like(acc_ref)
        rows_ref[...] = jnp.zeros_like(rows_ref)
        ys_ref[...] = jnp.zeros_like(ys_ref)
        in_groups(gather, list_base(0))

    @pl.when((s < n_work) & (ff == 0))
    def _():
        for c in range(ACC_ROWS):
            xs_ref[:, c * LANES:(c + 1) * LANES] = rowg_ref[pl.ds(c, cap, stride=ACC_ROWS), :].astype(BF16)
        lane = lax.broadcasted_iota(I32, affg_ref.shape, 1)
        gate_ref[...] = jnp.sum(jnp.where(lane == e, affg_ref[...], 0.0), axis=1, keepdims=True)

    @pl.when(s < n_work)
    def _():
        gather(list_base(jnp.minimum(e + 1, N_EXPERTS - 1)), ff * part_tok, part_tok)
        scatter(list_base(jnp.maximum(e - 1, 0)), ff * part_tok, part_tok)
        xs = xs_ref[...]
        hid = _silu(_dot(xs, wg_ref[0])) * _dot(xs, wu_ref[0])
        ys_ref[...] = _dot(hid.astype(BF16), wd_ref[0]) + jnp.where(ff == 0, 0.0, ys_ref[...])

    @pl.when((s < n_work) & (ff == n_ff - 1))
    def _():
        gate = gate_ref[...]
        for c in range(ACC_ROWS):
            rows_ref[pl.ds(c, cap, stride=ACC_ROWS), :] = ys_ref[:, c * LANES:(c + 1) * LANES] * gate

    @pl.when(s == n_work)
    def _():
        in_groups(scatter, list_base(N_EXPERTS - 1))

    @pl.when(s >= n_work)
    def _():
        row0 = (s - n_work) * (EPI_TOK * ACC_ROWS)
        moe = jnp.concatenate([acc_ref[pl.ds(row0 + c, EPI_TOK, stride=ACC_ROWS), :] for c in range(ACC_ROWS)], axis=1)
        g2 = mods_ref[pl.ds(b, 1), 5 * D_MODEL:6 * D_MODEL]
        out_ref[0] = _rms(x1_ref[0] + g2 * moe, nfin_ref[...])


def _moe(idx_flat, hr, ak, wg, wu, wd, x1, mods, nfin, cap):
    nb, t, _ = x1.shape
    dff = wg.shape[2]
    n_ff = dff // DFF_TILE
    n_work = N_EXPERTS * n_ff
    n_epi = t // EPI_TOK

    def w_idx(s):
        sc = jnp.minimum(s, n_work - 1)
        return sc // n_ff, sc % n_ff

    def epi(b, s, idx):
        return (b, jnp.maximum(s - n_work, 0), 0)

    one = pl.Buffered(1)
    grid_spec = pltpu.PrefetchScalarGridSpec(
        num_scalar_prefetch=1,
        grid=(nb, n_work + n_epi),
        in_specs=[pl.BlockSpec((1, t * ACC_ROWS, LANES), lambda b, s, idx: (b, 0, 0), pipeline_mode=one),
                  pl.BlockSpec((1, t, LANES), lambda b, s, idx: (b, 0, 0), pipeline_mode=one),
                  pl.BlockSpec((1, D_MODEL, DFF_TILE), lambda b, s, idx: (w_idx(s)[0], 0, w_idx(s)[1])),
                  pl.BlockSpec((1, D_MODEL, DFF_TILE), lambda b, s, idx: (w_idx(s)[0], 0, w_idx(s)[1])),
                  pl.BlockSpec((1, DFF_TILE, D_MODEL), lambda b, s, idx: (w_idx(s)[0], w_idx(s)[1], 0)),
                  pl.BlockSpec((1, EPI_TOK, D_MODEL), epi),
                  pl.BlockSpec(mods.shape, lambda b, s, idx: (0, 0)),
                  pl.BlockSpec((1, D_MODEL), lambda b, s, idx: (0, 0))],
        out_specs=pl.BlockSpec((1, EPI_TOK, D_MODEL), epi),
        scratch_shapes=[pltpu.VMEM((t * ACC_ROWS, LANES), F32),
                        pltpu.VMEM((cap * ACC_ROWS, LANES), F32),
                        pltpu.VMEM((cap * ACC_ROWS, LANES), F32),
                        pltpu.VMEM((cap, LANES), F32),
                        pltpu.VMEM((cap, 1), F32),
                        pltpu.VMEM((cap, D_MODEL), BF16),
                        pltpu.VMEM((cap, D_MODEL), F32)])
    return pl.pallas_call(
        functools.partial(_moe_body, cap=cap, n_ff=n_ff),
        grid_spec=grid_spec,
        out_shape=jax.ShapeDtypeStruct((nb, t, D_MODEL), F32),
        compiler_params=pltpu.CompilerParams(dimension_semantics=("arbitrary", "arbitrary"),
                                             vmem_limit_bytes=VMEM_LIMIT),
        name="moe",
    )(idx_flat, hr, ak, wg, wu, wd, x1, mods, nfin)


def kernel(x, c, ctx, c_ctx, w_ada, b_ada, norm_mix, norm_ffn, w_in, lb_logits, hg_norm, conv_w,
           w_out, w_router, w_gate, w_up, w_down, norm_final):
    nb, t, d = x.shape
    assert d == D_MODEL and w_ada.shape[0] == 1 and nb < MOD_ROWS and t % TT == 0 and TT % GRID_W == 0
    assert w_in.shape[2] == 3 * HG_KEYS + 2 * HG_WIDTH + 3 * SC_WIDTH and w_router.shape[2] == N_EXPERTS
    cap = EC_CAPACITY * t // N_EXPERTS
    assert cap % LANES == 0 and w_gate.shape[3] % DFF_TILE == 0

    cc = jnp.concatenate([c, c_ctx[None, :], jnp.zeros((MOD_ROWS - nb - 1, d), F32)], axis=0)
    w_in_bf = w_in[0].astype(BF16)
    w_out_bf = w_out[0].astype(BF16)
    wr_pad = jnp.pad(w_router[0], ((0, 0), (0, LANES - N_EXPERTS)))
    nm, nf, nfin = norm_mix[0][None, :], norm_ffn[0][None, :], norm_final[None, :]
    hgn = hg_norm[0][None, :]

    mods = _adaln(cc, w_ada[0], b_ada[0][None, :])
    s0_f, s0_b = _ctx_states(ctx, mods, nm, w_in_bf, lb_logits, ctx_row=nb)
    steps = nb * (t // TT)
    sliced = lambda w: w.reshape((steps, w.shape[0] * w.shape[1] // steps) + w.shape[2:])
    q, zb, vi, g, y, of, wg, wu = _mix_fwd(x, mods, nm, w_in_bf, lb_logits, conv_w[0], s0_f,
                                           sliced(w_gate[0]), sliced(w_up[0]))
    x1, hr, aff_t, aff_k, wd = _mix_bwd(x, q, zb, vi, g, y, of, mods, lb_logits, hgn, w_out_bf, nf, wr_pad, s0_b,
                                        sliced(w_down[0]))
    wg, wu, wd = wg.reshape(w_gate[0].shape), wu.reshape(w_up[0].shape), wd.reshape(w_down[0].shape)
    idx = _route(aff_t, cap)
    return _moe(idx.reshape(-1), hr, aff_k, wg, wu, wd, x1, mods, nfin, cap)
```

```python
import functools

import jax
import jax.numpy as jnp
from jax import lax
from jax.experimental import pallas as pl
from jax.experimental.pallas import tpu as pltpu

F32 = jnp.float32
BF16 = jnp.bfloat16
I32 = jnp.int32

D_MODEL = 1024
N_MOD = 6
HG_HEADS = 4
HG_KDIM = 128
HG_VDIM = 128
HG_KEYS = HG_HEADS * HG_KDIM
HG_WIDTH = HG_HEADS * HG_VDIM
SC_WIDTH = D_MODEL - HG_WIDTH
GRID_W = 64
CHUNK = 32
N_EXPERTS = 16
EC_CAPACITY = 2
EPS = 1e-6

LANES = 128
SUBLANES = 8
VMEM_LIMIT = 62 * 1024 * 1024

TT = 256
NCH = TT // CHUNK
MOD_ROWS = 16
DFF_TILE = 1024
EPI_TOK = 256
ACC_ROWS = D_MODEL // LANES


def _dot(a, b):
    return jnp.dot(a, b, preferred_element_type=F32)


def _dot_nt(a, b):
    return lax.dot_general(a, b, (((1,), (1,)), ((), ())), preferred_element_type=F32)


def _dot_tn(a, b):
    return lax.dot_general(a, b, (((0,), (0,)), ((), ())), preferred_element_type=F32)


def _split(x):
    hi = x.astype(BF16)
    lo = (x - hi.astype(F32)).astype(BF16)
    return hi, lo


def _rms(x, gain):
    return x * lax.rsqrt(jnp.mean(x * x, axis=-1, keepdims=True) + EPS) * gain


def _silu(x):
    return x * jax.nn.sigmoid(x)


def _lower_bounds(lbl_ref):
    rows = [lbl_ref[l] for l in range(lbl_ref.shape[0])]
    m = functools.reduce(jnp.maximum, rows)
    es = [jnp.exp(r - m) for r in rows]
    sm0 = es[0] / functools.reduce(lambda a, b: a + b, es)
    return sm0[0:1, :], sm0[1:2, :]


def _ada_body(c_ref, w_ref, b_ref, o_ref):
    s_hi, s_lo = _split(_silu(c_ref[...]))
    w_hi, w_lo = _split(w_ref[...])
    o_ref[...] = _dot(s_hi, w_hi) + _dot(s_hi, w_lo) + _dot(s_lo, w_hi) + b_ref[...]


def _adaln(cc, w, b):
    n = w.shape[1]
    tn = n // 4
    return pl.pallas_call(
        _ada_body,
        grid=(n // tn,),
        in_specs=[pl.BlockSpec((MOD_ROWS, D_MODEL), lambda i: (0, 0)),
                  pl.BlockSpec((D_MODEL, tn), lambda i: (0, i)),
                  pl.BlockSpec((1, tn), lambda i: (0, i))],
        out_specs=pl.BlockSpec((MOD_ROWS, tn), lambda i: (0, i)),
        out_shape=jax.ShapeDtypeStruct((MOD_ROWS, n), F32),
        compiler_params=pltpu.CompilerParams(vmem_limit_bytes=VMEM_LIMIT),
        name="adaln",
    )(cc, w, b)


def _ctx_body(ctx_ref, mods_ref, nm_ref, wf_ref, wb_ref, wi_ref, lbl_ref, sf_ref, sb_ref, *, ctx_row):
    x = ctx_ref[0]
    n = x.shape[0]
    csh = mods_ref[ctx_row:ctx_row + 1, 0:D_MODEL]
    csc = mods_ref[ctx_row:ctx_row + 1, D_MODEL:2 * D_MODEL]
    hb = (_rms(x, nm_ref[...]) * (1.0 + csc) + csh).astype(BF16)
    zf = _dot(hb, wf_ref[...])
    zb = _dot(hb, wb_ref[...])
    vi = _dot(hb, wi_ref[...]).astype(BF16)
    lb_f, lb_b = _lower_bounds(lbl_ref)
    f_f = lb_f + (1.0 - lb_f) * jax.nn.sigmoid(zf)
    f_b = lb_b + (1.0 - lb_b) * jax.nn.sigmoid(zb)
    ii = lax.broadcasted_iota(I32, (n, n), 0)
    jj = lax.broadcasted_iota(I32, (n, n), 1)
    after = jnp.where(jj > ii, 1.0, 0.0).astype(BF16)
    before = jnp.where(jj < ii, 1.0, 0.0).astype(BF16)
    hi, lo = _split(jnp.log(f_f))
    kd_f = ((1.0 - f_f) * jnp.exp(_dot(after, hi) + _dot(after, lo))).astype(BF16)
    hi, lo = _split(jnp.log(f_b))
    kd_b = ((1.0 - f_b) * jnp.exp(_dot(before, hi) + _dot(before, lo))).astype(BF16)
    for h in range(HG_HEADS):
        ks = slice(h * HG_KDIM, (h + 1) * HG_KDIM)
        vs = slice(h * HG_VDIM, (h + 1) * HG_VDIM)
        sf_ref[0, h] = _dot_tn(vi[:, vs], kd_f[:, ks])
        sb_ref[0, h] = _dot_tn(vi[:, vs], kd_b[:, ks])


def _ctx_states(ctx, mods, nm, w_in_bf, lbl, ctx_row):
    nb, n, _ = ctx.shape
    st = jax.ShapeDtypeStruct((nb, HG_HEADS, HG_VDIM, HG_KDIM), F32)
    st_spec = pl.BlockSpec((1, HG_HEADS, HG_VDIM, HG_KDIM), lambda b: (b, 0, 0, 0))
    wcol = lambda k: pl.BlockSpec((D_MODEL, HG_KEYS), lambda b, k=k: (0, k))
    return pl.pallas_call(
        functools.partial(_ctx_body, ctx_row=ctx_row),
        grid=(nb,),
        in_specs=[pl.BlockSpec((1, n, D_MODEL), lambda b: (b, 0, 0)),
                  pl.BlockSpec(mods.shape, lambda b: (0, 0)),
                  pl.BlockSpec((1, D_MODEL), lambda b: (0, 0)),
                  wcol(1), wcol(2), wcol(3),
                  pl.BlockSpec(lbl.shape, lambda b: (0, 0, 0))],
        out_specs=[st_spec, st_spec],
        out_shape=[st, st],
        compiler_params=pltpu.CompilerParams(vmem_limit_bytes=VMEM_LIMIT),
        name="ctx_states",
    )(ctx, mods, nm, w_in_bf, w_in_bf, w_in_bf, lbl)


def _init_maps(lm_ref, tri_ref, reverse):
    lm = _level_map(reverse)
    lm_ref[...] = lm
    tri_ref[...] = jnp.where(lm == 1, 1.0, 0.0).astype(BF16)


def _level_map(reverse):
    ii = lax.broadcasted_iota(I32, (TT, TT), 0)
    jj = lax.broadcasted_iota(I32, (TT, TT), 1)
    if reverse:
        ii = TT - 1 - ii
        jj = TT - 1 - jj
    sh0 = CHUNK.bit_length() - 1
    ci = ii >> sh0
    cj = jj >> sh0
    lm = jnp.where((ci == cj) & (jj <= ii), 1, 0)
    for lvl in range(1, NCH.bit_length()):
        m = ((ci >> lvl) == (cj >> lvl)) & (((ci >> (lvl - 1)) & 1) == 1) & (((cj >> (lvl - 1)) & 1) == 0)
        lm = jnp.where(m, lvl + 1, lm)
    return lm


def _gla_tile(q, zf, v, lb, st_ref, lm, tri, reverse):
    n_lvl = NCH.bit_length() - 1
    f = lb + (1.0 - lb) * jax.nn.sigmoid(zf)
    k = 1.0 - f
    hi, lo = _split(jnp.log(f))
    cum = _dot(tri, hi) + _dot(tri, lo)
    phys = list(range(NCH - 1, -1, -1)) if reverse else list(range(NCH))
    last = 0 if reverse else CHUNK - 1
    tot = [cum[a * CHUNK + last:a * CHUNK + last + 1, :] for a in range(NCH)]
    bnd = [jnp.zeros_like(tot[0])]
    for c in range(NCH):
        bnd.append(bnd[-1] + tot[phys[c]])
    q0 = q * jnp.exp(cum)
    kinv = (k * jnp.exp(-cum)).astype(BF16)
    qs = [[None] * NCH for _ in range(n_lvl + 1)]
    ks = [[None] * NCH for _ in range(n_lvl + 1)]
    for c in range(NCH):
        a = phys[c]
        rows = slice(a * CHUNK, (a + 1) * CHUNK)
        q0a = q0[rows]
        kend = k[rows] * jnp.exp(tot[a] - cum[rows])
        zero = jnp.zeros_like(q0a)
        for lvl in range(1, n_lvl + 1):
            s = 1 << (lvl - 1)
            mid = (c // (2 * s)) * 2 * s + s
            right = (c // s) % 2 == 1
            qs[lvl - 1][a] = q0a * jnp.exp(bnd[c] - bnd[mid]) if right else zero
            ks[lvl - 1][a] = zero if right else kend * jnp.exp(bnd[mid] - bnd[c + 1])
        qs[n_lvl][a] = q0a * jnp.exp(bnd[c])
        ks[n_lvl][a] = kend * jnp.exp(bnd[NCH] - bnd[c + 1])
    qs = [jnp.concatenate(p, axis=0).astype(BF16) for p in qs]
    ks = [jnp.concatenate(p, axis=0).astype(BF16) for p in ks]
    q0 = q0.astype(BF16)
    vb = v.astype(BF16)
    dec = jnp.exp(bnd[NCH])
    outs = []
    for h in range(HG_HEADS):
        kk = slice(h * HG_KDIM, (h + 1) * HG_KDIM)
        vv = slice(h * HG_VDIM, (h + 1) * HG_VDIM)
        sc = jnp.where(lm == 1, _dot_nt(q0[:, kk], kinv[:, kk]), 0.0)
        for lvl in range(n_lvl):
            sc = jnp.where(lm == lvl + 2, _dot_nt(qs[lvl][:, kk], ks[lvl][:, kk]), sc)
        st = st_ref[h]
        outs.append(_dot(sc.astype(BF16), vb[:, vv]) + _dot_nt(qs[n_lvl][:, kk], st.astype(BF16)))
        st_ref[h] = st * dec[:, kk] + _dot_tn(vb[:, vv], ks[n_lvl][:, kk])
    return jnp.concatenate(outs, axis=1)


def _mix_fwd_body(x_ref, mods_ref, nm_ref, w_ref, lbl_ref, cw_ref, s0_ref, wa_ref, wb_ref,
                  q_ref, zb_ref, i_ref, g_ref, y_ref, of_ref, wa_bf_ref, wb_bf_ref, st_ref, lm_ref, tri_ref):
    b = pl.program_id(0)
    j = pl.program_id(1)
    wa_bf_ref[...] = wa_ref[...].astype(BF16)
    wb_bf_ref[...] = wb_ref[...].astype(BF16)

    @pl.when((b == 0) & (j == 0))
    def _():
        _init_maps(lm_ref, tri_ref, False)

    @pl.when(j == 0)
    def _():
        st_ref[...] = s0_ref[0]

    sh = mods_ref[pl.ds(b, 1), 0:D_MODEL]
    sc = mods_ref[pl.ds(b, 1), D_MODEL:2 * D_MODEL]
    hx = (_rms(x_ref[0], nm_ref[...]) * (1.0 + sc) + sh).astype(BF16)
    w = HG_KEYS
    proj = lambda k: _dot(hx, w_ref[:, k * w:(k + 1) * w])
    q = proj(0)
    zff = proj(1)
    zb_ref[0] = proj(2)
    vi = proj(3)
    g_ref[0] = proj(4).astype(BF16)
    gate_b = proj(5)
    u = proj(6) * proj(7)
    col = lax.broadcasted_iota(I32, (TT, 1), 0) & (GRID_W - 1)
    up = jnp.where(col != 0, pltpu.roll(u, 1, axis=0), 0.0)
    un = jnp.where(col != GRID_W - 1, pltpu.roll(u, TT - 1, axis=0), 0.0)
    cw = cw_ref[...]
    y_ref[0] = (gate_b * (cw[0:1] * up + cw[1:2] * u + cw[2:3] * un)).astype(BF16)
    lb_f, _ = _lower_bounds(lbl_ref)
    of_ref[0] = _gla_tile(q, zff, vi, lb_f, st_ref, lm_ref[...], tri_ref[...], False)
    q_ref[0] = q.astype(BF16)
    i_ref[0] = vi.astype(BF16)


def _mix_fwd(x, mods, nm, w_in_bf, lbl, cw, s0_f, wa, wb):
    nb, t, _ = x.shape
    nt = t // TT
    tok = lambda dt: jax.ShapeDtypeStruct((nb, t, HG_WIDTH), dt)
    tspec = pl.BlockSpec((1, TT, HG_WIDTH), lambda b, j: (b, j, 0))
    cast_spec = lambda a: pl.BlockSpec((1,) + a.shape[1:], lambda b, j: (b * nt + j, 0, 0))
    cast_out = lambda a: jax.ShapeDtypeStruct(a.shape, BF16)
    return pl.pallas_call(
        _mix_fwd_body,
        grid=(nb, nt),
        in_specs=[pl.BlockSpec((1, TT, D_MODEL), lambda b, j: (b, j, 0)),
                  pl.BlockSpec(mods.shape, lambda b, j: (0, 0)),
                  pl.BlockSpec((1, D_MODEL), lambda b, j: (0, 0)),
                  pl.BlockSpec(w_in_bf.shape, lambda b, j: (0, 0)),
                  pl.BlockSpec(lbl.shape, lambda b, j: (0, 0, 0)),
                  pl.BlockSpec(cw.shape, lambda b, j: (0, 0)),
                  pl.BlockSpec((1, HG_HEADS, HG_VDIM, HG_KDIM), lambda b, j: (b, 0, 0, 0)),
                  cast_spec(wa), cast_spec(wb)],
        out_specs=[tspec] * 6 + [cast_spec(wa), cast_spec(wb)],
        out_shape=[tok(BF16), tok(F32), tok(BF16), tok(BF16), tok(BF16), tok(F32), cast_out(wa), cast_out(wb)],
        scratch_shapes=[pltpu.VMEM((HG_HEADS, HG_VDIM, HG_KDIM), F32), pltpu.VMEM((TT, TT), I32),
                        pltpu.VMEM((TT, TT), BF16)],
        compiler_params=pltpu.CompilerParams(dimension_semantics=("arbitrary", "arbitrary"),
                                             vmem_limit_bytes=VMEM_LIMIT),
        name="mix_fwd",
    )(x, mods, nm, w_in_bf, lbl, cw, s0_f, wa, wb)


def _mix_bwd_body(x_ref, q_ref, zb_ref, i_ref, g_ref, y_ref, of_ref, mods_ref, lbl_ref, hgn_ref,
                  wo_ref, nf_ref, wr_ref, s0_ref, wc_ref,
                  x1_ref, hr_ref, at_ref, ak_ref, wc_bf_ref, st_ref, lm_ref, tri_ref):
    b = pl.program_id(0)
    j = pl.program_id(1)
    wc_bf_ref[...] = wc_ref[...].astype(BF16)

    @pl.when((b == 0) & (j == 0))
    def _():
        _init_maps(lm_ref, tri_ref, True)

    @pl.when(j == 0)
    def _():
        st_ref[...] = s0_ref[0]

    _, lb_b = _lower_bounds(lbl_ref)
    o_b = _gla_tile(q_ref[0].astype(F32), zb_ref[0], i_ref[0], lb_b, st_ref, lm_ref[...], tri_ref[...], True)
    o = of_ref[0] + o_b
    hgn = hgn_ref[...]
    heads = [_rms(o[:, h * HG_VDIM:(h + 1) * HG_VDIM], hgn) for h in range(HG_HEADS)]
    g = g_ref[0].astype(F32)
    ohg = jnp.concatenate(heads, axis=1) * _silu(g)
    cat = jnp.concatenate([ohg.astype(BF16), y_ref[0]], axis=1)
    mx = _dot(cat, wo_ref[...])
    mod = lambda k: mods_ref[pl.ds(b, 1), k * D_MODEL:(k + 1) * D_MODEL]
    x1 = x_ref[0] + mod(2) * mx
    x1_ref[0] = x1
    h2 = _rms(x1, nf_ref[...]) * (1.0 + mod(4)) + mod(3)
    for c in range(ACC_ROWS):
        hr_ref[0, pl.ds(c, TT, stride=ACC_ROWS), :] = h2[:, c * LANES:(c + 1) * LANES]
    r = _dot(jnp.concatenate(_split(h2), axis=0), wr_ref[...])
    lg = r[:TT, :LANES] + r[:TT, LANES:] + r[TT:, :LANES]
    lane = lax.broadcasted_iota(I32, lg.shape, 1)
    lg = jnp.where(lane < N_EXPERTS, lg, -jnp.inf)
    e = jnp.exp(lg - jnp.max(lg, axis=-1, keepdims=True))
    aff = e / jnp.sum(e, axis=-1, keepdims=True)
    ak_ref[0] = aff
    at_ref[0] = aff.T[0:N_EXPERTS, :]


def _mix_bwd(x, q, zb, vi, g, y, of, mods, lbl, hgn, w_out_bf, nf, wr_pad, s0_b, wc):
    nb, t, _ = x.shape
    nt = t // TT
    cast_spec = pl.BlockSpec((1,) + wc.shape[1:], lambda b, j: (b * nt + j, 0, 0))
    rev = lambda b, j: (b, nt - 1 - j, 0)
    tspec = pl.BlockSpec((1, TT, HG_WIDTH), rev)
    xspec = pl.BlockSpec((1, TT, D_MODEL), rev)
    full2 = lambda a: pl.BlockSpec(a.shape, lambda b, j: (0, 0))
    return pl.pallas_call(
        _mix_bwd_body,
        grid=(nb, nt),
        in_specs=[xspec, tspec, tspec, tspec, tspec, tspec, tspec,
                  full2(mods), pl.BlockSpec(lbl.shape, lambda b, j: (0, 0, 0)), full2(hgn),
                  full2(w_out_bf), full2(nf), full2(wr_pad),
                  pl.BlockSpec((1, HG_HEADS, HG_VDIM, HG_KDIM), lambda b, j: (b, 0, 0, 0)),
                  cast_spec],
        out_specs=[xspec,
                   pl.BlockSpec((1, TT * ACC_ROWS, LANES), rev),
                   pl.BlockSpec((1, N_EXPERTS, TT), lambda b, j: (b, 0, nt - 1 - j)),
                   pl.BlockSpec((1, TT, LANES), rev),
                   cast_spec],
        out_shape=[jax.ShapeDtypeStruct((nb, t, D_MODEL), F32),
                   jax.ShapeDtypeStruct((nb, t * ACC_ROWS, LANES), F32),
                   jax.ShapeDtypeStruct((nb, N_EXPERTS, t), F32),
                   jax.ShapeDtypeStruct((nb, t, LANES), F32),
                   jax.ShapeDtypeStruct(wc.shape, BF16)],
        scratch_shapes=[pltpu.VMEM((HG_HEADS, HG_VDIM, HG_KDIM), F32), pltpu.VMEM((TT, TT), I32),
                        pltpu.VMEM((TT, TT), BF16)],
        compiler_params=pltpu.CompilerParams(dimension_semantics=("arbitrary", "arbitrary"),
                                             vmem_limit_bytes=VMEM_LIMIT),
        name="mix_bwd",
    )(x, q, zb, vi, g, y, of, mods, lbl, hgn, w_out_bf, nf, wr_pad, s0_b, wc)


def _route_body(at_ref, idx_ref, inc_ref, col_ref, *, cap):
    ne, t = at_ref.shape[1], at_ref.shape[2]
    blk = 2 * LANES
    aff = at_ref[0]

    def count(m):
        return jnp.sum(jnp.where(m, 1.0, 0.0), axis=1, keepdims=True)

    def bit_step(i, thr):
        cand = thr | jnp.left_shift(jnp.int32(1), 30 - i)
        return jnp.where(count(aff >= lax.bitcast_convert_type(cand, F32)) >= cap, cand, thr)

    thr = lax.fori_loop(0, 31, bit_step, jnp.zeros((ne, 1), I32))
    gt = aff >= lax.bitcast_convert_type(thr + 1, F32)
    eq = (aff >= lax.bitcast_convert_type(thr, F32)) & jnp.logical_not(gt)
    need = cap - count(gt)

    ii = lax.broadcasted_iota(I32, (blk, blk), 0)
    jj = lax.broadcasted_iota(I32, (blk, blk), 1)
    upper = jnp.where(ii <= jj, 1.0, 0.0).astype(BF16)

    def prefix_blocks(m):
        mb = jnp.where(m, 1.0, 0.0).astype(BF16)
        carry = jnp.zeros((ne, 1), F32)
        out = []
        for kb in range(t // blk):
            p = _dot(mb[:, kb * blk:(kb + 1) * blk], upper) + carry
            out.append(p)
            carry = p[:, blk - 1:blk]
        return out

    eq_rank = jnp.concatenate(prefix_blocks(eq), axis=1)
    sel = gt | (eq & (eq_rank <= need))
    for kb, p in enumerate(prefix_blocks(sel)):
        for half in range(blk // LANES):
            inc_ref[kb * (blk // LANES) + half] = p[:, half * LANES:(half + 1) * LANES]

    n_lb = t // LANES
    lane = lax.broadcasted_iota(I32, (LANES, LANES), 1)
    col_ref[...] = jnp.zeros_like(col_ref)

    def expert_step(e, carry):
        for jg in range(cap // LANES):
            slot = (lax.broadcasted_iota(I32, (LANES, LANES), 0) + jg * LANES).astype(F32)

            def lane_block(lb, acc):
                r = inc_ref[lb, pl.ds(e, 1), :]
                return acc + jnp.where(r <= slot, 1.0, 0.0)

            acc = lax.fori_loop(0, n_lb, lane_block, jnp.zeros((LANES, LANES), F32), unroll=4)
            cnt = jnp.sum(acc, axis=1, keepdims=True)
            rows = slice(jg * LANES, (jg + 1) * LANES)
            col_ref[rows, :] = jnp.where(lane == e, cnt, col_ref[rows, :])
        return carry

    lax.fori_loop(0, ne, expert_step, 0)
    idx_ref[0] = col_ref[...].T[0:ne, :].astype(I32)


def _route(aff_t, cap):
    nb, ne, t = aff_t.shape
    return pl.pallas_call(
        functools.partial(_route_body, cap=cap),
        grid=(nb,),
        in_specs=[pl.BlockSpec((1, ne, t), lambda b: (b, 0, 0))],
        out_specs=pl.BlockSpec((1, ne, cap), lambda b: (b, 0, 0)),
        out_shape=jax.ShapeDtypeStruct((nb, ne, cap), I32),
        scratch_shapes=[pltpu.VMEM((t // LANES, ne, LANES), F32), pltpu.VMEM((cap, LANES), F32)],
        compiler_params=pltpu.CompilerParams(vmem_limit_bytes=VMEM_LIMIT),
        name="route",
    )(aff_t)


def _moe_body(idx_ref, hr_ref, ak_ref, wg_ref, wu_ref, wd_ref, x1_ref, mods_ref, nfin_ref,
              out_ref, acc_ref, rowg_ref, rows_ref, affg_ref, gate_ref, xs_ref, ys_ref, *, cap, n_ff):
    b = pl.program_id(0)
    s = pl.program_id(1)
    n_work = N_EXPERTS * n_ff
    e = s // n_ff
    ff = s % n_ff
    group = SUBLANES
    part_tok = cap // n_ff

    def list_base(expert):
        return (b * N_EXPERTS + expert) * cap

    def gather(base, j0, n):
        for u in range(n):
            tkn = idx_ref[base + j0 + u]
            rowg_ref[pl.ds(pl.multiple_of((j0 + u) * ACC_ROWS, ACC_ROWS), ACC_ROWS), :] = (
                hr_ref[0, pl.ds(pl.multiple_of(tkn * ACC_ROWS, ACC_ROWS), ACC_ROWS), :])
            affg_ref[pl.ds(j0 + u, 1), :] = ak_ref[0, pl.ds(tkn, 1), :]

    def scatter(base, j0, n):
        for g0 in range(0, n, group):
            dst = []
            val = []
            for u in range(g0, g0 + group):
                r = pl.multiple_of(idx_ref[base + j0 + u] * ACC_ROWS, ACC_ROWS)
                dst.append(r)
                val.append(acc_ref[pl.ds(r, ACC_ROWS), :]
                           + rows_ref[pl.ds(pl.multiple_of((j0 + u) * ACC_ROWS, ACC_ROWS), ACC_ROWS), :])
            for r, v in zip(dst, val):
                acc_ref[pl.ds(r, ACC_ROWS), :] = v

    def in_groups(fn, base):
        def step(jo, carry):
            fn(base, jo * group, group)
            return carry
        lax.fori_loop(0, cap // group, step, 0)

    @pl.when(s == 0)
    def _():
        acc_ref[...] = jnp.zeros_like(acc_ref)
        rows_ref[...] = jnp.zeros_like(rows_ref)
        ys_ref[...] = jnp.zeros_like(ys_ref)
        in_groups(gather, list_base(0))

    @pl.when((s < n_work) & (ff == 0))
    def _():
        for c in range(ACC_ROWS):
            xs_ref[:, c * LANES:(c + 1) * LANES] = rowg_ref[pl.ds(c, cap, stride=ACC_ROWS), :].astype(BF16)
        lane = lax.broadcasted_iota(I32, affg_ref.shape, 1)
        gate_ref[...] = jnp.sum(jnp.where(lane == e, affg_ref[...], 0.0), axis=1, keepdims=True)

    @pl.when(s < n_work)
    def _():
        gather(list_base(jnp.minimum(e + 1, N_EXPERTS - 1)), ff * part_tok, part_tok)
        scatter(list_base(jnp.maximum(e - 1, 0)), ff * part_tok, part_tok)
        xs = xs_ref[...]
        hid = _silu(_dot(xs, wg_ref[0])) * _dot(xs, wu_ref[0])
        ys_ref[...] = _dot(hid.astype(BF16), wd_ref[0]) + jnp.where(ff == 0, 0.0, ys_ref[...])

    @pl.when((s < n_work) & (ff == n_ff - 1))
    def _():
        gate = gate_ref[...]
        for c in range(ACC_ROWS):
            rows_ref[pl.ds(c, cap, stride=ACC_ROWS), :] = ys_ref[:, c * LANES:(c + 1) * LANES] * gate

    @pl.when(s == n_work)
    def _():
        in_groups(scatter, list_base(N_EXPERTS - 1))

    @pl.when(s >= n_work)
    def _():
        row0 = (s - n_work) * (EPI_TOK * ACC_ROWS)
        moe = jnp.concatenate([acc_ref[pl.ds(row0 + c, EPI_TOK, stride=ACC_ROWS), :] for c in range(ACC_ROWS)], axis=1)
        g2 = mods_ref[pl.ds(b, 1), 5 * D_MODEL:6 * D_MODEL]
        out_ref[0] = _rms(x1_ref[0] + g2 * moe, nfin_ref[...])


def _moe(idx_flat, hr, ak, wg, wu, wd, x1, mods, nfin, cap):
    nb, t, _ = x1.shape
    dff = wg.shape[2]
    n_ff = dff // DFF_TILE
    n_work = N_EXPERTS * n_ff
    n_epi = t // EPI_TOK

    def w_idx(s):
        sc = jnp.minimum(s, n_work - 1)
        return sc // n_ff, sc % n_ff

    def epi(b, s, idx):
        return (b, jnp.maximum(s - n_work, 0), 0)

    one = pl.Buffered(1)
    grid_spec = pltpu.PrefetchScalarGridSpec(
        num_scalar_prefetch=1,
        grid=(nb, n_work + n_epi),
        in_specs=[pl.BlockSpec((1, t * ACC_ROWS, LANES), lambda b, s, idx: (b, 0, 0), pipeline_mode=one),
                  pl.BlockSpec((1, t, LANES), lambda b, s, idx: (b, 0, 0), pipeline_mode=one),
                  pl.BlockSpec((1, D_MODEL, DFF_TILE), lambda b, s, idx: (w_idx(s)[0], 0, w_idx(s)[1])),
                  pl.BlockSpec((1, D_MODEL, DFF_TILE), lambda b, s, idx: (w_idx(s)[0], 0, w_idx(s)[1])),
                  pl.BlockSpec((1, DFF_TILE, D_MODEL), lambda b, s, idx: (w_idx(s)[0], w_idx(s)[1], 0)),
                  pl.BlockSpec((1, EPI_TOK, D_MODEL), epi),
                  pl.BlockSpec(mods.shape, lambda b, s, idx: (0, 0)),
                  pl.BlockSpec((1, D_MODEL), lambda b, s, idx: (0, 0))],
        out_specs=pl.BlockSpec((1, EPI_TOK, D_MODEL), epi),
        scratch_shapes=[pltpu.VMEM((t * ACC_ROWS, LANES), F32),
                        pltpu.VMEM((cap * ACC_ROWS, LANES), F32),
                        pltpu.VMEM((cap * ACC_ROWS, LANES), F32),
                        pltpu.VMEM((cap, LANES), F32),
                        pltpu.VMEM((cap, 1), F32),
                        pltpu.VMEM((cap, D_MODEL), BF16),
                        pltpu.VMEM((cap, D_MODEL), F32)])
    return pl.pallas_call(
        functools.partial(_moe_body, cap=cap, n_ff=n_ff),
        grid_spec=grid_spec,
        out_shape=jax.ShapeDtypeStruct((nb, t, D_MODEL), F32),
        compiler_params=pltpu.CompilerParams(dimension_semantics=("arbitrary", "arbitrary"),
                                             vmem_limit_bytes=VMEM_LIMIT),
        name="moe",
    )(idx_flat, hr, ak, wg, wu, wd, x1, mods, nfin)


def kernel(x, c, ctx, c_ctx, w_ada, b_ada, norm_mix, norm_ffn, w_in, lb_logits, hg_norm, conv_w,
           w_out, w_router, w_gate, w_up, w_down, norm_final):
    nb, t, d = x.shape
    assert d == D_MODEL and w_ada.shape[0] == 1 and nb < MOD_ROWS and t % TT == 0 and TT % GRID_W == 0
    assert w_in.shape[2] == 3 * HG_KEYS + 2 * HG_WIDTH + 3 * SC_WIDTH and w_router.shape[2] == N_EXPERTS
    cap = EC_CAPACITY * t // N_EXPERTS
    assert cap % LANES == 0 and w_gate.shape[3] % DFF_TILE == 0

    cc = jnp.concatenate([c, c_ctx[None, :], jnp.zeros((MOD_ROWS - nb - 1, d), F32)], axis=0)
    w_in_bf = w_in[0].astype(BF16)
    w_out_bf = w_out[0].astype(BF16)
    wr_hi, wr_lo = _split(jnp.pad(w_router[0], ((0, 0), (0, LANES - N_EXPERTS))))
    wr_pad = jnp.concatenate([wr_hi, wr_lo], axis=1)
    nm, nf, nfin = norm_mix[0][None, :], norm_ffn[0][None, :], norm_final[None, :]
    hgn = hg_norm[0][None, :]

    mods = _adaln(cc, w_ada[0], b_ada[0][None, :])
    s0_f, s0_b = _ctx_states(ctx, mods, nm, w_in_bf, lb_logits, ctx_row=nb)
    steps = nb * (t // TT)
    sliced = lambda w: w.reshape((steps, w.shape[0] * w.shape[1] // steps) + w.shape[2:])
    q, zb, vi, g, y, of, wg, wu = _mix_fwd(x, mods, nm, w_in_bf, lb_logits, conv_w[0], s0_f,
                                           sliced(w_gate[0]), sliced(w_up[0]))
    x1, hr, aff_t, aff_k, wd = _mix_bwd(x, q, zb, vi, g, y, of, mods, lb_logits, hgn, w_out_bf, nf, wr_pad, s0_b,
                                        sliced(w_down[0]))
    wg, wu, wd = wg.reshape(w_gate[0].shape), wu.reshape(w_up[0].shape), wd.reshape(w_down[0].shape)
    idx = _route(aff_t, cap)
    return _moe(idx.reshape(-1), hr, aff_k, wg, wu, wd, x1, mods, nfin, cap)
```

```python
import functools

import jax
import jax.numpy as jnp
from jax import lax
from jax.experimental import pallas as pl
from jax.experimental.pallas import tpu as pltpu

F32 = jnp.float32
BF16 = jnp.bfloat16
I32 = jnp.int32

D_MODEL = 1024
N_MOD = 6
HG_HEADS = 4
HG_KDIM = 128
HG_VDIM = 128
HG_KEYS = HG_HEADS * HG_KDIM
HG_WIDTH = HG_HEADS * HG_VDIM
SC_WIDTH = D_MODEL - HG_WIDTH
GRID_W = 64
CHUNK = 32
N_EXPERTS = 16
EC_CAPACITY = 2
EPS = 1e-6

LANES = 128
SUBLANES = 8
VMEM_LIMIT = 62 * 1024 * 1024

TT = 256
NCH = TT // CHUNK
MOD_ROWS = 16
DFF_TILE = 1024
EPI_TOK = 256
ACC_ROWS = D_MODEL // LANES


def _dot(a, b):
    return jnp.dot(a, b, preferred_element_type=F32)


def _dot_nt(a, b):
    return lax.dot_general(a, b, (((1,), (1,)), ((), ())), preferred_element_type=F32)


def _dot_tn(a, b):
    return lax.dot_general(a, b, (((0,), (0,)), ((), ())), preferred_element_type=F32)


def _split(x):
    hi = x.astype(BF16)
    lo = (x - hi.astype(F32)).astype(BF16)
    return hi, lo


def _rms(x, gain):
    return x * lax.rsqrt(jnp.mean(x * x, axis=-1, keepdims=True) + EPS) * gain


def _silu(x):
    return x * jax.nn.sigmoid(x)


def _lower_bounds(lbl_ref):
    rows = [lbl_ref[l] for l in range(lbl_ref.shape[0])]
    m = functools.reduce(jnp.maximum, rows)
    es = [jnp.exp(r - m) for r in rows]
    sm0 = es[0] / functools.reduce(lambda a, b: a + b, es)
    return sm0[0:1, :], sm0[1:2, :]


def _ada_body(c_ref, w_ref, b_ref, o_ref):
    s_hi, s_lo = _split(_silu(c_ref[...]))
    w_hi, w_lo = _split(w_ref[...])
    o_ref[...] = _dot(s_hi, w_hi) + _dot(s_hi, w_lo) + _dot(s_lo, w_hi) + b_ref[...]


def _adaln(cc, w, b):
    n = w.shape[1]
    tn = n // 4
    return pl.pallas_call(
        _ada_body,
        grid=(n // tn,),
        in_specs=[pl.BlockSpec((MOD_ROWS, D_MODEL), lambda i: (0, 0)),
                  pl.BlockSpec((D_MODEL, tn), lambda i: (0, i)),
                  pl.BlockSpec((1, tn), lambda i: (0, i))],
        out_specs=pl.BlockSpec((MOD_ROWS, tn), lambda i: (0, i)),
        out_shape=jax.ShapeDtypeStruct((MOD_ROWS, n), F32),
        compiler_params=pltpu.CompilerParams(vmem_limit_bytes=VMEM_LIMIT),
        name="adaln",
    )(cc, w, b)


def _ctx_body(ctx_ref, mods_ref, nm_ref, wf_ref, wb_ref, wi_ref, lbl_ref, sf_ref, sb_ref, *, ctx_row):
    x = ctx_ref[0]
    n = x.shape[0]
    csh = mods_ref[ctx_row:ctx_row + 1, 0:D_MODEL]
    csc = mods_ref[ctx_row:ctx_row + 1, D_MODEL:2 * D_MODEL]
    hb = (_rms(x, nm_ref[...]) * (1.0 + csc) + csh).astype(BF16)
    zf = _dot(hb, wf_ref[...])
    zb = _dot(hb, wb_ref[...])
    vi = _dot(hb, wi_ref[...]).astype(BF16)
    lb_f, lb_b = _lower_bounds(lbl_ref)
    f_f = lb_f + (1.0 - lb_f) * jax.nn.sigmoid(zf)
    f_b = lb_b + (1.0 - lb_b) * jax.nn.sigmoid(zb)
    ii = lax.broadcasted_iota(I32, (n, n), 0)
    jj = lax.broadcasted_iota(I32, (n, n), 1)
    after = jnp.where(jj > ii, 1.0, 0.0).astype(BF16)
    before = jnp.where(jj < ii, 1.0, 0.0).astype(BF16)
    hi, lo = _split(jnp.log(f_f))
    kd_f = ((1.0 - f_f) * jnp.exp(_dot(after, hi) + _dot(after, lo))).astype(BF16)
    hi, lo = _split(jnp.log(f_b))
    kd_b = ((1.0 - f_b) * jnp.exp(_dot(before, hi) + _dot(before, lo))).astype(BF16)
    for h in range(HG_HEADS):
        ks = slice(h * HG_KDIM, (h + 1) * HG_KDIM)
        vs = slice(h * HG_VDIM, (h + 1) * HG_VDIM)
        sf_ref[0, h] = _dot_tn(vi[:, vs], kd_f[:, ks])
        sb_ref[0, h] = _dot_tn(vi[:, vs], kd_b[:, ks])


def _ctx_states(ctx, mods, nm, w_in_bf, lbl, ctx_row):
    nb, n, _ = ctx.shape
    st = jax.ShapeDtypeStruct((nb, HG_HEADS, HG_VDIM, HG_KDIM), F32)
    st_spec = pl.BlockSpec((1, HG_HEADS, HG_VDIM, HG_KDIM), lambda b: (b, 0, 0, 0))
    wcol = lambda k: pl.BlockSpec((D_MODEL, HG_KEYS), lambda b, k=k: (0, k))
    return pl.pallas_call(
        functools.partial(_ctx_body, ctx_row=ctx_row),
        grid=(nb,),
        in_specs=[pl.BlockSpec((1, n, D_MODEL), lambda b: (b, 0, 0)),
                  pl.BlockSpec(mods.shape, lambda b: (0, 0)),
                  pl.BlockSpec((1, D_MODEL), lambda b: (0, 0)),
                  wcol(1), wcol(2), wcol(3),
                  pl.BlockSpec(lbl.shape, lambda b: (0, 0, 0))],
        out_specs=[st_spec, st_spec],
        out_shape=[st, st],
        compiler_params=pltpu.CompilerParams(vmem_limit_bytes=VMEM_LIMIT),
        name="ctx_states",
    )(ctx, mods, nm, w_in_bf, w_in_bf, w_in_bf, lbl)


def _init_maps(lm_ref, tri_ref, reverse):
    lm = _level_map(reverse)
    lm_ref[...] = lm
    tri_ref[...] = jnp.where(lm == 1, 1.0, 0.0).astype(BF16)


def _level_map(reverse):
    ii = lax.broadcasted_iota(I32, (TT, TT), 0)
    jj = lax.broadcasted_iota(I32, (TT, TT), 1)
    if reverse:
        ii = TT - 1 - ii
        jj = TT - 1 - jj
    sh0 = CHUNK.bit_length() - 1
    ci = ii >> sh0
    cj = jj >> sh0
    lm = jnp.where((ci == cj) & (jj <= ii), 1, 0)
    for lvl in range(1, NCH.bit_length()):
        m = ((ci >> lvl) == (cj >> lvl)) & (((ci >> (lvl - 1)) & 1) == 1) & (((cj >> (lvl - 1)) & 1) == 0)
        lm = jnp.where(m, lvl + 1, lm)
    return lm


def _gla_tile(q, zf, v, lb, st_ref, lm, tri, reverse):
    n_lvl = NCH.bit_length() - 1
    f = lb + (1.0 - lb) * jax.nn.sigmoid(zf)
    k = 1.0 - f
    hi, lo = _split(jnp.log(f))
    cum = _dot(tri, hi) + _dot(tri, lo)
    phys = list(range(NCH - 1, -1, -1)) if reverse else list(range(NCH))
    last = 0 if reverse else CHUNK - 1
    tot = [cum[a * CHUNK + last:a * CHUNK + last + 1, :] for a in range(NCH)]
    bnd = [jnp.zeros_like(tot[0])]
    for c in range(NCH):
        bnd.append(bnd[-1] + tot[phys[c]])
    q0 = q * jnp.exp(cum)
    kinv = (k * jnp.exp(-cum)).astype(BF16)
    qs = [[None] * NCH for _ in range(n_lvl + 1)]
    ks = [[None] * NCH for _ in range(n_lvl + 1)]
    for c in range(NCH):
        a = phys[c]
        rows = slice(a * CHUNK, (a + 1) * CHUNK)
        q0a = q0[rows]
        kend = k[rows] * jnp.exp(tot[a] - cum[rows])
        zero = jnp.zeros_like(q0a)
        for lvl in range(1, n_lvl + 1):
            s = 1 << (lvl - 1)
            mid = (c // (2 * s)) * 2 * s + s
            right = (c // s) % 2 == 1
            qs[lvl - 1][a] = q0a * jnp.exp(bnd[c] - bnd[mid]) if right else zero
            ks[lvl - 1][a] = zero if right else kend * jnp.exp(bnd[mid] - bnd[c + 1])
        qs[n_lvl][a] = q0a * jnp.exp(bnd[c])
        ks[n_lvl][a] = kend * jnp.exp(bnd[NCH] - bnd[c + 1])
    qs = [jnp.concatenate(p, axis=0).astype(BF16) for p in qs]
    ks = [jnp.concatenate(p, axis=0).astype(BF16) for p in ks]
    q0 = q0.astype(BF16)
    vb = v.astype(BF16)
    dec = jnp.exp(bnd[NCH])
    outs = []
    for h in range(HG_HEADS):
        kk = slice(h * HG_KDIM, (h + 1) * HG_KDIM)
        vv = slice(h * HG_VDIM, (h + 1) * HG_VDIM)
        sc = jnp.where(lm == 1, _dot_nt(q0[:, kk], kinv[:, kk]), 0.0)
        for lvl in range(n_lvl):
            sc = jnp.where(lm == lvl + 2, _dot_nt(qs[lvl][:, kk], ks[lvl][:, kk]), sc)
        st = st_ref[h]
        outs.append(_dot(sc.astype(BF16), vb[:, vv]) + _dot_nt(qs[n_lvl][:, kk], st.astype(BF16)))
        st_ref[h] = st * dec[:, kk] + _dot_tn(vb[:, vv], ks[n_lvl][:, kk])
    return jnp.concatenate(outs, axis=1)


def _mix_fwd_body(x_ref, mods_ref, nm_ref, w_ref, lbl_ref, cw_ref, s0_ref, wa_ref, wb_ref,
                  q_ref, zb_ref, i_ref, g_ref, y_ref, of_ref, wa_bf_ref, wb_bf_ref, st_ref, lm_ref, tri_ref):
    b = pl.program_id(0)
    j = pl.program_id(1)
    wa_bf_ref[...] = wa_ref[...].astype(BF16)
    wb_bf_ref[...] = wb_ref[...].astype(BF16)

    @pl.when((b == 0) & (j == 0))
    def _():
        _init_maps(lm_ref, tri_ref, False)

    @pl.when(j == 0)
    def _():
        st_ref[...] = s0_ref[0]

    sh = mods_ref[pl.ds(b, 1), 0:D_MODEL]
    sc = mods_ref[pl.ds(b, 1), D_MODEL:2 * D_MODEL]
    hx = (_rms(x_ref[0], nm_ref[...]) * (1.0 + sc) + sh).astype(BF16)
    w = HG_KEYS
    proj = lambda k: _dot(hx, w_ref[:, k * w:(k + 1) * w])
    q = proj(0)
    zff = proj(1)
    zb_ref[0] = proj(2)
    vi = proj(3)
    g_ref[0] = proj(4).astype(BF16)
    gate_b = proj(5)
    u = proj(6) * proj(7)
    col = lax.broadcasted_iota(I32, (TT, 1), 0) & (GRID_W - 1)
    up = jnp.where(col != 0, pltpu.roll(u, 1, axis=0), 0.0)
    un = jnp.where(col != GRID_W - 1, pltpu.roll(u, TT - 1, axis=0), 0.0)
    cw = cw_ref[...]
    y_ref[0] = (gate_b * (cw[0:1] * up + cw[1:2] * u + cw[2:3] * un)).astype(BF16)
    lb_f, _ = _lower_bounds(lbl_ref)
    of_ref[0] = _gla_tile(q, zff, vi, lb_f, st_ref, lm_ref[...], tri_ref[...], False)
    q_ref[0] = q.astype(BF16)
    i_ref[0] = vi.astype(BF16)


def _mix_fwd(x, mods, nm, w_in_bf, lbl, cw, s0_f, wa, wb):
    nb, t, _ = x.shape
    nt = t // TT
    tok = lambda dt: jax.ShapeDtypeStruct((nb, t, HG_WIDTH), dt)
    tspec = pl.BlockSpec((1, TT, HG_WIDTH), lambda b, j: (b, j, 0))
    cast_spec = lambda a: pl.BlockSpec((1,) + a.shape[1:], lambda b, j: (b * nt + j, 0, 0))
    cast_out = lambda a: jax.ShapeDtypeStruct(a.shape, BF16)
    return pl.pallas_call(
        _mix_fwd_body,
        grid=(nb, nt),
        in_specs=[pl.BlockSpec((1, TT, D_MODEL), lambda b, j: (b, j, 0)),
                  pl.BlockSpec(mods.shape, lambda b, j: (0, 0)),
                  pl.BlockSpec((1, D_MODEL), lambda b, j: (0, 0)),
                  pl.BlockSpec(w_in_bf.shape, lambda b, j: (0, 0)),
                  pl.BlockSpec(lbl.shape, lambda b, j: (0, 0, 0)),
                  pl.BlockSpec(cw.shape, lambda b, j: (0, 0)),
                  pl.BlockSpec((1, HG_HEADS, HG_VDIM, HG_KDIM), lambda b, j: (b, 0, 0, 0)),
                  cast_spec(wa), cast_spec(wb)],
        out_specs=[tspec] * 6 + [cast_spec(wa), cast_spec(wb)],
        out_shape=[tok(BF16), tok(F32), tok(BF16), tok(BF16), tok(BF16), tok(F32), cast_out(wa), cast_out(wb)],
        scratch_shapes=[pltpu.VMEM((HG_HEADS, HG_VDIM, HG_KDIM), F32), pltpu.VMEM((TT, TT), I32),
                        pltpu.VMEM((TT, TT), BF16)],
        compiler_params=pltpu.CompilerParams(dimension_semantics=("arbitrary", "arbitrary"),
                                             vmem_limit_bytes=VMEM_LIMIT),
        name="mix_fwd",
    )(x, mods, nm, w_in_bf, lbl, cw, s0_f, wa, wb)


def _mix_bwd_body(x_ref, q_ref, zb_ref, i_ref, g_ref, y_ref, of_ref, mods_ref, lbl_ref, hgn_ref,
                  wo_ref, nf_ref, wr_ref, s0_ref, wc_ref,
                  x1_ref, hr_ref, at_ref, wc_bf_ref, st_ref, lm_ref, tri_ref):
    b = pl.program_id(0)
    j = pl.program_id(1)
    wc_bf_ref[...] = wc_ref[...].astype(BF16)

    @pl.when((b == 0) & (j == 0))
    def _():
        _init_maps(lm_ref, tri_ref, True)

    @pl.when(j == 0)
    def _():
        st_ref[...] = s0_ref[0]

    _, lb_b = _lower_bounds(lbl_ref)
    o_b = _gla_tile(q_ref[0].astype(F32), zb_ref[0], i_ref[0], lb_b, st_ref, lm_ref[...], tri_ref[...], True)
    o = of_ref[0] + o_b
    hgn = hgn_ref[...]
    heads = [_rms(o[:, h * HG_VDIM:(h + 1) * HG_VDIM], hgn) for h in range(HG_HEADS)]
    g = g_ref[0].astype(F32)
    ohg = jnp.concatenate(heads, axis=1) * _silu(g)
    cat = jnp.concatenate([ohg.astype(BF16), y_ref[0]], axis=1)
    mx = _dot(cat, wo_ref[...])
    mod = lambda k: mods_ref[pl.ds(b, 1), k * D_MODEL:(k + 1) * D_MODEL]
    x1 = x_ref[0] + mod(2) * mx
    x1_ref[0] = x1
    h2 = _rms(x1, nf_ref[...]) * (1.0 + mod(4)) + mod(3)
    for c in range(ACC_ROWS):
        hr_ref[0, pl.ds(c, TT, stride=ACC_ROWS), :] = h2[:, c * LANES:(c + 1) * LANES]
    r = _dot(jnp.concatenate(_split(h2), axis=0), wr_ref[...])
    lg = r[:TT, :LANES] + r[:TT, LANES:] + r[TT:, :LANES]
    lane = lax.broadcasted_iota(I32, lg.shape, 1)
    lg = jnp.where(lane < N_EXPERTS, lg, -jnp.inf)
    e = jnp.exp(lg - jnp.max(lg, axis=-1, keepdims=True))
    aff = e / jnp.sum(e, axis=-1, keepdims=True)
    at_ref[0] = aff.T[0:N_EXPERTS, :]


def _mix_bwd(x, q, zb, vi, g, y, of, mods, lbl, hgn, w_out_bf, nf, wr_pad, s0_b, wc):
    nb, t, _ = x.shape
    nt = t // TT
    cast_spec = pl.BlockSpec((1,) + wc.shape[1:], lambda b, j: (b * nt + j, 0, 0))
    rev = lambda b, j: (b, nt - 1 - j, 0)
    tspec = pl.BlockSpec((1, TT, HG_WIDTH), rev)
    xspec = pl.BlockSpec((1, TT, D_MODEL), rev)
    full2 = lambda a: pl.BlockSpec(a.shape, lambda b, j: (0, 0))
    return pl.pallas_call(
        _mix_bwd_body,
        grid=(nb, nt),
        in_specs=[xspec, tspec, tspec, tspec, tspec, tspec, tspec,
                  full2(mods), pl.BlockSpec(lbl.shape, lambda b, j: (0, 0, 0)), full2(hgn),
                  full2(w_out_bf), full2(nf), full2(wr_pad),
                  pl.BlockSpec((1, HG_HEADS, HG_VDIM, HG_KDIM), lambda b, j: (b, 0, 0, 0)),
                  cast_spec],
        out_specs=[xspec,
                   pl.BlockSpec((1, TT * ACC_ROWS, LANES), rev),
                   pl.BlockSpec((1, N_EXPERTS, TT), lambda b, j: (b, 0, nt - 1 - j)),
                   cast_spec],
        out_shape=[jax.ShapeDtypeStruct((nb, t, D_MODEL), F32),
                   jax.ShapeDtypeStruct((nb, t * ACC_ROWS, LANES), F32),
                   jax.ShapeDtypeStruct((nb, N_EXPERTS, t), F32),
                   jax.ShapeDtypeStruct(wc.shape, BF16)],
        scratch_shapes=[pltpu.VMEM((HG_HEADS, HG_VDIM, HG_KDIM), F32), pltpu.VMEM((TT, TT), I32),
                        pltpu.VMEM((TT, TT), BF16)],
        compiler_params=pltpu.CompilerParams(dimension_semantics=("arbitrary", "arbitrary"),
                                             vmem_limit_bytes=VMEM_LIMIT),
        name="mix_bwd",
    )(x, q, zb, vi, g, y, of, mods, lbl, hgn, w_out_bf, nf, wr_pad, s0_b, wc)


def _route_body(at_ref, idx_ref, gate_ref, *, cap):
    ne, t = at_ref.shape[1], at_ref.shape[2]
    blk = 2 * LANES
    aff = at_ref[0]

    def count(m):
        return jnp.sum(jnp.where(m, 1.0, 0.0), axis=1, keepdims=True)

    def bit_step(i, thr):
        cand = thr | jnp.left_shift(jnp.int32(1), 30 - i)
        return jnp.where(count(aff >= lax.bitcast_convert_type(cand, F32)) >= cap, cand, thr)

    thr = lax.fori_loop(0, 31, bit_step, jnp.zeros((ne, 1), I32))
    gt = aff >= lax.bitcast_convert_type(thr + 1, F32)
    eq = (aff >= lax.bitcast_convert_type(thr, F32)) & jnp.logical_not(gt)
    need = cap - count(gt)

    ii = lax.broadcasted_iota(I32, (blk, blk), 0)
    jj = lax.broadcasted_iota(I32, (blk, blk), 1)
    upper = jnp.where(ii <= jj, 1.0, 0.0).astype(BF16)

    def prefix_blocks(m):
        mb = jnp.where(m, 1.0, 0.0).astype(BF16)
        carry = jnp.zeros((ne, 1), F32)
        out = []
        for kb in range(t // blk):
            p = _dot(mb[:, kb * blk:(kb + 1) * blk], upper) + carry
            out.append(p)
            carry = p[:, blk - 1:blk]
        return out

    eq_rank = jnp.concatenate(prefix_blocks(eq), axis=1)
    sel = gt | (eq & (eq_rank <= need))
    rank = jnp.concatenate(prefix_blocks(sel), axis=1).astype(I32)

    tok = lax.broadcasted_iota(I32, (ne, t), 1)
    x = jnp.where(sel, tok, -1)
    d = jnp.where(sel, tok - rank + 1, 0)
    g = aff
    for k in range(t.bit_length() - 1):
        left = lambda a: pltpu.roll(a, t - (1 << k), axis=1)
        ds = left(d)
        take = ((ds >> k) & 1) == 1
        keep = ((d >> k) & 1) == 0
        x = jnp.where(take, left(x), jnp.where(keep, x, -1))
        g = jnp.where(take, left(g), g)
        d = jnp.where(take, ds, jnp.where(keep, d, 0))
    idx_ref[0] = x[:, :cap]
    gate_ref[0] = g[:, :cap]


def _route(aff_t, cap):
    nb, ne, t = aff_t.shape
    spec = pl.BlockSpec((1, ne, cap), lambda b: (b, 0, 0))
    return pl.pallas_call(
        functools.partial(_route_body, cap=cap),
        grid=(nb,),
        in_specs=[pl.BlockSpec((1, ne, t), lambda b: (b, 0, 0))],
        out_specs=[spec, spec],
        out_shape=[jax.ShapeDtypeStruct((nb, ne, cap), I32), jax.ShapeDtypeStruct((nb, ne, cap), F32)],
        compiler_params=pltpu.CompilerParams(vmem_limit_bytes=VMEM_LIMIT),
        name="route",
    )(aff_t)


def _moe_body(idx_ref, gate_ref, hr_ref, wg_ref, wu_ref, wd_ref, x1_ref, mods_ref, nfin_ref,
              out_ref, acc_ref, rowg_ref, rows_ref, xs_ref, ys_ref, *, cap, n_ff):
    b = pl.program_id(0)
    s = pl.program_id(1)
    n_work = N_EXPERTS * n_ff
    e = s // n_ff
    ff = s % n_ff
    group = SUBLANES
    part_tok = cap // n_ff

    def list_base(expert):
        return (b * N_EXPERTS + expert) * cap

    def gather(base, j0, n):
        for u in range(n):
            tkn = idx_ref[base + j0 + u]
            rowg_ref[pl.ds(pl.multiple_of((j0 + u) * ACC_ROWS, ACC_ROWS), ACC_ROWS), :] = (
                hr_ref[0, pl.ds(pl.multiple_of(tkn * ACC_ROWS, ACC_ROWS), ACC_ROWS), :])

    def scatter(base, j0, n):
        for g0 in range(0, n, group):
            dst = []
            val = []
            for u in range(g0, g0 + group):
                r = pl.multiple_of(idx_ref[base + j0 + u] * ACC_ROWS, ACC_ROWS)
                dst.append(r)
                val.append(acc_ref[pl.ds(r, ACC_ROWS), :] + gate_ref[base + j0 + u]
                           * rows_ref[pl.ds(pl.multiple_of((j0 + u) * ACC_ROWS, ACC_ROWS), ACC_ROWS), :])
            for r, v in zip(dst, val):
                acc_ref[pl.ds(r, ACC_ROWS), :] = v

    def in_groups(fn, base):
        def step(jo, carry):
            fn(base, jo * group, group)
            return carry
        lax.fori_loop(0, cap // group, step, 0)

    @pl.when(s == 0)
    def _():
        acc_ref[...] = jnp.zeros_like(acc_ref)
        rows_ref[...] = jnp.zeros_like(rows_ref)
        ys_ref[...] = jnp.zeros_like(ys_ref)
        in_groups(gather, list_base(0))

    @pl.when((s < n_work) & (ff == 0))
    def _():
        for c in range(ACC_ROWS):
            xs_ref[:, c * LANES:(c + 1) * LANES] = rowg_ref[pl.ds(c, cap, stride=ACC_ROWS), :].astype(BF16)

    @pl.when(s < n_work)
    def _():
        gather(list_base(jnp.minimum(e + 1, N_EXPERTS - 1)), ff * part_tok, part_tok)
        scatter(list_base(jnp.maximum(e - 1, 0)), ff * part_tok, part_tok)
        xs = xs_ref[...]
        hid = _silu(_dot(xs, wg_ref[0])) * _dot(xs, wu_ref[0])
        ys_ref[...] = _dot(hid.astype(BF16), wd_ref[0]) + jnp.where(ff == 0, 0.0, ys_ref[...])

    @pl.when((s < n_work) & (ff == n_ff - 1))
    def _():
        for c in range(ACC_ROWS):
            rows_ref[pl.ds(c, cap, stride=ACC_ROWS), :] = ys_ref[:, c * LANES:(c + 1) * LANES]

    @pl.when(s == n_work)
    def _():
        in_groups(scatter, list_base(N_EXPERTS - 1))

    @pl.when(s >= n_work)
    def _():
        row0 = (s - n_work) * (EPI_TOK * ACC_ROWS)
        moe = jnp.concatenate([acc_ref[pl.ds(row0 + c, EPI_TOK, stride=ACC_ROWS), :] for c in range(ACC_ROWS)], axis=1)
        g2 = mods_ref[pl.ds(b, 1), 5 * D_MODEL:6 * D_MODEL]
        out_ref[0] = _rms(x1_ref[0] + g2 * moe, nfin_ref[...])


def _moe(idx_flat, gate_flat, hr, wg, wu, wd, x1, mods, nfin, cap):
    nb, t, _ = x1.shape
    dff = wg.shape[2]
    n_ff = dff // DFF_TILE
    n_work = N_EXPERTS * n_ff
    n_epi = t // EPI_TOK

    def w_idx(s):
        sc = jnp.minimum(s, n_work - 1)
        return sc // n_ff, sc % n_ff

    def epi(b, s, *_):
        return (b, jnp.maximum(s - n_work, 0), 0)

    grid_spec = pltpu.PrefetchScalarGridSpec(
        num_scalar_prefetch=2,
        grid=(nb, n_work + n_epi),
        in_specs=[pl.BlockSpec((1, t * ACC_ROWS, LANES), lambda b, s, *_: (b, 0, 0), pipeline_mode=pl.Buffered(1)),
                  pl.BlockSpec((1, D_MODEL, DFF_TILE), lambda b, s, *_: (w_idx(s)[0], 0, w_idx(s)[1])),
                  pl.BlockSpec((1, D_MODEL, DFF_TILE), lambda b, s, *_: (w_idx(s)[0], 0, w_idx(s)[1])),
                  pl.BlockSpec((1, DFF_TILE, D_MODEL), lambda b, s, *_: (w_idx(s)[0], w_idx(s)[1], 0)),
                  pl.BlockSpec((1, EPI_TOK, D_MODEL), epi),
                  pl.BlockSpec(mods.shape, lambda b, s, *_: (0, 0)),
                  pl.BlockSpec((1, D_MODEL), lambda b, s, *_: (0, 0))],
        out_specs=pl.BlockSpec((1, EPI_TOK, D_MODEL), epi),
        scratch_shapes=[pltpu.VMEM((t * ACC_ROWS, LANES), F32),
                        pltpu.VMEM((cap * ACC_ROWS, LANES), F32),
                        pltpu.VMEM((cap * ACC_ROWS, LANES), F32),
                        pltpu.VMEM((cap, D_MODEL), BF16),
                        pltpu.VMEM((cap, D_MODEL), F32)])
    return pl.pallas_call(
        functools.partial(_moe_body, cap=cap, n_ff=n_ff),
        grid_spec=grid_spec,
        out_shape=jax.ShapeDtypeStruct((nb, t, D_MODEL), F32),
        compiler_params=pltpu.CompilerParams(dimension_semantics=("arbitrary", "arbitrary"),
                                             vmem_limit_bytes=VMEM_LIMIT),
        name="moe",
    )(idx_flat, gate_flat, hr, wg, wu, wd, x1, mods, nfin)


def kernel(x, c, ctx, c_ctx, w_ada, b_ada, norm_mix, norm_ffn, w_in, lb_logits, hg_norm, conv_w,
           w_out, w_router, w_gate, w_up, w_down, norm_final):
    nb, t, d = x.shape
    assert d == D_MODEL and w_ada.shape[0] == 1 and nb < MOD_ROWS and t % TT == 0 and TT % GRID_W == 0
    assert w_in.shape[2] == 3 * HG_KEYS + 2 * HG_WIDTH + 3 * SC_WIDTH and w_router.shape[2] == N_EXPERTS
    cap = EC_CAPACITY * t // N_EXPERTS
    assert cap % LANES == 0 and w_gate.shape[3] % DFF_TILE == 0

    cc = jnp.concatenate([c, c_ctx[None, :], jnp.zeros((MOD_ROWS - nb - 1, d), F32)], axis=0)
    w_in_bf = w_in[0].astype(BF16)
    w_out_bf = w_out[0].astype(BF16)
    wr_hi, wr_lo = _split(jnp.pad(w_router[0], ((0, 0), (0, LANES - N_EXPERTS))))
    wr_pad = jnp.concatenate([wr_hi, wr_lo], axis=1)
    nm, nf, nfin = norm_mix[0][None, :], norm_ffn[0][None, :], norm_final[None, :]
    hgn = hg_norm[0][None, :]

    mods = _adaln(cc, w_ada[0], b_ada[0][None, :])
    s0_f, s0_b = _ctx_states(ctx, mods, nm, w_in_bf, lb_logits, ctx_row=nb)
    steps = nb * (t // TT)
    sliced = lambda w: w.reshape((steps, w.shape[0] * w.shape[1] // steps) + w.shape[2:])
    q, zb, vi, g, y, of, wg, wu = _mix_fwd(x, mods, nm, w_in_bf, lb_logits, conv_w[0], s0_f,
                                           sliced(w_gate[0]), sliced(w_up[0]))
    x1, hr, aff_t, wd = _mix_bwd(x, q, zb, vi, g, y, of, mods, lb_logits, hgn, w_out_bf, nf, wr_pad, s0_b,
                                        sliced(w_down[0]))
    wg, wu, wd = wg.reshape(w_gate[0].shape), wu.reshape(w_up[0].shape), wd.reshape(w_down[0].shape)
    idx, gates = _route(aff_t, cap)
    return _moe(idx.reshape(-1), gates.reshape(-1), hr, wg, wu, wd, x1, mods, nfin, cap)
```

```python
import functools

import jax
import jax.numpy as jnp
from jax import lax
from jax.experimental import pallas as pl
from jax.experimental.pallas import tpu as pltpu

F32 = jnp.float32
BF16 = jnp.bfloat16
I32 = jnp.int32

D_MODEL = 1024
N_MOD = 6
HG_HEADS = 4
HG_KDIM = 128
HG_VDIM = 128
HG_KEYS = HG_HEADS * HG_KDIM
HG_WIDTH = HG_HEADS * HG_VDIM
SC_WIDTH = D_MODEL - HG_WIDTH
GRID_W = 64
CHUNK = 32
N_EXPERTS = 16
EC_CAPACITY = 2
EPS = 1e-6

LANES = 128
SUBLANES = 8
VMEM_LIMIT = 62 * 1024 * 1024

TT = 256
NCH = TT // CHUNK
MOD_ROWS = 16
DFF_TILE = 1024
EPI_TOK = 256
ACC_ROWS = D_MODEL // LANES


def _dot(a, b):
    return jnp.dot(a, b, preferred_element_type=F32)


def _dot_nt(a, b):
    return lax.dot_general(a, b, (((1,), (1,)), ((), ())), preferred_element_type=F32)


def _dot_tn(a, b):
    return lax.dot_general(a, b, (((0,), (0,)), ((), ())), preferred_element_type=F32)


def _split(x):
    hi = x.astype(BF16)
    lo = (x - hi.astype(F32)).astype(BF16)
    return hi, lo


def _rms(x, gain):
    return x * lax.rsqrt(jnp.mean(x * x, axis=-1, keepdims=True) + EPS) * gain


def _silu(x):
    return x * jax.nn.sigmoid(x)


def _lower_bounds(lbl_ref):
    rows = [lbl_ref[l] for l in range(lbl_ref.shape[0])]
    m = functools.reduce(jnp.maximum, rows)
    es = [jnp.exp(r - m) for r in rows]
    sm0 = es[0] / functools.reduce(lambda a, b: a + b, es)
    return sm0[0:1, :], sm0[1:2, :]


def _ada_body(c_ref, w_ref, b_ref, o_ref):
    s_hi, s_lo = _split(_silu(c_ref[...]))
    w_hi, w_lo = _split(w_ref[...])
    o_ref[...] = _dot(s_hi, w_hi) + _dot(s_hi, w_lo) + _dot(s_lo, w_hi) + b_ref[...]


def _adaln(cc, w, b):
    n = w.shape[1]
    tn = n // 4
    return pl.pallas_call(
        _ada_body,
        grid=(n // tn,),
        in_specs=[pl.BlockSpec((MOD_ROWS, D_MODEL), lambda i: (0, 0)),
                  pl.BlockSpec((D_MODEL, tn), lambda i: (0, i)),
                  pl.BlockSpec((1, tn), lambda i: (0, i))],
        out_specs=pl.BlockSpec((MOD_ROWS, tn), lambda i: (0, i)),
        out_shape=jax.ShapeDtypeStruct((MOD_ROWS, n), F32),
        compiler_params=pltpu.CompilerParams(vmem_limit_bytes=VMEM_LIMIT),
        name="adaln",
    )(cc, w, b)


def _ctx_body(ctx_ref, mods_ref, nm_ref, wf_ref, wb_ref, wi_ref, lbl_ref, sf_ref, sb_ref, *, ctx_row):
    x = ctx_ref[0]
    n = x.shape[0]
    csh = mods_ref[ctx_row:ctx_row + 1, 0:D_MODEL]
    csc = mods_ref[ctx_row:ctx_row + 1, D_MODEL:2 * D_MODEL]
    hb = (_rms(x, nm_ref[...]) * (1.0 + csc) + csh).astype(BF16)
    zf = _dot(hb, wf_ref[...])
    zb = _dot(hb, wb_ref[...])
    vi = _dot(hb, wi_ref[...]).astype(BF16)
    lb_f, lb_b = _lower_bounds(lbl_ref)
    f_f = lb_f + (1.0 - lb_f) * jax.nn.sigmoid(zf)
    f_b = lb_b + (1.0 - lb_b) * jax.nn.sigmoid(zb)
    ii = lax.broadcasted_iota(I32, (n, n), 0)
    jj = lax.broadcasted_iota(I32, (n, n), 1)
    after = jnp.where(jj > ii, 1.0, 0.0).astype(BF16)
    before = jnp.where(jj < ii, 1.0, 0.0).astype(BF16)
    hi, lo = _split(jnp.log(f_f))
    kd_f = ((1.0 - f_f) * jnp.exp(_dot(after, hi) + _dot(after, lo))).astype(BF16)
    hi, lo = _split(jnp.log(f_b))
    kd_b = ((1.0 - f_b) * jnp.exp(_dot(before, hi) + _dot(before, lo))).astype(BF16)
    for h in range(HG_HEADS):
        ks = slice(h * HG_KDIM, (h + 1) * HG_KDIM)
        vs = slice(h * HG_VDIM, (h + 1) * HG_VDIM)
        sf_ref[0, h] = _dot_tn(vi[:, vs], kd_f[:, ks])
        sb_ref[0, h] = _dot_tn(vi[:, vs], kd_b[:, ks])


def _ctx_states(ctx, mods, nm, w_in_bf, lbl, ctx_row):
    nb, n, _ = ctx.shape
    st = jax.ShapeDtypeStruct((nb, HG_HEADS, HG_VDIM, HG_KDIM), F32)
    st_spec = pl.BlockSpec((1, HG_HEADS, HG_VDIM, HG_KDIM), lambda b: (b, 0, 0, 0))
    wcol = lambda k: pl.BlockSpec((D_MODEL, HG_KEYS), lambda b, k=k: (0, k))
    return pl.pallas_call(
        functools.partial(_ctx_body, ctx_row=ctx_row),
        grid=(nb,),
        in_specs=[pl.BlockSpec((1, n, D_MODEL), lambda b: (b, 0, 0)),
                  pl.BlockSpec(mods.shape, lambda b: (0, 0)),
                  pl.BlockSpec((1, D_MODEL), lambda b: (0, 0)),
                  wcol(1), wcol(2), wcol(3),
                  pl.BlockSpec(lbl.shape, lambda b: (0, 0, 0))],
        out_specs=[st_spec, st_spec],
        out_shape=[st, st],
        compiler_params=pltpu.CompilerParams(vmem_limit_bytes=VMEM_LIMIT),
        name="ctx_states",
    )(ctx, mods, nm, w_in_bf, w_in_bf, w_in_bf, lbl)


def _init_maps(lm_ref, tri_ref, reverse):
    lm = _level_map(reverse)
    lm_ref[...] = lm
    tri_ref[...] = jnp.where(lm == 1, 1.0, 0.0).astype(BF16)


def _level_map(reverse):
    ii = lax.broadcasted_iota(I32, (TT, TT), 0)
    jj = lax.broadcasted_iota(I32, (TT, TT), 1)
    if reverse:
        ii = TT - 1 - ii
        jj = TT - 1 - jj
    sh0 = CHUNK.bit_length() - 1
    ci = ii >> sh0
    cj = jj >> sh0
    lm = jnp.where((ci == cj) & (jj <= ii), 1, 0)
    for lvl in range(1, NCH.bit_length()):
        m = ((ci >> lvl) == (cj >> lvl)) & (((ci >> (lvl - 1)) & 1) == 1) & (((cj >> (lvl - 1)) & 1) == 0)
        lm = jnp.where(m, lvl + 1, lm)
    return lm


def _gla_tile(q, zf, v, lb, st_ref, lm, tri, reverse):
    n_lvl = NCH.bit_length() - 1
    f = lb + (1.0 - lb) * jax.nn.sigmoid(zf)
    k = 1.0 - f
    hi, lo = _split(jnp.log(f))
    cum = _dot(tri, hi) + _dot(tri, lo)
    phys = list(range(NCH - 1, -1, -1)) if reverse else list(range(NCH))
    last = 0 if reverse else CHUNK - 1
    tot = [cum[a * CHUNK + last:a * CHUNK + last + 1, :] for a in range(NCH)]
    bnd = [jnp.zeros_like(tot[0])]
    for c in range(NCH):
        bnd.append(bnd[-1] + tot[phys[c]])
    q0 = q * jnp.exp(cum)
    kinv = (k * jnp.exp(-cum)).astype(BF16)
    qs = [[None] * NCH for _ in range(n_lvl + 1)]
    ks = [[None] * NCH for _ in range(n_lvl + 1)]
    for c in range(NCH):
        a = phys[c]
        rows = slice(a * CHUNK, (a + 1) * CHUNK)
        q0a = q0[rows]
        kend = k[rows] * jnp.exp(tot[a] - cum[rows])
        zero = jnp.zeros_like(q0a)
        for lvl in range(1, n_lvl + 1):
            s = 1 << (lvl - 1)
            mid = (c // (2 * s)) * 2 * s + s
            right = (c // s) % 2 == 1
            qs[lvl - 1][a] = q0a * jnp.exp(bnd[c] - bnd[mid]) if right else zero
            ks[lvl - 1][a] = zero if right else kend * jnp.exp(bnd[mid] - bnd[c + 1])
        qs[n_lvl][a] = q0a * jnp.exp(bnd[c])
        ks[n_lvl][a] = kend * jnp.exp(bnd[NCH] - bnd[c + 1])
    qs = [jnp.concatenate(p, axis=0).astype(BF16) for p in qs]
    ks = [jnp.concatenate(p, axis=0).astype(BF16) for p in ks]
    q0 = q0.astype(BF16)
    vb = v.astype(BF16)
    dec = jnp.exp(bnd[NCH])
    outs = []
    for h in range(HG_HEADS):
        kk = slice(h * HG_KDIM, (h + 1) * HG_KDIM)
        vv = slice(h * HG_VDIM, (h + 1) * HG_VDIM)
        sc = jnp.where(lm == 1, _dot_nt(q0[:, kk], kinv[:, kk]), 0.0)
        for lvl in range(n_lvl):
            sc = jnp.where(lm == lvl + 2, _dot_nt(qs[lvl][:, kk], ks[lvl][:, kk]), sc)
        st = st_ref[h]
        outs.append(_dot(sc.astype(BF16), vb[:, vv]) + _dot_nt(qs[n_lvl][:, kk], st.astype(BF16)))
        st_ref[h] = st * dec[:, kk] + _dot_tn(vb[:, vv], ks[n_lvl][:, kk])
    return jnp.concatenate(outs, axis=1)


def _mix_fwd_body(x_ref, mods_ref, nm_ref, w_ref, lbl_ref, cw_ref, s0_ref, wa_ref, wb_ref,
                  q_ref, zb_ref, i_ref, g_ref, y_ref, of_ref, wa_bf_ref, wb_bf_ref, st_ref, lm_ref, tri_ref):
    b = pl.program_id(0)
    j = pl.program_id(1)
    wa_bf_ref[...] = wa_ref[...].astype(BF16)
    wb_bf_ref[...] = wb_ref[...].astype(BF16)

    @pl.when((b == 0) & (j == 0))
    def _():
        _init_maps(lm_ref, tri_ref, False)

    @pl.when(j == 0)
    def _():
        st_ref[...] = s0_ref[0]

    sh = mods_ref[pl.ds(b, 1), 0:D_MODEL]
    sc = mods_ref[pl.ds(b, 1), D_MODEL:2 * D_MODEL]
    hx = (_rms(x_ref[0], nm_ref[...]) * (1.0 + sc) + sh).astype(BF16)
    w = HG_KEYS
    proj = lambda k: _dot(hx, w_ref[:, k * w:(k + 1) * w])
    q = proj(0)
    zff = proj(1)
    zb_ref[0] = proj(2)
    vi = proj(3)
    g_ref[0] = proj(4).astype(BF16)
    gate_b = proj(5)
    u = proj(6) * proj(7)
    col = lax.broadcasted_iota(I32, (TT, 1), 0) & (GRID_W - 1)
    up = jnp.where(col != 0, pltpu.roll(u, 1, axis=0), 0.0)
    un = jnp.where(col != GRID_W - 1, pltpu.roll(u, TT - 1, axis=0), 0.0)
    cw = cw_ref[...]
    y_ref[0] = (gate_b * (cw[0:1] * up + cw[1:2] * u + cw[2:3] * un)).astype(BF16)
    lb_f, _ = _lower_bounds(lbl_ref)
    of_ref[0] = _gla_tile(q, zff, vi, lb_f, st_ref, lm_ref[...], tri_ref[...], False)
    q_ref[0] = q.astype(BF16)
    i_ref[0] = vi.astype(BF16)


def _mix_fwd(x, mods, nm, w_in_bf, lbl, cw, s0_f, wa, wb):
    nb, t, _ = x.shape
    nt = t // TT
    tok = lambda dt: jax.ShapeDtypeStruct((nb, t, HG_WIDTH), dt)
    tspec = pl.BlockSpec((1, TT, HG_WIDTH), lambda b, j: (b, j, 0))
    cast_spec = lambda a: pl.BlockSpec((1,) + a.shape[1:], lambda b, j: (b * nt + j, 0, 0))
    cast_out = lambda a: jax.ShapeDtypeStruct(a.shape, BF16)
    return pl.pallas_call(
        _mix_fwd_body,
        grid=(nb, nt),
        in_specs=[pl.BlockSpec((1, TT, D_MODEL), lambda b, j: (b, j, 0)),
                  pl.BlockSpec(mods.shape, lambda b, j: (0, 0)),
                  pl.BlockSpec((1, D_MODEL), lambda b, j: (0, 0)),
                  pl.BlockSpec(w_in_bf.shape, lambda b, j: (0, 0)),
                  pl.BlockSpec(lbl.shape, lambda b, j: (0, 0, 0)),
                  pl.BlockSpec(cw.shape, lambda b, j: (0, 0)),
                  pl.BlockSpec((1, HG_HEADS, HG_VDIM, HG_KDIM), lambda b, j: (b, 0, 0, 0)),
                  cast_spec(wa), cast_spec(wb)],
        out_specs=[tspec] * 6 + [cast_spec(wa), cast_spec(wb)],
        out_shape=[tok(BF16), tok(F32), tok(BF16), tok(BF16), tok(BF16), tok(F32), cast_out(wa), cast_out(wb)],
        scratch_shapes=[pltpu.VMEM((HG_HEADS, HG_VDIM, HG_KDIM), F32), pltpu.VMEM((TT, TT), I32),
                        pltpu.VMEM((TT, TT), BF16)],
        compiler_params=pltpu.CompilerParams(dimension_semantics=("arbitrary", "arbitrary"),
                                             vmem_limit_bytes=VMEM_LIMIT),
        name="mix_fwd",
    )(x, mods, nm, w_in_bf, lbl, cw, s0_f, wa, wb)


def _mix_bwd_body(x_ref, q_ref, zb_ref, i_ref, g_ref, y_ref, of_ref, mods_ref, lbl_ref, hgn_ref,
                  wo_ref, nf_ref, wr_ref, s0_ref, wc_ref,
                  x1_ref, hr_ref, at_ref, wc_bf_ref, st_ref, lm_ref, tri_ref):
    b = pl.program_id(0)
    j = pl.program_id(1)
    wc_bf_ref[...] = wc_ref[...].astype(BF16)

    @pl.when((b == 0) & (j == 0))
    def _():
        _init_maps(lm_ref, tri_ref, True)

    @pl.when(j == 0)
    def _():
        st_ref[...] = s0_ref[0]

    _, lb_b = _lower_bounds(lbl_ref)
    o_b = _gla_tile(q_ref[0].astype(F32), zb_ref[0], i_ref[0], lb_b, st_ref, lm_ref[...], tri_ref[...], True)
    o = of_ref[0] + o_b
    hgn = hgn_ref[...]
    heads = [_rms(o[:, h * HG_VDIM:(h + 1) * HG_VDIM], hgn) for h in range(HG_HEADS)]
    g = g_ref[0].astype(F32)
    ohg = jnp.concatenate(heads, axis=1) * _silu(g)
    cat = jnp.concatenate([ohg.astype(BF16), y_ref[0]], axis=1)
    mx = _dot(cat, wo_ref[...])
    mod = lambda k: mods_ref[pl.ds(b, 1), k * D_MODEL:(k + 1) * D_MODEL]
    x1 = x_ref[0] + mod(2) * mx
    x1_ref[0] = x1
    h2 = _rms(x1, nf_ref[...]) * (1.0 + mod(4)) + mod(3)
    for c in range(ACC_ROWS):
        hr_ref[0, pl.ds(c, TT, stride=ACC_ROWS), :] = h2[:, c * LANES:(c + 1) * LANES]
    r = _dot(jnp.concatenate(_split(h2), axis=0), wr_ref[...])
    lg = r[:TT, :LANES] + r[:TT, LANES:] + r[TT:, :LANES]
    lane = lax.broadcasted_iota(I32, lg.shape, 1)
    lg = jnp.where(lane < N_EXPERTS, lg, -jnp.inf)
    e = jnp.exp(lg - jnp.max(lg, axis=-1, keepdims=True))
    aff = e / jnp.sum(e, axis=-1, keepdims=True)
    at_ref[0] = aff.T[0:N_EXPERTS, :]


def _mix_bwd(x, q, zb, vi, g, y, of, mods, lbl, hgn, w_out_bf, nf, wr_pad, s0_b, wc):
    nb, t, _ = x.shape
    nt = t // TT
    cast_spec = pl.BlockSpec((1,) + wc.shape[1:], lambda b, j: (b * nt + j, 0, 0))
    rev = lambda b, j: (b, nt - 1 - j, 0)
    tspec = pl.BlockSpec((1, TT, HG_WIDTH), rev)
    xspec = pl.BlockSpec((1, TT, D_MODEL), rev)
    full2 = lambda a: pl.BlockSpec(a.shape, lambda b, j: (0, 0))
    return pl.pallas_call(
        _mix_bwd_body,
        grid=(nb, nt),
        in_specs=[xspec, tspec, tspec, tspec, tspec, tspec, tspec,
                  full2(mods), pl.BlockSpec(lbl.shape, lambda b, j: (0, 0, 0)), full2(hgn),
                  full2(w_out_bf), full2(nf), full2(wr_pad),
                  pl.BlockSpec((1, HG_HEADS, HG_VDIM, HG_KDIM), lambda b, j: (b, 0, 0, 0)),
                  cast_spec],
        out_specs=[xspec,
                   pl.BlockSpec((1, TT * ACC_ROWS, LANES), rev),
                   pl.BlockSpec((1, N_EXPERTS, TT), lambda b, j: (b, 0, nt - 1 - j)),
                   cast_spec],
        out_shape=[jax.ShapeDtypeStruct((nb, t, D_MODEL), F32),
                   jax.ShapeDtypeStruct((nb, t * ACC_ROWS, LANES), F32),
                   jax.ShapeDtypeStruct((nb, N_EXPERTS, t), F32),
                   jax.ShapeDtypeStruct(wc.shape, BF16)],
        scratch_shapes=[pltpu.VMEM((HG_HEADS, HG_VDIM, HG_KDIM), F32), pltpu.VMEM((TT, TT), I32),
                        pltpu.VMEM((TT, TT), BF16)],
        compiler_params=pltpu.CompilerParams(dimension_semantics=("arbitrary", "arbitrary"),
                                             vmem_limit_bytes=VMEM_LIMIT),
        name="mix_bwd",
    )(x, q, zb, vi, g, y, of, mods, lbl, hgn, w_out_bf, nf, wr_pad, s0_b, wc)


def _route_body(at_ref, idx_ref, gate_ref, *, cap):
    ne, t = at_ref.shape[1], at_ref.shape[2]
    blk = 2 * LANES
    aff = at_ref[0]

    def count(m):
        return jnp.sum(jnp.where(m, 1.0, 0.0), axis=1, keepdims=True)

    def bit_step(i, thr):
        cand = thr | jnp.left_shift(jnp.int32(1), 30 - i)
        return jnp.where(count(aff >= lax.bitcast_convert_type(cand, F32)) >= cap, cand, thr)

    thr = lax.fori_loop(0, 31, bit_step, jnp.zeros((ne, 1), I32))
    gt = aff >= lax.bitcast_convert_type(thr + 1, F32)
    eq = (aff >= lax.bitcast_convert_type(thr, F32)) & jnp.logical_not(gt)
    need = cap - count(gt)

    ii = lax.broadcasted_iota(I32, (blk, blk), 0)
    jj = lax.broadcasted_iota(I32, (blk, blk), 1)
    upper = jnp.where(ii <= jj, 1.0, 0.0).astype(BF16)

    def prefix_blocks(m):
        mb = jnp.where(m, 1.0, 0.0).astype(BF16)
        carry = jnp.zeros((ne, 1), F32)
        out = []
        for kb in range(t // blk):
            p = _dot(mb[:, kb * blk:(kb + 1) * blk], upper) + carry
            out.append(p)
            carry = p[:, blk - 1:blk]
        return out

    eq_rank = jnp.concatenate(prefix_blocks(eq), axis=1)
    sel = gt | (eq & (eq_rank <= need))
    rank = jnp.concatenate(prefix_blocks(sel), axis=1).astype(I32)

    tok = lax.broadcasted_iota(I32, (ne, t), 1)
    x = jnp.where(sel, tok, -1)
    d = jnp.where(sel, tok - rank + 1, 0)
    g = aff
    for k in range(t.bit_length() - 1):
        left = lambda a: pltpu.roll(a, t - (1 << k), axis=1)
        ds = left(d)
        take = ((ds >> k) & 1) == 1
        keep = ((d >> k) & 1) == 0
        x = jnp.where(take, left(x), jnp.where(keep, x, -1))
        g = jnp.where(take, left(g), g)
        d = jnp.where(take, ds, jnp.where(keep, d, 0))
    idx_ref[0] = x[:, :cap]
    gate_ref[0] = g[:, :cap]


def _route(aff_t, cap):
    nb, ne, t = aff_t.shape
    spec = pl.BlockSpec((1, ne, cap), lambda b: (b, 0, 0))
    return pl.pallas_call(
        functools.partial(_route_body, cap=cap),
        grid=(nb,),
        in_specs=[pl.BlockSpec((1, ne, t), lambda b: (b, 0, 0))],
        out_specs=[spec, spec],
        out_shape=[jax.ShapeDtypeStruct((nb, ne, cap), I32), jax.ShapeDtypeStruct((nb, ne, cap), F32)],
        compiler_params=pltpu.CompilerParams(vmem_limit_bytes=VMEM_LIMIT),
        name="route",
    )(aff_t)


def _moe_body(idx_ref, gate_ref, hr_ref, wg_ref, wu_ref, wd_ref, x1_ref, mods_ref, nfin_ref,
              out_ref, acc_ref, rowg_ref, rows_ref, xs_ref, ys_ref, *, cap, n_ff):
    b = pl.program_id(0)
    s = pl.program_id(1)
    n_work = N_EXPERTS * n_ff
    e = s // n_ff
    ff = s % n_ff
    slot = e % 2
    group = SUBLANES

    def list_base(expert):
        return (b * N_EXPERTS + expert) * cap

    def gather(base, j0, n):
        for u in range(n):
            tkn = idx_ref[base + j0 + u]
            rowg_ref[pl.ds(pl.multiple_of((j0 + u) * ACC_ROWS, ACC_ROWS), ACC_ROWS), :] = (
                hr_ref[0, pl.ds(pl.multiple_of(tkn * ACC_ROWS, ACC_ROWS), ACC_ROWS), :])

    def scatter(base, j0, n):
        for g0 in range(0, n, group):
            dst = []
            val = []
            for u in range(g0, g0 + group):
                r = pl.multiple_of(idx_ref[base + j0 + u] * ACC_ROWS, ACC_ROWS)
                dst.append(r)
                val.append(acc_ref[pl.ds(r, ACC_ROWS), :] + gate_ref[base + j0 + u]
                           * rows_ref[pl.ds(pl.multiple_of((j0 + u) * ACC_ROWS, ACC_ROWS), ACC_ROWS), :])
            for r, v in zip(dst, val):
                acc_ref[pl.ds(r, ACC_ROWS), :] = v

    def in_groups(fn, base):
        def step(jo, carry):
            fn(base, jo * group, group)
            return carry
        lax.fori_loop(0, cap // group, step, 0)

    def lay_out_input(to_slot):
        for c in range(ACC_ROWS):
            xs_ref[to_slot, :, c * LANES:(c + 1) * LANES] = (
                rowg_ref[pl.ds(c, cap, stride=ACC_ROWS), :].astype(BF16))

    def lay_out_output():
        for c in range(ACC_ROWS):
            rows_ref[pl.ds(c, cap, stride=ACC_ROWS), :] = ys_ref[:, c * LANES:(c + 1) * LANES]

    def expert_tile():
        xs = xs_ref[slot]
        hid = _silu(_dot(xs, wg_ref[0])) * _dot(xs, wu_ref[0])
        ys_ref[...] = _dot(hid.astype(BF16), wd_ref[0]) + jnp.where(ff == 0, 0.0, ys_ref[...])

    @pl.when(s == 0)
    def _():
        acc_ref[...] = jnp.zeros_like(acc_ref)
        ys_ref[...] = jnp.zeros_like(ys_ref)
        in_groups(gather, list_base(0))
        lay_out_input(0)

    @pl.when((s < n_work) & (ff == 0))
    def _():
        gather(list_base(jnp.minimum(e + 1, N_EXPERTS - 1)), 0, cap)
        lay_out_output()
        expert_tile()

    if n_ff > 2:
        @pl.when((s < n_work) & (ff > 0) & (ff < n_ff - 1))
        def _():
            expert_tile()

    @pl.when((s < n_work) & (ff == n_ff - 1))
    def _():
        lay_out_input(1 - slot)
        scatter(list_base(jnp.maximum(e - 1, 0)), 0, cap)
        expert_tile()

    @pl.when(s == n_work)
    def _():
        lay_out_output()
        in_groups(scatter, list_base(N_EXPERTS - 1))

    @pl.when(s >= n_work)
    def _():
        row0 = (s - n_work) * (EPI_TOK * ACC_ROWS)
        moe = jnp.concatenate([acc_ref[pl.ds(row0 + c, EPI_TOK, stride=ACC_ROWS), :] for c in range(ACC_ROWS)], axis=1)
        g2 = mods_ref[pl.ds(b, 1), 5 * D_MODEL:6 * D_MODEL]
        out_ref[0] = _rms(x1_ref[0] + g2 * moe, nfin_ref[...])


def _moe(idx_flat, gate_flat, hr, wg, wu, wd, x1, mods, nfin, cap):
    nb, t, _ = x1.shape
    dff = wg.shape[2]
    n_ff = dff // DFF_TILE
    n_work = N_EXPERTS * n_ff
    n_epi = t // EPI_TOK

    def w_idx(s):
        sc = jnp.minimum(s, n_work - 1)
        return sc // n_ff, sc % n_ff

    def epi(b, s, *_):
        return (b, jnp.maximum(s - n_work, 0), 0)

    grid_spec = pltpu.PrefetchScalarGridSpec(
        num_scalar_prefetch=2,
        grid=(nb, n_work + n_epi),
        in_specs=[pl.BlockSpec((1, t * ACC_ROWS, LANES), lambda b, s, *_: (b, 0, 0), pipeline_mode=pl.Buffered(1)),
                  pl.BlockSpec((1, D_MODEL, DFF_TILE), lambda b, s, *_: (w_idx(s)[0], 0, w_idx(s)[1])),
                  pl.BlockSpec((1, D_MODEL, DFF_TILE), lambda b, s, *_: (w_idx(s)[0], 0, w_idx(s)[1])),
                  pl.BlockSpec((1, DFF_TILE, D_MODEL), lambda b, s, *_: (w_idx(s)[0], w_idx(s)[1], 0)),
                  pl.BlockSpec((1, EPI_TOK, D_MODEL), epi),
                  pl.BlockSpec(mods.shape, lambda b, s, *_: (0, 0)),
                  pl.BlockSpec((1, D_MODEL), lambda b, s, *_: (0, 0))],
        out_specs=pl.BlockSpec((1, EPI_TOK, D_MODEL), epi),
        scratch_shapes=[pltpu.VMEM((t * ACC_ROWS, LANES), F32),
                        pltpu.VMEM((cap * ACC_ROWS, LANES), F32),
                        pltpu.VMEM((cap * ACC_ROWS, LANES), F32),
                        pltpu.VMEM((2, cap, D_MODEL), BF16),
                        pltpu.VMEM((cap, D_MODEL), F32)])
    return pl.pallas_call(
        functools.partial(_moe_body, cap=cap, n_ff=n_ff),
        grid_spec=grid_spec,
        out_shape=jax.ShapeDtypeStruct((nb, t, D_MODEL), F32),
        compiler_params=pltpu.CompilerParams(dimension_semantics=("arbitrary", "arbitrary"),
                                             vmem_limit_bytes=VMEM_LIMIT),
        name="moe",
    )(idx_flat, gate_flat, hr, wg, wu, wd, x1, mods, nfin)


def kernel(x, c, ctx, c_ctx, w_ada, b_ada, norm_mix, norm_ffn, w_in, lb_logits, hg_norm, conv_w,
           w_out, w_router, w_gate, w_up, w_down, norm_final):
    nb, t, d = x.shape
    assert d == D_MODEL and w_ada.shape[0] == 1 and nb < MOD_ROWS and t % TT == 0 and TT % GRID_W == 0
    assert w_in.shape[2] == 3 * HG_KEYS + 2 * HG_WIDTH + 3 * SC_WIDTH and w_router.shape[2] == N_EXPERTS
    cap = EC_CAPACITY * t // N_EXPERTS
    assert cap % LANES == 0 and w_gate.shape[3] % DFF_TILE == 0 and w_gate.shape[3] // DFF_TILE >= 2

    cc = jnp.concatenate([c, c_ctx[None, :], jnp.zeros((MOD_ROWS - nb - 1, d), F32)], axis=0)
    w_in_bf = w_in[0].astype(BF16)
    w_out_bf = w_out[0].astype(BF16)
    wr_hi, wr_lo = _split(jnp.pad(w_router[0], ((0, 0), (0, LANES - N_EXPERTS))))
    wr_pad = jnp.concatenate([wr_hi, wr_lo], axis=1)
    nm, nf, nfin = norm_mix[0][None, :], norm_ffn[0][None, :], norm_final[None, :]
    hgn = hg_norm[0][None, :]

    mods = _adaln(cc, w_ada[0], b_ada[0][None, :])
    s0_f, s0_b = _ctx_states(ctx, mods, nm, w_in_bf, lb_logits, ctx_row=nb)
    steps = nb * (t // TT)
    sliced = lambda w: w.reshape((steps, w.shape[0] * w.shape[1] // steps) + w.shape[2:])
    q, zb, vi, g, y, of, wg, wu = _mix_fwd(x, mods, nm, w_in_bf, lb_logits, conv_w[0], s0_f,
                                           sliced(w_gate[0]), sliced(w_up[0]))
    x1, hr, aff_t, wd = _mix_bwd(x, q, zb, vi, g, y, of, mods, lb_logits, hgn, w_out_bf, nf, wr_pad, s0_b,
                                        sliced(w_down[0]))
    wg, wu, wd = wg.reshape(w_gate[0].shape), wu.reshape(w_up[0].shape), wd.reshape(w_down[0].shape)
    idx, gates = _route(aff_t, cap)
    return _moe(idx.reshape(-1), gates.reshape(-1), hr, wg, wu, wd, x1, mods, nfin, cap)
```

```python
import functools

import jax
import jax.numpy as jnp
from jax import lax
from jax.experimental import pallas as pl
from jax.experimental.pallas import tpu as pltpu

F32 = jnp.float32
BF16 = jnp.bfloat16
I32 = jnp.int32

D_MODEL = 1024
N_MOD = 6
HG_HEADS = 4
HG_KDIM = 128
HG_VDIM = 128
HG_KEYS = HG_HEADS * HG_KDIM
HG_WIDTH = HG_HEADS * HG_VDIM
SC_WIDTH = D_MODEL - HG_WIDTH
GRID_W = 64
CHUNK = 32
N_EXPERTS = 16
EC_CAPACITY = 2
EPS = 1e-6

LANES = 128
SUBLANES = 8
VMEM_LIMIT = 62 * 1024 * 1024

TT = 256
NCH = TT // CHUNK
MOD_ROWS = 16
DFF_TILE = 1024
EPI_TOK = 256
ACC_ROWS = D_MODEL // LANES


def _dot(a, b):
    return jnp.dot(a, b, preferred_element_type=F32)


def _dot_nt(a, b):
    return lax.dot_general(a, b, (((1,), (1,)), ((), ())), preferred_element_type=F32)


def _dot_tn(a, b):
    return lax.dot_general(a, b, (((0,), (0,)), ((), ())), preferred_element_type=F32)


def _split(x):
    hi = x.astype(BF16)
    lo = (x - hi.astype(F32)).astype(BF16)
    return hi, lo


def _rms(x, gain):
    return x * lax.rsqrt(jnp.mean(x * x, axis=-1, keepdims=True) + EPS) * gain


def _silu(x):
    return x * jax.nn.sigmoid(x)


def _lower_bounds(lbl_ref):
    rows = [lbl_ref[l] for l in range(lbl_ref.shape[0])]
    m = functools.reduce(jnp.maximum, rows)
    es = [jnp.exp(r - m) for r in rows]
    sm0 = es[0] / functools.reduce(lambda a, b: a + b, es)
    return sm0[0:1, :], sm0[1:2, :]


def _ada_body(c_ref, w_ref, b_ref, o_ref):
    s_hi, s_lo = _split(_silu(c_ref[...]))
    w_hi, w_lo = _split(w_ref[...])
    o_ref[...] = _dot(s_hi, w_hi) + _dot(s_hi, w_lo) + _dot(s_lo, w_hi) + b_ref[...]


def _adaln(cc, w, b):
    n = w.shape[1]
    tn = n // 4
    return pl.pallas_call(
        _ada_body,
        grid=(n // tn,),
        in_specs=[pl.BlockSpec((MOD_ROWS, D_MODEL), lambda i: (0, 0)),
                  pl.BlockSpec((D_MODEL, tn), lambda i: (0, i)),
                  pl.BlockSpec((1, tn), lambda i: (0, i))],
        out_specs=pl.BlockSpec((MOD_ROWS, tn), lambda i: (0, i)),
        out_shape=jax.ShapeDtypeStruct((MOD_ROWS, n), F32),
        compiler_params=pltpu.CompilerParams(vmem_limit_bytes=VMEM_LIMIT),
        name="adaln",
    )(cc, w, b)


def _ctx_body(ctx_ref, mods_ref, nm_ref, wf_ref, wb_ref, wi_ref, lbl_ref, sf_ref, sb_ref, *, ctx_row):
    x = ctx_ref[0]
    n = x.shape[0]
    csh = mods_ref[ctx_row:ctx_row + 1, 0:D_MODEL]
    csc = mods_ref[ctx_row:ctx_row + 1, D_MODEL:2 * D_MODEL]
    hb = (_rms(x, nm_ref[...]) * (1.0 + csc) + csh).astype(BF16)
    zf = _dot(hb, wf_ref[...])
    zb = _dot(hb, wb_ref[...])
    vi = _dot(hb, wi_ref[...]).astype(BF16)
    lb_f, lb_b = _lower_bounds(lbl_ref)
    f_f = lb_f + (1.0 - lb_f) * jax.nn.sigmoid(zf)
    f_b = lb_b + (1.0 - lb_b) * jax.nn.sigmoid(zb)
    ii = lax.broadcasted_iota(I32, (n, n), 0)
    jj = lax.broadcasted_iota(I32, (n, n), 1)
    after = jnp.where(jj > ii, 1.0, 0.0).astype(BF16)
    before = jnp.where(jj < ii, 1.0, 0.0).astype(BF16)
    hi, lo = _split(jnp.log(f_f))
    kd_f = ((1.0 - f_f) * jnp.exp(_dot(after, hi) + _dot(after, lo))).astype(BF16)
    hi, lo = _split(jnp.log(f_b))
    kd_b = ((1.0 - f_b) * jnp.exp(_dot(before, hi) + _dot(before, lo))).astype(BF16)
    for h in range(HG_HEADS):
        ks = slice(h * HG_KDIM, (h + 1) * HG_KDIM)
        vs = slice(h * HG_VDIM, (h + 1) * HG_VDIM)
        sf_ref[0, h] = _dot_tn(vi[:, vs], kd_f[:, ks])
        sb_ref[0, h] = _dot_tn(vi[:, vs], kd_b[:, ks])


def _ctx_states(ctx, mods, nm, w_in_bf, lbl, ctx_row):
    nb, n, _ = ctx.shape
    st = jax.ShapeDtypeStruct((nb, HG_HEADS, HG_VDIM, HG_KDIM), F32)
    st_spec = pl.BlockSpec((1, HG_HEADS, HG_VDIM, HG_KDIM), lambda b: (b, 0, 0, 0))
    wcol = lambda k: pl.BlockSpec((D_MODEL, HG_KEYS), lambda b, k=k: (0, k))
    return pl.pallas_call(
        functools.partial(_ctx_body, ctx_row=ctx_row),
        grid=(nb,),
        in_specs=[pl.BlockSpec((1, n, D_MODEL), lambda b: (b, 0, 0)),
                  pl.BlockSpec(mods.shape, lambda b: (0, 0)),
                  pl.BlockSpec((1, D_MODEL), lambda b: (0, 0)),
                  wcol(1), wcol(2), wcol(3),
                  pl.BlockSpec(lbl.shape, lambda b: (0, 0, 0))],
        out_specs=[st_spec, st_spec],
        out_shape=[st, st],
        compiler_params=pltpu.CompilerParams(vmem_limit_bytes=VMEM_LIMIT),
        name="ctx_states",
    )(ctx, mods, nm, w_in_bf, w_in_bf, w_in_bf, lbl)


def _init_maps(lm_ref, tri_ref, reverse):
    lm = _level_map(reverse)
    lm_ref[...] = lm
    tri_ref[...] = jnp.where(lm == 1, 1.0, 0.0).astype(BF16)


def _level_map(reverse):
    ii = lax.broadcasted_iota(I32, (TT, TT), 0)
    jj = lax.broadcasted_iota(I32, (TT, TT), 1)
    if reverse:
        ii = TT - 1 - ii
        jj = TT - 1 - jj
    sh0 = CHUNK.bit_length() - 1
    ci = ii >> sh0
    cj = jj >> sh0
    lm = jnp.where((ci == cj) & (jj <= ii), 1, 0)
    for lvl in range(1, NCH.bit_length()):
        m = ((ci >> lvl) == (cj >> lvl)) & (((ci >> (lvl - 1)) & 1) == 1) & (((cj >> (lvl - 1)) & 1) == 0)
        lm = jnp.where(m, lvl + 1, lm)
    return lm


def _gla_gates(zf, lb, tri):
    f = lb + (1.0 - lb) * jax.nn.sigmoid(zf)
    hi, lo = _split(jnp.log(f))
    return 1.0 - f, _dot(tri, hi) + _dot(tri, lo)


def _gla_tile(q, k, cum, v, st_ref, lm, reverse):
    n_lvl = NCH.bit_length() - 1
    phys = list(range(NCH - 1, -1, -1)) if reverse else list(range(NCH))
    last = 0 if reverse else CHUNK - 1
    tot = [cum[a * CHUNK + last:a * CHUNK + last + 1, :] for a in range(NCH)]
    bnd = [jnp.zeros_like(tot[0])]
    for c in range(NCH):
        bnd.append(bnd[-1] + tot[phys[c]])
    q0 = q * jnp.exp(cum)
    kinv = (k * jnp.exp(-cum)).astype(BF16)
    qs = [[None] * NCH for _ in range(n_lvl + 1)]
    ks = [[None] * NCH for _ in range(n_lvl + 1)]
    for c in range(NCH):
        a = phys[c]
        rows = slice(a * CHUNK, (a + 1) * CHUNK)
        q0a = q0[rows]
        kend = k[rows] * jnp.exp(tot[a] - cum[rows])
        zero = jnp.zeros_like(q0a)
        for lvl in range(1, n_lvl + 1):
            s = 1 << (lvl - 1)
            mid = (c // (2 * s)) * 2 * s + s
            right = (c // s) % 2 == 1
            qs[lvl - 1][a] = q0a * jnp.exp(bnd[c] - bnd[mid]) if right else zero
            ks[lvl - 1][a] = zero if right else kend * jnp.exp(bnd[mid] - bnd[c + 1])
        qs[n_lvl][a] = q0a * jnp.exp(bnd[c])
        ks[n_lvl][a] = kend * jnp.exp(bnd[NCH] - bnd[c + 1])
    qs = [jnp.concatenate(p, axis=0).astype(BF16) for p in qs]
    ks = [jnp.concatenate(p, axis=0).astype(BF16) for p in ks]
    q0 = q0.astype(BF16)
    vb = v.astype(BF16)
    dec = jnp.exp(bnd[NCH])
    outs = []
    for h in range(HG_HEADS):
        kk = slice(h * HG_KDIM, (h + 1) * HG_KDIM)
        vv = slice(h * HG_VDIM, (h + 1) * HG_VDIM)
        sc = jnp.where(lm == 1, _dot_nt(q0[:, kk], kinv[:, kk]), 0.0)
        for lvl in range(n_lvl):
            sc = jnp.where(lm == lvl + 2, _dot_nt(qs[lvl][:, kk], ks[lvl][:, kk]), sc)
        st = st_ref[h]
        outs.append(_dot(sc.astype(BF16), vb[:, vv]) + _dot_nt(qs[n_lvl][:, kk], st.astype(BF16)))
        st_ref[h] = st * dec[:, kk] + _dot_tn(vb[:, vv], ks[n_lvl][:, kk])
    return jnp.concatenate(outs, axis=1)


def _mix_fwd_body(x_ref, mods_ref, nm_ref, w_ref, lbl_ref, cw_ref, s0_ref, wa_ref, wb_ref,
                  q_ref, zb_ref, i_ref, g_ref, y_ref, of_ref, wa_bf_ref, wb_bf_ref, st_ref, lm_ref, tri_ref):
    b = pl.program_id(0)
    j = pl.program_id(1)

    @pl.when((b == 0) & (j == 0))
    def _():
        _init_maps(lm_ref, tri_ref, False)

    @pl.when(j == 0)
    def _():
        st_ref[...] = s0_ref[0]

    sh = mods_ref[pl.ds(b, 1), 0:D_MODEL]
    sc = mods_ref[pl.ds(b, 1), D_MODEL:2 * D_MODEL]
    hx = (_rms(x_ref[0], nm_ref[...]) * (1.0 + sc) + sh).astype(BF16)
    w = HG_KEYS
    proj = lambda k: _dot(hx, w_ref[:, k * w:(k + 1) * w])
    gate_b = proj(5)
    wa_bf_ref[...] = wa_ref[...].astype(BF16)
    u = proj(6) * proj(7)
    wb_bf_ref[...] = wb_ref[...].astype(BF16)
    zff = proj(1)
    col = lax.broadcasted_iota(I32, (TT, 1), 0) & (GRID_W - 1)
    up = jnp.where(col != 0, pltpu.roll(u, 1, axis=0), 0.0)
    un = jnp.where(col != GRID_W - 1, pltpu.roll(u, TT - 1, axis=0), 0.0)
    cw = cw_ref[...]
    y_ref[0] = (gate_b * (cw[0:1] * up + cw[1:2] * u + cw[2:3] * un)).astype(BF16)
    q = proj(0)
    lb_f, _ = _lower_bounds(lbl_ref)
    k, cum = _gla_gates(zff, lb_f, tri_ref[...])
    vi = proj(3)
    zb_ref[0] = proj(2)
    g_ref[0] = proj(4).astype(BF16)
    of_ref[0] = _gla_tile(q, k, cum, vi, st_ref, lm_ref[...], False)
    q_ref[0] = q.astype(BF16)
    i_ref[0] = vi.astype(BF16)


def _mix_fwd(x, mods, nm, w_in_bf, lbl, cw, s0_f, wa, wb):
    nb, t, _ = x.shape
    nt = t // TT
    tok = lambda dt: jax.ShapeDtypeStruct((nb, t, HG_WIDTH), dt)
    tspec = pl.BlockSpec((1, TT, HG_WIDTH), lambda b, j: (b, j, 0))
    cast_spec = lambda a: pl.BlockSpec((1,) + a.shape[1:], lambda b, j: (b * nt + j, 0, 0))
    cast_out = lambda a: jax.ShapeDtypeStruct(a.shape, BF16)
    return pl.pallas_call(
        _mix_fwd_body,
        grid=(nb, nt),
        in_specs=[pl.BlockSpec((1, TT, D_MODEL), lambda b, j: (b, j, 0)),
                  pl.BlockSpec(mods.shape, lambda b, j: (0, 0)),
                  pl.BlockSpec((1, D_MODEL), lambda b, j: (0, 0)),
                  pl.BlockSpec(w_in_bf.shape, lambda b, j: (0, 0)),
                  pl.BlockSpec(lbl.shape, lambda b, j: (0, 0, 0)),
                  pl.BlockSpec(cw.shape, lambda b, j: (0, 0)),
                  pl.BlockSpec((1, HG_HEADS, HG_VDIM, HG_KDIM), lambda b, j: (b, 0, 0, 0)),
                  cast_spec(wa), cast_spec(wb)],
        out_specs=[tspec] * 6 + [cast_spec(wa), cast_spec(wb)],
        out_shape=[tok(BF16), tok(F32), tok(BF16), tok(BF16), tok(BF16), tok(F32), cast_out(wa), cast_out(wb)],
        scratch_shapes=[pltpu.VMEM((HG_HEADS, HG_VDIM, HG_KDIM), F32), pltpu.VMEM((TT, TT), I32),
                        pltpu.VMEM((TT, TT), BF16)],
        compiler_params=pltpu.CompilerParams(dimension_semantics=("arbitrary", "arbitrary"),
                                             vmem_limit_bytes=VMEM_LIMIT),
        name="mix_fwd",
    )(x, mods, nm, w_in_bf, lbl, cw, s0_f, wa, wb)


def _mix_bwd_body(x_ref, q_ref, zb_ref, i_ref, g_ref, y_ref, of_ref, mods_ref, lbl_ref, hgn_ref,
                  wo_ref, nf_ref, wr_ref, s0_ref, wc_ref,
                  x1_ref, hr_ref, at_ref, wc_bf_ref, st_ref, lm_ref, tri_ref):
    b = pl.program_id(0)
    j = pl.program_id(1)
    wc_bf_ref[...] = wc_ref[...].astype(BF16)

    @pl.when((b == 0) & (j == 0))
    def _():
        _init_maps(lm_ref, tri_ref, True)

    @pl.when(j == 0)
    def _():
        st_ref[...] = s0_ref[0]

    _, lb_b = _lower_bounds(lbl_ref)
    k, cum = _gla_gates(zb_ref[0], lb_b, tri_ref[...])
    o_b = _gla_tile(q_ref[0].astype(F32), k, cum, i_ref[0], st_ref, lm_ref[...], True)
    o = of_ref[0] + o_b
    hgn = hgn_ref[...]
    heads = [_rms(o[:, h * HG_VDIM:(h + 1) * HG_VDIM], hgn) for h in range(HG_HEADS)]
    g = g_ref[0].astype(F32)
    ohg = jnp.concatenate(heads, axis=1) * _silu(g)
    cat = jnp.concatenate([ohg.astype(BF16), y_ref[0]], axis=1)
    mx = _dot(cat, wo_ref[...])
    mod = lambda k: mods_ref[pl.ds(b, 1), k * D_MODEL:(k + 1) * D_MODEL]
    x1 = x_ref[0] + mod(2) * mx
    x1_ref[0] = x1
    h2 = _rms(x1, nf_ref[...]) * (1.0 + mod(4)) + mod(3)
    for c in range(ACC_ROWS):
        hr_ref[0, pl.ds(c, TT, stride=ACC_ROWS), :] = h2[:, c * LANES:(c + 1) * LANES]
    r = _dot(jnp.concatenate(_split(h2), axis=0), wr_ref[...])
    lg = r[:TT, :LANES] + r[:TT, LANES:] + r[TT:, :LANES]
    lane = lax.broadcasted_iota(I32, lg.shape, 1)
    lg = jnp.where(lane < N_EXPERTS, lg, -jnp.inf)
    e = jnp.exp(lg - jnp.max(lg, axis=-1, keepdims=True))
    aff = e / jnp.sum(e, axis=-1, keepdims=True)
    at_ref[0] = aff.T[0:N_EXPERTS, :]


def _mix_bwd(x, q, zb, vi, g, y, of, mods, lbl, hgn, w_out_bf, nf, wr_pad, s0_b, wc):
    nb, t, _ = x.shape
    nt = t // TT
    cast_spec = pl.BlockSpec((1,) + wc.shape[1:], lambda b, j: (b * nt + j, 0, 0))
    rev = lambda b, j: (b, nt - 1 - j, 0)
    tspec = pl.BlockSpec((1, TT, HG_WIDTH), rev)
    xspec = pl.BlockSpec((1, TT, D_MODEL), rev)
    full2 = lambda a: pl.BlockSpec(a.shape, lambda b, j: (0, 0))
    return pl.pallas_call(
        _mix_bwd_body,
        grid=(nb, nt),
        in_specs=[xspec, tspec, tspec, tspec, tspec, tspec, tspec,
                  full2(mods), pl.BlockSpec(lbl.shape, lambda b, j: (0, 0, 0)), full2(hgn),
                  full2(w_out_bf), full2(nf), full2(wr_pad),
                  pl.BlockSpec((1, HG_HEADS, HG_VDIM, HG_KDIM), lambda b, j: (b, 0, 0, 0)),
                  cast_spec],
        out_specs=[xspec,
                   pl.BlockSpec((1, TT * ACC_ROWS, LANES), rev),
                   pl.BlockSpec((1, N_EXPERTS, TT), lambda b, j: (b, 0, nt - 1 - j)),
                   cast_spec],
        out_shape=[jax.ShapeDtypeStruct((nb, t, D_MODEL), F32),
                   jax.ShapeDtypeStruct((nb, t * ACC_ROWS, LANES), F32),
                   jax.ShapeDtypeStruct((nb, N_EXPERTS, t), F32),
                   jax.ShapeDtypeStruct(wc.shape, BF16)],
        scratch_shapes=[pltpu.VMEM((HG_HEADS, HG_VDIM, HG_KDIM), F32), pltpu.VMEM((TT, TT), I32),
                        pltpu.VMEM((TT, TT), BF16)],
        compiler_params=pltpu.CompilerParams(dimension_semantics=("arbitrary", "arbitrary"),
                                             vmem_limit_bytes=VMEM_LIMIT),
        name="mix_bwd",
    )(x, q, zb, vi, g, y, of, mods, lbl, hgn, w_out_bf, nf, wr_pad, s0_b, wc)


def _route_body(at_ref, idx_ref, gate_ref, *, cap):
    ne, t = at_ref.shape[1], at_ref.shape[2]
    blk = 2 * LANES
    aff = at_ref[0]

    def count(m):
        return jnp.sum(jnp.where(m, 1.0, 0.0), axis=1, keepdims=True)

    def enough(cand):
        return count(aff >= lax.bitcast_convert_type(cand, F32)) >= cap

    def bit_step(i, thr):
        hi = jnp.left_shift(jnp.int32(1), 30 - 2 * i)
        lo = jnp.right_shift(hi, 1)
        both, upper, lower = thr | hi | lo, thr | hi, thr | lo
        return jnp.where(enough(both), both, jnp.where(enough(upper), upper, jnp.where(enough(lower), lower, thr)))

    thr = lax.fori_loop(0, 16, bit_step, jnp.zeros((ne, 1), I32))
    gt = aff >= lax.bitcast_convert_type(thr + 1, F32)
    eq = (aff >= lax.bitcast_convert_type(thr, F32)) & jnp.logical_not(gt)
    need = cap - count(gt)

    ii = lax.broadcasted_iota(I32, (blk, blk), 0)
    jj = lax.broadcasted_iota(I32, (blk, blk), 1)
    upper = jnp.where(ii <= jj, 1.0, 0.0).astype(BF16)

    def prefix_blocks(m):
        mb = jnp.where(m, 1.0, 0.0).astype(BF16)
        carry = jnp.zeros((ne, 1), F32)
        out = []
        for kb in range(t // blk):
            p = _dot(mb[:, kb * blk:(kb + 1) * blk], upper) + carry
            out.append(p)
            carry = p[:, blk - 1:blk]
        return out

    eq_rank = jnp.concatenate(prefix_blocks(eq), axis=1)
    sel = gt | (eq & (eq_rank <= need))
    rank = jnp.concatenate(prefix_blocks(sel), axis=1).astype(I32)

    tok = lax.broadcasted_iota(I32, (ne, t), 1)
    x = jnp.where(sel, tok, -1)
    d = jnp.where(sel, tok - rank + 1, 0)
    g = aff
    for k in range(t.bit_length() - 1):
        left = lambda a: pltpu.roll(a, t - (1 << k), axis=1)
        ds = left(d)
        take = ((ds >> k) & 1) == 1
        keep = ((d >> k) & 1) == 0
        x = jnp.where(take, left(x), jnp.where(keep, x, -1))
        g = jnp.where(take, left(g), g)
        d = jnp.where(take, ds, jnp.where(keep, d, 0))
    idx_ref[0] = x[:, :cap]
    gate_ref[0] = g[:, :cap]


def _route(aff_t, cap):
    nb, ne, t = aff_t.shape
    spec = pl.BlockSpec((1, ne, cap), lambda b: (b, 0, 0))
    return pl.pallas_call(
        functools.partial(_route_body, cap=cap),
        grid=(nb,),
        in_specs=[pl.BlockSpec((1, ne, t), lambda b: (b, 0, 0))],
        out_specs=[spec, spec],
        out_shape=[jax.ShapeDtypeStruct((nb, ne, cap), I32), jax.ShapeDtypeStruct((nb, ne, cap), F32)],
        compiler_params=pltpu.CompilerParams(vmem_limit_bytes=VMEM_LIMIT),
        name="route",
    )(aff_t)


def _moe_body(idx_ref, gate_ref, hr_ref, wg_ref, wu_ref, wd_ref, x1_ref, mods_ref, nfin_ref,
              out_ref, acc_ref, rowg_ref, rows_ref, xs_ref, ys_ref, *, cap, n_ff):
    b = pl.program_id(0)
    s = pl.program_id(1)
    n_work = N_EXPERTS * n_ff
    e = s // n_ff
    ff = s % n_ff
    slot = e % 2
    group = SUBLANES

    def list_base(expert):
        return (b * N_EXPERTS + expert) * cap

    def gather(base, j0, n):
        for u in range(n):
            tkn = idx_ref[base + j0 + u]
            rowg_ref[pl.ds(pl.multiple_of((j0 + u) * ACC_ROWS, ACC_ROWS), ACC_ROWS), :] = (
                hr_ref[0, pl.ds(pl.multiple_of(tkn * ACC_ROWS, ACC_ROWS), ACC_ROWS), :])

    def scatter(base, j0, n):
        for g0 in range(0, n, group):
            dst = []
            val = []
            for u in range(g0, g0 + group):
                r = pl.multiple_of(idx_ref[base + j0 + u] * ACC_ROWS, ACC_ROWS)
                dst.append(r)
                val.append(acc_ref[pl.ds(r, ACC_ROWS), :] + gate_ref[base + j0 + u]
                           * rows_ref[pl.ds(pl.multiple_of((j0 + u) * ACC_ROWS, ACC_ROWS), ACC_ROWS), :])
            for r, v in zip(dst, val):
                acc_ref[pl.ds(r, ACC_ROWS), :] = v

    def in_groups(fn, base):
        def step(jo, carry):
            fn(base, jo * group, group)
            return carry
        lax.fori_loop(0, cap // group, step, 0)

    def lay_out_input(to_slot):
        for c in range(ACC_ROWS):
            xs_ref[to_slot, :, c * LANES:(c + 1) * LANES] = (
                rowg_ref[pl.ds(c, cap, stride=ACC_ROWS), :].astype(BF16))

    def lay_out_output():
        for c in range(ACC_ROWS):
            rows_ref[pl.ds(c, cap, stride=ACC_ROWS), :] = ys_ref[:, c * LANES:(c + 1) * LANES]

    def expert_tile():
        xs = xs_ref[slot]
        hid = _silu(_dot(xs, wg_ref[0])) * _dot(xs, wu_ref[0])
        ys_ref[...] = _dot(hid.astype(BF16), wd_ref[0]) + jnp.where(ff == 0, 0.0, ys_ref[...])

    @pl.when(s == 0)
    def _():
        acc_ref[...] = jnp.zeros_like(acc_ref)
        ys_ref[...] = jnp.zeros_like(ys_ref)
        in_groups(gather, list_base(0))
        lay_out_input(0)

    @pl.when((s < n_work) & (ff == 0))
    def _():
        gather(list_base(jnp.minimum(e + 1, N_EXPERTS - 1)), 0, cap)
        lay_out_output()
        expert_tile()

    if n_ff > 2:
        @pl.when((s < n_work) & (ff > 0) & (ff < n_ff - 1))
        def _():
            expert_tile()

    @pl.when((s < n_work) & (ff == n_ff - 1))
    def _():
        lay_out_input(1 - slot)
        scatter(list_base(jnp.maximum(e - 1, 0)), 0, cap)
        expert_tile()

    @pl.when(s == n_work)
    def _():
        lay_out_output()
        in_groups(scatter, list_base(N_EXPERTS - 1))

    @pl.when(s >= n_work)
    def _():
        row0 = (s - n_work) * (EPI_TOK * ACC_ROWS)
        moe = jnp.concatenate([acc_ref[pl.ds(row0 + c, EPI_TOK, stride=ACC_ROWS), :] for c in range(ACC_ROWS)], axis=1)
        g2 = mods_ref[pl.ds(b, 1), 5 * D_MODEL:6 * D_MODEL]
        out_ref[0] = _rms(x1_ref[0] + g2 * moe, nfin_ref[...])


def _moe(idx_flat, gate_flat, hr, wg, wu, wd, x1, mods, nfin, cap):
    nb, t, _ = x1.shape
    dff = wg.shape[2]
    n_ff = dff // DFF_TILE
    n_work = N_EXPERTS * n_ff
    n_epi = t // EPI_TOK

    def w_idx(s):
        sc = jnp.minimum(s, n_work - 1)
        return sc // n_ff, sc % n_ff

    def epi(b, s, *_):
        return (b, jnp.maximum(s - n_work, 0), 0)

    grid_spec = pltpu.PrefetchScalarGridSpec(
        num_scalar_prefetch=2,
        grid=(nb, n_work + n_epi),
        in_specs=[pl.BlockSpec((1, t * ACC_ROWS, LANES), lambda b, s, *_: (b, 0, 0), pipeline_mode=pl.Buffered(1)),
                  pl.BlockSpec((1, D_MODEL, DFF_TILE), lambda b, s, *_: (w_idx(s)[0], 0, w_idx(s)[1])),
                  pl.BlockSpec((1, D_MODEL, DFF_TILE), lambda b, s, *_: (w_idx(s)[0], 0, w_idx(s)[1])),
                  pl.BlockSpec((1, DFF_TILE, D_MODEL), lambda b, s, *_: (w_idx(s)[0], w_idx(s)[1], 0)),
                  pl.BlockSpec((1, EPI_TOK, D_MODEL), epi),
                  pl.BlockSpec(mods.shape, lambda b, s, *_: (0, 0)),
                  pl.BlockSpec((1, D_MODEL), lambda b, s, *_: (0, 0))],
        out_specs=pl.BlockSpec((1, EPI_TOK, D_MODEL), epi),
        scratch_shapes=[pltpu.VMEM((t * ACC_ROWS, LANES), F32),
                        pltpu.VMEM((cap * ACC_ROWS, LANES), F32),
                        pltpu.VMEM((cap * ACC_ROWS, LANES), F32),
                        pltpu.VMEM((2, cap, D_MODEL), BF16),
                        pltpu.VMEM((cap, D_MODEL), F32)])
    return pl.pallas_call(
        functools.partial(_moe_body, cap=cap, n_ff=n_ff),
        grid_spec=grid_spec,
        out_shape=jax.ShapeDtypeStruct((nb, t, D_MODEL), F32),
        compiler_params=pltpu.CompilerParams(dimension_semantics=("arbitrary", "arbitrary"),
                                             vmem_limit_bytes=VMEM_LIMIT),
        name="moe",
    )(idx_flat, gate_flat, hr, wg, wu, wd, x1, mods, nfin)


def kernel(x, c, ctx, c_ctx, w_ada, b_ada, norm_mix, norm_ffn, w_in, lb_logits, hg_norm, conv_w,
           w_out, w_router, w_gate, w_up, w_down, norm_final):
    nb, t, d = x.shape
    assert d == D_MODEL and w_ada.shape[0] == 1 and nb < MOD_ROWS and t % TT == 0 and TT % GRID_W == 0
    assert w_in.shape[2] == 3 * HG_KEYS + 2 * HG_WIDTH + 3 * SC_WIDTH and w_router.shape[2] == N_EXPERTS
    cap = EC_CAPACITY * t // N_EXPERTS
    assert cap % LANES == 0 and w_gate.shape[3] % DFF_TILE == 0 and w_gate.shape[3] // DFF_TILE >= 2

    cc = jnp.concatenate([c, c_ctx[None, :], jnp.zeros((MOD_ROWS - nb - 1, d), F32)], axis=0)
    w_in_bf = w_in[0].astype(BF16)
    w_out_bf = w_out[0].astype(BF16)
    wr_hi, wr_lo = _split(jnp.pad(w_router[0], ((0, 0), (0, LANES - N_EXPERTS))))
    wr_pad = jnp.concatenate([wr_hi, wr_lo], axis=1)
    nm, nf, nfin = norm_mix[0][None, :], norm_ffn[0][None, :], norm_final[None, :]
    hgn = hg_norm[0][None, :]

    mods = _adaln(cc, w_ada[0], b_ada[0][None, :])
    s0_f, s0_b = _ctx_states(ctx, mods, nm, w_in_bf, lb_logits, ctx_row=nb)
    steps = nb * (t // TT)
    sliced = lambda w: w.reshape((steps, w.shape[0] * w.shape[1] // steps) + w.shape[2:])
    q, zb, vi, g, y, of, wg, wu = _mix_fwd(x, mods, nm, w_in_bf, lb_logits, conv_w[0], s0_f,
                                           sliced(w_gate[0]), sliced(w_up[0]))
    x1, hr, aff_t, wd = _mix_bwd(x, q, zb, vi, g, y, of, mods, lb_logits, hgn, w_out_bf, nf, wr_pad, s0_b,
                                        sliced(w_down[0]))
    wg, wu, wd = wg.reshape(w_gate[0].shape), wu.reshape(w_up[0].shape), wd.reshape(w_down[0].shape)
    idx, gates = _route(aff_t, cap)
    return _moe(idx.reshape(-1), gates.reshape(-1), hr, wg, wu, wd, x1, mods, nfin, cap)
```

```python
import functools

import jax
import jax.numpy as jnp
from jax import lax
from jax.experimental import pallas as pl
from jax.experimental.pallas import tpu as pltpu

F32 = jnp.float32
BF16 = jnp.bfloat16
I32 = jnp.int32

D_MODEL = 1024
N_MOD = 6
HG_HEADS = 4
HG_KDIM = 128
HG_VDIM = 128
HG_KEYS = HG_HEADS * HG_KDIM
HG_WIDTH = HG_HEADS * HG_VDIM
SC_WIDTH = D_MODEL - HG_WIDTH
GRID_W = 64
CHUNK = 32
N_EXPERTS = 16
EC_CAPACITY = 2
EPS = 1e-6

LANES = 128
SUBLANES = 8
VMEM_LIMIT = 62 * 1024 * 1024

TT = 256
NCH = TT // CHUNK
MOD_ROWS = 16
DFF_TILE = 1024
EPI_TOK = 256
ACC_ROWS = D_MODEL // LANES


def _dot(a, b):
    return jnp.dot(a, b, preferred_element_type=F32)


def _dot_nt(a, b):
    return lax.dot_general(a, b, (((1,), (1,)), ((), ())), preferred_element_type=F32)


def _dot_tn(a, b):
    return lax.dot_general(a, b, (((0,), (0,)), ((), ())), preferred_element_type=F32)


def _split(x):
    hi = x.astype(BF16)
    lo = (x - hi.astype(F32)).astype(BF16)
    return hi, lo


def _rms(x, gain):
    return x * lax.rsqrt(jnp.mean(x * x, axis=-1, keepdims=True) + EPS) * gain


def _silu(x):
    return x * jax.nn.sigmoid(x)


def _lower_bounds(lbl_ref):
    rows = [lbl_ref[l] for l in range(lbl_ref.shape[0])]
    m = functools.reduce(jnp.maximum, rows)
    es = [jnp.exp(r - m) for r in rows]
    sm0 = es[0] / functools.reduce(lambda a, b: a + b, es)
    return sm0[0:1, :], sm0[1:2, :]


def _ada_body(c_ref, w_ref, b_ref, o_ref):
    s_hi, s_lo = _split(_silu(c_ref[...]))
    w_hi, w_lo = _split(w_ref[...])
    o_ref[...] = _dot(s_hi, w_hi) + _dot(s_hi, w_lo) + _dot(s_lo, w_hi) + b_ref[...]


def _adaln(cc, w, b):
    n = w.shape[1]
    tn = n // 4
    return pl.pallas_call(
        _ada_body,
        grid=(n // tn,),
        in_specs=[pl.BlockSpec((MOD_ROWS, D_MODEL), lambda i: (0, 0)),
                  pl.BlockSpec((D_MODEL, tn), lambda i: (0, i)),
                  pl.BlockSpec((1, tn), lambda i: (0, i))],
        out_specs=pl.BlockSpec((MOD_ROWS, tn), lambda i: (0, i)),
        out_shape=jax.ShapeDtypeStruct((MOD_ROWS, n), F32),
        compiler_params=pltpu.CompilerParams(vmem_limit_bytes=VMEM_LIMIT),
        name="adaln",
    )(cc, w, b)


def _ctx_body(ctx_ref, mods_ref, nm_ref, wf_ref, wb_ref, wi_ref, lbl_ref, sf_ref, sb_ref, *, ctx_row):
    x = ctx_ref[0]
    n = x.shape[0]
    csh = mods_ref[ctx_row:ctx_row + 1, 0:D_MODEL]
    csc = mods_ref[ctx_row:ctx_row + 1, D_MODEL:2 * D_MODEL]
    hb = (_rms(x, nm_ref[...]) * (1.0 + csc) + csh).astype(BF16)
    zf = _dot(hb, wf_ref[...])
    zb = _dot(hb, wb_ref[...])
    vi = _dot(hb, wi_ref[...]).astype(BF16)
    lb_f, lb_b = _lower_bounds(lbl_ref)
    f_f = lb_f + (1.0 - lb_f) * jax.nn.sigmoid(zf)
    f_b = lb_b + (1.0 - lb_b) * jax.nn.sigmoid(zb)
    ii = lax.broadcasted_iota(I32, (n, n), 0)
    jj = lax.broadcasted_iota(I32, (n, n), 1)
    after = jnp.where(jj > ii, 1.0, 0.0).astype(BF16)
    before = jnp.where(jj < ii, 1.0, 0.0).astype(BF16)
    hi, lo = _split(jnp.log(f_f))
    kd_f = ((1.0 - f_f) * jnp.exp(_dot(after, hi) + _dot(after, lo))).astype(BF16)
    hi, lo = _split(jnp.log(f_b))
    kd_b = ((1.0 - f_b) * jnp.exp(_dot(before, hi) + _dot(before, lo))).astype(BF16)
    for h in range(HG_HEADS):
        ks = slice(h * HG_KDIM, (h + 1) * HG_KDIM)
        vs = slice(h * HG_VDIM, (h + 1) * HG_VDIM)
        sf_ref[0, h] = _dot_tn(vi[:, vs], kd_f[:, ks])
        sb_ref[0, h] = _dot_tn(vi[:, vs], kd_b[:, ks])


def _ctx_states(ctx, mods, nm, w_in_bf, lbl, ctx_row):
    nb, n, _ = ctx.shape
    st = jax.ShapeDtypeStruct((nb, HG_HEADS, HG_VDIM, HG_KDIM), F32)
    st_spec = pl.BlockSpec((1, HG_HEADS, HG_VDIM, HG_KDIM), lambda b: (b, 0, 0, 0))
    wcol = lambda k: pl.BlockSpec((D_MODEL, HG_KEYS), lambda b, k=k: (0, k))
    return pl.pallas_call(
        functools.partial(_ctx_body, ctx_row=ctx_row),
        grid=(nb,),
        in_specs=[pl.BlockSpec((1, n, D_MODEL), lambda b: (b, 0, 0)),
                  pl.BlockSpec(mods.shape, lambda b: (0, 0)),
                  pl.BlockSpec((1, D_MODEL), lambda b: (0, 0)),
                  wcol(1), wcol(2), wcol(3),
                  pl.BlockSpec(lbl.shape, lambda b: (0, 0, 0))],
        out_specs=[st_spec, st_spec],
        out_shape=[st, st],
        compiler_params=pltpu.CompilerParams(vmem_limit_bytes=VMEM_LIMIT),
        name="ctx_states",
    )(ctx, mods, nm, w_in_bf, w_in_bf, w_in_bf, lbl)


def _init_maps(lm_ref, tri_ref, reverse):
    lm = _level_map(reverse)
    lm_ref[...] = lm
    tri_ref[...] = jnp.where(lm == 1, 1.0, 0.0).astype(BF16)


def _level_map(reverse):
    ii = lax.broadcasted_iota(I32, (TT, TT), 0)
    jj = lax.broadcasted_iota(I32, (TT, TT), 1)
    if reverse:
        ii = TT - 1 - ii
        jj = TT - 1 - jj
    sh0 = CHUNK.bit_length() - 1
    ci = ii >> sh0
    cj = jj >> sh0
    lm = jnp.where((ci == cj) & (jj <= ii), 1, 0)
    for lvl in range(1, NCH.bit_length()):
        m = ((ci >> lvl) == (cj >> lvl)) & (((ci >> (lvl - 1)) & 1) == 1) & (((cj >> (lvl - 1)) & 1) == 0)
        lm = jnp.where(m, lvl + 1, lm)
    return lm


def _gla_gates(zf, lb, tri):
    f = lb + (1.0 - lb) * jax.nn.sigmoid(zf)
    hi, lo = _split(jnp.log(f))
    return 1.0 - f, _dot(tri, hi) + _dot(tri, lo)


def _gla_tile(q, k, cum, v, st_ref, lm, reverse):
    n_lvl = NCH.bit_length() - 1
    phys = list(range(NCH - 1, -1, -1)) if reverse else list(range(NCH))
    last = 0 if reverse else CHUNK - 1
    tot = [cum[a * CHUNK + last:a * CHUNK + last + 1, :] for a in range(NCH)]
    bnd = [jnp.zeros_like(tot[0])]
    for c in range(NCH):
        bnd.append(bnd[-1] + tot[phys[c]])
    q0 = q * jnp.exp(cum)
    kinv = (k * jnp.exp(-cum)).astype(BF16)
    qs = [[None] * NCH for _ in range(n_lvl + 1)]
    ks = [[None] * NCH for _ in range(n_lvl + 1)]
    for c in range(NCH):
        a = phys[c]
        rows = slice(a * CHUNK, (a + 1) * CHUNK)
        q0a = q0[rows]
        kend = k[rows] * jnp.exp(tot[a] - cum[rows])
        zero = jnp.zeros_like(q0a)
        for lvl in range(1, n_lvl + 1):
            s = 1 << (lvl - 1)
            mid = (c // (2 * s)) * 2 * s + s
            right = (c // s) % 2 == 1
            qs[lvl - 1][a] = q0a * jnp.exp(bnd[c] - bnd[mid]) if right else zero
            ks[lvl - 1][a] = zero if right else kend * jnp.exp(bnd[mid] - bnd[c + 1])
        qs[n_lvl][a] = q0a * jnp.exp(bnd[c])
        ks[n_lvl][a] = kend * jnp.exp(bnd[NCH] - bnd[c + 1])
    qs = [jnp.concatenate(p, axis=0).astype(BF16) for p in qs]
    ks = [jnp.concatenate(p, axis=0).astype(BF16) for p in ks]
    q0 = q0.astype(BF16)
    vb = v.astype(BF16)
    dec = jnp.exp(bnd[NCH])
    outs = []
    for h in range(HG_HEADS):
        kk = slice(h * HG_KDIM, (h + 1) * HG_KDIM)
        vv = slice(h * HG_VDIM, (h + 1) * HG_VDIM)
        sc = jnp.where(lm == 1, _dot_nt(q0[:, kk], kinv[:, kk]), 0.0)
        for lvl in range(n_lvl):
            sc = jnp.where(lm == lvl + 2, _dot_nt(qs[lvl][:, kk], ks[lvl][:, kk]), sc)
        st = st_ref[h]
        outs.append(_dot(sc.astype(BF16), vb[:, vv]) + _dot_nt(qs[n_lvl][:, kk], st.astype(BF16)))
        st_ref[h] = st * dec[:, kk] + _dot_tn(vb[:, vv], ks[n_lvl][:, kk])
    return jnp.concatenate(outs, axis=1)


def _mix_fwd_body(x_ref, mods_ref, nm_ref, w_ref, lbl_ref, cw_ref, s0_ref, wa_ref, wb_ref,
                  pk_ref, fz_ref, wa_bf_ref, wb_bf_ref, st_ref, lm_ref, tri_ref):
    b = pl.program_id(0)
    j = pl.program_id(1)

    @pl.when((b == 0) & (j == 0))
    def _():
        _init_maps(lm_ref, tri_ref, False)

    @pl.when(j == 0)
    def _():
        st_ref[...] = s0_ref[0]

    sh = mods_ref[pl.ds(b, 1), 0:D_MODEL]
    sc = mods_ref[pl.ds(b, 1), D_MODEL:2 * D_MODEL]
    hx = (_rms(x_ref[0], nm_ref[...]) * (1.0 + sc) + sh).astype(BF16)
    w = HG_KEYS
    proj = lambda k: _dot(hx, w_ref[:, k * w:(k + 1) * w])
    gate_b = proj(5)
    wa_bf_ref[...] = wa_ref[...].astype(BF16)
    u = proj(6) * proj(7)
    wb_bf_ref[...] = wb_ref[...].astype(BF16)
    zff = proj(1)
    col = lax.broadcasted_iota(I32, (TT, 1), 0) & (GRID_W - 1)
    up = jnp.where(col != 0, pltpu.roll(u, 1, axis=0), 0.0)
    un = jnp.where(col != GRID_W - 1, pltpu.roll(u, TT - 1, axis=0), 0.0)
    cw = cw_ref[...]
    pk_ref[0, :, 3 * w:4 * w] = (gate_b * (cw[0:1] * up + cw[1:2] * u + cw[2:3] * un)).astype(BF16)
    q = proj(0)
    lb_f, _ = _lower_bounds(lbl_ref)
    k, cum = _gla_gates(zff, lb_f, tri_ref[...])
    vi = proj(3)
    fz_ref[0, :, 0:w] = proj(2)
    pk_ref[0, :, 2 * w:3 * w] = proj(4).astype(BF16)
    fz_ref[0, :, w:2 * w] = _gla_tile(q, k, cum, vi, st_ref, lm_ref[...], False)
    pk_ref[0, :, 0:w] = q.astype(BF16)
    pk_ref[0, :, w:2 * w] = vi.astype(BF16)


def _mix_fwd(x, mods, nm, w_in_bf, lbl, cw, s0_f, wa, wb):
    nb, t, _ = x.shape
    nt = t // TT
    tok = lambda n, dt: jax.ShapeDtypeStruct((nb, t, n * HG_WIDTH), dt)
    tspec = lambda n: pl.BlockSpec((1, TT, n * HG_WIDTH), lambda b, j: (b, j, 0))
    cast_spec = lambda a: pl.BlockSpec((1,) + a.shape[1:], lambda b, j: (b * nt + j, 0, 0))
    cast_out = lambda a: jax.ShapeDtypeStruct(a.shape, BF16)
    return pl.pallas_call(
        _mix_fwd_body,
        grid=(nb, nt),
        in_specs=[pl.BlockSpec((1, TT, D_MODEL), lambda b, j: (b, j, 0)),
                  pl.BlockSpec(mods.shape, lambda b, j: (0, 0)),
                  pl.BlockSpec((1, D_MODEL), lambda b, j: (0, 0)),
                  pl.BlockSpec(w_in_bf.shape, lambda b, j: (0, 0)),
                  pl.BlockSpec(lbl.shape, lambda b, j: (0, 0, 0)),
                  pl.BlockSpec(cw.shape, lambda b, j: (0, 0)),
                  pl.BlockSpec((1, HG_HEADS, HG_VDIM, HG_KDIM), lambda b, j: (b, 0, 0, 0)),
                  cast_spec(wa), cast_spec(wb)],
        out_specs=[tspec(4), tspec(2), cast_spec(wa), cast_spec(wb)],
        out_shape=[tok(4, BF16), tok(2, F32), cast_out(wa), cast_out(wb)],
        scratch_shapes=[pltpu.VMEM((HG_HEADS, HG_VDIM, HG_KDIM), F32), pltpu.VMEM((TT, TT), I32),
                        pltpu.VMEM((TT, TT), BF16)],
        compiler_params=pltpu.CompilerParams(dimension_semantics=("arbitrary", "arbitrary"),
                                             vmem_limit_bytes=VMEM_LIMIT),
        name="mix_fwd",
    )(x, mods, nm, w_in_bf, lbl, cw, s0_f, wa, wb)


def _mix_bwd_body(x_ref, pk_ref, fz_ref, mods_ref, lbl_ref, hgn_ref,
                  wo_ref, nf_ref, wr_ref, s0_ref, wc_ref,
                  x1_ref, hr_ref, at_ref, wc_bf_ref, st_ref, lm_ref, tri_ref):
    b = pl.program_id(0)
    j = pl.program_id(1)
    wc_bf_ref[...] = wc_ref[...].astype(BF16)

    @pl.when((b == 0) & (j == 0))
    def _():
        _init_maps(lm_ref, tri_ref, True)

    @pl.when(j == 0)
    def _():
        st_ref[...] = s0_ref[0]

    _, lb_b = _lower_bounds(lbl_ref)
    w = HG_WIDTH
    k, cum = _gla_gates(fz_ref[0, :, 0:w], lb_b, tri_ref[...])
    o_b = _gla_tile(pk_ref[0, :, 0:w].astype(F32), k, cum, pk_ref[0, :, w:2 * w], st_ref, lm_ref[...], True)
    o = fz_ref[0, :, w:2 * w] + o_b
    hgn = hgn_ref[...]
    heads = [_rms(o[:, h * HG_VDIM:(h + 1) * HG_VDIM], hgn) for h in range(HG_HEADS)]
    g = pk_ref[0, :, 2 * w:3 * w].astype(F32)
    ohg = jnp.concatenate(heads, axis=1) * _silu(g)
    cat = jnp.concatenate([ohg.astype(BF16), pk_ref[0, :, 3 * w:4 * w]], axis=1)
    mx = _dot(cat, wo_ref[...])
    mod = lambda k: mods_ref[pl.ds(b, 1), k * D_MODEL:(k + 1) * D_MODEL]
    x1 = x_ref[0] + mod(2) * mx
    x1_ref[0] = x1
    h2 = _rms(x1, nf_ref[...]) * (1.0 + mod(4)) + mod(3)
    for c in range(ACC_ROWS):
        hr_ref[0, pl.ds(c, TT, stride=ACC_ROWS), :] = h2[:, c * LANES:(c + 1) * LANES]
    r = _dot(jnp.concatenate(_split(h2), axis=0), wr_ref[...])
    lg = r[:TT, :LANES] + r[:TT, LANES:] + r[TT:, :LANES]
    lane = lax.broadcasted_iota(I32, lg.shape, 1)
    lg = jnp.where(lane < N_EXPERTS, lg, -jnp.inf)
    e = jnp.exp(lg - jnp.max(lg, axis=-1, keepdims=True))
    aff = e / jnp.sum(e, axis=-1, keepdims=True)
    at_ref[0] = aff.T[0:N_EXPERTS, :]


def _mix_bwd(x, pk, fz, mods, lbl, hgn, w_out_bf, nf, wr_pad, s0_b, wc):
    nb, t, _ = x.shape
    nt = t // TT
    cast_spec = pl.BlockSpec((1,) + wc.shape[1:], lambda b, j: (b * nt + j, 0, 0))
    rev = lambda b, j: (b, nt - 1 - j, 0)
    tspec = lambda a: pl.BlockSpec((1, TT, a.shape[2]), rev)
    xspec = pl.BlockSpec((1, TT, D_MODEL), rev)
    full2 = lambda a: pl.BlockSpec(a.shape, lambda b, j: (0, 0))
    return pl.pallas_call(
        _mix_bwd_body,
        grid=(nb, nt),
        in_specs=[xspec, tspec(pk), tspec(fz),
                  full2(mods), pl.BlockSpec(lbl.shape, lambda b, j: (0, 0, 0)), full2(hgn),
                  full2(w_out_bf), full2(nf), full2(wr_pad),
                  pl.BlockSpec((1, HG_HEADS, HG_VDIM, HG_KDIM), lambda b, j: (b, 0, 0, 0)),
                  cast_spec],
        out_specs=[xspec,
                   pl.BlockSpec((1, TT * ACC_ROWS, LANES), rev),
                   pl.BlockSpec((1, N_EXPERTS, TT), lambda b, j: (b, 0, nt - 1 - j)),
                   cast_spec],
        out_shape=[jax.ShapeDtypeStruct((nb, t, D_MODEL), F32),
                   jax.ShapeDtypeStruct((nb, t * ACC_ROWS, LANES), F32),
                   jax.ShapeDtypeStruct((nb, N_EXPERTS, t), F32),
                   jax.ShapeDtypeStruct(wc.shape, BF16)],
        scratch_shapes=[pltpu.VMEM((HG_HEADS, HG_VDIM, HG_KDIM), F32), pltpu.VMEM((TT, TT), I32),
                        pltpu.VMEM((TT, TT), BF16)],
        compiler_params=pltpu.CompilerParams(dimension_semantics=("arbitrary", "arbitrary"),
                                             vmem_limit_bytes=VMEM_LIMIT),
        name="mix_bwd",
    )(x, pk, fz, mods, lbl, hgn, w_out_bf, nf, wr_pad, s0_b, wc)


def _route_body(at_ref, idx_ref, gate_ref, *, cap):
    ne, t = at_ref.shape[1], at_ref.shape[2]
    blk = 2 * LANES
    aff = at_ref[0]

    def count(m):
        return jnp.sum(jnp.where(m, 1.0, 0.0), axis=1, keepdims=True)

    def enough(cand):
        return count(aff >= lax.bitcast_convert_type(cand, F32)) >= cap

    def bit_step(i, thr):
        hi = jnp.left_shift(jnp.int32(1), 30 - 2 * i)
        lo = jnp.right_shift(hi, 1)
        both, upper, lower = thr | hi | lo, thr | hi, thr | lo
        return jnp.where(enough(both), both, jnp.where(enough(upper), upper, jnp.where(enough(lower), lower, thr)))

    thr = lax.fori_loop(0, 16, bit_step, jnp.zeros((ne, 1), I32))
    gt = aff >= lax.bitcast_convert_type(thr + 1, F32)
    eq = (aff >= lax.bitcast_convert_type(thr, F32)) & jnp.logical_not(gt)
    need = cap - count(gt)

    ii = lax.broadcasted_iota(I32, (blk, blk), 0)
    jj = lax.broadcasted_iota(I32, (blk, blk), 1)
    upper = jnp.where(ii <= jj, 1.0, 0.0).astype(BF16)

    def prefix_blocks(m):
        mb = jnp.where(m, 1.0, 0.0).astype(BF16)
        carry = jnp.zeros((ne, 1), F32)
        out = []
        for kb in range(t // blk):
            p = _dot(mb[:, kb * blk:(kb + 1) * blk], upper) + carry
            out.append(p)
            carry = p[:, blk - 1:blk]
        return out

    eq_rank = jnp.concatenate(prefix_blocks(eq), axis=1)
    sel = gt | (eq & (eq_rank <= need))
    rank = jnp.concatenate(prefix_blocks(sel), axis=1).astype(I32)

    tok = lax.broadcasted_iota(I32, (ne, t), 1)
    x = jnp.where(sel, tok, -1)
    d = jnp.where(sel, tok - rank + 1, 0)
    g = aff
    for k in range(t.bit_length() - 1):
        left = lambda a: pltpu.roll(a, t - (1 << k), axis=1)
        ds = left(d)
        take = ((ds >> k) & 1) == 1
        keep = ((d >> k) & 1) == 0
        x = jnp.where(take, left(x), jnp.where(keep, x, -1))
        g = jnp.where(take, left(g), g)
        d = jnp.where(take, ds, jnp.where(keep, d, 0))
    idx_ref[0] = x[:, :cap]
    gate_ref[0] = g[:, :cap]


def _route(aff_t, cap):
    nb, ne, t = aff_t.shape
    spec = pl.BlockSpec((1, ne, cap), lambda b: (b, 0, 0))
    return pl.pallas_call(
        functools.partial(_route_body, cap=cap),
        grid=(nb,),
        in_specs=[pl.BlockSpec((1, ne, t), lambda b: (b, 0, 0))],
        out_specs=[spec, spec],
        out_shape=[jax.ShapeDtypeStruct((nb, ne, cap), I32), jax.ShapeDtypeStruct((nb, ne, cap), F32)],
        compiler_params=pltpu.CompilerParams(vmem_limit_bytes=VMEM_LIMIT),
        name="route",
    )(aff_t)


def _moe_body(idx_ref, gate_ref, hr_ref, wg_ref, wu_ref, wd_ref, x1_ref, mods_ref, nfin_ref,
              out_ref, acc_ref, rowg_ref, rows_ref, xs_ref, ys_ref, *, cap, n_ff):
    b = pl.program_id(0)
    s = pl.program_id(1)
    n_work = N_EXPERTS * n_ff
    e = s // n_ff
    ff = s % n_ff
    slot = e % 2
    group = SUBLANES

    def list_base(expert):
        return (b * N_EXPERTS + expert) * cap

    def gather(base, j0, n):
        for u in range(n):
            tkn = idx_ref[base + j0 + u]
            rowg_ref[pl.ds(pl.multiple_of((j0 + u) * ACC_ROWS, ACC_ROWS), ACC_ROWS), :] = (
                hr_ref[0, pl.ds(pl.multiple_of(tkn * ACC_ROWS, ACC_ROWS), ACC_ROWS), :])

    def scatter(base, j0, n):
        for g0 in range(0, n, group):
            dst = []
            val = []
            for u in range(g0, g0 + group):
                r = pl.multiple_of(idx_ref[base + j0 + u] * ACC_ROWS, ACC_ROWS)
                dst.append(r)
                val.append(acc_ref[pl.ds(r, ACC_ROWS), :] + gate_ref[base + j0 + u]
                           * rows_ref[pl.ds(pl.multiple_of((j0 + u) * ACC_ROWS, ACC_ROWS), ACC_ROWS), :])
            for r, v in zip(dst, val):
                acc_ref[pl.ds(r, ACC_ROWS), :] = v

    def in_groups(fn, base):
        def step(jo, carry):
            fn(base, jo * group, group)
            return carry
        lax.fori_loop(0, cap // group, step, 0)

    def lay_out_input(to_slot):
        for c in range(ACC_ROWS):
            xs_ref[to_slot, :, c * LANES:(c + 1) * LANES] = (
                rowg_ref[pl.ds(c, cap, stride=ACC_ROWS), :].astype(BF16))

    def lay_out_output():
        for c in range(ACC_ROWS):
            rows_ref[pl.ds(c, cap, stride=ACC_ROWS), :] = ys_ref[:, c * LANES:(c + 1) * LANES]

    def expert_tile():
        xs = xs_ref[slot]
        hid = _silu(_dot(xs, wg_ref[0])) * _dot(xs, wu_ref[0])
        ys_ref[...] = _dot(hid.astype(BF16), wd_ref[0]) + jnp.where(ff == 0, 0.0, ys_ref[...])

    @pl.when(s == 0)
    def _():
        acc_ref[...] = jnp.zeros_like(acc_ref)
        ys_ref[...] = jnp.zeros_like(ys_ref)
        in_groups(gather, list_base(0))
        lay_out_input(0)

    @pl.when((s < n_work) & (ff == 0))
    def _():
        gather(list_base(jnp.minimum(e + 1, N_EXPERTS - 1)), 0, cap)
        lay_out_output()
        expert_tile()

    if n_ff > 2:
        @pl.when((s < n_work) & (ff > 0) & (ff < n_ff - 1))
        def _():
            expert_tile()

    @pl.when((s < n_work) & (ff == n_ff - 1))
    def _():
        lay_out_input(1 - slot)
        scatter(list_base(jnp.maximum(e - 1, 0)), 0, cap)
        expert_tile()

    @pl.when(s == n_work)
    def _():
        lay_out_output()
        in_groups(scatter, list_base(N_EXPERTS - 1))

    @pl.when(s >= n_work)
    def _():
        row0 = (s - n_work) * (EPI_TOK * ACC_ROWS)
        moe = jnp.concatenate([acc_ref[pl.ds(row0 + c, EPI_TOK, stride=ACC_ROWS), :] for c in range(ACC_ROWS)], axis=1)
        g2 = mods_ref[pl.ds(b, 1), 5 * D_MODEL:6 * D_MODEL]
        out_ref[0] = _rms(x1_ref[0] + g2 * moe, nfin_ref[...])


def _moe(idx_flat, gate_flat, hr, wg, wu, wd, x1, mods, nfin, cap):
    nb, t, _ = x1.shape
    dff = wg.shape[2]
    n_ff = dff // DFF_TILE
    n_work = N_EXPERTS * n_ff
    n_epi = t // EPI_TOK

    def w_idx(s):
        sc = jnp.minimum(s, n_work - 1)
        return sc // n_ff, sc % n_ff

    def epi(b, s, *_):
        return (b, jnp.maximum(s - n_work, 0), 0)

    grid_spec = pltpu.PrefetchScalarGridSpec(
        num_scalar_prefetch=2,
        grid=(nb, n_work + n_epi),
        in_specs=[pl.BlockSpec((1, t * ACC_ROWS, LANES), lambda b, s, *_: (b, 0, 0), pipeline_mode=pl.Buffered(1)),
                  pl.BlockSpec((1, D_MODEL, DFF_TILE), lambda b, s, *_: (w_idx(s)[0], 0, w_idx(s)[1])),
                  pl.BlockSpec((1, D_MODEL, DFF_TILE), lambda b, s, *_: (w_idx(s)[0], 0, w_idx(s)[1])),
                  pl.BlockSpec((1, DFF_TILE, D_MODEL), lambda b, s, *_: (w_idx(s)[0], w_idx(s)[1], 0)),
                  pl.BlockSpec((1, EPI_TOK, D_MODEL), epi),
                  pl.BlockSpec(mods.shape, lambda b, s, *_: (0, 0)),
                  pl.BlockSpec((1, D_MODEL), lambda b, s, *_: (0, 0))],
        out_specs=pl.BlockSpec((1, EPI_TOK, D_MODEL), epi),
        scratch_shapes=[pltpu.VMEM((t * ACC_ROWS, LANES), F32),
                        pltpu.VMEM((cap * ACC_ROWS, LANES), F32),
                        pltpu.VMEM((cap * ACC_ROWS, LANES), F32),
                        pltpu.VMEM((2, cap, D_MODEL), BF16),
                        pltpu.VMEM((cap, D_MODEL), F32)])
    return pl.pallas_call(
        functools.partial(_moe_body, cap=cap, n_ff=n_ff),
        grid_spec=grid_spec,
        out_shape=jax.ShapeDtypeStruct((nb, t, D_MODEL), F32),
        compiler_params=pltpu.CompilerParams(dimension_semantics=("arbitrary", "arbitrary"),
                                             vmem_limit_bytes=VMEM_LIMIT),
        name="moe",
    )(idx_flat, gate_flat, hr, wg, wu, wd, x1, mods, nfin)


def kernel(x, c, ctx, c_ctx, w_ada, b_ada, norm_mix, norm_ffn, w_in, lb_logits, hg_norm, conv_w,
           w_out, w_router, w_gate, w_up, w_down, norm_final):
    nb, t, d = x.shape
    assert d == D_MODEL and w_ada.shape[0] == 1 and nb < MOD_ROWS and t % TT == 0 and TT % GRID_W == 0
    assert w_in.shape[2] == 3 * HG_KEYS + 2 * HG_WIDTH + 3 * SC_WIDTH and w_router.shape[2] == N_EXPERTS
    cap = EC_CAPACITY * t // N_EXPERTS
    assert cap % LANES == 0 and w_gate.shape[3] % DFF_TILE == 0 and w_gate.shape[3] // DFF_TILE >= 2

    cc = jnp.concatenate([c, c_ctx[None, :], jnp.zeros((MOD_ROWS - nb - 1, d), F32)], axis=0)
    w_in_bf = w_in[0].astype(BF16)
    w_out_bf = w_out[0].astype(BF16)
    wr_hi, wr_lo = _split(jnp.pad(w_router[0], ((0, 0), (0, LANES - N_EXPERTS))))
    wr_pad = jnp.concatenate([wr_hi, wr_lo], axis=1)
    nm, nf, nfin = norm_mix[0][None, :], norm_ffn[0][None, :], norm_final[None, :]
    hgn = hg_norm[0][None, :]

    mods = _adaln(cc, w_ada[0], b_ada[0][None, :])
    s0_f, s0_b = _ctx_states(ctx, mods, nm, w_in_bf, lb_logits, ctx_row=nb)
    steps = nb * (t // TT)
    sliced = lambda w: w.reshape((steps, w.shape[0] * w.shape[1] // steps) + w.shape[2:])
    pk, fz, wg, wu = _mix_fwd(x, mods, nm, w_in_bf, lb_logits, conv_w[0], s0_f, sliced(w_gate[0]), sliced(w_up[0]))
    x1, hr, aff_t, wd = _mix_bwd(x, pk, fz, mods, lb_logits, hgn, w_out_bf, nf, wr_pad, s0_b, sliced(w_down[0]))
    wg, wu, wd = wg.reshape(w_gate[0].shape), wu.reshape(w_up[0].shape), wd.reshape(w_down[0].shape)
    idx, gates = _route(aff_t, cap)
    return _moe(idx.reshape(-1), gates.reshape(-1), hr, wg, wu, wd, x1, mods, nfin, cap)
```

```python
import functools

import jax
import jax.numpy as jnp
from jax import lax
from jax.experimental import pallas as pl
from jax.experimental.pallas import tpu as pltpu

F32 = jnp.float32
BF16 = jnp.bfloat16
I32 = jnp.int32

D_MODEL = 1024
N_MOD = 6
HG_HEADS = 4
HG_KDIM = 128
HG_VDIM = 128
HG_KEYS = HG_HEADS * HG_KDIM
HG_WIDTH = HG_HEADS * HG_VDIM
SC_WIDTH = D_MODEL - HG_WIDTH
GRID_W = 64
CHUNK = 32
N_EXPERTS = 16
EC_CAPACITY = 2
EPS = 1e-6

LANES = 128
SUBLANES = 8
VMEM_LIMIT = 62 * 1024 * 1024

TT = 256
NCH = TT // CHUNK
MOD_ROWS = 16
DFF_TILE = 1024
EPI_TOK = 256
ACC_ROWS = D_MODEL // LANES


def _dot(a, b):
    return jnp.dot(a, b, preferred_element_type=F32)


def _dot_nt(a, b):
    return lax.dot_general(a, b, (((1,), (1,)), ((), ())), preferred_element_type=F32)


def _dot_tn(a, b):
    return lax.dot_general(a, b, (((0,), (0,)), ((), ())), preferred_element_type=F32)


def _split(x):
    hi = x.astype(BF16)
    lo = (x - hi.astype(F32)).astype(BF16)
    return hi, lo


def _rms(x, gain):
    return x * lax.rsqrt(jnp.mean(x * x, axis=-1, keepdims=True) + EPS) * gain


def _silu(x):
    return x * jax.nn.sigmoid(x)


def _lower_bounds(lbl_ref):
    rows = [lbl_ref[l] for l in range(lbl_ref.shape[0])]
    m = functools.reduce(jnp.maximum, rows)
    es = [jnp.exp(r - m) for r in rows]
    sm0 = es[0] / functools.reduce(lambda a, b: a + b, es)
    return sm0[0:1, :], sm0[1:2, :]


def _ada_body(c_ref, w_ref, b_ref, o_ref):
    s_hi, s_lo = _split(_silu(c_ref[...]))
    w_hi, w_lo = _split(w_ref[...])
    o_ref[...] = _dot(s_hi, w_hi) + _dot(s_hi, w_lo) + _dot(s_lo, w_hi) + b_ref[...]


def _adaln(cc, w, b):
    n = w.shape[1]
    tn = n // 4
    return pl.pallas_call(
        _ada_body,
        grid=(n // tn,),
        in_specs=[pl.BlockSpec((MOD_ROWS, D_MODEL), lambda i: (0, 0)),
                  pl.BlockSpec((D_MODEL, tn), lambda i: (0, i)),
                  pl.BlockSpec((1, tn), lambda i: (0, i))],
        out_specs=pl.BlockSpec((MOD_ROWS, tn), lambda i: (0, i)),
        out_shape=jax.ShapeDtypeStruct((MOD_ROWS, n), F32),
        compiler_params=pltpu.CompilerParams(vmem_limit_bytes=VMEM_LIMIT),
        name="adaln",
    )(cc, w, b)


def _ctx_body(ctx_ref, mods_ref, nm_ref, wf_ref, wb_ref, wi_ref, lbl_ref, sf_ref, sb_ref, *, ctx_row):
    x = ctx_ref[0]
    n = x.shape[0]
    csh = mods_ref[ctx_row:ctx_row + 1, 0:D_MODEL]
    csc = mods_ref[ctx_row:ctx_row + 1, D_MODEL:2 * D_MODEL]
    hb = (_rms(x, nm_ref[...]) * (1.0 + csc) + csh).astype(BF16)
    zf = _dot(hb, wf_ref[...])
    zb = _dot(hb, wb_ref[...])
    vi = _dot(hb, wi_ref[...]).astype(BF16)
    lb_f, lb_b = _lower_bounds(lbl_ref)
    f_f = lb_f + (1.0 - lb_f) * jax.nn.sigmoid(zf)
    f_b = lb_b + (1.0 - lb_b) * jax.nn.sigmoid(zb)
    ii = lax.broadcasted_iota(I32, (n, n), 0)
    jj = lax.broadcasted_iota(I32, (n, n), 1)
    after = jnp.where(jj > ii, 1.0, 0.0).astype(BF16)
    before = jnp.where(jj < ii, 1.0, 0.0).astype(BF16)
    hi, lo = _split(jnp.log(f_f))
    kd_f = ((1.0 - f_f) * jnp.exp(_dot(after, hi) + _dot(after, lo))).astype(BF16)
    hi, lo = _split(jnp.log(f_b))
    kd_b = ((1.0 - f_b) * jnp.exp(_dot(before, hi) + _dot(before, lo))).astype(BF16)
    for h in range(HG_HEADS):
        ks = slice(h * HG_KDIM, (h + 1) * HG_KDIM)
        vs = slice(h * HG_VDIM, (h + 1) * HG_VDIM)
        sf_ref[0, h] = _dot_tn(vi[:, vs], kd_f[:, ks])
        sb_ref[0, h] = _dot_tn(vi[:, vs], kd_b[:, ks])


def _ctx_states(ctx, mods, nm, w_in_bf, lbl, ctx_row):
    nb, n, _ = ctx.shape
    st = jax.ShapeDtypeStruct((nb, HG_HEADS, HG_VDIM, HG_KDIM), F32)
    st_spec = pl.BlockSpec((1, HG_HEADS, HG_VDIM, HG_KDIM), lambda b: (b, 0, 0, 0))
    wcol = lambda k: pl.BlockSpec((D_MODEL, HG_KEYS), lambda b, k=k: (0, k))
    return pl.pallas_call(
        functools.partial(_ctx_body, ctx_row=ctx_row),
        grid=(nb,),
        in_specs=[pl.BlockSpec((1, n, D_MODEL), lambda b: (b, 0, 0)),
                  pl.BlockSpec(mods.shape, lambda b: (0, 0)),
                  pl.BlockSpec((1, D_MODEL), lambda b: (0, 0)),
                  wcol(1), wcol(2), wcol(3),
                  pl.BlockSpec(lbl.shape, lambda b: (0, 0, 0))],
        out_specs=[st_spec, st_spec],
        out_shape=[st, st],
        compiler_params=pltpu.CompilerParams(vmem_limit_bytes=VMEM_LIMIT),
        name="ctx_states",
    )(ctx, mods, nm, w_in_bf, w_in_bf, w_in_bf, lbl)


def _init_maps(lm_ref, tri_ref, reverse):
    lm = _level_map(reverse)
    lm_ref[...] = lm
    tri_ref[...] = jnp.where(lm == 1, 1.0, 0.0).astype(BF16)


def _level_map(reverse):
    ii = lax.broadcasted_iota(I32, (TT, TT), 0)
    jj = lax.broadcasted_iota(I32, (TT, TT), 1)
    if reverse:
        ii = TT - 1 - ii
        jj = TT - 1 - jj
    sh0 = CHUNK.bit_length() - 1
    ci = ii >> sh0
    cj = jj >> sh0
    lm = jnp.where((ci == cj) & (jj <= ii), 1, 0)
    for lvl in range(1, NCH.bit_length()):
        m = ((ci >> lvl) == (cj >> lvl)) & (((ci >> (lvl - 1)) & 1) == 1) & (((cj >> (lvl - 1)) & 1) == 0)
        lm = jnp.where(m, lvl + 1, lm)
    return lm


def _gla_gates(zf, lb, tri):
    f = lb + (1.0 - lb) * jax.nn.sigmoid(zf)
    hi, lo = _split(jnp.log(f))
    return 1.0 - f, _dot(tri, hi) + _dot(tri, lo)


def _gla_tile(q, k, cum, v, st_ref, lm, reverse):
    n_lvl = NCH.bit_length() - 1
    phys = list(range(NCH - 1, -1, -1)) if reverse else list(range(NCH))
    last = 0 if reverse else CHUNK - 1
    tot = [cum[a * CHUNK + last:a * CHUNK + last + 1, :] for a in range(NCH)]
    bnd = [jnp.zeros_like(tot[0])]
    for c in range(NCH):
        bnd.append(bnd[-1] + tot[phys[c]])
    q0 = q * jnp.exp(cum)
    kinv = (k * jnp.exp(-cum)).astype(BF16)
    qs = [[None] * NCH for _ in range(n_lvl + 1)]
    ks = [[None] * NCH for _ in range(n_lvl + 1)]
    for c in range(NCH):
        a = phys[c]
        rows = slice(a * CHUNK, (a + 1) * CHUNK)
        q0a = q0[rows]
        kend = k[rows] * jnp.exp(tot[a] - cum[rows])
        zero = jnp.zeros_like(q0a)
        for lvl in range(1, n_lvl + 1):
            s = 1 << (lvl - 1)
            mid = (c // (2 * s)) * 2 * s + s
            right = (c // s) % 2 == 1
            qs[lvl - 1][a] = q0a * jnp.exp(bnd[c] - bnd[mid]) if right else zero
            ks[lvl - 1][a] = zero if right else kend * jnp.exp(bnd[mid] - bnd[c + 1])
        qs[n_lvl][a] = q0a * jnp.exp(bnd[c])
        ks[n_lvl][a] = kend * jnp.exp(bnd[NCH] - bnd[c + 1])
    qs = [jnp.concatenate(p, axis=0).astype(BF16) for p in qs]
    ks = [jnp.concatenate(p, axis=0).astype(BF16) for p in ks]
    q0 = q0.astype(BF16)
    vb = v.astype(BF16)
    dec = jnp.exp(bnd[NCH])
    outs = []
    for h in range(HG_HEADS):
        kk = slice(h * HG_KDIM, (h + 1) * HG_KDIM)
        vv = slice(h * HG_VDIM, (h + 1) * HG_VDIM)
        sc = jnp.where(lm == 1, _dot_nt(q0[:, kk], kinv[:, kk]), 0.0)
        for lvl in range(n_lvl):
            sc = jnp.where(lm == lvl + 2, _dot_nt(qs[lvl][:, kk], ks[lvl][:, kk]), sc)
        st = st_ref[h]
        outs.append(_dot(sc.astype(BF16), vb[:, vv]) + _dot_nt(qs[n_lvl][:, kk], st.astype(BF16)))
        st_ref[h] = st * dec[:, kk] + _dot_tn(vb[:, vv], ks[n_lvl][:, kk])
    return jnp.concatenate(outs, axis=1)


def _mix_fwd_body(x_ref, mods_ref, nm_ref, w_ref, lbl_ref, cw_ref, s0_ref, wa_ref, wb_ref,
                  pk_ref, fz_ref, wa_bf_ref, wb_bf_ref, st_ref, lm_ref, tri_ref):
    b = pl.program_id(0)
    j = pl.program_id(1)

    @pl.when((b == 0) & (j == 0))
    def _():
        _init_maps(lm_ref, tri_ref, False)

    @pl.when(j == 0)
    def _():
        st_ref[...] = s0_ref[0]

    sh = mods_ref[pl.ds(b, 1), 0:D_MODEL]
    sc = mods_ref[pl.ds(b, 1), D_MODEL:2 * D_MODEL]
    hx = (_rms(x_ref[0], nm_ref[...]) * (1.0 + sc) + sh).astype(BF16)
    w = HG_KEYS
    proj = lambda k: _dot(hx, w_ref[:, k * w:(k + 1) * w])

    def cast_column_tiles(src_ref, dst_ref):
        tile = dst_ref.shape[3]
        for f in range(dst_ref.shape[1]):
            dst_ref[0, f] = src_ref[0, :, f * tile:(f + 1) * tile].astype(BF16)

    gate_b = proj(5)
    cast_column_tiles(wa_ref, wa_bf_ref)
    u = proj(6) * proj(7)
    cast_column_tiles(wb_ref, wb_bf_ref)
    zff = proj(1)
    col = lax.broadcasted_iota(I32, (TT, 1), 0) & (GRID_W - 1)
    up = jnp.where(col != 0, pltpu.roll(u, 1, axis=0), 0.0)
    un = jnp.where(col != GRID_W - 1, pltpu.roll(u, TT - 1, axis=0), 0.0)
    cw = cw_ref[...]
    pk_ref[0, :, 3 * w:4 * w] = (gate_b * (cw[0:1] * up + cw[1:2] * u + cw[2:3] * un)).astype(BF16)
    q = proj(0)
    lb_f, _ = _lower_bounds(lbl_ref)
    k, cum = _gla_gates(zff, lb_f, tri_ref[...])
    vi = proj(3)
    fz_ref[0, :, 0:w] = proj(2)
    pk_ref[0, :, 2 * w:3 * w] = proj(4).astype(BF16)
    fz_ref[0, :, w:2 * w] = _gla_tile(q, k, cum, vi, st_ref, lm_ref[...], False)
    pk_ref[0, :, 0:w] = q.astype(BF16)
    pk_ref[0, :, w:2 * w] = vi.astype(BF16)


def _mix_fwd(x, mods, nm, w_in_bf, lbl, cw, s0_f, wa, wb, col_tile):
    nb, t, _ = x.shape
    nt = t // TT
    ne, rows, cols = wa.shape
    per_expert = nb * nt // ne
    step_rows = rows // per_expert
    tok = lambda n, dt: jax.ShapeDtypeStruct((nb, t, n * HG_WIDTH), dt)
    tspec = lambda n: pl.BlockSpec((1, TT, n * HG_WIDTH), lambda b, j: (b, j, 0))
    cast_in = pl.BlockSpec((1, step_rows, cols), lambda b, j: ((b * nt + j) // per_expert, (b * nt + j) % per_expert, 0))
    cast_out = pl.BlockSpec((1, cols // col_tile, step_rows, col_tile),
                            lambda b, j: ((b * nt + j) // per_expert, 0, (b * nt + j) % per_expert, 0))
    cast_shape = jax.ShapeDtypeStruct((ne, cols // col_tile, rows, col_tile), BF16)
    return pl.pallas_call(
        _mix_fwd_body,
        grid=(nb, nt),
        in_specs=[pl.BlockSpec((1, TT, D_MODEL), lambda b, j: (b, j, 0)),
                  pl.BlockSpec(mods.shape, lambda b, j: (0, 0)),
                  pl.BlockSpec((1, D_MODEL), lambda b, j: (0, 0)),
                  pl.BlockSpec(w_in_bf.shape, lambda b, j: (0, 0)),
                  pl.BlockSpec(lbl.shape, lambda b, j: (0, 0, 0)),
                  pl.BlockSpec(cw.shape, lambda b, j: (0, 0)),
                  pl.BlockSpec((1, HG_HEADS, HG_VDIM, HG_KDIM), lambda b, j: (b, 0, 0, 0)),
                  cast_in, cast_in],
        out_specs=[tspec(4), tspec(2), cast_out, cast_out],
        out_shape=[tok(4, BF16), tok(2, F32), cast_shape, cast_shape],
        scratch_shapes=[pltpu.VMEM((HG_HEADS, HG_VDIM, HG_KDIM), F32), pltpu.VMEM((TT, TT), I32),
                        pltpu.VMEM((TT, TT), BF16)],
        compiler_params=pltpu.CompilerParams(dimension_semantics=("arbitrary", "arbitrary"),
                                             vmem_limit_bytes=VMEM_LIMIT),
        name="mix_fwd",
    )(x, mods, nm, w_in_bf, lbl, cw, s0_f, wa, wb)


def _mix_bwd_body(x_ref, pk_ref, fz_ref, mods_ref, lbl_ref, hgn_ref,
                  wo_ref, nf_ref, wr_ref, s0_ref, wc_ref,
                  x1_ref, hr_ref, at_ref, wc_bf_ref, st_ref, lm_ref, tri_ref):
    b = pl.program_id(0)
    j = pl.program_id(1)
    wc_bf_ref[...] = wc_ref[...].astype(BF16)

    @pl.when((b == 0) & (j == 0))
    def _():
        _init_maps(lm_ref, tri_ref, True)

    @pl.when(j == 0)
    def _():
        st_ref[...] = s0_ref[0]

    _, lb_b = _lower_bounds(lbl_ref)
    w = HG_WIDTH
    k, cum = _gla_gates(fz_ref[0, :, 0:w], lb_b, tri_ref[...])
    o_b = _gla_tile(pk_ref[0, :, 0:w].astype(F32), k, cum, pk_ref[0, :, w:2 * w], st_ref, lm_ref[...], True)
    o = fz_ref[0, :, w:2 * w] + o_b
    hgn = hgn_ref[...]
    heads = [_rms(o[:, h * HG_VDIM:(h + 1) * HG_VDIM], hgn) for h in range(HG_HEADS)]
    g = pk_ref[0, :, 2 * w:3 * w].astype(F32)
    ohg = jnp.concatenate(heads, axis=1) * _silu(g)
    cat = jnp.concatenate([ohg.astype(BF16), pk_ref[0, :, 3 * w:4 * w]], axis=1)
    mx = _dot(cat, wo_ref[...])
    mod = lambda k: mods_ref[pl.ds(b, 1), k * D_MODEL:(k + 1) * D_MODEL]
    x1 = x_ref[0] + mod(2) * mx
    x1_ref[0] = x1
    h2 = _rms(x1, nf_ref[...]) * (1.0 + mod(4)) + mod(3)
    for c in range(ACC_ROWS):
        hr_ref[0, pl.ds(c, TT, stride=ACC_ROWS), :] = h2[:, c * LANES:(c + 1) * LANES]
    r = _dot(jnp.concatenate(_split(h2), axis=0), wr_ref[...])
    lg = r[:TT, :LANES] + r[:TT, LANES:] + r[TT:, :LANES]
    lane = lax.broadcasted_iota(I32, lg.shape, 1)
    lg = jnp.where(lane < N_EXPERTS, lg, -jnp.inf)
    e = jnp.exp(lg - jnp.max(lg, axis=-1, keepdims=True))
    aff = e / jnp.sum(e, axis=-1, keepdims=True)
    at_ref[0] = aff.T[0:N_EXPERTS, :]


def _mix_bwd(x, pk, fz, mods, lbl, hgn, w_out_bf, nf, wr_pad, s0_b, wc):
    nb, t, _ = x.shape
    nt = t // TT
    cast_spec = pl.BlockSpec((1,) + wc.shape[1:], lambda b, j: (b * nt + j, 0, 0))
    rev = lambda b, j: (b, nt - 1 - j, 0)
    tspec = lambda a: pl.BlockSpec((1, TT, a.shape[2]), rev)
    xspec = pl.BlockSpec((1, TT, D_MODEL), rev)
    full2 = lambda a: pl.BlockSpec(a.shape, lambda b, j: (0, 0))
    return pl.pallas_call(
        _mix_bwd_body,
        grid=(nb, nt),
        in_specs=[xspec, tspec(pk), tspec(fz),
                  full2(mods), pl.BlockSpec(lbl.shape, lambda b, j: (0, 0, 0)), full2(hgn),
                  full2(w_out_bf), full2(nf), full2(wr_pad),
                  pl.BlockSpec((1, HG_HEADS, HG_VDIM, HG_KDIM), lambda b, j: (b, 0, 0, 0)),
                  cast_spec],
        out_specs=[xspec,
                   pl.BlockSpec((1, TT * ACC_ROWS, LANES), rev),
                   pl.BlockSpec((1, N_EXPERTS, TT), lambda b, j: (b, 0, nt - 1 - j)),
                   cast_spec],
        out_shape=[jax.ShapeDtypeStruct((nb, t, D_MODEL), F32),
                   jax.ShapeDtypeStruct((nb, t * ACC_ROWS, LANES), F32),
                   jax.ShapeDtypeStruct((nb, N_EXPERTS, t), F32),
                   jax.ShapeDtypeStruct(wc.shape, BF16)],
        scratch_shapes=[pltpu.VMEM((HG_HEADS, HG_VDIM, HG_KDIM), F32), pltpu.VMEM((TT, TT), I32),
                        pltpu.VMEM((TT, TT), BF16)],
        compiler_params=pltpu.CompilerParams(dimension_semantics=("arbitrary", "arbitrary"),
                                             vmem_limit_bytes=VMEM_LIMIT),
        name="mix_bwd",
    )(x, pk, fz, mods, lbl, hgn, w_out_bf, nf, wr_pad, s0_b, wc)


def _route_body(at_ref, idx_ref, gate_ref, *, cap):
    ne, t = at_ref.shape[1], at_ref.shape[2]
    blk = 2 * LANES
    aff = at_ref[0]

    def count(m):
        return jnp.sum(jnp.where(m, 1.0, 0.0), axis=1, keepdims=True)

    def enough(cand):
        return count(aff >= lax.bitcast_convert_type(cand, F32)) >= cap

    def bit_step(i, thr):
        hi = jnp.left_shift(jnp.int32(1), 30 - 2 * i)
        lo = jnp.right_shift(hi, 1)
        both, upper, lower = thr | hi | lo, thr | hi, thr | lo
        return jnp.where(enough(both), both, jnp.where(enough(upper), upper, jnp.where(enough(lower), lower, thr)))

    thr = lax.fori_loop(0, 16, bit_step, jnp.zeros((ne, 1), I32))
    gt = aff >= lax.bitcast_convert_type(thr + 1, F32)
    eq = (aff >= lax.bitcast_convert_type(thr, F32)) & jnp.logical_not(gt)
    need = cap - count(gt)

    ii = lax.broadcasted_iota(I32, (blk, blk), 0)
    jj = lax.broadcasted_iota(I32, (blk, blk), 1)
    upper = jnp.where(ii <= jj, 1.0, 0.0).astype(BF16)

    def prefix_blocks(m):
        mb = jnp.where(m, 1.0, 0.0).astype(BF16)
        carry = jnp.zeros((ne, 1), F32)
        out = []
        for kb in range(t // blk):
            p = _dot(mb[:, kb * blk:(kb + 1) * blk], upper) + carry
            out.append(p)
            carry = p[:, blk - 1:blk]
        return out

    eq_rank = jnp.concatenate(prefix_blocks(eq), axis=1)
    sel = gt | (eq & (eq_rank <= need))
    rank = jnp.concatenate(prefix_blocks(sel), axis=1).astype(I32)

    tok = lax.broadcasted_iota(I32, (ne, t), 1)
    x = jnp.where(sel, tok, -1)
    d = jnp.where(sel, tok - rank + 1, 0)
    g = aff
    for k in range(t.bit_length() - 1):
        left = lambda a: pltpu.roll(a, t - (1 << k), axis=1)
        ds = left(d)
        take = ((ds >> k) & 1) == 1
        keep = ((d >> k) & 1) == 0
        x = jnp.where(take, left(x), jnp.where(keep, x, -1))
        g = jnp.where(take, left(g), g)
        d = jnp.where(take, ds, jnp.where(keep, d, 0))
    idx_ref[0] = x[:, :cap]
    gate_ref[0] = g[:, :cap]


def _route(aff_t, cap):
    nb, ne, t = aff_t.shape
    spec = pl.BlockSpec((1, ne, cap), lambda b: (b, 0, 0))
    return pl.pallas_call(
        functools.partial(_route_body, cap=cap),
        grid=(nb,),
        in_specs=[pl.BlockSpec((1, ne, t), lambda b: (b, 0, 0))],
        out_specs=[spec, spec],
        out_shape=[jax.ShapeDtypeStruct((nb, ne, cap), I32), jax.ShapeDtypeStruct((nb, ne, cap), F32)],
        compiler_params=pltpu.CompilerParams(vmem_limit_bytes=VMEM_LIMIT),
        name="route",
    )(aff_t)


def _moe_body(idx_ref, gate_ref, hr_ref, wg_ref, wu_ref, wd_ref, x1_ref, mods_ref, nfin_ref,
              out_ref, acc_ref, rowg_ref, rows_ref, xs_ref, ys_ref, *, cap, n_ff):
    b = pl.program_id(0)
    s = pl.program_id(1)
    n_work = N_EXPERTS * n_ff
    e = s // n_ff
    ff = s % n_ff
    slot = e % 2
    group = SUBLANES

    def list_base(expert):
        return (b * N_EXPERTS + expert) * cap

    def gather(base, j0, n):
        for u in range(n):
            tkn = idx_ref[base + j0 + u]
            rowg_ref[pl.ds(pl.multiple_of((j0 + u) * ACC_ROWS, ACC_ROWS), ACC_ROWS), :] = (
                hr_ref[0, pl.ds(pl.multiple_of(tkn * ACC_ROWS, ACC_ROWS), ACC_ROWS), :])

    def scatter(base, j0, n):
        for g0 in range(0, n, group):
            dst = []
            val = []
            for u in range(g0, g0 + group):
                r = pl.multiple_of(idx_ref[base + j0 + u] * ACC_ROWS, ACC_ROWS)
                dst.append(r)
                val.append(acc_ref[pl.ds(r, ACC_ROWS), :] + gate_ref[base + j0 + u]
                           * rows_ref[pl.ds(pl.multiple_of((j0 + u) * ACC_ROWS, ACC_ROWS), ACC_ROWS), :])
            for r, v in zip(dst, val):
                acc_ref[pl.ds(r, ACC_ROWS), :] = v

    def in_groups(fn, base):
        def step(jo, carry):
            fn(base, jo * group, group)
            return carry
        lax.fori_loop(0, cap // group, step, 0)

    def lay_out_input(to_slot):
        for c in range(ACC_ROWS):
            xs_ref[to_slot, :, c * LANES:(c + 1) * LANES] = (
                rowg_ref[pl.ds(c, cap, stride=ACC_ROWS), :].astype(BF16))

    def lay_out_output():
        for c in range(ACC_ROWS):
            rows_ref[pl.ds(c, cap, stride=ACC_ROWS), :] = ys_ref[:, c * LANES:(c + 1) * LANES]

    def expert_tile():
        xs = xs_ref[slot]
        hid = _silu(_dot(xs, wg_ref[0, 0])) * _dot(xs, wu_ref[0, 0])
        ys_ref[...] = _dot(hid.astype(BF16), wd_ref[0]) + jnp.where(ff == 0, 0.0, ys_ref[...])

    @pl.when(s == 0)
    def _():
        acc_ref[...] = jnp.zeros_like(acc_ref)
        ys_ref[...] = jnp.zeros_like(ys_ref)
        in_groups(gather, list_base(0))
        lay_out_input(0)

    @pl.when((s < n_work) & (ff == 0))
    def _():
        gather(list_base(jnp.minimum(e + 1, N_EXPERTS - 1)), 0, cap)
        lay_out_output()
        expert_tile()

    if n_ff > 2:
        @pl.when((s < n_work) & (ff > 0) & (ff < n_ff - 1))
        def _():
            expert_tile()

    @pl.when((s < n_work) & (ff == n_ff - 1))
    def _():
        lay_out_input(1 - slot)
        scatter(list_base(jnp.maximum(e - 1, 0)), 0, cap)
        expert_tile()

    @pl.when(s == n_work)
    def _():
        lay_out_output()
        in_groups(scatter, list_base(N_EXPERTS - 1))

    @pl.when(s >= n_work)
    def _():
        row0 = (s - n_work) * (EPI_TOK * ACC_ROWS)
        moe = jnp.concatenate([acc_ref[pl.ds(row0 + c, EPI_TOK, stride=ACC_ROWS), :] for c in range(ACC_ROWS)], axis=1)
        g2 = mods_ref[pl.ds(b, 1), 5 * D_MODEL:6 * D_MODEL]
        out_ref[0] = _rms(x1_ref[0] + g2 * moe, nfin_ref[...])


def _moe(idx_flat, gate_flat, hr, wg, wu, wd, x1, mods, nfin, cap):
    nb, t, _ = x1.shape
    n_ff = wg.shape[1]
    n_work = N_EXPERTS * n_ff
    n_epi = t // EPI_TOK

    def w_idx(s):
        sc = jnp.minimum(s, n_work - 1)
        return sc // n_ff, sc % n_ff

    def epi(b, s, *_):
        return (b, jnp.maximum(s - n_work, 0), 0)

    grid_spec = pltpu.PrefetchScalarGridSpec(
        num_scalar_prefetch=2,
        grid=(nb, n_work + n_epi),
        in_specs=[pl.BlockSpec((1, t * ACC_ROWS, LANES), lambda b, s, *_: (b, 0, 0), pipeline_mode=pl.Buffered(1)),
                  pl.BlockSpec((1, 1, D_MODEL, DFF_TILE), lambda b, s, *_: w_idx(s) + (0, 0)),
                  pl.BlockSpec((1, 1, D_MODEL, DFF_TILE), lambda b, s, *_: w_idx(s) + (0, 0)),
                  pl.BlockSpec((1, DFF_TILE, D_MODEL), lambda b, s, *_: (w_idx(s)[0], w_idx(s)[1], 0)),
                  pl.BlockSpec((1, EPI_TOK, D_MODEL), epi),
                  pl.BlockSpec(mods.shape, lambda b, s, *_: (0, 0)),
                  pl.BlockSpec((1, D_MODEL), lambda b, s, *_: (0, 0))],
        out_specs=pl.BlockSpec((1, EPI_TOK, D_MODEL), epi),
        scratch_shapes=[pltpu.VMEM((t * ACC_ROWS, LANES), F32),
                        pltpu.VMEM((cap * ACC_ROWS, LANES), F32),
                        pltpu.VMEM((cap * ACC_ROWS, LANES), F32),
                        pltpu.VMEM((2, cap, D_MODEL), BF16),
                        pltpu.VMEM((cap, D_MODEL), F32)])
    return pl.pallas_call(
        functools.partial(_moe_body, cap=cap, n_ff=n_ff),
        grid_spec=grid_spec,
        out_shape=jax.ShapeDtypeStruct((nb, t, D_MODEL), F32),
        compiler_params=pltpu.CompilerParams(dimension_semantics=("arbitrary", "arbitrary"),
                                             vmem_limit_bytes=VMEM_LIMIT),
        name="moe",
    )(idx_flat, gate_flat, hr, wg, wu, wd, x1, mods, nfin)


def kernel(x, c, ctx, c_ctx, w_ada, b_ada, norm_mix, norm_ffn, w_in, lb_logits, hg_norm, conv_w,
           w_out, w_router, w_gate, w_up, w_down, norm_final):
    nb, t, d = x.shape
    assert d == D_MODEL and w_ada.shape[0] == 1 and nb < MOD_ROWS and t % TT == 0 and TT % GRID_W == 0
    assert w_in.shape[2] == 3 * HG_KEYS + 2 * HG_WIDTH + 3 * SC_WIDTH and w_router.shape[2] == N_EXPERTS
    cap = EC_CAPACITY * t // N_EXPERTS
    assert cap % LANES == 0 and w_gate.shape[3] % DFF_TILE == 0 and w_gate.shape[3] // DFF_TILE >= 2
    assert (nb * (t // TT)) % N_EXPERTS == 0 and (D_MODEL * N_EXPERTS) % (nb * (t // TT)) == 0

    cc = jnp.concatenate([c, c_ctx[None, :], jnp.zeros((MOD_ROWS - nb - 1, d), F32)], axis=0)
    w_in_bf = w_in[0].astype(BF16)
    w_out_bf = w_out[0].astype(BF16)
    wr_hi, wr_lo = _split(jnp.pad(w_router[0], ((0, 0), (0, LANES - N_EXPERTS))))
    wr_pad = jnp.concatenate([wr_hi, wr_lo], axis=1)
    nm, nf, nfin = norm_mix[0][None, :], norm_ffn[0][None, :], norm_final[None, :]
    hgn = hg_norm[0][None, :]

    mods = _adaln(cc, w_ada[0], b_ada[0][None, :])
    s0_f, s0_b = _ctx_states(ctx, mods, nm, w_in_bf, lb_logits, ctx_row=nb)
    steps = nb * (t // TT)
    sliced = lambda w: w.reshape((steps, w.shape[0] * w.shape[1] // steps) + w.shape[2:])
    pk, fz, wg, wu = _mix_fwd(x, mods, nm, w_in_bf, lb_logits, conv_w[0], s0_f, w_gate[0], w_up[0], DFF_TILE)
    x1, hr, aff_t, wd = _mix_bwd(x, pk, fz, mods, lb_logits, hgn, w_out_bf, nf, wr_pad, s0_b, sliced(w_down[0]))
    wd = wd.reshape(w_down[0].shape)
    idx, gates = _route(aff_t, cap)
    return _moe(idx.reshape(-1), gates.reshape(-1), hr, wg, wu, wd, x1, mods, nfin, cap)
```

```python
import functools

import jax
import jax.numpy as jnp
from jax import lax
from jax.experimental import pallas as pl
from jax.experimental.pallas import tpu as pltpu

F32 = jnp.float32
BF16 = jnp.bfloat16
I32 = jnp.int32

D_MODEL = 1024
N_MOD = 6
HG_HEADS = 4
HG_KDIM = 128
HG_VDIM = 128
HG_KEYS = HG_HEADS * HG_KDIM
HG_WIDTH = HG_HEADS * HG_VDIM
SC_WIDTH = D_MODEL - HG_WIDTH
GRID_W = 64
CHUNK = 32
N_EXPERTS = 16
EC_CAPACITY = 2
EPS = 1e-6

LANES = 128
SUBLANES = 8
VMEM_LIMIT = 62 * 1024 * 1024

TT = 256
NCH = TT // CHUNK
MOD_ROWS = 16
DFF_TILE = 1024
EPI_TOK = 256
ACC_ROWS = D_MODEL // LANES


def _dot(a, b):
    return jnp.dot(a, b, preferred_element_type=F32)


def _dot_nt(a, b):
    return lax.dot_general(a, b, (((1,), (1,)), ((), ())), preferred_element_type=F32)


def _dot_tn(a, b):
    return lax.dot_general(a, b, (((0,), (0,)), ((), ())), preferred_element_type=F32)


def _split(x):
    hi = x.astype(BF16)
    lo = (x - hi.astype(F32)).astype(BF16)
    return hi, lo


def _rms(x, gain):
    return x * lax.rsqrt(jnp.mean(x * x, axis=-1, keepdims=True) + EPS) * gain


def _silu(x):
    return x * jax.nn.sigmoid(x)


def _lower_bounds(lbl_ref):
    rows = [lbl_ref[l] for l in range(lbl_ref.shape[0])]
    m = functools.reduce(jnp.maximum, rows)
    es = [jnp.exp(r - m) for r in rows]
    sm0 = es[0] / functools.reduce(lambda a, b: a + b, es)
    return sm0[0:1, :], sm0[1:2, :]


def _ada_body(c_ref, w_ref, b_ref, o_ref):
    s_hi, s_lo = _split(_silu(c_ref[...]))
    w_hi, w_lo = _split(w_ref[...])
    o_ref[...] = _dot(s_hi, w_hi) + _dot(s_hi, w_lo) + _dot(s_lo, w_hi) + b_ref[...]


def _adaln(cc, w, b):
    n = w.shape[1]
    tn = n // 4
    return pl.pallas_call(
        _ada_body,
        grid=(n // tn,),
        in_specs=[pl.BlockSpec((MOD_ROWS, D_MODEL), lambda i: (0, 0)),
                  pl.BlockSpec((D_MODEL, tn), lambda i: (0, i)),
                  pl.BlockSpec((1, tn), lambda i: (0, i))],
        out_specs=pl.BlockSpec((MOD_ROWS, tn), lambda i: (0, i)),
        out_shape=jax.ShapeDtypeStruct((MOD_ROWS, n), F32),
        compiler_params=pltpu.CompilerParams(vmem_limit_bytes=VMEM_LIMIT),
        name="adaln",
    )(cc, w, b)


def _ctx_body(ctx_ref, mods_ref, nm_ref, wf_ref, wb_ref, wi_ref, lbl_ref, sf_ref, sb_ref, *, ctx_row):
    x = ctx_ref[0]
    n = x.shape[0]
    csh = mods_ref[ctx_row:ctx_row + 1, 0:D_MODEL]
    csc = mods_ref[ctx_row:ctx_row + 1, D_MODEL:2 * D_MODEL]
    hb = (_rms(x, nm_ref[...]) * (1.0 + csc) + csh).astype(BF16)
    zf = _dot(hb, wf_ref[...])
    zb = _dot(hb, wb_ref[...])
    vi = _dot(hb, wi_ref[...]).astype(BF16)
    lb_f, lb_b = _lower_bounds(lbl_ref)
    f_f = lb_f + (1.0 - lb_f) * jax.nn.sigmoid(zf)
    f_b = lb_b + (1.0 - lb_b) * jax.nn.sigmoid(zb)
    ii = lax.broadcasted_iota(I32, (n, n), 0)
    jj = lax.broadcasted_iota(I32, (n, n), 1)
    after = jnp.where(jj > ii, 1.0, 0.0).astype(BF16)
    before = jnp.where(jj < ii, 1.0, 0.0).astype(BF16)
    hi, lo = _split(jnp.log(f_f))
    kd_f = ((1.0 - f_f) * jnp.exp(_dot(after, hi) + _dot(after, lo))).astype(BF16)
    hi, lo = _split(jnp.log(f_b))
    kd_b = ((1.0 - f_b) * jnp.exp(_dot(before, hi) + _dot(before, lo))).astype(BF16)
    for h in range(HG_HEADS):
        ks = slice(h * HG_KDIM, (h + 1) * HG_KDIM)
        vs = slice(h * HG_VDIM, (h + 1) * HG_VDIM)
        sf_ref[0, h] = _dot_tn(vi[:, vs], kd_f[:, ks])
        sb_ref[0, h] = _dot_tn(vi[:, vs], kd_b[:, ks])


def _ctx_states(ctx, mods, nm, w_in_bf, lbl, ctx_row):
    nb, n, _ = ctx.shape
    st = jax.ShapeDtypeStruct((nb, HG_HEADS, HG_VDIM, HG_KDIM), F32)
    st_spec = pl.BlockSpec((1, HG_HEADS, HG_VDIM, HG_KDIM), lambda b: (b, 0, 0, 0))
    wcol = lambda k: pl.BlockSpec((D_MODEL, HG_KEYS), lambda b, k=k: (0, k))
    return pl.pallas_call(
        functools.partial(_ctx_body, ctx_row=ctx_row),
        grid=(nb,),
        in_specs=[pl.BlockSpec((1, n, D_MODEL), lambda b: (b, 0, 0)),
                  pl.BlockSpec(mods.shape, lambda b: (0, 0)),
                  pl.BlockSpec((1, D_MODEL), lambda b: (0, 0)),
                  wcol(1), wcol(2), wcol(3),
                  pl.BlockSpec(lbl.shape, lambda b: (0, 0, 0))],
        out_specs=[st_spec, st_spec],
        out_shape=[st, st],
        compiler_params=pltpu.CompilerParams(vmem_limit_bytes=VMEM_LIMIT),
        name="ctx_states",
    )(ctx, mods, nm, w_in_bf, w_in_bf, w_in_bf, lbl)


def _init_maps(lm_ref, tri_ref, reverse):
    lm = _level_map(reverse)
    lm_ref[...] = lm
    tri_ref[...] = jnp.where(lm == 1, 1.0, 0.0).astype(BF16)


def _level_map(reverse):
    ii = lax.broadcasted_iota(I32, (TT, TT), 0)
    jj = lax.broadcasted_iota(I32, (TT, TT), 1)
    if reverse:
        ii = TT - 1 - ii
        jj = TT - 1 - jj
    sh0 = CHUNK.bit_length() - 1
    ci = ii >> sh0
    cj = jj >> sh0
    lm = jnp.where((ci == cj) & (jj <= ii), 1, 0)
    for lvl in range(1, NCH.bit_length()):
        m = ((ci >> lvl) == (cj >> lvl)) & (((ci >> (lvl - 1)) & 1) == 1) & (((cj >> (lvl - 1)) & 1) == 0)
        lm = jnp.where(m, lvl + 1, lm)
    return lm


def _gla_gates(zf, lb, tri):
    f = lb + (1.0 - lb) * jax.nn.sigmoid(zf)
    hi, lo = _split(jnp.log(f))
    return 1.0 - f, _dot(tri, hi) + _dot(tri, lo)


def _gla_tile(q, k, cum, v, st_ref, lm, reverse):
    n_lvl = NCH.bit_length() - 1
    phys = list(range(NCH - 1, -1, -1)) if reverse else list(range(NCH))
    last = 0 if reverse else CHUNK - 1
    tot = [cum[a * CHUNK + last:a * CHUNK + last + 1, :] for a in range(NCH)]
    bnd = [jnp.zeros_like(tot[0])]
    for c in range(NCH):
        bnd.append(bnd[-1] + tot[phys[c]])
    q0 = q * jnp.exp(cum)
    kinv = (k * jnp.exp(-cum)).astype(BF16)
    qs = [[None] * NCH for _ in range(n_lvl + 1)]
    ks = [[None] * NCH for _ in range(n_lvl + 1)]
    for c in range(NCH):
        a = phys[c]
        rows = slice(a * CHUNK, (a + 1) * CHUNK)
        q0a = q0[rows]
        kend = k[rows] * jnp.exp(tot[a] - cum[rows])
        zero = jnp.zeros_like(q0a)
        for lvl in range(1, n_lvl + 1):
            s = 1 << (lvl - 1)
            mid = (c // (2 * s)) * 2 * s + s
            right = (c // s) % 2 == 1
            qs[lvl - 1][a] = q0a * jnp.exp(bnd[c] - bnd[mid]) if right else zero
            ks[lvl - 1][a] = zero if right else kend * jnp.exp(bnd[mid] - bnd[c + 1])
        qs[n_lvl][a] = q0a * jnp.exp(bnd[c])
        ks[n_lvl][a] = kend * jnp.exp(bnd[NCH] - bnd[c + 1])
    qs = [jnp.concatenate(p, axis=0).astype(BF16) for p in qs]
    ks = [jnp.concatenate(p, axis=0).astype(BF16) for p in ks]
    q0 = q0.astype(BF16)
    vb = v.astype(BF16)
    dec = jnp.exp(bnd[NCH])
    outs = []
    for h in range(HG_HEADS):
        kk = slice(h * HG_KDIM, (h + 1) * HG_KDIM)
        vv = slice(h * HG_VDIM, (h + 1) * HG_VDIM)
        sc = jnp.where(lm == 1, _dot_nt(q0[:, kk], kinv[:, kk]), 0.0)
        for lvl in range(n_lvl):
            sc = jnp.where(lm == lvl + 2, _dot_nt(qs[lvl][:, kk], ks[lvl][:, kk]), sc)
        st = st_ref[h]
        outs.append(_dot(sc.astype(BF16), vb[:, vv]) + _dot_nt(qs[n_lvl][:, kk], st.astype(BF16)))
        st_ref[h] = st * dec[:, kk] + _dot_tn(vb[:, vv], ks[n_lvl][:, kk])
    return jnp.concatenate(outs, axis=1)


def _mix_fwd_body(x_ref, mods_ref, nm_ref, w_ref, lbl_ref, cw_ref, s0_ref, wa_ref, wb_ref,
                  pk_ref, fz_ref, wa_bf_ref, wb_bf_ref, st_ref, lm_ref, tri_ref):
    b = pl.program_id(0)
    j = pl.program_id(1)

    @pl.when((b == 0) & (j == 0))
    def _():
        _init_maps(lm_ref, tri_ref, False)

    @pl.when(j == 0)
    def _():
        st_ref[...] = s0_ref[0]

    sh = mods_ref[pl.ds(b, 1), 0:D_MODEL]
    sc = mods_ref[pl.ds(b, 1), D_MODEL:2 * D_MODEL]
    hx = (_rms(x_ref[0], nm_ref[...]) * (1.0 + sc) + sh).astype(BF16)
    w = HG_KEYS
    proj = lambda k: _dot(hx, w_ref[:, k * w:(k + 1) * w])

    def cast_column_tiles(src_ref, dst_ref):
        tile = dst_ref.shape[3]
        for f in range(dst_ref.shape[1]):
            dst_ref[0, f] = src_ref[0, :, f * tile:(f + 1) * tile].astype(BF16)

    gate_b = proj(5)
    cast_column_tiles(wa_ref, wa_bf_ref)
    u = proj(6) * proj(7)
    cast_column_tiles(wb_ref, wb_bf_ref)
    zff = proj(1)
    col = lax.broadcasted_iota(I32, (TT, 1), 0) & (GRID_W - 1)
    up = jnp.where(col != 0, pltpu.roll(u, 1, axis=0), 0.0)
    un = jnp.where(col != GRID_W - 1, pltpu.roll(u, TT - 1, axis=0), 0.0)
    cw = cw_ref[...]
    pk_ref[0, :, 3 * w:4 * w] = (gate_b * (cw[0:1] * up + cw[1:2] * u + cw[2:3] * un)).astype(BF16)
    q = proj(0)
    lb_f, _ = _lower_bounds(lbl_ref)
    k, cum = _gla_gates(zff, lb_f, tri_ref[...])
    vi = proj(3)
    fz_ref[0, :, 0:w] = proj(2)
    pk_ref[0, :, 2 * w:3 * w] = proj(4).astype(BF16)
    fz_ref[0, :, w:2 * w] = _gla_tile(q, k, cum, vi, st_ref, lm_ref[...], False)
    pk_ref[0, :, 0:w] = q.astype(BF16)
    pk_ref[0, :, w:2 * w] = vi.astype(BF16)


def _mix_fwd(x, mods, nm, w_in_bf, lbl, cw, s0_f, wa, wb, col_tile):
    nb, t, _ = x.shape
    nt = t // TT
    ne, rows, cols = wa.shape
    per_expert = nb * nt // ne
    step_rows = rows // per_expert
    tok = lambda n, dt: jax.ShapeDtypeStruct((nb, t, n * HG_WIDTH), dt)
    tspec = lambda n: pl.BlockSpec((1, TT, n * HG_WIDTH), lambda b, j: (b, j, 0))
    cast_in = pl.BlockSpec((1, step_rows, cols), lambda b, j: ((b * nt + j) // per_expert, (b * nt + j) % per_expert, 0))
    cast_out = pl.BlockSpec((1, cols // col_tile, step_rows, col_tile),
                            lambda b, j: ((b * nt + j) // per_expert, 0, (b * nt + j) % per_expert, 0))
    cast_shape = jax.ShapeDtypeStruct((ne, cols // col_tile, rows, col_tile), BF16)
    return pl.pallas_call(
        _mix_fwd_body,
        grid=(nb, nt),
        in_specs=[pl.BlockSpec((1, TT, D_MODEL), lambda b, j: (b, j, 0)),
                  pl.BlockSpec(mods.shape, lambda b, j: (0, 0)),
                  pl.BlockSpec((1, D_MODEL), lambda b, j: (0, 0)),
                  pl.BlockSpec(w_in_bf.shape, lambda b, j: (0, 0)),
                  pl.BlockSpec(lbl.shape, lambda b, j: (0, 0, 0)),
                  pl.BlockSpec(cw.shape, lambda b, j: (0, 0)),
                  pl.BlockSpec((1, HG_HEADS, HG_VDIM, HG_KDIM), lambda b, j: (b, 0, 0, 0)),
                  cast_in, cast_in],
        out_specs=[tspec(4), tspec(2), cast_out, cast_out],
        out_shape=[tok(4, BF16), tok(2, F32), cast_shape, cast_shape],
        scratch_shapes=[pltpu.VMEM((HG_HEADS, HG_VDIM, HG_KDIM), F32), pltpu.VMEM((TT, TT), I32),
                        pltpu.VMEM((TT, TT), BF16)],
        compiler_params=pltpu.CompilerParams(dimension_semantics=("arbitrary", "arbitrary"),
                                             vmem_limit_bytes=VMEM_LIMIT),
        name="mix_fwd",
    )(x, mods, nm, w_in_bf, lbl, cw, s0_f, wa, wb)


def _mix_bwd_body(x_ref, pk_ref, fz_ref, mods_ref, lbl_ref, hgn_ref,
                  wo_ref, nf_ref, wr_ref, s0_ref, wc_ref,
                  x1_ref, hr_ref, at_ref, wc_bf_ref, st_ref, lm_ref, tri_ref):
    b = pl.program_id(0)
    j = pl.program_id(1)
    wc_bf_ref[...] = wc_ref[...].astype(BF16)

    @pl.when((b == 0) & (j == 0))
    def _():
        _init_maps(lm_ref, tri_ref, True)

    @pl.when(j == 0)
    def _():
        st_ref[...] = s0_ref[0]

    _, lb_b = _lower_bounds(lbl_ref)
    w = HG_WIDTH
    k, cum = _gla_gates(fz_ref[0, :, 0:w], lb_b, tri_ref[...])
    o_b = _gla_tile(pk_ref[0, :, 0:w].astype(F32), k, cum, pk_ref[0, :, w:2 * w], st_ref, lm_ref[...], True)
    o = fz_ref[0, :, w:2 * w] + o_b
    hgn = hgn_ref[...]
    heads = [_rms(o[:, h * HG_VDIM:(h + 1) * HG_VDIM], hgn) for h in range(HG_HEADS)]
    g = pk_ref[0, :, 2 * w:3 * w].astype(F32)
    ohg = jnp.concatenate(heads, axis=1) * _silu(g)
    cat = jnp.concatenate([ohg.astype(BF16), pk_ref[0, :, 3 * w:4 * w]], axis=1)
    mx = _dot(cat, wo_ref[...])
    mod = lambda k: mods_ref[pl.ds(b, 1), k * D_MODEL:(k + 1) * D_MODEL]
    x1 = x_ref[0] + mod(2) * mx
    h2 = _rms(x1, nf_ref[...]) * (1.0 + mod(4)) + mod(3)
    for c in range(ACC_ROWS):
        x1_ref[0, pl.ds(c, TT, stride=ACC_ROWS), :] = x1[:, c * LANES:(c + 1) * LANES]
        hr_ref[0, pl.ds(c, TT, stride=ACC_ROWS), :] = h2[:, c * LANES:(c + 1) * LANES]
    r = _dot(jnp.concatenate(_split(h2), axis=0), wr_ref[...])
    lg = r[:TT, :LANES] + r[:TT, LANES:] + r[TT:, :LANES]
    lane = lax.broadcasted_iota(I32, lg.shape, 1)
    lg = jnp.where(lane < N_EXPERTS, lg, -jnp.inf)
    e = jnp.exp(lg - jnp.max(lg, axis=-1, keepdims=True))
    aff = e / jnp.sum(e, axis=-1, keepdims=True)
    at_ref[0] = aff.T[0:N_EXPERTS, :]


def _mix_bwd(x, pk, fz, mods, lbl, hgn, w_out_bf, nf, wr_pad, s0_b, wc):
    nb, t, _ = x.shape
    nt = t // TT
    cast_spec = pl.BlockSpec((1,) + wc.shape[1:], lambda b, j: (b * nt + j, 0, 0))
    rev = lambda b, j: (b, nt - 1 - j, 0)
    tspec = lambda a: pl.BlockSpec((1, TT, a.shape[2]), rev)
    xspec = pl.BlockSpec((1, TT, D_MODEL), rev)
    full2 = lambda a: pl.BlockSpec(a.shape, lambda b, j: (0, 0))
    return pl.pallas_call(
        _mix_bwd_body,
        grid=(nb, nt),
        in_specs=[xspec, tspec(pk), tspec(fz),
                  full2(mods), pl.BlockSpec(lbl.shape, lambda b, j: (0, 0, 0)), full2(hgn),
                  full2(w_out_bf), full2(nf), full2(wr_pad),
                  pl.BlockSpec((1, HG_HEADS, HG_VDIM, HG_KDIM), lambda b, j: (b, 0, 0, 0)),
                  cast_spec],
        out_specs=[pl.BlockSpec((1, TT * ACC_ROWS, LANES), rev),
                   pl.BlockSpec((1, TT * ACC_ROWS, LANES), rev),
                   pl.BlockSpec((1, N_EXPERTS, TT), lambda b, j: (b, 0, nt - 1 - j)),
                   cast_spec],
        out_shape=[jax.ShapeDtypeStruct((nb, t * ACC_ROWS, LANES), F32),
                   jax.ShapeDtypeStruct((nb, t * ACC_ROWS, LANES), F32),
                   jax.ShapeDtypeStruct((nb, N_EXPERTS, t), F32),
                   jax.ShapeDtypeStruct(wc.shape, BF16)],
        scratch_shapes=[pltpu.VMEM((HG_HEADS, HG_VDIM, HG_KDIM), F32), pltpu.VMEM((TT, TT), I32),
                        pltpu.VMEM((TT, TT), BF16)],
        compiler_params=pltpu.CompilerParams(dimension_semantics=("arbitrary", "arbitrary"),
                                             vmem_limit_bytes=VMEM_LIMIT),
        name="mix_bwd",
    )(x, pk, fz, mods, lbl, hgn, w_out_bf, nf, wr_pad, s0_b, wc)


def _route_body(at_ref, idx_ref, gate_ref, *, cap):
    ne, t = at_ref.shape[1], at_ref.shape[2]
    blk = 2 * LANES
    aff = at_ref[0]

    def count(m):
        return jnp.sum(jnp.where(m, 1.0, 0.0), axis=1, keepdims=True)

    def enough(cand):
        return count(aff >= lax.bitcast_convert_type(cand, F32)) >= cap

    def bit_step(i, thr):
        hi = jnp.left_shift(jnp.int32(1), 30 - 2 * i)
        lo = jnp.right_shift(hi, 1)
        both, upper, lower = thr | hi | lo, thr | hi, thr | lo
        return jnp.where(enough(both), both, jnp.where(enough(upper), upper, jnp.where(enough(lower), lower, thr)))

    thr = lax.fori_loop(0, 16, bit_step, jnp.zeros((ne, 1), I32))
    gt = aff >= lax.bitcast_convert_type(thr + 1, F32)
    eq = (aff >= lax.bitcast_convert_type(thr, F32)) & jnp.logical_not(gt)
    need = cap - count(gt)

    ii = lax.broadcasted_iota(I32, (blk, blk), 0)
    jj = lax.broadcasted_iota(I32, (blk, blk), 1)
    upper = jnp.where(ii <= jj, 1.0, 0.0).astype(BF16)

    def prefix_blocks(m):
        mb = jnp.where(m, 1.0, 0.0).astype(BF16)
        carry = jnp.zeros((ne, 1), F32)
        out = []
        for kb in range(t // blk):
            p = _dot(mb[:, kb * blk:(kb + 1) * blk], upper) + carry
            out.append(p)
            carry = p[:, blk - 1:blk]
        return out

    eq_rank = jnp.concatenate(prefix_blocks(eq), axis=1)
    sel = gt | (eq & (eq_rank <= need))
    rank = jnp.concatenate(prefix_blocks(sel), axis=1).astype(I32)

    tok = lax.broadcasted_iota(I32, (ne, t), 1)
    x = jnp.where(sel, tok, -1)
    d = jnp.where(sel, tok - rank + 1, 0)
    g = aff
    for k in range(t.bit_length() - 1):
        left = lambda a: pltpu.roll(a, t - (1 << k), axis=1)
        ds = left(d)
        take = ((ds >> k) & 1) == 1
        keep = ((d >> k) & 1) == 0
        x = jnp.where(take, left(x), jnp.where(keep, x, -1))
        g = jnp.where(take, left(g), g)
        d = jnp.where(take, ds, jnp.where(keep, d, 0))
    idx_ref[0] = x[:, :cap]
    gate_ref[0] = g[:, :cap]


def _route(aff_t, cap):
    nb, ne, t = aff_t.shape
    spec = pl.BlockSpec((1, ne, cap), lambda b: (b, 0, 0))
    return pl.pallas_call(
        functools.partial(_route_body, cap=cap),
        grid=(nb,),
        in_specs=[pl.BlockSpec((1, ne, t), lambda b: (b, 0, 0))],
        out_specs=[spec, spec],
        out_shape=[jax.ShapeDtypeStruct((nb, ne, cap), I32), jax.ShapeDtypeStruct((nb, ne, cap), F32)],
        compiler_params=pltpu.CompilerParams(vmem_limit_bytes=VMEM_LIMIT),
        name="route",
    )(aff_t)


def _moe_body(idx_ref, gate_ref, hr_ref, wg_ref, wu_ref, wd_ref, x1_hbm, mods_ref, nfin_ref,
              out_ref, acc_ref, rowg_ref, rows_ref, xs_ref, ys_ref, seed_sem, *, cap, n_ff, n_epi):
    b = pl.program_id(0)
    s = pl.program_id(1)
    n_work = N_EXPERTS * n_ff
    e = s // n_ff
    ff = s % n_ff
    slot = e % 2
    group = SUBLANES

    def list_base(expert):
        return (b * N_EXPERTS + expert) * cap

    def gather(base, j0, n):
        for u in range(n):
            tkn = idx_ref[base + j0 + u]
            rowg_ref[pl.ds(pl.multiple_of((j0 + u) * ACC_ROWS, ACC_ROWS), ACC_ROWS), :] = (
                hr_ref[0, pl.ds(pl.multiple_of(tkn * ACC_ROWS, ACC_ROWS), ACC_ROWS), :])

    def scatter(base, j0, n):
        for g0 in range(0, n, group):
            dst = []
            val = []
            for u in range(g0, g0 + group):
                r = pl.multiple_of(idx_ref[base + j0 + u] * ACC_ROWS, ACC_ROWS)
                dst.append(r)
                val.append(acc_ref[pl.ds(r, ACC_ROWS), :] + gate_ref[base + j0 + u]
                           * rows_ref[pl.ds(pl.multiple_of((j0 + u) * ACC_ROWS, ACC_ROWS), ACC_ROWS), :])
            for r, v in zip(dst, val):
                acc_ref[pl.ds(r, ACC_ROWS), :] = v

    def in_groups(fn, base):
        def step(jo, carry):
            fn(base, jo * group, group)
            return carry
        lax.fori_loop(0, cap // group, step, 0)

    def lay_out_input(to_slot):
        for c in range(ACC_ROWS):
            xs_ref[to_slot, :, c * LANES:(c + 1) * LANES] = (
                rowg_ref[pl.ds(c, cap, stride=ACC_ROWS), :].astype(BF16))

    def lay_out_output():
        g2 = mods_ref[pl.ds(b, 1), 5 * D_MODEL:6 * D_MODEL]
        for c in range(ACC_ROWS):
            cols = slice(c * LANES, (c + 1) * LANES)
            rows_ref[pl.ds(c, cap, stride=ACC_ROWS), :] = ys_ref[:, cols] * g2[:, cols]

    chunk_rows = EPI_TOK * ACC_ROWS

    def seed_copy(sample, chunk):
        rows = pl.ds(chunk * chunk_rows, chunk_rows)
        return pltpu.make_async_copy(x1_hbm.at[sample, rows], acc_ref.at[rows], seed_sem.at[chunk])

    def expert_tile():
        xs = xs_ref[slot]
        hid = _silu(_dot(xs, wg_ref[0, 0])) * _dot(xs, wu_ref[0, 0])
        ys_ref[...] = _dot(hid.astype(BF16), wd_ref[0]) + jnp.where(ff == 0, 0.0, ys_ref[...])

    @pl.when((s == 0) & (b == 0))
    def _():
        for chunk in range(n_epi):
            seed_copy(0, chunk).start()

    @pl.when(s == 0)
    def _():
        ys_ref[...] = jnp.zeros_like(ys_ref)
        in_groups(gather, list_base(0))
        lay_out_input(0)

    @pl.when(s == n_ff - 1)
    def _():
        for chunk in range(n_epi):
            seed_copy(b, chunk).wait()

    @pl.when((s < n_work) & (ff == 0))
    def _():
        gather(list_base(jnp.minimum(e + 1, N_EXPERTS - 1)), 0, cap)
        lay_out_output()
        expert_tile()

    if n_ff > 2:
        @pl.when((s < n_work) & (ff > 0) & (ff < n_ff - 1))
        def _():
            expert_tile()

    @pl.when((s < n_work) & (ff == n_ff - 1))
    def _():
        lay_out_input(1 - slot)
        scatter(list_base(jnp.maximum(e - 1, 0)), 0, cap)
        expert_tile()

    @pl.when(s == n_work)
    def _():
        lay_out_output()
        in_groups(scatter, list_base(N_EXPERTS - 1))

    @pl.when(s >= n_work)
    def _():
        chunk = s - n_work
        row0 = chunk * chunk_rows
        tot = jnp.concatenate([acc_ref[pl.ds(row0 + c, EPI_TOK, stride=ACC_ROWS), :] for c in range(ACC_ROWS)], axis=1)
        out_ref[0] = _rms(tot, nfin_ref[...])

        @pl.when(b + 1 < pl.num_programs(0))
        def _():
            seed_copy(b + 1, chunk).start()


def _moe(idx_flat, gate_flat, hr, wg, wu, wd, x1, mods, nfin, cap):
    nb, t = x1.shape[0], x1.shape[1] // ACC_ROWS
    n_ff = wg.shape[1]
    n_work = N_EXPERTS * n_ff
    n_epi = t // EPI_TOK

    def w_idx(s):
        sc = jnp.minimum(s, n_work - 1)
        return sc // n_ff, sc % n_ff

    def epi(b, s, *_):
        return (b, jnp.maximum(s - n_work, 0), 0)

    grid_spec = pltpu.PrefetchScalarGridSpec(
        num_scalar_prefetch=2,
        grid=(nb, n_work + n_epi),
        in_specs=[pl.BlockSpec((1, t * ACC_ROWS, LANES), lambda b, s, *_: (b, 0, 0), pipeline_mode=pl.Buffered(1)),
                  pl.BlockSpec((1, 1, D_MODEL, DFF_TILE), lambda b, s, *_: w_idx(s) + (0, 0)),
                  pl.BlockSpec((1, 1, D_MODEL, DFF_TILE), lambda b, s, *_: w_idx(s) + (0, 0)),
                  pl.BlockSpec((1, DFF_TILE, D_MODEL), lambda b, s, *_: (w_idx(s)[0], w_idx(s)[1], 0)),
                  pl.BlockSpec(memory_space=pl.ANY),
                  pl.BlockSpec(mods.shape, lambda b, s, *_: (0, 0)),
                  pl.BlockSpec((1, D_MODEL), lambda b, s, *_: (0, 0))],
        out_specs=pl.BlockSpec((1, EPI_TOK, D_MODEL), epi),
        scratch_shapes=[pltpu.VMEM((t * ACC_ROWS, LANES), F32),
                        pltpu.VMEM((cap * ACC_ROWS, LANES), F32),
                        pltpu.VMEM((cap * ACC_ROWS, LANES), F32),
                        pltpu.VMEM((2, cap, D_MODEL), BF16),
                        pltpu.VMEM((cap, D_MODEL), F32),
                        pltpu.SemaphoreType.DMA((n_epi,))])
    return pl.pallas_call(
        functools.partial(_moe_body, cap=cap, n_ff=n_ff, n_epi=n_epi),
        grid_spec=grid_spec,
        out_shape=jax.ShapeDtypeStruct((nb, t, D_MODEL), F32),
        compiler_params=pltpu.CompilerParams(dimension_semantics=("arbitrary", "arbitrary"),
                                             vmem_limit_bytes=VMEM_LIMIT),
        name="moe",
    )(idx_flat, gate_flat, hr, wg, wu, wd, x1, mods, nfin)


def kernel(x, c, ctx, c_ctx, w_ada, b_ada, norm_mix, norm_ffn, w_in, lb_logits, hg_norm, conv_w,
           w_out, w_router, w_gate, w_up, w_down, norm_final):
    nb, t, d = x.shape
    assert d == D_MODEL and w_ada.shape[0] == 1 and nb < MOD_ROWS and t % TT == 0 and TT % GRID_W == 0
    assert w_in.shape[2] == 3 * HG_KEYS + 2 * HG_WIDTH + 3 * SC_WIDTH and w_router.shape[2] == N_EXPERTS
    cap = EC_CAPACITY * t // N_EXPERTS
    assert cap % LANES == 0 and w_gate.shape[3] % DFF_TILE == 0 and w_gate.shape[3] // DFF_TILE >= 2
    assert (nb * (t // TT)) % N_EXPERTS == 0 and (D_MODEL * N_EXPERTS) % (nb * (t // TT)) == 0

    cc = jnp.concatenate([c, c_ctx[None, :], jnp.zeros((MOD_ROWS - nb - 1, d), F32)], axis=0)
    w_in_bf = w_in[0].astype(BF16)
    w_out_bf = w_out[0].astype(BF16)
    wr_hi, wr_lo = _split(jnp.pad(w_router[0], ((0, 0), (0, LANES - N_EXPERTS))))
    wr_pad = jnp.concatenate([wr_hi, wr_lo], axis=1)
    nm, nf, nfin = norm_mix[0][None, :], norm_ffn[0][None, :], norm_final[None, :]
    hgn = hg_norm[0][None, :]

    mods = _adaln(cc, w_ada[0], b_ada[0][None, :])
    s0_f, s0_b = _ctx_states(ctx, mods, nm, w_in_bf, lb_logits, ctx_row=nb)
    steps = nb * (t // TT)
    sliced = lambda w: w.reshape((steps, w.shape[0] * w.shape[1] // steps) + w.shape[2:])
    pk, fz, wg, wu = _mix_fwd(x, mods, nm, w_in_bf, lb_logits, conv_w[0], s0_f, w_gate[0], w_up[0], DFF_TILE)
    x1, hr, aff_t, wd = _mix_bwd(x, pk, fz, mods, lb_logits, hgn, w_out_bf, nf, wr_pad, s0_b, sliced(w_down[0]))
    wd = wd.reshape(w_down[0].shape)
    idx, gates = _route(aff_t, cap)
    return _moe(idx.reshape(-1), gates.reshape(-1), hr, wg, wu, wd, x1, mods, nfin, cap)
```

```python
import functools

import jax
import jax.numpy as jnp
from jax import lax
from jax.experimental import pallas as pl
from jax.experimental.pallas import tpu as pltpu

F32 = jnp.float32
BF16 = jnp.bfloat16
I32 = jnp.int32

D_MODEL = 1024
N_MOD = 6
HG_HEADS = 4
HG_KDIM = 128
HG_VDIM = 128
HG_KEYS = HG_HEADS * HG_KDIM
HG_WIDTH = HG_HEADS * HG_VDIM
SC_WIDTH = D_MODEL - HG_WIDTH
GRID_W = 64
CHUNK = 32
N_EXPERTS = 16
EC_CAPACITY = 2
EPS = 1e-6

LANES = 128
SUBLANES = 8
VMEM_LIMIT = 62 * 1024 * 1024

TT = 256
NCH = TT // CHUNK
MOD_ROWS = 16
DFF_TILE = 1024
EPI_TOK = 256
ACC_ROWS = D_MODEL // LANES


def _dot(a, b):
    return jnp.dot(a, b, preferred_element_type=F32)


def _dot_nt(a, b):
    return lax.dot_general(a, b, (((1,), (1,)), ((), ())), preferred_element_type=F32)


def _dot_tn(a, b):
    return lax.dot_general(a, b, (((0,), (0,)), ((), ())), preferred_element_type=F32)


def _split(x):
    hi = x.astype(BF16)
    lo = (x - hi.astype(F32)).astype(BF16)
    return hi, lo


def _rms(x, gain):
    return x * lax.rsqrt(jnp.mean(x * x, axis=-1, keepdims=True) + EPS) * gain


def _silu(x):
    return x * jax.nn.sigmoid(x)


def _lower_bounds(lbl_ref):
    rows = [lbl_ref[l] for l in range(lbl_ref.shape[0])]
    m = functools.reduce(jnp.maximum, rows)
    es = [jnp.exp(r - m) for r in rows]
    sm0 = es[0] / functools.reduce(lambda a, b: a + b, es)
    return sm0[0:1, :], sm0[1:2, :]


def _ada_body(c_ref, w_ref, b_ref, o_ref):
    s_hi, s_lo = _split(_silu(c_ref[...]))
    w_hi, w_lo = _split(w_ref[...])
    o_ref[...] = _dot(s_hi, w_hi) + _dot(s_hi, w_lo) + _dot(s_lo, w_hi) + b_ref[...]


def _adaln(cc, w, b):
    n = w.shape[1]
    tn = n // 4
    return pl.pallas_call(
        _ada_body,
        grid=(n // tn,),
        in_specs=[pl.BlockSpec((MOD_ROWS, D_MODEL), lambda i: (0, 0)),
                  pl.BlockSpec((D_MODEL, tn), lambda i: (0, i)),
                  pl.BlockSpec((1, tn), lambda i: (0, i))],
        out_specs=pl.BlockSpec((MOD_ROWS, tn), lambda i: (0, i)),
        out_shape=jax.ShapeDtypeStruct((MOD_ROWS, n), F32),
        compiler_params=pltpu.CompilerParams(vmem_limit_bytes=VMEM_LIMIT),
        name="adaln",
    )(cc, w, b)


def _ctx_body(ctx_ref, mods_ref, nm_ref, wf_ref, wb_ref, wi_ref, lbl_ref, sf_ref, sb_ref, *, ctx_row):
    x = ctx_ref[0]
    n = x.shape[0]
    csh = mods_ref[ctx_row:ctx_row + 1, 0:D_MODEL]
    csc = mods_ref[ctx_row:ctx_row + 1, D_MODEL:2 * D_MODEL]
    hb = (_rms(x, nm_ref[...]) * (1.0 + csc) + csh).astype(BF16)
    zf = _dot(hb, wf_ref[...])
    zb = _dot(hb, wb_ref[...])
    vi = _dot(hb, wi_ref[...]).astype(BF16)
    lb_f, lb_b = _lower_bounds(lbl_ref)
    f_f = lb_f + (1.0 - lb_f) * jax.nn.sigmoid(zf)
    f_b = lb_b + (1.0 - lb_b) * jax.nn.sigmoid(zb)
    ii = lax.broadcasted_iota(I32, (n, n), 0)
    jj = lax.broadcasted_iota(I32, (n, n), 1)
    after = jnp.where(jj > ii, 1.0, 0.0).astype(BF16)
    before = jnp.where(jj < ii, 1.0, 0.0).astype(BF16)
    hi, lo = _split(jnp.log(f_f))
    kd_f = ((1.0 - f_f) * jnp.exp(_dot(after, hi) + _dot(after, lo))).astype(BF16)
    hi, lo = _split(jnp.log(f_b))
    kd_b = ((1.0 - f_b) * jnp.exp(_dot(before, hi) + _dot(before, lo))).astype(BF16)
    for h in range(HG_HEADS):
        ks = slice(h * HG_KDIM, (h + 1) * HG_KDIM)
        vs = slice(h * HG_VDIM, (h + 1) * HG_VDIM)
        sf_ref[0, h] = _dot_tn(vi[:, vs], kd_f[:, ks])
        sb_ref[0, h] = _dot_tn(vi[:, vs], kd_b[:, ks])


def _ctx_states(ctx, mods, nm, w_in_bf, lbl, ctx_row):
    nb, n, _ = ctx.shape
    st = jax.ShapeDtypeStruct((nb, HG_HEADS, HG_VDIM, HG_KDIM), F32)
    st_spec = pl.BlockSpec((1, HG_HEADS, HG_VDIM, HG_KDIM), lambda b: (b, 0, 0, 0))
    wcol = lambda k: pl.BlockSpec((D_MODEL, HG_KEYS), lambda b, k=k: (0, k))
    return pl.pallas_call(
        functools.partial(_ctx_body, ctx_row=ctx_row),
        grid=(nb,),
        in_specs=[pl.BlockSpec((1, n, D_MODEL), lambda b: (b, 0, 0)),
                  pl.BlockSpec(mods.shape, lambda b: (0, 0)),
                  pl.BlockSpec((1, D_MODEL), lambda b: (0, 0)),
                  wcol(1), wcol(2), wcol(3),
                  pl.BlockSpec(lbl.shape, lambda b: (0, 0, 0))],
        out_specs=[st_spec, st_spec],
        out_shape=[st, st],
        compiler_params=pltpu.CompilerParams(vmem_limit_bytes=VMEM_LIMIT),
        name="ctx_states",
    )(ctx, mods, nm, w_in_bf, w_in_bf, w_in_bf, lbl)


def _init_maps(lm_ref, tri_ref, reverse):
    lm = _level_map(reverse)
    lm_ref[...] = lm
    tri_ref[...] = jnp.where(lm == 1, 1.0, 0.0).astype(BF16)


def _level_map(reverse):
    ii = lax.broadcasted_iota(I32, (TT, TT), 0)
    jj = lax.broadcasted_iota(I32, (TT, TT), 1)
    if reverse:
        ii = TT - 1 - ii
        jj = TT - 1 - jj
    sh0 = CHUNK.bit_length() - 1
    ci = ii >> sh0
    cj = jj >> sh0
    lm = jnp.where((ci == cj) & (jj <= ii), 1, 0)
    for lvl in range(1, NCH.bit_length()):
        m = ((ci >> lvl) == (cj >> lvl)) & (((ci >> (lvl - 1)) & 1) == 1) & (((cj >> (lvl - 1)) & 1) == 0)
        lm = jnp.where(m, lvl + 1, lm)
    return lm


def _gla_gates(zf, lb, tri):
    f = lb + (1.0 - lb) * jax.nn.sigmoid(zf)
    hi, lo = _split(jnp.log(f))
    return 1.0 - f, _dot(tri, hi) + _dot(tri, lo)


def _gla_tile(q, k, cum, v, st_ref, lm, reverse):
    n_lvl = NCH.bit_length() - 1
    phys = list(range(NCH - 1, -1, -1)) if reverse else list(range(NCH))
    last = 0 if reverse else CHUNK - 1
    tot = [cum[a * CHUNK + last:a * CHUNK + last + 1, :] for a in range(NCH)]
    bnd = [jnp.zeros_like(tot[0])]
    for c in range(NCH):
        bnd.append(bnd[-1] + tot[phys[c]])
    q0 = q * jnp.exp(cum)
    kinv = (k * jnp.exp(-cum)).astype(BF16)
    qs = [[None] * NCH for _ in range(n_lvl + 1)]
    ks = [[None] * NCH for _ in range(n_lvl + 1)]
    for c in range(NCH):
        a = phys[c]
        rows = slice(a * CHUNK, (a + 1) * CHUNK)
        q0a = q0[rows]
        kend = k[rows] * jnp.exp(tot[a] - cum[rows])
        zero = jnp.zeros_like(q0a)
        for lvl in range(1, n_lvl + 1):
            s = 1 << (lvl - 1)
            mid = (c // (2 * s)) * 2 * s + s
            right = (c // s) % 2 == 1
            qs[lvl - 1][a] = q0a * jnp.exp(bnd[c] - bnd[mid]) if right else zero
            ks[lvl - 1][a] = zero if right else kend * jnp.exp(bnd[mid] - bnd[c + 1])
        qs[n_lvl][a] = q0a * jnp.exp(bnd[c])
        ks[n_lvl][a] = kend * jnp.exp(bnd[NCH] - bnd[c + 1])
    qs = [jnp.concatenate(p, axis=0).astype(BF16) for p in qs]
    ks = [jnp.concatenate(p, axis=0).astype(BF16) for p in ks]
    q0 = q0.astype(BF16)
    vb = v.astype(BF16)
    dec = jnp.exp(bnd[NCH])
    outs = []
    for h in range(HG_HEADS):
        kk = slice(h * HG_KDIM, (h + 1) * HG_KDIM)
        vv = slice(h * HG_VDIM, (h + 1) * HG_VDIM)
        sc = jnp.where(lm == 1, _dot_nt(q0[:, kk], kinv[:, kk]), 0.0)
        for lvl in range(n_lvl):
            sc = jnp.where(lm == lvl + 2, _dot_nt(qs[lvl][:, kk], ks[lvl][:, kk]), sc)
        st = st_ref[h]
        outs.append(_dot(sc.astype(BF16), vb[:, vv]) + _dot_nt(qs[n_lvl][:, kk], st.astype(BF16)))
        st_ref[h] = st * dec[:, kk] + _dot_tn(vb[:, vv], ks[n_lvl][:, kk])
    return jnp.concatenate(outs, axis=1)


def _mix_fwd_body(x_ref, mods_ref, nm_ref, w_ref, lbl_ref, cw_ref, s0_ref, wa_ref, wb_ref,
                  pk_ref, fz_ref, wa_bf_ref, wb_bf_ref, st_ref, lm_ref, tri_ref):
    b = pl.program_id(0)
    j = pl.program_id(1)

    @pl.when((b == 0) & (j == 0))
    def _():
        _init_maps(lm_ref, tri_ref, False)

    @pl.when(j == 0)
    def _():
        st_ref[...] = s0_ref[0]

    sh = mods_ref[pl.ds(b, 1), 0:D_MODEL]
    sc = mods_ref[pl.ds(b, 1), D_MODEL:2 * D_MODEL]
    hx = (_rms(x_ref[0], nm_ref[...]) * (1.0 + sc) + sh).astype(BF16)
    w = HG_KEYS
    proj = lambda k: _dot(hx, w_ref[:, k * w:(k + 1) * w])

    def cast_column_tiles(src_ref, dst_ref):
        tile = dst_ref.shape[3]
        for f in range(dst_ref.shape[1]):
            dst_ref[0, f] = src_ref[0, :, f * tile:(f + 1) * tile].astype(BF16)

    gate_b = proj(5)
    cast_column_tiles(wa_ref, wa_bf_ref)
    u = proj(6) * proj(7)
    cast_column_tiles(wb_ref, wb_bf_ref)
    zff = proj(1)
    col = lax.broadcasted_iota(I32, (TT, 1), 0) & (GRID_W - 1)
    up = jnp.where(col != 0, pltpu.roll(u, 1, axis=0), 0.0)
    un = jnp.where(col != GRID_W - 1, pltpu.roll(u, TT - 1, axis=0), 0.0)
    cw = cw_ref[...]
    pk_ref[0, :, 3 * w:4 * w] = (gate_b * (cw[0:1] * up + cw[1:2] * u + cw[2:3] * un)).astype(BF16)
    q = proj(0)
    lb_f, _ = _lower_bounds(lbl_ref)
    k, cum = _gla_gates(zff, lb_f, tri_ref[...])
    vi = proj(3)
    fz_ref[0, :, 0:w] = proj(2)
    pk_ref[0, :, 2 * w:3 * w] = proj(4).astype(BF16)
    fz_ref[0, :, w:2 * w] = _gla_tile(q, k, cum, vi, st_ref, lm_ref[...], False)
    pk_ref[0, :, 0:w] = q.astype(BF16)
    pk_ref[0, :, w:2 * w] = vi.astype(BF16)


def _mix_fwd(x, mods, nm, w_in_bf, lbl, cw, s0_f, wa, wb, col_tile):
    nb, t, _ = x.shape
    nt = t // TT
    ne, rows, cols = wa.shape
    per_expert = nb * nt // ne
    step_rows = rows // per_expert
    tok = lambda n, dt: jax.ShapeDtypeStruct((nb, t, n * HG_WIDTH), dt)
    tspec = lambda n: pl.BlockSpec((1, TT, n * HG_WIDTH), lambda b, j: (b, j, 0))
    cast_in = pl.BlockSpec((1, step_rows, cols), lambda b, j: ((b * nt + j) // per_expert, (b * nt + j) % per_expert, 0))
    cast_out = pl.BlockSpec((1, cols // col_tile, step_rows, col_tile),
                            lambda b, j: ((b * nt + j) // per_expert, 0, (b * nt + j) % per_expert, 0))
    cast_shape = jax.ShapeDtypeStruct((ne, cols // col_tile, rows, col_tile), BF16)
    return pl.pallas_call(
        _mix_fwd_body,
        grid=(nb, nt),
        in_specs=[pl.BlockSpec((1, TT, D_MODEL), lambda b, j: (b, j, 0)),
                  pl.BlockSpec(mods.shape, lambda b, j: (0, 0)),
                  pl.BlockSpec((1, D_MODEL), lambda b, j: (0, 0)),
                  pl.BlockSpec(w_in_bf.shape, lambda b, j: (0, 0)),
                  pl.BlockSpec(lbl.shape, lambda b, j: (0, 0, 0)),
                  pl.BlockSpec(cw.shape, lambda b, j: (0, 0)),
                  pl.BlockSpec((1, HG_HEADS, HG_VDIM, HG_KDIM), lambda b, j: (b, 0, 0, 0)),
                  cast_in, cast_in],
        out_specs=[tspec(4), tspec(2), cast_out, cast_out],
        out_shape=[tok(4, BF16), tok(2, F32), cast_shape, cast_shape],
        scratch_shapes=[pltpu.VMEM((HG_HEADS, HG_VDIM, HG_KDIM), F32), pltpu.VMEM((TT, TT), I32),
                        pltpu.VMEM((TT, TT), BF16)],
        compiler_params=pltpu.CompilerParams(dimension_semantics=("arbitrary", "arbitrary"),
                                             vmem_limit_bytes=VMEM_LIMIT),
        name="mix_fwd",
    )(x, mods, nm, w_in_bf, lbl, cw, s0_f, wa, wb)


def _mix_bwd_body(x_ref, pk_ref, fz_ref, mods_ref, lbl_ref, hgn_ref,
                  wo_ref, nf_ref, wr_ref, s0_ref, wc_ref,
                  x1_ref, hr_ref, at_ref, wc_bf_ref, st_ref, lm_ref, tri_ref):
    b = pl.program_id(0)
    j = pl.program_id(1)
    wc_bf_ref[...] = wc_ref[...].astype(BF16)

    @pl.when((b == 0) & (j == 0))
    def _():
        _init_maps(lm_ref, tri_ref, True)

    @pl.when(j == 0)
    def _():
        st_ref[...] = s0_ref[0]

    _, lb_b = _lower_bounds(lbl_ref)
    w = HG_WIDTH
    k, cum = _gla_gates(fz_ref[0, :, 0:w], lb_b, tri_ref[...])
    o_b = _gla_tile(pk_ref[0, :, 0:w].astype(F32), k, cum, pk_ref[0, :, w:2 * w], st_ref, lm_ref[...], True)
    o = fz_ref[0, :, w:2 * w] + o_b
    hgn = hgn_ref[...]
    heads = [_rms(o[:, h * HG_VDIM:(h + 1) * HG_VDIM], hgn) for h in range(HG_HEADS)]
    g = pk_ref[0, :, 2 * w:3 * w].astype(F32)
    ohg = jnp.concatenate(heads, axis=1) * _silu(g)
    cat = jnp.concatenate([ohg.astype(BF16), pk_ref[0, :, 3 * w:4 * w]], axis=1)
    mx = _dot(cat, wo_ref[...])
    mod = lambda k: mods_ref[pl.ds(b, 1), k * D_MODEL:(k + 1) * D_MODEL]
    x1 = x_ref[0] + mod(2) * mx
    h2 = _rms(x1, nf_ref[...]) * (1.0 + mod(4)) + mod(3)
    for c in range(ACC_ROWS):
        x1_ref[0, pl.ds(c, TT, stride=ACC_ROWS), :] = x1[:, c * LANES:(c + 1) * LANES]
        hr_ref[0, pl.ds(c, TT, stride=ACC_ROWS), :] = h2[:, c * LANES:(c + 1) * LANES]
    r = _dot(jnp.concatenate(_split(h2), axis=0), wr_ref[...])
    lg = r[:TT, :LANES] + r[:TT, LANES:] + r[TT:, :LANES]
    lane = lax.broadcasted_iota(I32, lg.shape, 1)
    lg = jnp.where(lane < N_EXPERTS, lg, -jnp.inf)
    e = jnp.exp(lg - jnp.max(lg, axis=-1, keepdims=True))
    aff = e / jnp.sum(e, axis=-1, keepdims=True)
    at_ref[0] = aff.T[0:N_EXPERTS, :]


def _mix_bwd(x, pk, fz, mods, lbl, hgn, w_out_bf, nf, wr_pad, s0_b, wc):
    nb, t, _ = x.shape
    nt = t // TT
    cast_spec = pl.BlockSpec((1,) + wc.shape[1:], lambda b, j: (b * nt + j, 0, 0))
    rev = lambda b, j: (b, nt - 1 - j, 0)
    tspec = lambda a: pl.BlockSpec((1, TT, a.shape[2]), rev)
    xspec = pl.BlockSpec((1, TT, D_MODEL), rev)
    full2 = lambda a: pl.BlockSpec(a.shape, lambda b, j: (0, 0))
    return pl.pallas_call(
        _mix_bwd_body,
        grid=(nb, nt),
        in_specs=[xspec, tspec(pk), tspec(fz),
                  full2(mods), pl.BlockSpec(lbl.shape, lambda b, j: (0, 0, 0)), full2(hgn),
                  full2(w_out_bf), full2(nf), full2(wr_pad),
                  pl.BlockSpec((1, HG_HEADS, HG_VDIM, HG_KDIM), lambda b, j: (b, 0, 0, 0)),
                  cast_spec],
        out_specs=[pl.BlockSpec((1, TT * ACC_ROWS, LANES), rev),
                   pl.BlockSpec((1, TT * ACC_ROWS, LANES), rev),
                   pl.BlockSpec((1, N_EXPERTS, TT), lambda b, j: (b, 0, nt - 1 - j)),
                   cast_spec],
        out_shape=[jax.ShapeDtypeStruct((nb, t * ACC_ROWS, LANES), F32),
                   jax.ShapeDtypeStruct((nb, t * ACC_ROWS, LANES), F32),
                   jax.ShapeDtypeStruct((nb, N_EXPERTS, t), F32),
                   jax.ShapeDtypeStruct(wc.shape, BF16)],
        scratch_shapes=[pltpu.VMEM((HG_HEADS, HG_VDIM, HG_KDIM), F32), pltpu.VMEM((TT, TT), I32),
                        pltpu.VMEM((TT, TT), BF16)],
        compiler_params=pltpu.CompilerParams(dimension_semantics=("arbitrary", "arbitrary"),
                                             vmem_limit_bytes=VMEM_LIMIT),
        name="mix_bwd",
    )(x, pk, fz, mods, lbl, hgn, w_out_bf, nf, wr_pad, s0_b, wc)


def _route_body(at_ref, idx_ref, gate_ref, *, cap):
    ne, t = at_ref.shape[1], at_ref.shape[2]
    blk = 2 * LANES
    aff = at_ref[0]

    def count(m):
        return jnp.sum(jnp.where(m, 1.0, 0.0), axis=1, keepdims=True)

    def enough(cand):
        return count(aff >= lax.bitcast_convert_type(cand, F32)) >= cap

    def bit_step(i, thr):
        hi = jnp.left_shift(jnp.int32(1), 30 - 2 * i)
        lo = jnp.right_shift(hi, 1)
        both, upper, lower = thr | hi | lo, thr | hi, thr | lo
        return jnp.where(enough(both), both, jnp.where(enough(upper), upper, jnp.where(enough(lower), lower, thr)))

    thr = lax.fori_loop(0, 16, bit_step, jnp.zeros((ne, 1), I32))
    gt = aff >= lax.bitcast_convert_type(thr + 1, F32)
    eq = (aff >= lax.bitcast_convert_type(thr, F32)) & jnp.logical_not(gt)
    need = cap - count(gt)

    ii = lax.broadcasted_iota(I32, (blk, blk), 0)
    jj = lax.broadcasted_iota(I32, (blk, blk), 1)
    upper = jnp.where(ii <= jj, 1.0, 0.0).astype(BF16)

    def prefix_blocks(m):
        mb = jnp.where(m, 1.0, 0.0).astype(BF16)
        carry = jnp.zeros((ne, 1), F32)
        out = []
        for kb in range(t // blk):
            p = _dot(mb[:, kb * blk:(kb + 1) * blk], upper) + carry
            out.append(p)
            carry = p[:, blk - 1:blk]
        return out

    eq_rank = jnp.concatenate(prefix_blocks(eq), axis=1)
    sel = gt | (eq & (eq_rank <= need))
    rank = jnp.concatenate(prefix_blocks(sel), axis=1).astype(I32)

    tok = lax.broadcasted_iota(I32, (ne, t), 1)
    x = jnp.where(sel, tok, -1)
    d = jnp.where(sel, tok - rank + 1, 0)
    g = aff
    for k in range(t.bit_length() - 1):
        left = lambda a: pltpu.roll(a, t - (1 << k), axis=1)
        ds = left(d)
        take = ((ds >> k) & 1) == 1
        keep = ((d >> k) & 1) == 0
        x = jnp.where(take, left(x), jnp.where(keep, x, -1))
        g = jnp.where(take, left(g), g)
        d = jnp.where(take, ds, jnp.where(keep, d, 0))
    idx_ref[0] = x[:, :cap]
    gate_ref[0] = g[:, :cap]


def _route(aff_t, cap):
    nb, ne, t = aff_t.shape
    spec = pl.BlockSpec((1, ne, cap), lambda b: (b, 0, 0))
    return pl.pallas_call(
        functools.partial(_route_body, cap=cap),
        grid=(nb,),
        in_specs=[pl.BlockSpec((1, ne, t), lambda b: (b, 0, 0))],
        out_specs=[spec, spec],
        out_shape=[jax.ShapeDtypeStruct((nb, ne, cap), I32), jax.ShapeDtypeStruct((nb, ne, cap), F32)],
        compiler_params=pltpu.CompilerParams(vmem_limit_bytes=VMEM_LIMIT),
        name="route",
    )(aff_t)


def _moe_body(idx_ref, gate_ref, hr_hbm, wg_ref, wu_ref, wd_ref, x1_hbm, mods_ref, nfin_ref,
              out_ref, acc_ref, hr_ref, rowg_ref, rows_ref, xs_ref, ys_ref, seed_sem, hr_sem, *, cap, n_ff, n_epi):
    b = pl.program_id(0)
    s = pl.program_id(1)
    n_work = N_EXPERTS * n_ff
    e = s // n_ff
    ff = s % n_ff
    slot = e % 2
    group = SUBLANES

    def list_base(expert):
        return (b * N_EXPERTS + expert) * cap

    def gather(base, j0, n):
        for u in range(n):
            tkn = idx_ref[base + j0 + u]
            rowg_ref[pl.ds(pl.multiple_of((j0 + u) * ACC_ROWS, ACC_ROWS), ACC_ROWS), :] = (
                hr_ref[pl.ds(pl.multiple_of(tkn * ACC_ROWS, ACC_ROWS), ACC_ROWS), :])

    def scatter(base, j0, n):
        for g0 in range(0, n, group):
            dst = []
            val = []
            for u in range(g0, g0 + group):
                r = pl.multiple_of(idx_ref[base + j0 + u] * ACC_ROWS, ACC_ROWS)
                dst.append(r)
                val.append(acc_ref[pl.ds(r, ACC_ROWS), :] + gate_ref[base + j0 + u]
                           * rows_ref[pl.ds(pl.multiple_of((j0 + u) * ACC_ROWS, ACC_ROWS), ACC_ROWS), :])
            for r, v in zip(dst, val):
                acc_ref[pl.ds(r, ACC_ROWS), :] = v

    def in_groups(fn, base):
        def step(jo, carry):
            fn(base, jo * group, group)
            return carry
        lax.fori_loop(0, cap // group, step, 0)

    def lay_out_input(to_slot):
        for c in range(ACC_ROWS):
            xs_ref[to_slot, :, c * LANES:(c + 1) * LANES] = (
                rowg_ref[pl.ds(c, cap, stride=ACC_ROWS), :].astype(BF16))

    def lay_out_output():
        g2 = mods_ref[pl.ds(b, 1), 5 * D_MODEL:6 * D_MODEL]
        for c in range(ACC_ROWS):
            cols = slice(c * LANES, (c + 1) * LANES)
            rows_ref[pl.ds(c, cap, stride=ACC_ROWS), :] = ys_ref[:, cols] * g2[:, cols]

    chunk_rows = EPI_TOK * ACC_ROWS

    def seed_copy(sample, chunk):
        rows = pl.ds(chunk * chunk_rows, chunk_rows)
        return pltpu.make_async_copy(x1_hbm.at[sample, rows], acc_ref.at[rows], seed_sem.at[chunk])

    def expert_tile():
        xs = xs_ref[slot]
        hid = _silu(_dot(xs, wg_ref[0, 0])) * _dot(xs, wu_ref[0, 0])
        ys_ref[...] = _dot(hid.astype(BF16), wd_ref[0]) + jnp.where(ff == 0, 0.0, ys_ref[...])

    def input_copy(sample):
        return pltpu.make_async_copy(hr_hbm.at[sample], hr_ref, hr_sem.at[0])

    @pl.when((s == 0) & (b == 0))
    def _():
        input_copy(0).start()
        for chunk in range(n_epi):
            seed_copy(0, chunk).start()

    @pl.when(s == 0)
    def _():
        input_copy(b).wait()
        ys_ref[...] = jnp.zeros_like(ys_ref)
        in_groups(gather, list_base(0))
        lay_out_input(0)

    @pl.when(s == n_ff - 1)
    def _():
        for chunk in range(n_epi):
            seed_copy(b, chunk).wait()

    @pl.when((s < n_work) & (ff == 0))
    def _():
        gather(list_base(jnp.minimum(e + 1, N_EXPERTS - 1)), 0, cap)
        lay_out_output()
        expert_tile()

    if n_ff > 2:
        @pl.when((s < n_work) & (ff > 0) & (ff < n_ff - 1))
        def _():
            expert_tile()

    @pl.when((s < n_work) & (ff == n_ff - 1))
    def _():
        lay_out_input(1 - slot)
        scatter(list_base(jnp.maximum(e - 1, 0)), 0, cap)
        expert_tile()

    @pl.when(s == n_work)
    def _():
        @pl.when(b + 1 < pl.num_programs(0))
        def _():
            input_copy(b + 1).start()

        lay_out_output()
        in_groups(scatter, list_base(N_EXPERTS - 1))

    @pl.when(s >= n_work)
    def _():
        chunk = s - n_work
        row0 = chunk * chunk_rows
        tot = jnp.concatenate([acc_ref[pl.ds(row0 + c, EPI_TOK, stride=ACC_ROWS), :] for c in range(ACC_ROWS)], axis=1)
        out_ref[0] = _rms(tot, nfin_ref[...])

        @pl.when(b + 1 < pl.num_programs(0))
        def _():
            seed_copy(b + 1, chunk).start()


def _moe(idx_flat, gate_flat, hr, wg, wu, wd, x1, mods, nfin, cap):
    nb, t = x1.shape[0], x1.shape[1] // ACC_ROWS
    n_ff = wg.shape[1]
    n_work = N_EXPERTS * n_ff
    n_epi = t // EPI_TOK

    def w_idx(s):
        sc = jnp.minimum(s, n_work - 1)
        return sc // n_ff, sc % n_ff

    def epi(b, s, *_):
        return (b, jnp.maximum(s - n_work, 0), 0)

    grid_spec = pltpu.PrefetchScalarGridSpec(
        num_scalar_prefetch=2,
        grid=(nb, n_work + n_epi),
        in_specs=[pl.BlockSpec(memory_space=pl.ANY),
                  pl.BlockSpec((1, 1, D_MODEL, DFF_TILE), lambda b, s, *_: w_idx(s) + (0, 0)),
                  pl.BlockSpec((1, 1, D_MODEL, DFF_TILE), lambda b, s, *_: w_idx(s) + (0, 0)),
                  pl.BlockSpec((1, DFF_TILE, D_MODEL), lambda b, s, *_: (w_idx(s)[0], w_idx(s)[1], 0)),
                  pl.BlockSpec(memory_space=pl.ANY),
                  pl.BlockSpec(mods.shape, lambda b, s, *_: (0, 0)),
                  pl.BlockSpec((1, D_MODEL), lambda b, s, *_: (0, 0))],
        out_specs=pl.BlockSpec((1, EPI_TOK, D_MODEL), epi),
        scratch_shapes=[pltpu.VMEM((t * ACC_ROWS, LANES), F32),
                        pltpu.VMEM((t * ACC_ROWS, LANES), F32),
                        pltpu.VMEM((cap * ACC_ROWS, LANES), F32),
                        pltpu.VMEM((cap * ACC_ROWS, LANES), F32),
                        pltpu.VMEM((2, cap, D_MODEL), BF16),
                        pltpu.VMEM((cap, D_MODEL), F32),
                        pltpu.SemaphoreType.DMA((n_epi,)),
                        pltpu.SemaphoreType.DMA((1,))])
    return pl.pallas_call(
        functools.partial(_moe_body, cap=cap, n_ff=n_ff, n_epi=n_epi),
        grid_spec=grid_spec,
        out_shape=jax.ShapeDtypeStruct((nb, t, D_MODEL), F32),
        compiler_params=pltpu.CompilerParams(dimension_semantics=("arbitrary", "arbitrary"),
                                             vmem_limit_bytes=VMEM_LIMIT),
        name="moe",
    )(idx_flat, gate_flat, hr, wg, wu, wd, x1, mods, nfin)


def kernel(x, c, ctx, c_ctx, w_ada, b_ada, norm_mix, norm_ffn, w_in, lb_logits, hg_norm, conv_w,
           w_out, w_router, w_gate, w_up, w_down, norm_final):
    nb, t, d = x.shape
    assert d == D_MODEL and w_ada.shape[0] == 1 and nb < MOD_ROWS and t % TT == 0 and TT % GRID_W == 0
    assert w_in.shape[2] == 3 * HG_KEYS + 2 * HG_WIDTH + 3 * SC_WIDTH and w_router.shape[2] == N_EXPERTS
    cap = EC_CAPACITY * t // N_EXPERTS
    assert cap % LANES == 0 and w_gate.shape[3] % DFF_TILE == 0 and w_gate.shape[3] // DFF_TILE >= 2
    assert (nb * (t // TT)) % N_EXPERTS == 0 and (D_MODEL * N_EXPERTS) % (nb * (t // TT)) == 0

    cc = jnp.concatenate([c, c_ctx[None, :], jnp.zeros((MOD_ROWS - nb - 1, d), F32)], axis=0)
    w_in_bf = w_in[0].astype(BF16)
    w_out_bf = w_out[0].astype(BF16)
    wr_hi, wr_lo = _split(jnp.pad(w_router[0], ((0, 0), (0, LANES - N_EXPERTS))))
    wr_pad = jnp.concatenate([wr_hi, wr_lo], axis=1)
    nm, nf, nfin = norm_mix[0][None, :], norm_ffn[0][None, :], norm_final[None, :]
    hgn = hg_norm[0][None, :]

    mods = _adaln(cc, w_ada[0], b_ada[0][None, :])
    s0_f, s0_b = _ctx_states(ctx, mods, nm, w_in_bf, lb_logits, ctx_row=nb)
    steps = nb * (t // TT)
    sliced = lambda w: w.reshape((steps, w.shape[0] * w.shape[1] // steps) + w.shape[2:])
    pk, fz, wg, wu = _mix_fwd(x, mods, nm, w_in_bf, lb_logits, conv_w[0], s0_f, w_gate[0], w_up[0], DFF_TILE)
    x1, hr, aff_t, wd = _mix_bwd(x, pk, fz, mods, lb_logits, hgn, w_out_bf, nf, wr_pad, s0_b, sliced(w_down[0]))
    wd = wd.reshape(w_down[0].shape)
    idx, gates = _route(aff_t, cap)
    return _moe(idx.reshape(-1), gates.reshape(-1), hr, wg, wu, wd, x1, mods, nfin, cap)
```

```python
import functools

import jax
import jax.numpy as jnp
from jax import lax
from jax.experimental import pallas as pl
from jax.experimental.pallas import tpu as pltpu

F32 = jnp.float32
BF16 = jnp.bfloat16
I32 = jnp.int32

D_MODEL = 1024
N_MOD = 6
HG_HEADS = 4
HG_KDIM = 128
HG_VDIM = 128
HG_KEYS = HG_HEADS * HG_KDIM
HG_WIDTH = HG_HEADS * HG_VDIM
SC_WIDTH = D_MODEL - HG_WIDTH
GRID_W = 64
CHUNK = 32
N_EXPERTS = 16
EC_CAPACITY = 2
EPS = 1e-6

LANES = 128
SUBLANES = 8
VMEM_LIMIT = 62 * 1024 * 1024

TT = 256
NCH = TT // CHUNK
MOD_ROWS = 16
DFF_TILE = 1024
EPI_TOK = 512
ACC_ROWS = D_MODEL // LANES


def _dot(a, b):
    return jnp.dot(a, b, preferred_element_type=F32)


def _dot_nt(a, b):
    return lax.dot_general(a, b, (((1,), (1,)), ((), ())), preferred_element_type=F32)


def _dot_tn(a, b):
    return lax.dot_general(a, b, (((0,), (0,)), ((), ())), preferred_element_type=F32)


def _split(x):
    hi = x.astype(BF16)
    lo = (x - hi.astype(F32)).astype(BF16)
    return hi, lo


def _rms(x, gain):
    return x * lax.rsqrt(jnp.mean(x * x, axis=-1, keepdims=True) + EPS) * gain


def _silu(x):
    return x * jax.nn.sigmoid(x)


def _lower_bounds(lbl_ref):
    rows = [lbl_ref[l] for l in range(lbl_ref.shape[0])]
    m = functools.reduce(jnp.maximum, rows)
    es = [jnp.exp(r - m) for r in rows]
    sm0 = es[0] / functools.reduce(lambda a, b: a + b, es)
    return sm0[0:1, :], sm0[1:2, :]


def _ada_body(c_ref, w_ref, b_ref, o_ref):
    s_hi, s_lo = _split(_silu(c_ref[...]))
    w_hi, w_lo = _split(w_ref[...])
    o_ref[...] = _dot(s_hi, w_hi) + _dot(s_hi, w_lo) + _dot(s_lo, w_hi) + b_ref[...]


def _adaln(cc, w, b):
    n = w.shape[1]
    tn = n // 4
    return pl.pallas_call(
        _ada_body,
        grid=(n // tn,),
        in_specs=[pl.BlockSpec((MOD_ROWS, D_MODEL), lambda i: (0, 0)),
                  pl.BlockSpec((D_MODEL, tn), lambda i: (0, i)),
                  pl.BlockSpec((1, tn), lambda i: (0, i))],
        out_specs=pl.BlockSpec((MOD_ROWS, tn), lambda i: (0, i)),
        out_shape=jax.ShapeDtypeStruct((MOD_ROWS, n), F32),
        compiler_params=pltpu.CompilerParams(vmem_limit_bytes=VMEM_LIMIT),
        name="adaln",
    )(cc, w, b)


def _ctx_body(ctx_ref, mods_ref, nm_ref, wf_ref, wb_ref, wi_ref, lbl_ref, sf_ref, sb_ref, *, ctx_row):
    x = ctx_ref[0]
    n = x.shape[0]
    csh = mods_ref[ctx_row:ctx_row + 1, 0:D_MODEL]
    csc = mods_ref[ctx_row:ctx_row + 1, D_MODEL:2 * D_MODEL]
    hb = (_rms(x, nm_ref[...]) * (1.0 + csc) + csh).astype(BF16)
    zf = _dot(hb, wf_ref[...])
    zb = _dot(hb, wb_ref[...])
    vi = _dot(hb, wi_ref[...]).astype(BF16)
    lb_f, lb_b = _lower_bounds(lbl_ref)
    f_f = lb_f + (1.0 - lb_f) * jax.nn.sigmoid(zf)
    f_b = lb_b + (1.0 - lb_b) * jax.nn.sigmoid(zb)
    ii = lax.broadcasted_iota(I32, (n, n), 0)
    jj = lax.broadcasted_iota(I32, (n, n), 1)
    after = jnp.where(jj > ii, 1.0, 0.0).astype(BF16)
    before = jnp.where(jj < ii, 1.0, 0.0).astype(BF16)
    hi, lo = _split(jnp.log(f_f))
    kd_f = ((1.0 - f_f) * jnp.exp(_dot(after, hi) + _dot(after, lo))).astype(BF16)
    hi, lo = _split(jnp.log(f_b))
    kd_b = ((1.0 - f_b) * jnp.exp(_dot(before, hi) + _dot(before, lo))).astype(BF16)
    for h in range(HG_HEADS):
        ks = slice(h * HG_KDIM, (h + 1) * HG_KDIM)
        vs = slice(h * HG_VDIM, (h + 1) * HG_VDIM)
        sf_ref[0, h] = _dot_tn(vi[:, vs], kd_f[:, ks])
        sb_ref[0, h] = _dot_tn(vi[:, vs], kd_b[:, ks])


def _ctx_states(ctx, mods, nm, w_in_bf, lbl, ctx_row):
    nb, n, _ = ctx.shape
    st = jax.ShapeDtypeStruct((nb, HG_HEADS, HG_VDIM, HG_KDIM), F32)
    st_spec = pl.BlockSpec((1, HG_HEADS, HG_VDIM, HG_KDIM), lambda b: (b, 0, 0, 0))
    wcol = lambda k: pl.BlockSpec((D_MODEL, HG_KEYS), lambda b, k=k: (0, k))
    return pl.pallas_call(
        functools.partial(_ctx_body, ctx_row=ctx_row),
        grid=(nb,),
        in_specs=[pl.BlockSpec((1, n, D_MODEL), lambda b: (b, 0, 0)),
                  pl.BlockSpec(mods.shape, lambda b: (0, 0)),
                  pl.BlockSpec((1, D_MODEL), lambda b: (0, 0)),
                  wcol(1), wcol(2), wcol(3),
                  pl.BlockSpec(lbl.shape, lambda b: (0, 0, 0))],
        out_specs=[st_spec, st_spec],
        out_shape=[st, st],
        compiler_params=pltpu.CompilerParams(vmem_limit_bytes=VMEM_LIMIT),
        name="ctx_states",
    )(ctx, mods, nm, w_in_bf, w_in_bf, w_in_bf, lbl)


def _init_maps(lm_ref, tri_ref, reverse):
    lm = _level_map(reverse)
    lm_ref[...] = lm
    tri_ref[...] = jnp.where(lm == 1, 1.0, 0.0).astype(BF16)


def _level_map(reverse):
    ii = lax.broadcasted_iota(I32, (TT, TT), 0)
    jj = lax.broadcasted_iota(I32, (TT, TT), 1)
    if reverse:
        ii = TT - 1 - ii
        jj = TT - 1 - jj
    sh0 = CHUNK.bit_length() - 1
    ci = ii >> sh0
    cj = jj >> sh0
    lm = jnp.where((ci == cj) & (jj <= ii), 1, 0)
    for lvl in range(1, NCH.bit_length()):
        m = ((ci >> lvl) == (cj >> lvl)) & (((ci >> (lvl - 1)) & 1) == 1) & (((cj >> (lvl - 1)) & 1) == 0)
        lm = jnp.where(m, lvl + 1, lm)
    return lm


def _gla_gates(zf, lb, tri):
    f = lb + (1.0 - lb) * jax.nn.sigmoid(zf)
    hi, lo = _split(jnp.log(f))
    return 1.0 - f, _dot(tri, hi) + _dot(tri, lo)


def _gla_tile(q, k, cum, v, st_ref, lm, reverse):
    n_lvl = NCH.bit_length() - 1
    phys = list(range(NCH - 1, -1, -1)) if reverse else list(range(NCH))
    last = 0 if reverse else CHUNK - 1
    tot = [cum[a * CHUNK + last:a * CHUNK + last + 1, :] for a in range(NCH)]
    bnd = [jnp.zeros_like(tot[0])]
    for c in range(NCH):
        bnd.append(bnd[-1] + tot[phys[c]])
    q0 = q * jnp.exp(cum)
    kinv = (k * jnp.exp(-cum)).astype(BF16)
    qs = [[None] * NCH for _ in range(n_lvl + 1)]
    ks = [[None] * NCH for _ in range(n_lvl + 1)]
    for c in range(NCH):
        a = phys[c]
        rows = slice(a * CHUNK, (a + 1) * CHUNK)
        q0a = q0[rows]
        kend = k[rows] * jnp.exp(tot[a] - cum[rows])
        zero = jnp.zeros_like(q0a)
        for lvl in range(1, n_lvl + 1):
            s = 1 << (lvl - 1)
            mid = (c // (2 * s)) * 2 * s + s
            right = (c // s) % 2 == 1
            qs[lvl - 1][a] = q0a * jnp.exp(bnd[c] - bnd[mid]) if right else zero
            ks[lvl - 1][a] = zero if right else kend * jnp.exp(bnd[mid] - bnd[c + 1])
        qs[n_lvl][a] = q0a * jnp.exp(bnd[c])
        ks[n_lvl][a] = kend * jnp.exp(bnd[NCH] - bnd[c + 1])
    qs = [jnp.concatenate(p, axis=0).astype(BF16) for p in qs]
    ks = [jnp.concatenate(p, axis=0).astype(BF16) for p in ks]
    q0 = q0.astype(BF16)
    vb = v.astype(BF16)
    dec = jnp.exp(bnd[NCH])
    outs = []
    for h in range(HG_HEADS):
        kk = slice(h * HG_KDIM, (h + 1) * HG_KDIM)
        vv = slice(h * HG_VDIM, (h + 1) * HG_VDIM)
        sc = jnp.where(lm == 1, _dot_nt(q0[:, kk], kinv[:, kk]), 0.0)
        for lvl in range(n_lvl):
            sc = jnp.where(lm == lvl + 2, _dot_nt(qs[lvl][:, kk], ks[lvl][:, kk]), sc)
        st = st_ref[h]
        outs.append(_dot(sc.astype(BF16), vb[:, vv]) + _dot_nt(qs[n_lvl][:, kk], st.astype(BF16)))
        st_ref[h] = st * dec[:, kk] + _dot_tn(vb[:, vv], ks[n_lvl][:, kk])
    return jnp.concatenate(outs, axis=1)


def _mix_fwd_body(x_ref, mods_ref, nm_ref, w_ref, lbl_ref, cw_ref, s0_ref, wa_ref, wb_ref,
                  pk_ref, fz_ref, wa_bf_ref, wb_bf_ref, st_ref, lm_ref, tri_ref):
    b = pl.program_id(0)
    j = pl.program_id(1)

    @pl.when((b == 0) & (j == 0))
    def _():
        _init_maps(lm_ref, tri_ref, False)

    @pl.when(j == 0)
    def _():
        st_ref[...] = s0_ref[0]

    sh = mods_ref[pl.ds(b, 1), 0:D_MODEL]
    sc = mods_ref[pl.ds(b, 1), D_MODEL:2 * D_MODEL]
    hx = (_rms(x_ref[0], nm_ref[...]) * (1.0 + sc) + sh).astype(BF16)
    w = HG_KEYS
    proj = lambda k: _dot(hx, w_ref[:, k * w:(k + 1) * w])

    def cast_column_tiles(src_ref, dst_ref):
        tile = dst_ref.shape[3]
        for f in range(dst_ref.shape[1]):
            dst_ref[0, f] = src_ref[0, :, f * tile:(f + 1) * tile].astype(BF16)

    gate_b = proj(5)
    cast_column_tiles(wa_ref, wa_bf_ref)
    u = proj(6) * proj(7)
    cast_column_tiles(wb_ref, wb_bf_ref)
    zff = proj(1)
    col = lax.broadcasted_iota(I32, (TT, 1), 0) & (GRID_W - 1)
    up = jnp.where(col != 0, pltpu.roll(u, 1, axis=0), 0.0)
    un = jnp.where(col != GRID_W - 1, pltpu.roll(u, TT - 1, axis=0), 0.0)
    cw = cw_ref[...]
    pk_ref[0, :, 3 * w:4 * w] = (gate_b * (cw[0:1] * up + cw[1:2] * u + cw[2:3] * un)).astype(BF16)
    q = proj(0)
    lb_f, _ = _lower_bounds(lbl_ref)
    k, cum = _gla_gates(zff, lb_f, tri_ref[...])
    vi = proj(3)
    fz_ref[0, :, 0:w] = proj(2)
    pk_ref[0, :, 2 * w:3 * w] = proj(4).astype(BF16)
    fz_ref[0, :, w:2 * w] = _gla_tile(q, k, cum, vi, st_ref, lm_ref[...], False)
    pk_ref[0, :, 0:w] = q.astype(BF16)
    pk_ref[0, :, w:2 * w] = vi.astype(BF16)


def _mix_fwd(x, mods, nm, w_in_bf, lbl, cw, s0_f, wa, wb, col_tile):
    nb, t, _ = x.shape
    nt = t // TT
    ne, rows, cols = wa.shape
    per_expert = nb * nt // ne
    step_rows = rows // per_expert
    tok = lambda n, dt: jax.ShapeDtypeStruct((nb, t, n * HG_WIDTH), dt)
    tspec = lambda n: pl.BlockSpec((1, TT, n * HG_WIDTH), lambda b, j: (b, j, 0))
    cast_in = pl.BlockSpec((1, step_rows, cols), lambda b, j: ((b * nt + j) // per_expert, (b * nt + j) % per_expert, 0))
    cast_out = pl.BlockSpec((1, cols // col_tile, step_rows, col_tile),
                            lambda b, j: ((b * nt + j) // per_expert, 0, (b * nt + j) % per_expert, 0))
    cast_shape = jax.ShapeDtypeStruct((ne, cols // col_tile, rows, col_tile), BF16)
    return pl.pallas_call(
        _mix_fwd_body,
        grid=(nb, nt),
        in_specs=[pl.BlockSpec((1, TT, D_MODEL), lambda b, j: (b, j, 0)),
                  pl.BlockSpec(mods.shape, lambda b, j: (0, 0)),
                  pl.BlockSpec((1, D_MODEL), lambda b, j: (0, 0)),
                  pl.BlockSpec(w_in_bf.shape, lambda b, j: (0, 0)),
                  pl.BlockSpec(lbl.shape, lambda b, j: (0, 0, 0)),
                  pl.BlockSpec(cw.shape, lambda b, j: (0, 0)),
                  pl.BlockSpec((1, HG_HEADS, HG_VDIM, HG_KDIM), lambda b, j: (b, 0, 0, 0)),
                  cast_in, cast_in],
        out_specs=[tspec(4), tspec(2), cast_out, cast_out],
        out_shape=[tok(4, BF16), tok(2, F32), cast_shape, cast_shape],
        scratch_shapes=[pltpu.VMEM((HG_HEADS, HG_VDIM, HG_KDIM), F32), pltpu.VMEM((TT, TT), I32),
                        pltpu.VMEM((TT, TT), BF16)],
        compiler_params=pltpu.CompilerParams(dimension_semantics=("arbitrary", "arbitrary"),
                                             vmem_limit_bytes=VMEM_LIMIT),
        name="mix_fwd",
    )(x, mods, nm, w_in_bf, lbl, cw, s0_f, wa, wb)


def _mix_bwd_body(x_ref, pk_ref, fz_ref, mods_ref, lbl_ref, hgn_ref,
                  wo_ref, nf_ref, wr_ref, s0_ref, wc_ref,
                  x1_ref, hr_ref, at_ref, wc_bf_ref, st_ref, lm_ref, tri_ref):
    b = pl.program_id(0)
    j = pl.program_id(1)
    wc_bf_ref[...] = wc_ref[...].astype(BF16)

    @pl.when((b == 0) & (j == 0))
    def _():
        _init_maps(lm_ref, tri_ref, True)

    @pl.when(j == 0)
    def _():
        st_ref[...] = s0_ref[0]

    _, lb_b = _lower_bounds(lbl_ref)
    w = HG_WIDTH
    k, cum = _gla_gates(fz_ref[0, :, 0:w], lb_b, tri_ref[...])
    o_b = _gla_tile(pk_ref[0, :, 0:w].astype(F32), k, cum, pk_ref[0, :, w:2 * w], st_ref, lm_ref[...], True)
    o = fz_ref[0, :, w:2 * w] + o_b
    hgn = hgn_ref[...]
    heads = [_rms(o[:, h * HG_VDIM:(h + 1) * HG_VDIM], hgn) for h in range(HG_HEADS)]
    g = pk_ref[0, :, 2 * w:3 * w].astype(F32)
    ohg = jnp.concatenate(heads, axis=1) * _silu(g)
    cat = jnp.concatenate([ohg.astype(BF16), pk_ref[0, :, 3 * w:4 * w]], axis=1)
    mx = _dot(cat, wo_ref[...])
    mod = lambda k: mods_ref[pl.ds(b, 1), k * D_MODEL:(k + 1) * D_MODEL]
    x1 = x_ref[0] + mod(2) * mx
    h2 = _rms(x1, nf_ref[...]) * (1.0 + mod(4)) + mod(3)
    for c in range(ACC_ROWS):
        x1_ref[0, pl.ds(c, TT, stride=ACC_ROWS), :] = x1[:, c * LANES:(c + 1) * LANES]
        hr_ref[0, pl.ds(c, TT, stride=ACC_ROWS), :] = h2[:, c * LANES:(c + 1) * LANES]
    r = _dot(jnp.concatenate(_split(h2), axis=0), wr_ref[...])
    lg = r[:TT, :LANES] + r[:TT, LANES:] + r[TT:, :LANES]
    lane = lax.broadcasted_iota(I32, lg.shape, 1)
    lg = jnp.where(lane < N_EXPERTS, lg, -jnp.inf)
    e = jnp.exp(lg - jnp.max(lg, axis=-1, keepdims=True))
    aff = e / jnp.sum(e, axis=-1, keepdims=True)
    at_ref[0] = aff.T[0:N_EXPERTS, :]


def _mix_bwd(x, pk, fz, mods, lbl, hgn, w_out_bf, nf, wr_pad, s0_b, wc):
    nb, t, _ = x.shape
    nt = t // TT
    cast_spec = pl.BlockSpec((1,) + wc.shape[1:], lambda b, j: (b * nt + j, 0, 0))
    rev = lambda b, j: (b, nt - 1 - j, 0)
    tspec = lambda a: pl.BlockSpec((1, TT, a.shape[2]), rev)
    xspec = pl.BlockSpec((1, TT, D_MODEL), rev)
    full2 = lambda a: pl.BlockSpec(a.shape, lambda b, j: (0, 0))
    return pl.pallas_call(
        _mix_bwd_body,
        grid=(nb, nt),
        in_specs=[xspec, tspec(pk), tspec(fz),
                  full2(mods), pl.BlockSpec(lbl.shape, lambda b, j: (0, 0, 0)), full2(hgn),
                  full2(w_out_bf), full2(nf), full2(wr_pad),
                  pl.BlockSpec((1, HG_HEADS, HG_VDIM, HG_KDIM), lambda b, j: (b, 0, 0, 0)),
                  cast_spec],
        out_specs=[pl.BlockSpec((1, TT * ACC_ROWS, LANES), rev),
                   pl.BlockSpec((1, TT * ACC_ROWS, LANES), rev),
                   pl.BlockSpec((1, N_EXPERTS, TT), lambda b, j: (b, 0, nt - 1 - j)),
                   cast_spec],
        out_shape=[jax.ShapeDtypeStruct((nb, t * ACC_ROWS, LANES), F32),
                   jax.ShapeDtypeStruct((nb, t * ACC_ROWS, LANES), F32),
                   jax.ShapeDtypeStruct((nb, N_EXPERTS, t), F32),
                   jax.ShapeDtypeStruct(wc.shape, BF16)],
        scratch_shapes=[pltpu.VMEM((HG_HEADS, HG_VDIM, HG_KDIM), F32), pltpu.VMEM((TT, TT), I32),
                        pltpu.VMEM((TT, TT), BF16)],
        compiler_params=pltpu.CompilerParams(dimension_semantics=("arbitrary", "arbitrary"),
                                             vmem_limit_bytes=VMEM_LIMIT),
        name="mix_bwd",
    )(x, pk, fz, mods, lbl, hgn, w_out_bf, nf, wr_pad, s0_b, wc)


def _route_body(at_ref, idx_ref, gate_ref, *, cap):
    ne, t = at_ref.shape[1], at_ref.shape[2]
    blk = 2 * LANES
    aff = at_ref[0]

    def count(m):
        return jnp.sum(jnp.where(m, 1.0, 0.0), axis=1, keepdims=True)

    def enough(cand):
        return count(aff >= lax.bitcast_convert_type(cand, F32)) >= cap

    def bit_step(i, thr):
        hi = jnp.left_shift(jnp.int32(1), 30 - 2 * i)
        lo = jnp.right_shift(hi, 1)
        both, upper, lower = thr | hi | lo, thr | hi, thr | lo
        return jnp.where(enough(both), both, jnp.where(enough(upper), upper, jnp.where(enough(lower), lower, thr)))

    thr = lax.fori_loop(0, 16, bit_step, jnp.zeros((ne, 1), I32))
    gt = aff >= lax.bitcast_convert_type(thr + 1, F32)
    eq = (aff >= lax.bitcast_convert_type(thr, F32)) & jnp.logical_not(gt)
    need = cap - count(gt)

    ii = lax.broadcasted_iota(I32, (blk, blk), 0)
    jj = lax.broadcasted_iota(I32, (blk, blk), 1)
    upper = jnp.where(ii <= jj, 1.0, 0.0).astype(BF16)

    def prefix_blocks(m):
        mb = jnp.where(m, 1.0, 0.0).astype(BF16)
        carry = jnp.zeros((ne, 1), F32)
        out = []
        for kb in range(t // blk):
            p = _dot(mb[:, kb * blk:(kb + 1) * blk], upper) + carry
            out.append(p)
            carry = p[:, blk - 1:blk]
        return out

    eq_rank = jnp.concatenate(prefix_blocks(eq), axis=1)
    sel = gt | (eq & (eq_rank <= need))
    rank = jnp.concatenate(prefix_blocks(sel), axis=1).astype(I32)

    tok = lax.broadcasted_iota(I32, (ne, t), 1)
    x = jnp.where(sel, tok, -1)
    d = jnp.where(sel, tok - rank + 1, 0)
    g = aff
    for k in range(t.bit_length() - 1):
        left = lambda a: pltpu.roll(a, t - (1 << k), axis=1)
        ds = left(d)
        take = ((ds >> k) & 1) == 1
        keep = ((d >> k) & 1) == 0
        x = jnp.where(take, left(x), jnp.where(keep, x, -1))
        g = jnp.where(take, left(g), g)
        d = jnp.where(take, ds, jnp.where(keep, d, 0))
    idx_ref[0] = x[:, :cap]
    gate_ref[0] = g[:, :cap]


def _route(aff_t, cap):
    nb, ne, t = aff_t.shape
    spec = pl.BlockSpec((1, ne, cap), lambda b: (b, 0, 0))
    return pl.pallas_call(
        functools.partial(_route_body, cap=cap),
        grid=(nb,),
        in_specs=[pl.BlockSpec((1, ne, t), lambda b: (b, 0, 0))],
        out_specs=[spec, spec],
        out_shape=[jax.ShapeDtypeStruct((nb, ne, cap), I32), jax.ShapeDtypeStruct((nb, ne, cap), F32)],
        compiler_params=pltpu.CompilerParams(vmem_limit_bytes=VMEM_LIMIT),
        name="route",
    )(aff_t)


def _moe_body(idx_ref, gate_ref, hr_hbm, wg_ref, wu_ref, wd_ref, x1_hbm, mods_ref, nfin_ref,
              out_ref, acc_ref, hr_ref, rowg_ref, rows_ref, xs_ref, ys_ref, seed_sem, hr_sem, *, cap, n_ff, n_epi):
    b = pl.program_id(0)
    s = pl.program_id(1)
    n_work = N_EXPERTS * n_ff
    e = s // n_ff
    ff = s % n_ff
    slot = e % 2
    group = SUBLANES

    def list_base(expert):
        return (b * N_EXPERTS + expert) * cap

    def gather(base, j0, n):
        for u in range(n):
            tkn = idx_ref[base + j0 + u]
            rowg_ref[pl.ds(pl.multiple_of((j0 + u) * ACC_ROWS, ACC_ROWS), ACC_ROWS), :] = (
                hr_ref[pl.ds(pl.multiple_of(tkn * ACC_ROWS, ACC_ROWS), ACC_ROWS), :])

    def scatter(base, j0, n):
        for g0 in range(0, n, group):
            dst = []
            val = []
            for u in range(g0, g0 + group):
                r = pl.multiple_of(idx_ref[base + j0 + u] * ACC_ROWS, ACC_ROWS)
                dst.append(r)
                val.append(acc_ref[pl.ds(r, ACC_ROWS), :] + gate_ref[base + j0 + u]
                           * rows_ref[pl.ds(pl.multiple_of((j0 + u) * ACC_ROWS, ACC_ROWS), ACC_ROWS), :])
            for r, v in zip(dst, val):
                acc_ref[pl.ds(r, ACC_ROWS), :] = v

    def in_groups(fn, base):
        def step(jo, carry):
            fn(base, jo * group, group)
            return carry
        lax.fori_loop(0, cap // group, step, 0)

    def lay_out_input(to_slot):
        for c in range(ACC_ROWS):
            xs_ref[to_slot, :, c * LANES:(c + 1) * LANES] = (
                rowg_ref[pl.ds(c, cap, stride=ACC_ROWS), :].astype(BF16))

    def lay_out_output():
        g2 = mods_ref[pl.ds(b, 1), 5 * D_MODEL:6 * D_MODEL]
        for c in range(ACC_ROWS):
            cols = slice(c * LANES, (c + 1) * LANES)
            rows_ref[pl.ds(c, cap, stride=ACC_ROWS), :] = ys_ref[:, cols] * g2[:, cols]

    chunk_rows = EPI_TOK * ACC_ROWS

    def seed_copy(sample, chunk):
        rows = pl.ds(chunk * chunk_rows, chunk_rows)
        return pltpu.make_async_copy(x1_hbm.at[sample, rows], acc_ref.at[rows], seed_sem.at[chunk])

    def expert_tile():
        xs = xs_ref[slot]
        hid = _silu(_dot(xs, wg_ref[0, 0])) * _dot(xs, wu_ref[0, 0])
        ys_ref[...] = _dot(hid.astype(BF16), wd_ref[0]) + jnp.where(ff == 0, 0.0, ys_ref[...])

    def input_copy(sample):
        return pltpu.make_async_copy(hr_hbm.at[sample], hr_ref, hr_sem.at[0])

    @pl.when((s == 0) & (b == 0))
    def _():
        input_copy(0).start()
        for chunk in range(n_epi):
            seed_copy(0, chunk).start()

    @pl.when(s == 0)
    def _():
        input_copy(b).wait()
        ys_ref[...] = jnp.zeros_like(ys_ref)
        in_groups(gather, list_base(0))
        lay_out_input(0)

    @pl.when(s == n_ff - 1)
    def _():
        for chunk in range(n_epi):
            seed_copy(b, chunk).wait()

    @pl.when((s < n_work) & (ff == 0))
    def _():
        gather(list_base(jnp.minimum(e + 1, N_EXPERTS - 1)), 0, cap)
        lay_out_output()
        expert_tile()

    if n_ff > 2:
        @pl.when((s < n_work) & (ff > 0) & (ff < n_ff - 1))
        def _():
            expert_tile()

    @pl.when((s < n_work) & (ff == n_ff - 1))
    def _():
        lay_out_input(1 - slot)
        scatter(list_base(jnp.maximum(e - 1, 0)), 0, cap)
        expert_tile()

    @pl.when(s == n_work)
    def _():
        @pl.when(b + 1 < pl.num_programs(0))
        def _():
            input_copy(b + 1).start()

        lay_out_output()
        in_groups(scatter, list_base(N_EXPERTS - 1))

    @pl.when(s >= n_work)
    def _():
        chunk = s - n_work
        row0 = chunk * chunk_rows
        tot = jnp.concatenate([acc_ref[pl.ds(row0 + c, EPI_TOK, stride=ACC_ROWS), :] for c in range(ACC_ROWS)], axis=1)
        out_ref[0] = _rms(tot, nfin_ref[...])

        @pl.when(b + 1 < pl.num_programs(0))
        def _():
            seed_copy(b + 1, chunk).start()


def _moe(idx_flat, gate_flat, hr, wg, wu, wd, x1, mods, nfin, cap):
    nb, t = x1.shape[0], x1.shape[1] // ACC_ROWS
    n_ff = wg.shape[1]
    n_work = N_EXPERTS * n_ff
    n_epi = t // EPI_TOK

    def w_idx(s):
        sc = jnp.minimum(s, n_work - 1)
        return sc // n_ff, sc % n_ff

    def epi(b, s, *_):
        return (b, jnp.maximum(s - n_work, 0), 0)

    grid_spec = pltpu.PrefetchScalarGridSpec(
        num_scalar_prefetch=2,
        grid=(nb, n_work + n_epi),
        in_specs=[pl.BlockSpec(memory_space=pl.ANY),
                  pl.BlockSpec((1, 1, D_MODEL, DFF_TILE), lambda b, s, *_: w_idx(s) + (0, 0)),
                  pl.BlockSpec((1, 1, D_MODEL, DFF_TILE), lambda b, s, *_: w_idx(s) + (0, 0)),
                  pl.BlockSpec((1, DFF_TILE, D_MODEL), lambda b, s, *_: (w_idx(s)[0], w_idx(s)[1], 0)),
                  pl.BlockSpec(memory_space=pl.ANY),
                  pl.BlockSpec(mods.shape, lambda b, s, *_: (0, 0)),
                  pl.BlockSpec((1, D_MODEL), lambda b, s, *_: (0, 0))],
        out_specs=pl.BlockSpec((1, EPI_TOK, D_MODEL), epi),
        scratch_shapes=[pltpu.VMEM((t * ACC_ROWS, LANES), F32),
                        pltpu.VMEM((t * ACC_ROWS, LANES), F32),
                        pltpu.VMEM((cap * ACC_ROWS, LANES), F32),
                        pltpu.VMEM((cap * ACC_ROWS, LANES), F32),
                        pltpu.VMEM((2, cap, D_MODEL), BF16),
                        pltpu.VMEM((cap, D_MODEL), F32),
                        pltpu.SemaphoreType.DMA((n_epi,)),
                        pltpu.SemaphoreType.DMA((1,))])
    return pl.pallas_call(
        functools.partial(_moe_body, cap=cap, n_ff=n_ff, n_epi=n_epi),
        grid_spec=grid_spec,
        out_shape=jax.ShapeDtypeStruct((nb, t, D_MODEL), F32),
        compiler_params=pltpu.CompilerParams(dimension_semantics=("arbitrary", "arbitrary"),
                                             vmem_limit_bytes=VMEM_LIMIT),
        name="moe",
    )(idx_flat, gate_flat, hr, wg, wu, wd, x1, mods, nfin)


def kernel(x, c, ctx, c_ctx, w_ada, b_ada, norm_mix, norm_ffn, w_in, lb_logits, hg_norm, conv_w,
           w_out, w_router, w_gate, w_up, w_down, norm_final):
    nb, t, d = x.shape
    assert d == D_MODEL and w_ada.shape[0] == 1 and nb < MOD_ROWS and t % TT == 0 and TT % GRID_W == 0
    assert w_in.shape[2] == 3 * HG_KEYS + 2 * HG_WIDTH + 3 * SC_WIDTH and w_router.shape[2] == N_EXPERTS
    cap = EC_CAPACITY * t // N_EXPERTS
    assert cap % LANES == 0 and w_gate.shape[3] % DFF_TILE == 0 and w_gate.shape[3] // DFF_TILE >= 2
    assert (nb * (t // TT)) % N_EXPERTS == 0 and (D_MODEL * N_EXPERTS) % (nb * (t // TT)) == 0

    cc = jnp.concatenate([c, c_ctx[None, :], jnp.zeros((MOD_ROWS - nb - 1, d), F32)], axis=0)
    w_in_bf = w_in[0].astype(BF16)
    w_out_bf = w_out[0].astype(BF16)
    wr_hi, wr_lo = _split(jnp.pad(w_router[0], ((0, 0), (0, LANES - N_EXPERTS))))
    wr_pad = jnp.concatenate([wr_hi, wr_lo], axis=1)
    nm, nf, nfin = norm_mix[0][None, :], norm_ffn[0][None, :], norm_final[None, :]
    hgn = hg_norm[0][None, :]

    mods = _adaln(cc, w_ada[0], b_ada[0][None, :])
    s0_f, s0_b = _ctx_states(ctx, mods, nm, w_in_bf, lb_logits, ctx_row=nb)
    steps = nb * (t // TT)
    sliced = lambda w: w.reshape((steps, w.shape[0] * w.shape[1] // steps) + w.shape[2:])
    pk, fz, wg, wu = _mix_fwd(x, mods, nm, w_in_bf, lb_logits, conv_w[0], s0_f, w_gate[0], w_up[0], DFF_TILE)
    x1, hr, aff_t, wd = _mix_bwd(x, pk, fz, mods, lb_logits, hgn, w_out_bf, nf, wr_pad, s0_b, sliced(w_down[0]))
    wd = wd.reshape(w_down[0].shape)
    idx, gates = _route(aff_t, cap)
    return _moe(idx.reshape(-1), gates.reshape(-1), hr, wg, wu, wd, x1, mods, nfin, cap)
```

```python
import functools

import jax
import jax.numpy as jnp
from jax import lax
from jax.experimental import pallas as pl
from jax.experimental.pallas import tpu as pltpu

F32 = jnp.float32
BF16 = jnp.bfloat16
I32 = jnp.int32

D_MODEL = 1024
N_MOD = 6
HG_HEADS = 4
HG_KDIM = 128
HG_VDIM = 128
HG_KEYS = HG_HEADS * HG_KDIM
HG_WIDTH = HG_HEADS * HG_VDIM
SC_WIDTH = D_MODEL - HG_WIDTH
GRID_W = 64
CHUNK = 32
N_EXPERTS = 16
EC_CAPACITY = 2
EPS = 1e-6

LANES = 128
SUBLANES = 8
VMEM_LIMIT = 62 * 1024 * 1024

TT = 256
NCH = TT // CHUNK
MOD_ROWS = 16
DFF_TILE = 1024
EPI_TOK = 512
ACC_ROWS = D_MODEL // LANES


def _dot(a, b):
    return jnp.dot(a, b, preferred_element_type=F32)


def _dot_nt(a, b):
    return lax.dot_general(a, b, (((1,), (1,)), ((), ())), preferred_element_type=F32)


def _dot_tn(a, b):
    return lax.dot_general(a, b, (((0,), (0,)), ((), ())), preferred_element_type=F32)


def _split(x):
    hi = x.astype(BF16)
    lo = (x - hi.astype(F32)).astype(BF16)
    return hi, lo


def _rms(x, gain):
    return x * lax.rsqrt(jnp.mean(x * x, axis=-1, keepdims=True) + EPS) * gain


def _silu(x):
    return x * jax.nn.sigmoid(x)


def _lower_bounds(lbl_ref):
    rows = [lbl_ref[l] for l in range(lbl_ref.shape[0])]
    m = functools.reduce(jnp.maximum, rows)
    es = [jnp.exp(r - m) for r in rows]
    sm0 = es[0] / functools.reduce(lambda a, b: a + b, es)
    return sm0[0:1, :], sm0[1:2, :]


def _ada_body(c_ref, w_ref, b_ref, o_ref):
    s_hi, s_lo = _split(_silu(c_ref[...]))
    w_hi, w_lo = _split(w_ref[...])
    o_ref[...] = _dot(s_hi, w_hi) + _dot(s_hi, w_lo) + _dot(s_lo, w_hi) + b_ref[...]


def _adaln(cc, w, b):
    n = w.shape[1]
    tn = n // 4
    return pl.pallas_call(
        _ada_body,
        grid=(n // tn,),
        in_specs=[pl.BlockSpec((MOD_ROWS, D_MODEL), lambda i: (0, 0)),
                  pl.BlockSpec((D_MODEL, tn), lambda i: (0, i)),
                  pl.BlockSpec((1, tn), lambda i: (0, i))],
        out_specs=pl.BlockSpec((MOD_ROWS, tn), lambda i: (0, i)),
        out_shape=jax.ShapeDtypeStruct((MOD_ROWS, n), F32),
        compiler_params=pltpu.CompilerParams(vmem_limit_bytes=VMEM_LIMIT),
        name="adaln",
    )(cc, w, b)


def _ctx_body(ctx_ref, mods_ref, nm_ref, wf_ref, wb_ref, wi_ref, lbl_ref, sf_ref, sb_ref, *, ctx_row):
    x = ctx_ref[0]
    n = x.shape[0]
    csh = mods_ref[ctx_row:ctx_row + 1, 0:D_MODEL]
    csc = mods_ref[ctx_row:ctx_row + 1, D_MODEL:2 * D_MODEL]
    hb = (_rms(x, nm_ref[...]) * (1.0 + csc) + csh).astype(BF16)
    zf = _dot(hb, wf_ref[...])
    zb = _dot(hb, wb_ref[...])
    vi = _dot(hb, wi_ref[...]).astype(BF16)
    lb_f, lb_b = _lower_bounds(lbl_ref)
    f_f = lb_f + (1.0 - lb_f) * jax.nn.sigmoid(zf)
    f_b = lb_b + (1.0 - lb_b) * jax.nn.sigmoid(zb)
    ii = lax.broadcasted_iota(I32, (n, n), 0)
    jj = lax.broadcasted_iota(I32, (n, n), 1)
    after = jnp.where(jj > ii, 1.0, 0.0).astype(BF16)
    before = jnp.where(jj < ii, 1.0, 0.0).astype(BF16)
    hi, lo = _split(jnp.log(f_f))
    kd_f = ((1.0 - f_f) * jnp.exp(_dot(after, hi) + _dot(after, lo))).astype(BF16)
    hi, lo = _split(jnp.log(f_b))
    kd_b = ((1.0 - f_b) * jnp.exp(_dot(before, hi) + _dot(before, lo))).astype(BF16)
    for h in range(HG_HEADS):
        ks = slice(h * HG_KDIM, (h + 1) * HG_KDIM)
        vs = slice(h * HG_VDIM, (h + 1) * HG_VDIM)
        sf_ref[0, h] = _dot_tn(vi[:, vs], kd_f[:, ks])
        sb_ref[0, h] = _dot_tn(vi[:, vs], kd_b[:, ks])


def _ctx_states(ctx, mods, nm, w_in_bf, lbl, ctx_row):
    nb, n, _ = ctx.shape
    st = jax.ShapeDtypeStruct((nb, HG_HEADS, HG_VDIM, HG_KDIM), F32)
    st_spec = pl.BlockSpec((1, HG_HEADS, HG_VDIM, HG_KDIM), lambda b: (b, 0, 0, 0))
    wcol = lambda k: pl.BlockSpec((D_MODEL, HG_KEYS), lambda b, k=k: (0, k))
    return pl.pallas_call(
        functools.partial(_ctx_body, ctx_row=ctx_row),
        grid=(nb,),
        in_specs=[pl.BlockSpec((1, n, D_MODEL), lambda b: (b, 0, 0)),
                  pl.BlockSpec(mods.shape, lambda b: (0, 0)),
                  pl.BlockSpec((1, D_MODEL), lambda b: (0, 0)),
                  wcol(1), wcol(2), wcol(3),
                  pl.BlockSpec(lbl.shape, lambda b: (0, 0, 0))],
        out_specs=[st_spec, st_spec],
        out_shape=[st, st],
        compiler_params=pltpu.CompilerParams(vmem_limit_bytes=VMEM_LIMIT),
        name="ctx_states",
    )(ctx, mods, nm, w_in_bf, w_in_bf, w_in_bf, lbl)


def _init_maps(lm_ref, tri_ref, reverse):
    lm = _level_map(reverse)
    lm_ref[...] = lm
    tri_ref[...] = jnp.where(lm == 1, 1.0, 0.0).astype(BF16)


def _level_map(reverse):
    ii = lax.broadcasted_iota(I32, (TT, TT), 0)
    jj = lax.broadcasted_iota(I32, (TT, TT), 1)
    if reverse:
        ii = TT - 1 - ii
        jj = TT - 1 - jj
    sh0 = CHUNK.bit_length() - 1
    ci = ii >> sh0
    cj = jj >> sh0
    lm = jnp.where((ci == cj) & (jj <= ii), 1, 0)
    for lvl in range(1, NCH.bit_length()):
        m = ((ci >> lvl) == (cj >> lvl)) & (((ci >> (lvl - 1)) & 1) == 1) & (((cj >> (lvl - 1)) & 1) == 0)
        lm = jnp.where(m, lvl + 1, lm)
    return lm


def _gla_gates(zf, lb, tri):
    f = lb + (1.0 - lb) * jax.nn.sigmoid(zf)
    hi, lo = _split(jnp.log(f))
    return 1.0 - f, _dot(tri, hi) + _dot(tri, lo)


def _gla_tile(q, k, cum, v, st_ref, lm, reverse):
    n_lvl = NCH.bit_length() - 1
    phys = list(range(NCH - 1, -1, -1)) if reverse else list(range(NCH))
    last = 0 if reverse else CHUNK - 1
    tot = [cum[a * CHUNK + last:a * CHUNK + last + 1, :] for a in range(NCH)]
    bnd = [jnp.zeros_like(tot[0])]
    for c in range(NCH):
        bnd.append(bnd[-1] + tot[phys[c]])
    q0 = q * jnp.exp(cum)
    kinv = (k * jnp.exp(-cum)).astype(BF16)
    qs = [[None] * NCH for _ in range(n_lvl + 1)]
    ks = [[None] * NCH for _ in range(n_lvl + 1)]
    for c in range(NCH):
        a = phys[c]
        rows = slice(a * CHUNK, (a + 1) * CHUNK)
        q0a = q0[rows]
        kend = k[rows] * jnp.exp(tot[a] - cum[rows])
        zero = jnp.zeros_like(q0a)
        for lvl in range(1, n_lvl + 1):
            s = 1 << (lvl - 1)
            mid = (c // (2 * s)) * 2 * s + s
            right = (c // s) % 2 == 1
            qs[lvl - 1][a] = q0a * jnp.exp(bnd[c] - bnd[mid]) if right else zero
            ks[lvl - 1][a] = zero if right else kend * jnp.exp(bnd[mid] - bnd[c + 1])
        qs[n_lvl][a] = q0a * jnp.exp(bnd[c])
        ks[n_lvl][a] = kend * jnp.exp(bnd[NCH] - bnd[c + 1])
    qs = [jnp.concatenate(p, axis=0).astype(BF16) for p in qs]
    ks = [jnp.concatenate(p, axis=0).astype(BF16) for p in ks]
    q0 = q0.astype(BF16)
    vb = v.astype(BF16)
    dec = jnp.exp(bnd[NCH])
    outs = []
    for h in range(HG_HEADS):
        kk = slice(h * HG_KDIM, (h + 1) * HG_KDIM)
        vv = slice(h * HG_VDIM, (h + 1) * HG_VDIM)
        sc = jnp.where(lm == 1, _dot_nt(q0[:, kk], kinv[:, kk]), 0.0)
        for lvl in range(n_lvl):
            sc = jnp.where(lm == lvl + 2, _dot_nt(qs[lvl][:, kk], ks[lvl][:, kk]), sc)
        st = st_ref[h]
        outs.append(_dot(sc.astype(BF16), vb[:, vv]) + _dot_nt(qs[n_lvl][:, kk], st.astype(BF16)))
        st_ref[h] = st * dec[:, kk] + _dot_tn(vb[:, vv], ks[n_lvl][:, kk])
    return jnp.concatenate(outs, axis=1)


def _mix_fwd_body(x_ref, mods_ref, nm_ref, w_ref, lbl_ref, cw_ref, s0_ref, wa_ref, wb_ref,
                  pk_ref, fz_ref, wa_bf_ref, wb_bf_ref, st_ref, lm_ref, tri_ref):
    b = pl.program_id(0)
    j = pl.program_id(1)

    @pl.when((b == 0) & (j == 0))
    def _():
        _init_maps(lm_ref, tri_ref, False)

    @pl.when(j == 0)
    def _():
        st_ref[...] = s0_ref[0]

    sh = mods_ref[pl.ds(b, 1), 0:D_MODEL]
    sc = mods_ref[pl.ds(b, 1), D_MODEL:2 * D_MODEL]
    hx = (_rms(x_ref[0], nm_ref[...]) * (1.0 + sc) + sh).astype(BF16)
    w = HG_KEYS
    proj = lambda k: _dot(hx, w_ref[:, k * w:(k + 1) * w])

    def cast_column_tiles(src_ref, dst_ref):
        tile = dst_ref.shape[3]
        for f in range(dst_ref.shape[1]):
            dst_ref[0, f] = src_ref[0, :, f * tile:(f + 1) * tile].astype(BF16)

    gate_b = proj(5)
    cast_column_tiles(wa_ref, wa_bf_ref)
    u = proj(6) * proj(7)
    cast_column_tiles(wb_ref, wb_bf_ref)
    zff = proj(1)
    col = lax.broadcasted_iota(I32, (TT, 1), 0) & (GRID_W - 1)
    up = jnp.where(col != 0, pltpu.roll(u, 1, axis=0), 0.0)
    un = jnp.where(col != GRID_W - 1, pltpu.roll(u, TT - 1, axis=0), 0.0)
    cw = cw_ref[...]
    pk_ref[0, :, 3 * w:4 * w] = (gate_b * (cw[0:1] * up + cw[1:2] * u + cw[2:3] * un)).astype(BF16)
    q = proj(0)
    lb_f, _ = _lower_bounds(lbl_ref)
    k, cum = _gla_gates(zff, lb_f, tri_ref[...])
    vi = proj(3)
    fz_ref[0, :, 0:w] = proj(2)
    pk_ref[0, :, 2 * w:3 * w] = proj(4).astype(BF16)
    fz_ref[0, :, w:2 * w] = _gla_tile(q, k, cum, vi, st_ref, lm_ref[...], False)
    pk_ref[0, :, 0:w] = q.astype(BF16)
    pk_ref[0, :, w:2 * w] = vi.astype(BF16)


def _mix_fwd(x, mods, nm, w_in_bf, lbl, cw, s0_f, wa, wb, col_tile):
    nb, t, _ = x.shape
    nt = t // TT
    ne, rows, cols = wa.shape
    per_expert = nb * nt // ne
    step_rows = rows // per_expert
    tok = lambda n, dt: jax.ShapeDtypeStruct((nb, t, n * HG_WIDTH), dt)
    tspec = lambda n: pl.BlockSpec((1, TT, n * HG_WIDTH), lambda b, j: (b, j, 0))
    cast_in = pl.BlockSpec((1, step_rows, cols), lambda b, j: ((b * nt + j) // per_expert, (b * nt + j) % per_expert, 0))
    cast_out = pl.BlockSpec((1, cols // col_tile, step_rows, col_tile),
                            lambda b, j: ((b * nt + j) // per_expert, 0, (b * nt + j) % per_expert, 0))
    cast_shape = jax.ShapeDtypeStruct((ne, cols // col_tile, rows, col_tile), BF16)
    return pl.pallas_call(
        _mix_fwd_body,
        grid=(nb, nt),
        in_specs=[pl.BlockSpec((1, TT, D_MODEL), lambda b, j: (b, j, 0)),
                  pl.BlockSpec(mods.shape, lambda b, j: (0, 0)),
                  pl.BlockSpec((1, D_MODEL), lambda b, j: (0, 0)),
                  pl.BlockSpec(w_in_bf.shape, lambda b, j: (0, 0)),
                  pl.BlockSpec(lbl.shape, lambda b, j: (0, 0, 0)),
                  pl.BlockSpec(cw.shape, lambda b, j: (0, 0)),
                  pl.BlockSpec((1, HG_HEADS, HG_VDIM, HG_KDIM), lambda b, j: (b, 0, 0, 0)),
                  cast_in, cast_in],
        out_specs=[tspec(4), tspec(2), cast_out, cast_out],
        out_shape=[tok(4, BF16), tok(2, F32), cast_shape, cast_shape],
        scratch_shapes=[pltpu.VMEM((HG_HEADS, HG_VDIM, HG_KDIM), F32), pltpu.VMEM((TT, TT), I32),
                        pltpu.VMEM((TT, TT), BF16)],
        compiler_params=pltpu.CompilerParams(dimension_semantics=("arbitrary", "arbitrary"),
                                             vmem_limit_bytes=VMEM_LIMIT),
        name="mix_fwd",
    )(x, mods, nm, w_in_bf, lbl, cw, s0_f, wa, wb)


def _mix_bwd_body(x_ref, pk_ref, fz_ref, mods_ref, lbl_ref, hgn_ref,
                  wo_ref, nf_ref, wr_ref, s0_ref, wc_ref,
                  x1_ref, hr_ref, at_ref, wc_bf_ref, st_ref, lm_ref, tri_ref):
    b = pl.program_id(0)
    j = pl.program_id(1)
    wc_bf_ref[...] = wc_ref[...].astype(BF16)

    @pl.when((b == 0) & (j == 0))
    def _():
        _init_maps(lm_ref, tri_ref, True)

    @pl.when(j == 0)
    def _():
        st_ref[...] = s0_ref[0]

    _, lb_b = _lower_bounds(lbl_ref)
    w = HG_WIDTH
    k, cum = _gla_gates(fz_ref[0, :, 0:w], lb_b, tri_ref[...])
    o_b = _gla_tile(pk_ref[0, :, 0:w].astype(F32), k, cum, pk_ref[0, :, w:2 * w], st_ref, lm_ref[...], True)
    o = fz_ref[0, :, w:2 * w] + o_b
    hgn = hgn_ref[...]
    heads = [_rms(o[:, h * HG_VDIM:(h + 1) * HG_VDIM], hgn) for h in range(HG_HEADS)]
    g = pk_ref[0, :, 2 * w:3 * w].astype(F32)
    ohg = jnp.concatenate(heads, axis=1) * _silu(g)
    cat = jnp.concatenate([ohg.astype(BF16), pk_ref[0, :, 3 * w:4 * w]], axis=1)
    mx = _dot(cat, wo_ref[...])
    mod = lambda k: mods_ref[pl.ds(b, 1), k * D_MODEL:(k + 1) * D_MODEL]
    x1 = x_ref[0] + mod(2) * mx
    h2 = _rms(x1, nf_ref[...]) * (1.0 + mod(4)) + mod(3)
    for c in range(ACC_ROWS):
        x1_ref[0, pl.ds(c, TT, stride=ACC_ROWS), :] = x1[:, c * LANES:(c + 1) * LANES]
        hr_ref[0, pl.ds(c, TT, stride=ACC_ROWS), :] = h2[:, c * LANES:(c + 1) * LANES]
    r = _dot(jnp.concatenate(_split(h2), axis=0), wr_ref[...])
    lg = r[:TT, :LANES] + r[:TT, LANES:] + r[TT:, :LANES]
    lgt = lg.T[0:N_EXPERTS, :]
    e = jnp.exp(lgt - jnp.max(lgt, axis=0, keepdims=True))
    at_ref[0] = e / jnp.sum(e, axis=0, keepdims=True)


def _mix_bwd(x, pk, fz, mods, lbl, hgn, w_out_bf, nf, wr_pad, s0_b, wc):
    nb, t, _ = x.shape
    nt = t // TT
    cast_spec = pl.BlockSpec((1,) + wc.shape[1:], lambda b, j: (b * nt + j, 0, 0))
    rev = lambda b, j: (b, nt - 1 - j, 0)
    tspec = lambda a: pl.BlockSpec((1, TT, a.shape[2]), rev)
    xspec = pl.BlockSpec((1, TT, D_MODEL), rev)
    full2 = lambda a: pl.BlockSpec(a.shape, lambda b, j: (0, 0))
    return pl.pallas_call(
        _mix_bwd_body,
        grid=(nb, nt),
        in_specs=[xspec, tspec(pk), tspec(fz),
                  full2(mods), pl.BlockSpec(lbl.shape, lambda b, j: (0, 0, 0)), full2(hgn),
                  full2(w_out_bf), full2(nf), full2(wr_pad),
                  pl.BlockSpec((1, HG_HEADS, HG_VDIM, HG_KDIM), lambda b, j: (b, 0, 0, 0)),
                  cast_spec],
        out_specs=[pl.BlockSpec((1, TT * ACC_ROWS, LANES), rev),
                   pl.BlockSpec((1, TT * ACC_ROWS, LANES), rev),
                   pl.BlockSpec((1, N_EXPERTS, TT), lambda b, j: (b, 0, nt - 1 - j)),
                   cast_spec],
        out_shape=[jax.ShapeDtypeStruct((nb, t * ACC_ROWS, LANES), F32),
                   jax.ShapeDtypeStruct((nb, t * ACC_ROWS, LANES), F32),
                   jax.ShapeDtypeStruct((nb, N_EXPERTS, t), F32),
                   jax.ShapeDtypeStruct(wc.shape, BF16)],
        scratch_shapes=[pltpu.VMEM((HG_HEADS, HG_VDIM, HG_KDIM), F32), pltpu.VMEM((TT, TT), I32),
                        pltpu.VMEM((TT, TT), BF16)],
        compiler_params=pltpu.CompilerParams(dimension_semantics=("arbitrary", "arbitrary"),
                                             vmem_limit_bytes=VMEM_LIMIT),
        name="mix_bwd",
    )(x, pk, fz, mods, lbl, hgn, w_out_bf, nf, wr_pad, s0_b, wc)


def _route_body(at_ref, idx_ref, gate_ref, *, cap):
    ne, t = at_ref.shape[1], at_ref.shape[2]
    blk = 2 * LANES
    aff = at_ref[0]

    def count(m):
        return jnp.sum(jnp.where(m, 1.0, 0.0), axis=1, keepdims=True)

    def enough(cand):
        return count(aff >= lax.bitcast_convert_type(cand, F32)) >= cap

    def bit_step(i, thr):
        hi = jnp.left_shift(jnp.int32(1), 30 - 2 * i)
        lo = jnp.right_shift(hi, 1)
        both, upper, lower = thr | hi | lo, thr | hi, thr | lo
        return jnp.where(enough(both), both, jnp.where(enough(upper), upper, jnp.where(enough(lower), lower, thr)))

    thr = lax.fori_loop(0, 16, bit_step, jnp.zeros((ne, 1), I32))
    gt = aff >= lax.bitcast_convert_type(thr + 1, F32)
    eq = (aff >= lax.bitcast_convert_type(thr, F32)) & jnp.logical_not(gt)
    need = cap - count(gt)

    ii = lax.broadcasted_iota(I32, (blk, blk), 0)
    jj = lax.broadcasted_iota(I32, (blk, blk), 1)
    upper = jnp.where(ii <= jj, 1.0, 0.0).astype(BF16)

    def prefix_blocks(m):
        mb = jnp.where(m, 1.0, 0.0).astype(BF16)
        carry = jnp.zeros((ne, 1), F32)
        out = []
        for kb in range(t // blk):
            p = _dot(mb[:, kb * blk:(kb + 1) * blk], upper) + carry
            out.append(p)
            carry = p[:, blk - 1:blk]
        return out

    eq_rank = jnp.concatenate(prefix_blocks(eq), axis=1)
    sel = gt | (eq & (eq_rank <= need))
    rank = jnp.concatenate(prefix_blocks(sel), axis=1).astype(I32)

    tok = lax.broadcasted_iota(I32, (ne, t), 1)
    x = jnp.where(sel, tok, -1)
    d = jnp.where(sel, tok - rank + 1, 0)
    g = aff
    for k in range(t.bit_length() - 1):
        left = lambda a: pltpu.roll(a, t - (1 << k), axis=1)
        ds = left(d)
        take = ((ds >> k) & 1) == 1
        keep = ((d >> k) & 1) == 0
        x = jnp.where(take, left(x), jnp.where(keep, x, -1))
        g = jnp.where(take, left(g), g)
        d = jnp.where(take, ds, jnp.where(keep, d, 0))
    idx_ref[0] = x[:, :cap]
    gate_ref[0] = g[:, :cap]


def _route(aff_t, cap):
    nb, ne, t = aff_t.shape
    spec = pl.BlockSpec((1, ne, cap), lambda b: (b, 0, 0))
    return pl.pallas_call(
        functools.partial(_route_body, cap=cap),
        grid=(nb,),
        in_specs=[pl.BlockSpec((1, ne, t), lambda b: (b, 0, 0))],
        out_specs=[spec, spec],
        out_shape=[jax.ShapeDtypeStruct((nb, ne, cap), I32), jax.ShapeDtypeStruct((nb, ne, cap), F32)],
        compiler_params=pltpu.CompilerParams(vmem_limit_bytes=VMEM_LIMIT),
        name="route",
    )(aff_t)


def _moe_body(idx_ref, gate_ref, hr_hbm, wg_ref, wu_ref, wd_ref, x1_hbm, mods_ref, nfin_ref,
              out_ref, acc_ref, hr_ref, rowg_ref, rows_ref, xs_ref, ys_ref, seed_sem, hr_sem, *, cap, n_ff, n_epi):
    b = pl.program_id(0)
    s = pl.program_id(1)
    n_work = N_EXPERTS * n_ff
    e = s // n_ff
    ff = s % n_ff
    slot = e % 2
    group = SUBLANES

    def list_base(expert):
        return (b * N_EXPERTS + expert) * cap

    def gather(base, j0, n):
        for u in range(n):
            tkn = idx_ref[base + j0 + u]
            rowg_ref[pl.ds(pl.multiple_of((j0 + u) * ACC_ROWS, ACC_ROWS), ACC_ROWS), :] = (
                hr_ref[pl.ds(pl.multiple_of(tkn * ACC_ROWS, ACC_ROWS), ACC_ROWS), :])

    def scatter(base, j0, n):
        for g0 in range(0, n, group):
            dst = []
            val = []
            for u in range(g0, g0 + group):
                r = pl.multiple_of(idx_ref[base + j0 + u] * ACC_ROWS, ACC_ROWS)
                dst.append(r)
                val.append(acc_ref[pl.ds(r, ACC_ROWS), :] + gate_ref[base + j0 + u]
                           * rows_ref[pl.ds(pl.multiple_of((j0 + u) * ACC_ROWS, ACC_ROWS), ACC_ROWS), :])
            for r, v in zip(dst, val):
                acc_ref[pl.ds(r, ACC_ROWS), :] = v

    def in_groups(fn, base):
        def step(jo, carry):
            fn(base, jo * group, group)
            return carry
        lax.fori_loop(0, cap // group, step, 0)

    def lay_out_input(to_slot):
        for c in range(ACC_ROWS):
            xs_ref[to_slot, :, c * LANES:(c + 1) * LANES] = (
                rowg_ref[pl.ds(c, cap, stride=ACC_ROWS), :].astype(BF16))

    def lay_out_output():
        g2 = mods_ref[pl.ds(b, 1), 5 * D_MODEL:6 * D_MODEL]
        for c in range(ACC_ROWS):
            cols = slice(c * LANES, (c + 1) * LANES)
            rows_ref[pl.ds(c, cap, stride=ACC_ROWS), :] = ys_ref[:, cols] * g2[:, cols]

    chunk_rows = EPI_TOK * ACC_ROWS

    def seed_copy(sample, chunk):
        rows = pl.ds(chunk * chunk_rows, chunk_rows)
        return pltpu.make_async_copy(x1_hbm.at[sample, rows], acc_ref.at[rows], seed_sem.at[chunk])

    def expert_tile():
        xs = xs_ref[slot]
        hid = _silu(_dot(xs, wg_ref[0, 0])) * _dot(xs, wu_ref[0, 0])
        ys_ref[...] = _dot(hid.astype(BF16), wd_ref[0]) + jnp.where(ff == 0, 0.0, ys_ref[...])

    def input_copy(sample):
        return pltpu.make_async_copy(hr_hbm.at[sample], hr_ref, hr_sem.at[0])

    @pl.when((s == 0) & (b == 0))
    def _():
        input_copy(0).start()
        for chunk in range(n_epi):
            seed_copy(0, chunk).start()

    @pl.when(s == 0)
    def _():
        input_copy(b).wait()
        ys_ref[...] = jnp.zeros_like(ys_ref)
        in_groups(gather, list_base(0))
        lay_out_input(0)

    @pl.when(s == n_ff - 1)
    def _():
        for chunk in range(n_epi):
            seed_copy(b, chunk).wait()

    @pl.when((s < n_work) & (ff == 0))
    def _():
        gather(list_base(jnp.minimum(e + 1, N_EXPERTS - 1)), 0, cap)
        lay_out_output()
        expert_tile()

    if n_ff > 2:
        @pl.when((s < n_work) & (ff > 0) & (ff < n_ff - 1))
        def _():
            expert_tile()

    @pl.when((s < n_work) & (ff == n_ff - 1))
    def _():
        lay_out_input(1 - slot)
        scatter(list_base(jnp.maximum(e - 1, 0)), 0, cap)
        expert_tile()

    @pl.when(s == n_work)
    def _():
        @pl.when(b + 1 < pl.num_programs(0))
        def _():
            input_copy(b + 1).start()

        lay_out_output()
        in_groups(scatter, list_base(N_EXPERTS - 1))

    @pl.when(s >= n_work)
    def _():
        chunk = s - n_work
        row0 = chunk * chunk_rows
        tot = jnp.concatenate([acc_ref[pl.ds(row0 + c, EPI_TOK, stride=ACC_ROWS), :] for c in range(ACC_ROWS)], axis=1)
        out_ref[0] = _rms(tot, nfin_ref[...])

        @pl.when(b + 1 < pl.num_programs(0))
        def _():
            seed_copy(b + 1, chunk).start()


def _moe(idx_flat, gate_flat, hr, wg, wu, wd, x1, mods, nfin, cap):
    nb, t = x1.shape[0], x1.shape[1] // ACC_ROWS
    n_ff = wg.shape[1]
    n_work = N_EXPERTS * n_ff
    n_epi = t // EPI_TOK

    def w_idx(s):
        sc = jnp.minimum(s, n_work - 1)
        return sc // n_ff, sc % n_ff

    def epi(b, s, *_):
        return (b, jnp.maximum(s - n_work, 0), 0)

    grid_spec = pltpu.PrefetchScalarGridSpec(
        num_scalar_prefetch=2,
        grid=(nb, n_work + n_epi),
        in_specs=[pl.BlockSpec(memory_space=pl.ANY),
                  pl.BlockSpec((1, 1, D_MODEL, DFF_TILE), lambda b, s, *_: w_idx(s) + (0, 0)),
                  pl.BlockSpec((1, 1, D_MODEL, DFF_TILE), lambda b, s, *_: w_idx(s) + (0, 0)),
                  pl.BlockSpec((1, DFF_TILE, D_MODEL), lambda b, s, *_: (w_idx(s)[0], w_idx(s)[1], 0)),
                  pl.BlockSpec(memory_space=pl.ANY),
                  pl.BlockSpec(mods.shape, lambda b, s, *_: (0, 0)),
                  pl.BlockSpec((1, D_MODEL), lambda b, s, *_: (0, 0))],
        out_specs=pl.BlockSpec((1, EPI_TOK, D_MODEL), epi),
        scratch_shapes=[pltpu.VMEM((t * ACC_ROWS, LANES), F32),
                        pltpu.VMEM((t * ACC_ROWS, LANES), F32),
                        pltpu.VMEM((cap * ACC_ROWS, LANES), F32),
                        pltpu.VMEM((cap * ACC_ROWS, LANES), F32),
                        pltpu.VMEM((2, cap, D_MODEL), BF16),
                        pltpu.VMEM((cap, D_MODEL), F32),
                        pltpu.SemaphoreType.DMA((n_epi,)),
                        pltpu.SemaphoreType.DMA((1,))])
    return pl.pallas_call(
        functools.partial(_moe_body, cap=cap, n_ff=n_ff, n_epi=n_epi),
        grid_spec=grid_spec,
        out_shape=jax.ShapeDtypeStruct((nb, t, D_MODEL), F32),
        compiler_params=pltpu.CompilerParams(dimension_semantics=("arbitrary", "arbitrary"),
                                             vmem_limit_bytes=VMEM_LIMIT),
        name="moe",
    )(idx_flat, gate_flat, hr, wg, wu, wd, x1, mods, nfin)


def kernel(x, c, ctx, c_ctx, w_ada, b_ada, norm_mix, norm_ffn, w_in, lb_logits, hg_norm, conv_w,
           w_out, w_router, w_gate, w_up, w_down, norm_final):
    nb, t, d = x.shape
    assert d == D_MODEL and w_ada.shape[0] == 1 and nb < MOD_ROWS and t % TT == 0 and TT % GRID_W == 0
    assert w_in.shape[2] == 3 * HG_KEYS + 2 * HG_WIDTH + 3 * SC_WIDTH and w_router.shape[2] == N_EXPERTS
    cap = EC_CAPACITY * t // N_EXPERTS
    assert cap % LANES == 0 and w_gate.shape[3] % DFF_TILE == 0 and w_gate.shape[3] // DFF_TILE >= 2
    assert (nb * (t // TT)) % N_EXPERTS == 0 and (D_MODEL * N_EXPERTS) % (nb * (t // TT)) == 0

    cc = jnp.concatenate([c, c_ctx[None, :], jnp.zeros((MOD_ROWS - nb - 1, d), F32)], axis=0)
    w_in_bf = w_in[0].astype(BF16)
    w_out_bf = w_out[0].astype(BF16)
    wr_hi, wr_lo = _split(jnp.pad(w_router[0], ((0, 0), (0, LANES - N_EXPERTS))))
    wr_pad = jnp.concatenate([wr_hi, wr_lo], axis=1)
    nm, nf, nfin = norm_mix[0][None, :], norm_ffn[0][None, :], norm_final[None, :]
    hgn = hg_norm[0][None, :]

    mods = _adaln(cc, w_ada[0], b_ada[0][None, :])
    s0_f, s0_b = _ctx_states(ctx, mods, nm, w_in_bf, lb_logits, ctx_row=nb)
    steps = nb * (t // TT)
    sliced = lambda w: w.reshape((steps, w.shape[0] * w.shape[1] // steps) + w.shape[2:])
    pk, fz, wg, wu = _mix_fwd(x, mods, nm, w_in_bf, lb_logits, conv_w[0], s0_f, w_gate[0], w_up[0], DFF_TILE)
    x1, hr, aff_t, wd = _mix_bwd(x, pk, fz, mods, lb_logits, hgn, w_out_bf, nf, wr_pad, s0_b, sliced(w_down[0]))
    wd = wd.reshape(w_down[0].shape)
    idx, gates = _route(aff_t, cap)
    return _moe(idx.reshape(-1), gates.reshape(-1), hr, wg, wu, wd, x1, mods, nfin, cap)
```

```python
import functools

import jax
import jax.numpy as jnp
from jax import lax
from jax.experimental import pallas as pl
from jax.experimental.pallas import tpu as pltpu

F32 = jnp.float32
BF16 = jnp.bfloat16
I32 = jnp.int32

D_MODEL = 1024
N_MOD = 6
HG_HEADS = 4
HG_KDIM = 128
HG_VDIM = 128
HG_KEYS = HG_HEADS * HG_KDIM
HG_WIDTH = HG_HEADS * HG_VDIM
SC_WIDTH = D_MODEL - HG_WIDTH
GRID_W = 64
CHUNK = 32
N_EXPERTS = 16
EC_CAPACITY = 2
EPS = 1e-6

LANES = 128
SUBLANES = 8
VMEM_LIMIT = 62 * 1024 * 1024

TT = 256
NCH = TT // CHUNK
MOD_ROWS = 16
DFF_TILE = 1024
EPI_TOK = 512
ACC_ROWS = D_MODEL // LANES


def _dot(a, b):
    return jnp.dot(a, b, preferred_element_type=F32)


def _dot_nt(a, b):
    return lax.dot_general(a, b, (((1,), (1,)), ((), ())), preferred_element_type=F32)


def _dot_tn(a, b):
    return lax.dot_general(a, b, (((0,), (0,)), ((), ())), preferred_element_type=F32)


def _split(x):
    hi = x.astype(BF16)
    lo = (x - hi.astype(F32)).astype(BF16)
    return hi, lo


def _rms(x, gain):
    return x * lax.rsqrt(jnp.mean(x * x, axis=-1, keepdims=True) + EPS) * gain


def _silu(x):
    return x * jax.nn.sigmoid(x)


def _lower_bounds(lbl_ref):
    rows = [lbl_ref[l] for l in range(lbl_ref.shape[0])]
    m = functools.reduce(jnp.maximum, rows)
    es = [jnp.exp(r - m) for r in rows]
    sm0 = es[0] / functools.reduce(lambda a, b: a + b, es)
    return sm0[0:1, :], sm0[1:2, :]


def _ada_body(c_ref, w_ref, b_ref, o_ref):
    s_hi, s_lo = _split(_silu(c_ref[...]))
    w_hi, w_lo = _split(w_ref[...])
    o_ref[...] = _dot(s_hi, w_hi) + _dot(s_hi, w_lo) + _dot(s_lo, w_hi) + b_ref[...]


def _adaln(cc, w, b):
    n = w.shape[1]
    tn = n // 4
    return pl.pallas_call(
        _ada_body,
        grid=(n // tn,),
        in_specs=[pl.BlockSpec((MOD_ROWS, D_MODEL), lambda i: (0, 0)),
                  pl.BlockSpec((D_MODEL, tn), lambda i: (0, i)),
                  pl.BlockSpec((1, tn), lambda i: (0, i))],
        out_specs=pl.BlockSpec((MOD_ROWS, tn), lambda i: (0, i)),
        out_shape=jax.ShapeDtypeStruct((MOD_ROWS, n), F32),
        compiler_params=pltpu.CompilerParams(vmem_limit_bytes=VMEM_LIMIT),
        name="adaln",
    )(cc, w, b)


def _ctx_body(ctx_ref, mods_ref, nm_ref, wf_ref, wb_ref, wi_ref, lbl_ref, sf_ref, sb_ref, *, ctx_row):
    x = ctx_ref[0]
    n = x.shape[0]
    csh = mods_ref[ctx_row:ctx_row + 1, 0:D_MODEL]
    csc = mods_ref[ctx_row:ctx_row + 1, D_MODEL:2 * D_MODEL]
    hb = (_rms(x, nm_ref[...]) * (1.0 + csc) + csh).astype(BF16)
    zf = _dot(hb, wf_ref[...])
    zb = _dot(hb, wb_ref[...])
    vi = _dot(hb, wi_ref[...]).astype(BF16)
    lb_f, lb_b = _lower_bounds(lbl_ref)
    f_f = lb_f + (1.0 - lb_f) * jax.nn.sigmoid(zf)
    f_b = lb_b + (1.0 - lb_b) * jax.nn.sigmoid(zb)
    ii = lax.broadcasted_iota(I32, (n, n), 0)
    jj = lax.broadcasted_iota(I32, (n, n), 1)
    after = jnp.where(jj > ii, 1.0, 0.0).astype(BF16)
    before = jnp.where(jj < ii, 1.0, 0.0).astype(BF16)
    hi, lo = _split(jnp.log(f_f))
    kd_f = ((1.0 - f_f) * jnp.exp(_dot(after, hi) + _dot(after, lo))).astype(BF16)
    hi, lo = _split(jnp.log(f_b))
    kd_b = ((1.0 - f_b) * jnp.exp(_dot(before, hi) + _dot(before, lo))).astype(BF16)
    for h in range(HG_HEADS):
        ks = slice(h * HG_KDIM, (h + 1) * HG_KDIM)
        vs = slice(h * HG_VDIM, (h + 1) * HG_VDIM)
        sf_ref[0, h] = _dot_tn(vi[:, vs], kd_f[:, ks])
        sb_ref[0, h] = _dot_tn(vi[:, vs], kd_b[:, ks])


def _ctx_states(ctx, mods, nm, w_in_bf, lbl, ctx_row):
    nb, n, _ = ctx.shape
    st = jax.ShapeDtypeStruct((nb, HG_HEADS, HG_VDIM, HG_KDIM), F32)
    st_spec = pl.BlockSpec((1, HG_HEADS, HG_VDIM, HG_KDIM), lambda b: (b, 0, 0, 0))
    wcol = lambda k: pl.BlockSpec((D_MODEL, HG_KEYS), lambda b, k=k: (0, k))
    return pl.pallas_call(
        functools.partial(_ctx_body, ctx_row=ctx_row),
        grid=(nb,),
        in_specs=[pl.BlockSpec((1, n, D_MODEL), lambda b: (b, 0, 0)),
                  pl.BlockSpec(mods.shape, lambda b: (0, 0)),
                  pl.BlockSpec((1, D_MODEL), lambda b: (0, 0)),
                  wcol(1), wcol(2), wcol(3),
                  pl.BlockSpec(lbl.shape, lambda b: (0, 0, 0))],
        out_specs=[st_spec, st_spec],
        out_shape=[st, st],
        compiler_params=pltpu.CompilerParams(vmem_limit_bytes=VMEM_LIMIT),
        name="ctx_states",
    )(ctx, mods, nm, w_in_bf, w_in_bf, w_in_bf, lbl)


def _tile_positions(reverse):
    ii = lax.broadcasted_iota(I32, (TT, TT), 0)
    jj = lax.broadcasted_iota(I32, (TT, TT), 1)
    if reverse:
        ii = TT - 1 - ii
        jj = TT - 1 - jj
    sh0 = CHUNK.bit_length() - 1
    return ii, jj, ii >> sh0, jj >> sh0


def _init_maps(lm_ref, tri_ref, reverse):
    lm_ref[...] = _level_map(reverse)
    ii, jj, ci, cj = _tile_positions(reverse)
    tri_ref[...] = jnp.where((ci == cj) & (jj <= ii), 1.0, 0.0).astype(BF16)


def _level_map(reverse):
    ii, jj, ci, cj = _tile_positions(reverse)
    lm = jnp.where(((ci >> 1) == (cj >> 1)) & (jj <= ii), 1, 0)
    for lvl in range(2, NCH.bit_length()):
        m = ((ci >> lvl) == (cj >> lvl)) & (((ci >> (lvl - 1)) & 1) == 1) & (((cj >> (lvl - 1)) & 1) == 0)
        lm = jnp.where(m, lvl, lm)
    return lm


def _gla_gates(zf, lb, tri):
    f = lb + (1.0 - lb) * jax.nn.sigmoid(zf)
    hi, lo = _split(jnp.log(f))
    return 1.0 - f, _dot(tri, hi) + _dot(tri, lo)


def _gla_tile(q, k, cum, v, st_ref, lm, reverse):
    n_lvl = NCH.bit_length() - 2
    phys = list(range(NCH - 1, -1, -1)) if reverse else list(range(NCH))
    last = 0 if reverse else CHUNK - 1
    tot = [cum[a * CHUNK + last:a * CHUNK + last + 1, :] for a in range(NCH)]
    bnd = [jnp.zeros_like(tot[0])]
    for c in range(NCH):
        bnd.append(bnd[-1] + tot[phys[c]])
    first = [None] * NCH
    for c in range(NCH):
        first[phys[c]] = c % 2 == 0
    ref = jnp.concatenate([cum[a * CHUNK:(a + 1) * CHUNK] - tot[a] if first[a] else cum[a * CHUNK:(a + 1) * CHUNK]
                           for a in range(NCH)], axis=0)
    qp = q * jnp.exp(ref)
    kp = k * jnp.exp(-ref)
    qs = [[None] * NCH for _ in range(n_lvl + 1)]
    ks = [[None] * NCH for _ in range(n_lvl + 1)]
    for c in range(NCH):
        a = phys[c]
        rows = slice(a * CHUNK, (a + 1) * CHUNK)
        to_start = tot[a] if first[a] else 0.0
        to_end = 0.0 if first[a] else tot[a]
        zero = jnp.zeros_like(qp[rows])
        for lvl in range(1, n_lvl + 1):
            s = 1 << lvl
            mid = (c // (2 * s)) * 2 * s + s
            right = (c // s) % 2 == 1
            qs[lvl - 1][a] = qp[rows] * jnp.exp(bnd[c] - bnd[mid] + to_start) if right else zero
            ks[lvl - 1][a] = zero if right else kp[rows] * jnp.exp(bnd[mid] - bnd[c + 1] + to_end)
        qs[n_lvl][a] = qp[rows] * jnp.exp(bnd[c] + to_start)
        ks[n_lvl][a] = kp[rows] * jnp.exp(bnd[NCH] - bnd[c + 1] + to_end)
    qs = [jnp.concatenate(p, axis=0).astype(BF16) for p in qs]
    ks = [jnp.concatenate(p, axis=0).astype(BF16) for p in ks]
    qp = qp.astype(BF16)
    kp = kp.astype(BF16)
    vb = v.astype(BF16)
    dec = jnp.exp(bnd[NCH])
    outs = []
    for h in range(HG_HEADS):
        kk = slice(h * HG_KDIM, (h + 1) * HG_KDIM)
        vv = slice(h * HG_VDIM, (h + 1) * HG_VDIM)
        sc = jnp.where(lm == 1, _dot_nt(qp[:, kk], kp[:, kk]), 0.0)
        for lvl in range(n_lvl):
            sc = jnp.where(lm == lvl + 2, _dot_nt(qs[lvl][:, kk], ks[lvl][:, kk]), sc)
        st = st_ref[h]
        outs.append(_dot(sc.astype(BF16), vb[:, vv]) + _dot_nt(qs[n_lvl][:, kk], st.astype(BF16)))
        st_ref[h] = st * dec[:, kk] + _dot_tn(vb[:, vv], ks[n_lvl][:, kk])
    return jnp.concatenate(outs, axis=1)


def _mix_fwd_body(x_ref, mods_ref, nm_ref, w_ref, lbl_ref, cw_ref, s0_ref, wa_ref, wb_ref,
                  pk_ref, fz_ref, wa_bf_ref, wb_bf_ref, st_ref, lm_ref, tri_ref):
    b = pl.program_id(0)
    j = pl.program_id(1)

    @pl.when((b == 0) & (j == 0))
    def _():
        _init_maps(lm_ref, tri_ref, False)

    @pl.when(j == 0)
    def _():
        st_ref[...] = s0_ref[0]

    sh = mods_ref[pl.ds(b, 1), 0:D_MODEL]
    sc = mods_ref[pl.ds(b, 1), D_MODEL:2 * D_MODEL]
    hx = (_rms(x_ref[0], nm_ref[...]) * (1.0 + sc) + sh).astype(BF16)
    w = HG_KEYS
    proj = lambda k: _dot(hx, w_ref[:, k * w:(k + 1) * w])

    def cast_column_tiles(src_ref, dst_ref):
        tile = dst_ref.shape[3]
        for f in range(dst_ref.shape[1]):
            dst_ref[0, f] = src_ref[0, :, f * tile:(f + 1) * tile].astype(BF16)

    gate_b = proj(5)
    cast_column_tiles(wa_ref, wa_bf_ref)
    u = proj(6) * proj(7)
    cast_column_tiles(wb_ref, wb_bf_ref)
    zff = proj(1)
    col = lax.broadcasted_iota(I32, (TT, 1), 0) & (GRID_W - 1)
    up = jnp.where(col != 0, pltpu.roll(u, 1, axis=0), 0.0)
    un = jnp.where(col != GRID_W - 1, pltpu.roll(u, TT - 1, axis=0), 0.0)
    cw = cw_ref[...]
    pk_ref[0, :, 3 * w:4 * w] = (gate_b * (cw[0:1] * up + cw[1:2] * u + cw[2:3] * un)).astype(BF16)
    q = proj(0)
    lb_f, _ = _lower_bounds(lbl_ref)
    k, cum = _gla_gates(zff, lb_f, tri_ref[...])
    vi = proj(3)
    fz_ref[0, :, 0:w] = proj(2)
    pk_ref[0, :, 2 * w:3 * w] = proj(4).astype(BF16)
    fz_ref[0, :, w:2 * w] = _gla_tile(q, k, cum, vi, st_ref, lm_ref[...], False)
    pk_ref[0, :, 0:w] = q.astype(BF16)
    pk_ref[0, :, w:2 * w] = vi.astype(BF16)


def _mix_fwd(x, mods, nm, w_in_bf, lbl, cw, s0_f, wa, wb, col_tile):
    nb, t, _ = x.shape
    nt = t // TT
    ne, rows, cols = wa.shape
    per_expert = nb * nt // ne
    step_rows = rows // per_expert
    tok = lambda n, dt: jax.ShapeDtypeStruct((nb, t, n * HG_WIDTH), dt)
    tspec = lambda n: pl.BlockSpec((1, TT, n * HG_WIDTH), lambda b, j: (b, j, 0))
    cast_in = pl.BlockSpec((1, step_rows, cols), lambda b, j: ((b * nt + j) // per_expert, (b * nt + j) % per_expert, 0))
    cast_out = pl.BlockSpec((1, cols // col_tile, step_rows, col_tile),
                            lambda b, j: ((b * nt + j) // per_expert, 0, (b * nt + j) % per_expert, 0))
    cast_shape = jax.ShapeDtypeStruct((ne, cols // col_tile, rows, col_tile), BF16)
    return pl.pallas_call(
        _mix_fwd_body,
        grid=(nb, nt),
        in_specs=[pl.BlockSpec((1, TT, D_MODEL), lambda b, j: (b, j, 0)),
                  pl.BlockSpec(mods.shape, lambda b, j: (0, 0)),
                  pl.BlockSpec((1, D_MODEL), lambda b, j: (0, 0)),
                  pl.BlockSpec(w_in_bf.shape, lambda b, j: (0, 0)),
                  pl.BlockSpec(lbl.shape, lambda b, j: (0, 0, 0)),
                  pl.BlockSpec(cw.shape, lambda b, j: (0, 0)),
                  pl.BlockSpec((1, HG_HEADS, HG_VDIM, HG_KDIM), lambda b, j: (b, 0, 0, 0)),
                  cast_in, cast_in],
        out_specs=[tspec(4), tspec(2), cast_out, cast_out],
        out_shape=[tok(4, BF16), tok(2, F32), cast_shape, cast_shape],
        scratch_shapes=[pltpu.VMEM((HG_HEADS, HG_VDIM, HG_KDIM), F32), pltpu.VMEM((TT, TT), I32),
                        pltpu.VMEM((TT, TT), BF16)],
        compiler_params=pltpu.CompilerParams(dimension_semantics=("arbitrary", "arbitrary"),
                                             vmem_limit_bytes=VMEM_LIMIT),
        name="mix_fwd",
    )(x, mods, nm, w_in_bf, lbl, cw, s0_f, wa, wb)


def _mix_bwd_body(x_ref, pk_ref, fz_ref, mods_ref, lbl_ref, hgn_ref,
                  wo_ref, nf_ref, wr_ref, s0_ref, wc_ref,
                  x1_ref, hr_ref, at_ref, wc_bf_ref, st_ref, lm_ref, tri_ref):
    b = pl.program_id(0)
    j = pl.program_id(1)
    wc_bf_ref[...] = wc_ref[...].astype(BF16)

    @pl.when((b == 0) & (j == 0))
    def _():
        _init_maps(lm_ref, tri_ref, True)

    @pl.when(j == 0)
    def _():
        st_ref[...] = s0_ref[0]

    _, lb_b = _lower_bounds(lbl_ref)
    w = HG_WIDTH
    k, cum = _gla_gates(fz_ref[0, :, 0:w], lb_b, tri_ref[...])
    o_b = _gla_tile(pk_ref[0, :, 0:w].astype(F32), k, cum, pk_ref[0, :, w:2 * w], st_ref, lm_ref[...], True)
    o = fz_ref[0, :, w:2 * w] + o_b
    hgn = hgn_ref[...]
    heads = [_rms(o[:, h * HG_VDIM:(h + 1) * HG_VDIM], hgn) for h in range(HG_HEADS)]
    g = pk_ref[0, :, 2 * w:3 * w].astype(F32)
    ohg = jnp.concatenate(heads, axis=1) * _silu(g)
    cat = jnp.concatenate([ohg.astype(BF16), pk_ref[0, :, 3 * w:4 * w]], axis=1)
    mx = _dot(cat, wo_ref[...])
    mod = lambda k: mods_ref[pl.ds(b, 1), k * D_MODEL:(k + 1) * D_MODEL]
    x1 = x_ref[0] + mod(2) * mx
    h2 = _rms(x1, nf_ref[...]) * (1.0 + mod(4)) + mod(3)
    for c in range(ACC_ROWS):
        x1_ref[0, pl.ds(c, TT, stride=ACC_ROWS), :] = x1[:, c * LANES:(c + 1) * LANES]
        hr_ref[0, pl.ds(c, TT, stride=ACC_ROWS), :] = h2[:, c * LANES:(c + 1) * LANES]
    r = _dot(jnp.concatenate(_split(h2), axis=0), wr_ref[...])
    lg = r[:TT, :LANES] + r[:TT, LANES:] + r[TT:, :LANES]
    lgt = lg.T[0:N_EXPERTS, :]
    e = jnp.exp(lgt - jnp.max(lgt, axis=0, keepdims=True))
    at_ref[0] = e / jnp.sum(e, axis=0, keepdims=True)


def _mix_bwd(x, pk, fz, mods, lbl, hgn, w_out_bf, nf, wr_pad, s0_b, wc):
    nb, t, _ = x.shape
    nt = t // TT
    cast_spec = pl.BlockSpec((1,) + wc.shape[1:], lambda b, j: (b * nt + j, 0, 0))
    rev = lambda b, j: (b, nt - 1 - j, 0)
    tspec = lambda a: pl.BlockSpec((1, TT, a.shape[2]), rev)
    xspec = pl.BlockSpec((1, TT, D_MODEL), rev)
    full2 = lambda a: pl.BlockSpec(a.shape, lambda b, j: (0, 0))
    return pl.pallas_call(
        _mix_bwd_body,
        grid=(nb, nt),
        in_specs=[xspec, tspec(pk), tspec(fz),
                  full2(mods), pl.BlockSpec(lbl.shape, lambda b, j: (0, 0, 0)), full2(hgn),
                  full2(w_out_bf), full2(nf), full2(wr_pad),
                  pl.BlockSpec((1, HG_HEADS, HG_VDIM, HG_KDIM), lambda b, j: (b, 0, 0, 0)),
                  cast_spec],
        out_specs=[pl.BlockSpec((1, TT * ACC_ROWS, LANES), rev),
                   pl.BlockSpec((1, TT * ACC_ROWS, LANES), rev),
                   pl.BlockSpec((1, N_EXPERTS, TT), lambda b, j: (b, 0, nt - 1 - j)),
                   cast_spec],
        out_shape=[jax.ShapeDtypeStruct((nb, t * ACC_ROWS, LANES), F32),
                   jax.ShapeDtypeStruct((nb, t * ACC_ROWS, LANES), F32),
                   jax.ShapeDtypeStruct((nb, N_EXPERTS, t), F32),
                   jax.ShapeDtypeStruct(wc.shape, BF16)],
        scratch_shapes=[pltpu.VMEM((HG_HEADS, HG_VDIM, HG_KDIM), F32), pltpu.VMEM((TT, TT), I32),
                        pltpu.VMEM((TT, TT), BF16)],
        compiler_params=pltpu.CompilerParams(dimension_semantics=("arbitrary", "arbitrary"),
                                             vmem_limit_bytes=VMEM_LIMIT),
        name="mix_bwd",
    )(x, pk, fz, mods, lbl, hgn, w_out_bf, nf, wr_pad, s0_b, wc)


def _route_body(at_ref, idx_ref, gate_ref, *, cap):
    ne, t = at_ref.shape[1], at_ref.shape[2]
    blk = 2 * LANES
    aff = at_ref[0]

    def count(m):
        return jnp.sum(jnp.where(m, 1.0, 0.0), axis=1, keepdims=True)

    def enough(cand):
        return count(aff >= lax.bitcast_convert_type(cand, F32)) >= cap

    def bit_step(i, thr):
        hi = jnp.left_shift(jnp.int32(1), 30 - 2 * i)
        lo = jnp.right_shift(hi, 1)
        both, upper, lower = thr | hi | lo, thr | hi, thr | lo
        return jnp.where(enough(both), both, jnp.where(enough(upper), upper, jnp.where(enough(lower), lower, thr)))

    thr = lax.fori_loop(0, 16, bit_step, jnp.zeros((ne, 1), I32))
    gt = aff >= lax.bitcast_convert_type(thr + 1, F32)
    eq = (aff >= lax.bitcast_convert_type(thr, F32)) & jnp.logical_not(gt)
    need = cap - count(gt)

    ii = lax.broadcasted_iota(I32, (blk, blk), 0)
    jj = lax.broadcasted_iota(I32, (blk, blk), 1)
    upper = jnp.where(ii <= jj, 1.0, 0.0).astype(BF16)

    def prefix_blocks(m):
        mb = jnp.where(m, 1.0, 0.0).astype(BF16)
        carry = jnp.zeros((ne, 1), F32)
        out = []
        for kb in range(t // blk):
            p = _dot(mb[:, kb * blk:(kb + 1) * blk], upper) + carry
            out.append(p)
            carry = p[:, blk - 1:blk]
        return out

    eq_rank = jnp.concatenate(prefix_blocks(eq), axis=1)
    sel = gt | (eq & (eq_rank <= need))
    rank = jnp.concatenate(prefix_blocks(sel), axis=1).astype(I32)

    tok = lax.broadcasted_iota(I32, (ne, t), 1)
    x = jnp.where(sel, tok, -1)
    d = jnp.where(sel, tok - rank + 1, 0)
    g = aff
    for k in range(t.bit_length() - 1):
        left = lambda a: pltpu.roll(a, t - (1 << k), axis=1)
        ds = left(d)
        take = ((ds >> k) & 1) == 1
        keep = ((d >> k) & 1) == 0
        x = jnp.where(take, left(x), jnp.where(keep, x, -1))
        g = jnp.where(take, left(g), g)
        d = jnp.where(take, ds, jnp.where(keep, d, 0))
    idx_ref[0] = x[:, :cap]
    gate_ref[0] = g[:, :cap]


def _route(aff_t, cap):
    nb, ne, t = aff_t.shape
    spec = pl.BlockSpec((1, ne, cap), lambda b: (b, 0, 0))
    return pl.pallas_call(
        functools.partial(_route_body, cap=cap),
        grid=(nb,),
        in_specs=[pl.BlockSpec((1, ne, t), lambda b: (b, 0, 0))],
        out_specs=[spec, spec],
        out_shape=[jax.ShapeDtypeStruct((nb, ne, cap), I32), jax.ShapeDtypeStruct((nb, ne, cap), F32)],
        compiler_params=pltpu.CompilerParams(vmem_limit_bytes=VMEM_LIMIT),
        name="route",
    )(aff_t)


def _moe_body(idx_ref, gate_ref, hr_hbm, wg_ref, wu_ref, wd_ref, x1_hbm, mods_ref, nfin_ref,
              out_ref, acc_ref, hr_ref, rowg_ref, rows_ref, xs_ref, ys_ref, seed_sem, hr_sem, *, cap, n_ff, n_epi):
    b = pl.program_id(0)
    s = pl.program_id(1)
    n_work = N_EXPERTS * n_ff
    e = s // n_ff
    ff = s % n_ff
    slot = e % 2
    group = SUBLANES

    def list_base(expert):
        return (b * N_EXPERTS + expert) * cap

    def gather(base, j0, n):
        for u in range(n):
            tkn = idx_ref[base + j0 + u]
            rowg_ref[pl.ds(pl.multiple_of((j0 + u) * ACC_ROWS, ACC_ROWS), ACC_ROWS), :] = (
                hr_ref[pl.ds(pl.multiple_of(tkn * ACC_ROWS, ACC_ROWS), ACC_ROWS), :])

    def scatter(base, j0, n):
        for g0 in range(0, n, group):
            dst = []
            val = []
            for u in range(g0, g0 + group):
                r = pl.multiple_of(idx_ref[base + j0 + u] * ACC_ROWS, ACC_ROWS)
                dst.append(r)
                val.append(acc_ref[pl.ds(r, ACC_ROWS), :] + gate_ref[base + j0 + u]
                           * rows_ref[pl.ds(pl.multiple_of((j0 + u) * ACC_ROWS, ACC_ROWS), ACC_ROWS), :])
            for r, v in zip(dst, val):
                acc_ref[pl.ds(r, ACC_ROWS), :] = v

    def in_groups(fn, base):
        def step(jo, carry):
            fn(base, jo * group, group)
            return carry
        lax.fori_loop(0, cap // group, step, 0)

    def lay_out_input(to_slot):
        for c in range(ACC_ROWS):
            xs_ref[to_slot, :, c * LANES:(c + 1) * LANES] = (
                rowg_ref[pl.ds(c, cap, stride=ACC_ROWS), :].astype(BF16))

    def lay_out_output():
        g2 = mods_ref[pl.ds(b, 1), 5 * D_MODEL:6 * D_MODEL]
        for c in range(ACC_ROWS):
            cols = slice(c * LANES, (c + 1) * LANES)
            rows_ref[pl.ds(c, cap, stride=ACC_ROWS), :] = ys_ref[:, cols] * g2[:, cols]

    chunk_rows = EPI_TOK * ACC_ROWS

    def seed_copy(sample, chunk):
        rows = pl.ds(chunk * chunk_rows, chunk_rows)
        return pltpu.make_async_copy(x1_hbm.at[sample, rows], acc_ref.at[rows], seed_sem.at[chunk])

    def expert_tile():
        xs = xs_ref[slot]
        hid = _silu(_dot(xs, wg_ref[0, 0])) * _dot(xs, wu_ref[0, 0])
        ys_ref[...] = _dot(hid.astype(BF16), wd_ref[0]) + jnp.where(ff == 0, 0.0, ys_ref[...])

    def input_copy(sample):
        return pltpu.make_async_copy(hr_hbm.at[sample], hr_ref, hr_sem.at[0])

    @pl.when((s == 0) & (b == 0))
    def _():
        input_copy(0).start()
        for chunk in range(n_epi):
            seed_copy(0, chunk).start()

    @pl.when(s == 0)
    def _():
        input_copy(b).wait()
        ys_ref[...] = jnp.zeros_like(ys_ref)
        in_groups(gather, list_base(0))
        lay_out_input(0)

    @pl.when(s == n_ff - 1)
    def _():
        for chunk in range(n_epi):
            seed_copy(b, chunk).wait()

    @pl.when((s < n_work) & (ff == 0))
    def _():
        gather(list_base(jnp.minimum(e + 1, N_EXPERTS - 1)), 0, cap)
        lay_out_output()
        expert_tile()

    if n_ff > 2:
        @pl.when((s < n_work) & (ff > 0) & (ff < n_ff - 1))
        def _():
            expert_tile()

    @pl.when((s < n_work) & (ff == n_ff - 1))
    def _():
        lay_out_input(1 - slot)
        scatter(list_base(jnp.maximum(e - 1, 0)), 0, cap)
        expert_tile()

    @pl.when(s == n_work)
    def _():
        @pl.when(b + 1 < pl.num_programs(0))
        def _():
            input_copy(b + 1).start()

        lay_out_output()
        in_groups(scatter, list_base(N_EXPERTS - 1))

    @pl.when(s >= n_work)
    def _():
        chunk = s - n_work
        row0 = chunk * chunk_rows
        tot = jnp.concatenate([acc_ref[pl.ds(row0 + c, EPI_TOK, stride=ACC_ROWS), :] for c in range(ACC_ROWS)], axis=1)
        out_ref[0] = _rms(tot, nfin_ref[...])

        @pl.when(b + 1 < pl.num_programs(0))
        def _():
            seed_copy(b + 1, chunk).start()


def _moe(idx_flat, gate_flat, hr, wg, wu, wd, x1, mods, nfin, cap):
    nb, t = x1.shape[0], x1.shape[1] // ACC_ROWS
    n_ff = wg.shape[1]
    n_work = N_EXPERTS * n_ff
    n_epi = t // EPI_TOK

    def w_idx(s):
        sc = jnp.minimum(s, n_work - 1)
        return sc // n_ff, sc % n_ff

    def epi(b, s, *_):
        return (b, jnp.maximum(s - n_work, 0), 0)

    grid_spec = pltpu.PrefetchScalarGridSpec(
        num_scalar_prefetch=2,
        grid=(nb, n_work + n_epi),
        in_specs=[pl.BlockSpec(memory_space=pl.ANY),
                  pl.BlockSpec((1, 1, D_MODEL, DFF_TILE), lambda b, s, *_: w_idx(s) + (0, 0)),
                  pl.BlockSpec((1, 1, D_MODEL, DFF_TILE), lambda b, s, *_: w_idx(s) + (0, 0)),
                  pl.BlockSpec((1, DFF_TILE, D_MODEL), lambda b, s, *_: (w_idx(s)[0], w_idx(s)[1], 0)),
                  pl.BlockSpec(memory_space=pl.ANY),
                  pl.BlockSpec(mods.shape, lambda b, s, *_: (0, 0)),
                  pl.BlockSpec((1, D_MODEL), lambda b, s, *_: (0, 0))],
        out_specs=pl.BlockSpec((1, EPI_TOK, D_MODEL), epi),
        scratch_shapes=[pltpu.VMEM((t * ACC_ROWS, LANES), F32),
                        pltpu.VMEM((t * ACC_ROWS, LANES), F32),
                        pltpu.VMEM((cap * ACC_ROWS, LANES), F32),
                        pltpu.VMEM((cap * ACC_ROWS, LANES), F32),
                        pltpu.VMEM((2, cap, D_MODEL), BF16),
                        pltpu.VMEM((cap, D_MODEL), F32),
                        pltpu.SemaphoreType.DMA((n_epi,)),
                        pltpu.SemaphoreType.DMA((1,))])
    return pl.pallas_call(
        functools.partial(_moe_body, cap=cap, n_ff=n_ff, n_epi=n_epi),
        grid_spec=grid_spec,
        out_shape=jax.ShapeDtypeStruct((nb, t, D_MODEL), F32),
        compiler_params=pltpu.CompilerParams(dimension_semantics=("arbitrary", "arbitrary"),
                                             vmem_limit_bytes=VMEM_LIMIT),
        name="moe",
    )(idx_flat, gate_flat, hr, wg, wu, wd, x1, mods, nfin)


def kernel(x, c, ctx, c_ctx, w_ada, b_ada, norm_mix, norm_ffn, w_in, lb_logits, hg_norm, conv_w,
           w_out, w_router, w_gate, w_up, w_down, norm_final):
    nb, t, d = x.shape
    assert d == D_MODEL and w_ada.shape[0] == 1 and nb < MOD_ROWS and t % TT == 0 and TT % GRID_W == 0
    assert w_in.shape[2] == 3 * HG_KEYS + 2 * HG_WIDTH + 3 * SC_WIDTH and w_router.shape[2] == N_EXPERTS
    cap = EC_CAPACITY * t // N_EXPERTS
    assert cap % LANES == 0 and w_gate.shape[3] % DFF_TILE == 0 and w_gate.shape[3] // DFF_TILE >= 2
    assert (nb * (t // TT)) % N_EXPERTS == 0 and (D_MODEL * N_EXPERTS) % (nb * (t // TT)) == 0

    cc = jnp.concatenate([c, c_ctx[None, :], jnp.zeros((MOD_ROWS - nb - 1, d), F32)], axis=0)
    w_in_bf = w_in[0].astype(BF16)
    w_out_bf = w_out[0].astype(BF16)
    wr_hi, wr_lo = _split(jnp.pad(w_router[0], ((0, 0), (0, LANES - N_EXPERTS))))
    wr_pad = jnp.concatenate([wr_hi, wr_lo], axis=1)
    nm, nf, nfin = norm_mix[0][None, :], norm_ffn[0][None, :], norm_final[None, :]
    hgn = hg_norm[0][None, :]

    mods = _adaln(cc, w_ada[0], b_ada[0][None, :])
    s0_f, s0_b = _ctx_states(ctx, mods, nm, w_in_bf, lb_logits, ctx_row=nb)
    steps = nb * (t // TT)
    sliced = lambda w: w.reshape((steps, w.shape[0] * w.shape[1] // steps) + w.shape[2:])
    pk, fz, wg, wu = _mix_fwd(x, mods, nm, w_in_bf, lb_logits, conv_w[0], s0_f, w_gate[0], w_up[0], DFF_TILE)
    x1, hr, aff_t, wd = _mix_bwd(x, pk, fz, mods, lb_logits, hgn, w_out_bf, nf, wr_pad, s0_b, sliced(w_down[0]))
    wd = wd.reshape(w_down[0].shape)
    idx, gates = _route(aff_t, cap)
    return _moe(idx.reshape(-1), gates.reshape(-1), hr, wg, wu, wd, x1, mods, nfin, cap)
```

```python
import functools

import jax
import jax.numpy as jnp
from jax import lax
from jax.experimental import pallas as pl
from jax.experimental.pallas import tpu as pltpu

F32 = jnp.float32
BF16 = jnp.bfloat16
I32 = jnp.int32

D_MODEL = 1024
N_MOD = 6
HG_HEADS = 4
HG_KDIM = 128
HG_VDIM = 128
HG_KEYS = HG_HEADS * HG_KDIM
HG_WIDTH = HG_HEADS * HG_VDIM
SC_WIDTH = D_MODEL - HG_WIDTH
GRID_W = 64
CHUNK = 32
N_EXPERTS = 16
EC_CAPACITY = 2
EPS = 1e-6

LANES = 128
SUBLANES = 8
VMEM_LIMIT = 62 * 1024 * 1024

TT = 256
STEP_TILES = 2
STEP_TOK = TT * STEP_TILES
NCH = TT // CHUNK
MOD_ROWS = 16
DFF_TILE = 1024
EPI_TOK = 512
ACC_ROWS = D_MODEL // LANES


def _dot(a, b):
    return jnp.dot(a, b, preferred_element_type=F32)


def _dot_nt(a, b):
    return lax.dot_general(a, b, (((1,), (1,)), ((), ())), preferred_element_type=F32)


def _dot_tn(a, b):
    return lax.dot_general(a, b, (((0,), (0,)), ((), ())), preferred_element_type=F32)


def _split(x):
    hi = x.astype(BF16)
    lo = (x - hi.astype(F32)).astype(BF16)
    return hi, lo


def _rms(x, gain):
    return x * lax.rsqrt(jnp.mean(x * x, axis=-1, keepdims=True) + EPS) * gain


def _silu(x):
    return x * jax.nn.sigmoid(x)


def _lower_bounds(lbl_ref):
    rows = [lbl_ref[l] for l in range(lbl_ref.shape[0])]
    m = functools.reduce(jnp.maximum, rows)
    es = [jnp.exp(r - m) for r in rows]
    sm0 = es[0] / functools.reduce(lambda a, b: a + b, es)
    return sm0[0:1, :], sm0[1:2, :]


def _ada_body(c_ref, w_ref, b_ref, o_ref):
    s_hi, s_lo = _split(_silu(c_ref[...]))
    w_hi, w_lo = _split(w_ref[...])
    o_ref[...] = _dot(s_hi, w_hi) + _dot(s_hi, w_lo) + _dot(s_lo, w_hi) + b_ref[...]


def _adaln(cc, w, b):
    n = w.shape[1]
    tn = n // 4
    return pl.pallas_call(
        _ada_body,
        grid=(n // tn,),
        in_specs=[pl.BlockSpec((MOD_ROWS, D_MODEL), lambda i: (0, 0)),
                  pl.BlockSpec((D_MODEL, tn), lambda i: (0, i)),
                  pl.BlockSpec((1, tn), lambda i: (0, i))],
        out_specs=pl.BlockSpec((MOD_ROWS, tn), lambda i: (0, i)),
        out_shape=jax.ShapeDtypeStruct((MOD_ROWS, n), F32),
        compiler_params=pltpu.CompilerParams(vmem_limit_bytes=VMEM_LIMIT),
        name="adaln",
    )(cc, w, b)


def _ctx_body(ctx_ref, mods_ref, nm_ref, wf_ref, wb_ref, wi_ref, lbl_ref, sf_ref, sb_ref, *, ctx_row):
    x = ctx_ref[0]
    n = x.shape[0]
    csh = mods_ref[ctx_row:ctx_row + 1, 0:D_MODEL]
    csc = mods_ref[ctx_row:ctx_row + 1, D_MODEL:2 * D_MODEL]
    hb = (_rms(x, nm_ref[...]) * (1.0 + csc) + csh).astype(BF16)
    zf = _dot(hb, wf_ref[...])
    zb = _dot(hb, wb_ref[...])
    vi = _dot(hb, wi_ref[...]).astype(BF16)
    lb_f, lb_b = _lower_bounds(lbl_ref)
    f_f = lb_f + (1.0 - lb_f) * jax.nn.sigmoid(zf)
    f_b = lb_b + (1.0 - lb_b) * jax.nn.sigmoid(zb)
    ii = lax.broadcasted_iota(I32, (n, n), 0)
    jj = lax.broadcasted_iota(I32, (n, n), 1)
    after = jnp.where(jj > ii, 1.0, 0.0).astype(BF16)
    before = jnp.where(jj < ii, 1.0, 0.0).astype(BF16)
    hi, lo = _split(jnp.log(f_f))
    kd_f = ((1.0 - f_f) * jnp.exp(_dot(after, hi) + _dot(after, lo))).astype(BF16)
    hi, lo = _split(jnp.log(f_b))
    kd_b = ((1.0 - f_b) * jnp.exp(_dot(before, hi) + _dot(before, lo))).astype(BF16)
    for h in range(HG_HEADS):
        ks = slice(h * HG_KDIM, (h + 1) * HG_KDIM)
        vs = slice(h * HG_VDIM, (h + 1) * HG_VDIM)
        sf_ref[0, h] = _dot_tn(vi[:, vs], kd_f[:, ks])
        sb_ref[0, h] = _dot_tn(vi[:, vs], kd_b[:, ks])


def _ctx_states(ctx, mods, nm, w_in_bf, lbl, ctx_row):
    nb, n, _ = ctx.shape
    st = jax.ShapeDtypeStruct((nb, HG_HEADS, HG_VDIM, HG_KDIM), F32)
    st_spec = pl.BlockSpec((1, HG_HEADS, HG_VDIM, HG_KDIM), lambda b: (b, 0, 0, 0))
    wcol = lambda k: pl.BlockSpec((D_MODEL, HG_KEYS), lambda b, k=k: (0, k))
    return pl.pallas_call(
        functools.partial(_ctx_body, ctx_row=ctx_row),
        grid=(nb,),
        in_specs=[pl.BlockSpec((1, n, D_MODEL), lambda b: (b, 0, 0)),
                  pl.BlockSpec(mods.shape, lambda b: (0, 0)),
                  pl.BlockSpec((1, D_MODEL), lambda b: (0, 0)),
                  wcol(1), wcol(2), wcol(3),
                  pl.BlockSpec(lbl.shape, lambda b: (0, 0, 0))],
        out_specs=[st_spec, st_spec],
        out_shape=[st, st],
        compiler_params=pltpu.CompilerParams(vmem_limit_bytes=VMEM_LIMIT),
        name="ctx_states",
    )(ctx, mods, nm, w_in_bf, w_in_bf, w_in_bf, lbl)


def _tile_positions(reverse):
    ii = lax.broadcasted_iota(I32, (TT, TT), 0)
    jj = lax.broadcasted_iota(I32, (TT, TT), 1)
    if reverse:
        ii = TT - 1 - ii
        jj = TT - 1 - jj
    sh0 = CHUNK.bit_length() - 1
    return ii, jj, ii >> sh0, jj >> sh0


def _init_maps(lm_ref, tri_ref, reverse):
    lm_ref[...] = _level_map(reverse)
    ii, jj, ci, cj = _tile_positions(reverse)
    tri_ref[...] = jnp.where((ci == cj) & (jj <= ii), 1.0, 0.0).astype(BF16)


def _level_map(reverse):
    ii, jj, ci, cj = _tile_positions(reverse)
    lm = jnp.where(((ci >> 1) == (cj >> 1)) & (jj <= ii), 1, 0)
    for lvl in range(2, NCH.bit_length()):
        m = ((ci >> lvl) == (cj >> lvl)) & (((ci >> (lvl - 1)) & 1) == 1) & (((cj >> (lvl - 1)) & 1) == 0)
        lm = jnp.where(m, lvl, lm)
    return lm


def _gla_gates(zf, lb, tri):
    f = lb + (1.0 - lb) * jax.nn.sigmoid(zf)
    hi, lo = _split(jnp.log(f))
    return 1.0 - f, _dot(tri, hi) + _dot(tri, lo)


def _gla_tile(q, k, cum, v, st_ref, lm, reverse):
    n_lvl = NCH.bit_length() - 2
    phys = list(range(NCH - 1, -1, -1)) if reverse else list(range(NCH))
    last = 0 if reverse else CHUNK - 1
    tot = [cum[a * CHUNK + last:a * CHUNK + last + 1, :] for a in range(NCH)]
    bnd = [jnp.zeros_like(tot[0])]
    for c in range(NCH):
        bnd.append(bnd[-1] + tot[phys[c]])
    first = [None] * NCH
    for c in range(NCH):
        first[phys[c]] = c % 2 == 0
    ref = jnp.concatenate([cum[a * CHUNK:(a + 1) * CHUNK] - tot[a] if first[a] else cum[a * CHUNK:(a + 1) * CHUNK]
                           for a in range(NCH)], axis=0)
    qp = q * jnp.exp(ref)
    kp = k * jnp.exp(-ref)
    qs = [[None] * NCH for _ in range(n_lvl + 1)]
    ks = [[None] * NCH for _ in range(n_lvl + 1)]
    for c in range(NCH):
        a = phys[c]
        rows = slice(a * CHUNK, (a + 1) * CHUNK)
        to_start = tot[a] if first[a] else 0.0
        to_end = 0.0 if first[a] else tot[a]
        zero = jnp.zeros_like(qp[rows])
        for lvl in range(1, n_lvl + 1):
            s = 1 << lvl
            mid = (c // (2 * s)) * 2 * s + s
            right = (c // s) % 2 == 1
            qs[lvl - 1][a] = qp[rows] * jnp.exp(bnd[c] - bnd[mid] + to_start) if right else zero
            ks[lvl - 1][a] = zero if right else kp[rows] * jnp.exp(bnd[mid] - bnd[c + 1] + to_end)
        qs[n_lvl][a] = qp[rows] * jnp.exp(bnd[c] + to_start)
        ks[n_lvl][a] = kp[rows] * jnp.exp(bnd[NCH] - bnd[c + 1] + to_end)
    qs = [jnp.concatenate(p, axis=0).astype(BF16) for p in qs]
    ks = [jnp.concatenate(p, axis=0).astype(BF16) for p in ks]
    qp = qp.astype(BF16)
    kp = kp.astype(BF16)
    vb = v.astype(BF16)
    dec = jnp.exp(bnd[NCH])
    outs = []
    for h in range(HG_HEADS):
        kk = slice(h * HG_KDIM, (h + 1) * HG_KDIM)
        vv = slice(h * HG_VDIM, (h + 1) * HG_VDIM)
        sc = jnp.where(lm == 1, _dot_nt(qp[:, kk], kp[:, kk]), 0.0)
        for lvl in range(n_lvl):
            sc = jnp.where(lm == lvl + 2, _dot_nt(qs[lvl][:, kk], ks[lvl][:, kk]), sc)
        st = st_ref[h]
        outs.append(_dot(sc.astype(BF16), vb[:, vv]) + _dot_nt(qs[n_lvl][:, kk], st.astype(BF16)))
        st_ref[h] = st * dec[:, kk] + _dot_tn(vb[:, vv], ks[n_lvl][:, kk])
    return jnp.concatenate(outs, axis=1)


def _mix_fwd_body(x_ref, mods_ref, nm_ref, w_ref, lbl_ref, cw_ref, s0_ref, wa_ref, wb_ref,
                  pk_ref, fz_ref, wa_bf_ref, wb_bf_ref, st_ref, lm_ref, tri_ref):
    b = pl.program_id(0)
    j = pl.program_id(1)

    @pl.when((b == 0) & (j == 0))
    def _():
        _init_maps(lm_ref, tri_ref, False)

    @pl.when(j == 0)
    def _():
        st_ref[...] = s0_ref[0]

    sh = mods_ref[pl.ds(b, 1), 0:D_MODEL]
    sc = mods_ref[pl.ds(b, 1), D_MODEL:2 * D_MODEL]
    w = HG_KEYS
    lb_f, _ = _lower_bounds(lbl_ref)
    cw = cw_ref[...]
    col = lax.broadcasted_iota(I32, (TT, 1), 0) & (GRID_W - 1)

    def cast_column_tiles(src_ref, dst_ref):
        tile = dst_ref.shape[3]
        for f in range(dst_ref.shape[1]):
            dst_ref[0, f] = src_ref[0, :, f * tile:(f + 1) * tile].astype(BF16)

    for u in range(STEP_TILES):
        rows = slice(u * TT, (u + 1) * TT)
        hx = (_rms(x_ref[0, rows], nm_ref[...]) * (1.0 + sc) + sh).astype(BF16)
        proj = lambda k: _dot(hx, w_ref[:, k * w:(k + 1) * w])
        gate_b = proj(5)
        if u == 0:
            cast_column_tiles(wa_ref, wa_bf_ref)
        u_conv = proj(6) * proj(7)
        if u == 0:
            cast_column_tiles(wb_ref, wb_bf_ref)
        zff = proj(1)
        up = jnp.where(col != 0, pltpu.roll(u_conv, 1, axis=0), 0.0)
        un = jnp.where(col != GRID_W - 1, pltpu.roll(u_conv, TT - 1, axis=0), 0.0)
        pk_ref[0, rows, 3 * w:4 * w] = (gate_b * (cw[0:1] * up + cw[1:2] * u_conv + cw[2:3] * un)).astype(BF16)
        q = proj(0)
        k, cum = _gla_gates(zff, lb_f, tri_ref[...])
        vi = proj(3)
        fz_ref[0, rows, 0:w] = proj(2)
        pk_ref[0, rows, 2 * w:3 * w] = proj(4).astype(BF16)
        fz_ref[0, rows, w:2 * w] = _gla_tile(q, k, cum, vi, st_ref, lm_ref[...], False)
        pk_ref[0, rows, 0:w] = q.astype(BF16)
        pk_ref[0, rows, w:2 * w] = vi.astype(BF16)


def _mix_fwd(x, mods, nm, w_in_bf, lbl, cw, s0_f, wa, wb, col_tile):
    nb, t, _ = x.shape
    nt = t // STEP_TOK
    ne, rows, cols = wa.shape
    per_expert = nb * nt // ne
    step_rows = rows // per_expert
    tok = lambda n, dt: jax.ShapeDtypeStruct((nb, t, n * HG_WIDTH), dt)
    tspec = lambda n: pl.BlockSpec((1, STEP_TOK, n * HG_WIDTH), lambda b, j: (b, j, 0))
    cast_in = pl.BlockSpec((1, step_rows, cols), lambda b, j: ((b * nt + j) // per_expert, (b * nt + j) % per_expert, 0))
    cast_out = pl.BlockSpec((1, cols // col_tile, step_rows, col_tile),
                            lambda b, j: ((b * nt + j) // per_expert, 0, (b * nt + j) % per_expert, 0))
    cast_shape = jax.ShapeDtypeStruct((ne, cols // col_tile, rows, col_tile), BF16)
    return pl.pallas_call(
        _mix_fwd_body,
        grid=(nb, nt),
        in_specs=[pl.BlockSpec((1, STEP_TOK, D_MODEL), lambda b, j: (b, j, 0)),
                  pl.BlockSpec(mods.shape, lambda b, j: (0, 0)),
                  pl.BlockSpec((1, D_MODEL), lambda b, j: (0, 0)),
                  pl.BlockSpec(w_in_bf.shape, lambda b, j: (0, 0)),
                  pl.BlockSpec(lbl.shape, lambda b, j: (0, 0, 0)),
                  pl.BlockSpec(cw.shape, lambda b, j: (0, 0)),
                  pl.BlockSpec((1, HG_HEADS, HG_VDIM, HG_KDIM), lambda b, j: (b, 0, 0, 0)),
                  cast_in, cast_in],
        out_specs=[tspec(4), tspec(2), cast_out, cast_out],
        out_shape=[tok(4, BF16), tok(2, F32), cast_shape, cast_shape],
        scratch_shapes=[pltpu.VMEM((HG_HEADS, HG_VDIM, HG_KDIM), F32), pltpu.VMEM((TT, TT), I32),
                        pltpu.VMEM((TT, TT), BF16)],
        compiler_params=pltpu.CompilerParams(dimension_semantics=("arbitrary", "arbitrary"),
                                             vmem_limit_bytes=VMEM_LIMIT),
        name="mix_fwd",
    )(x, mods, nm, w_in_bf, lbl, cw, s0_f, wa, wb)


def _mix_bwd_body(x_ref, pk_ref, fz_ref, mods_ref, lbl_ref, hgn_ref,
                  wo_ref, nf_ref, wr_ref, s0_ref, wc_ref,
                  x1_ref, hr_ref, at_ref, wc_bf_ref, st_ref, lm_ref, tri_ref):
    b = pl.program_id(0)
    j = pl.program_id(1)
    wc_bf_ref[...] = wc_ref[...].astype(BF16)

    @pl.when((b == 0) & (j == 0))
    def _():
        _init_maps(lm_ref, tri_ref, True)

    @pl.when(j == 0)
    def _():
        st_ref[...] = s0_ref[0]

    _, lb_b = _lower_bounds(lbl_ref)
    w = HG_WIDTH
    hgn = hgn_ref[...]
    mod = lambda k: mods_ref[pl.ds(b, 1), k * D_MODEL:(k + 1) * D_MODEL]
    for u in reversed(range(STEP_TILES)):
        rows = slice(u * TT, (u + 1) * TT)
        k, cum = _gla_gates(fz_ref[0, rows, 0:w], lb_b, tri_ref[...])
        o_b = _gla_tile(pk_ref[0, rows, 0:w].astype(F32), k, cum, pk_ref[0, rows, w:2 * w], st_ref, lm_ref[...], True)
        o = fz_ref[0, rows, w:2 * w] + o_b
        heads = [_rms(o[:, h * HG_VDIM:(h + 1) * HG_VDIM], hgn) for h in range(HG_HEADS)]
        g = pk_ref[0, rows, 2 * w:3 * w].astype(F32)
        ohg = jnp.concatenate(heads, axis=1) * _silu(g)
        cat = jnp.concatenate([ohg.astype(BF16), pk_ref[0, rows, 3 * w:4 * w]], axis=1)
        mx = _dot(cat, wo_ref[...])
        x1 = x_ref[0, rows] + mod(2) * mx
        h2 = _rms(x1, nf_ref[...]) * (1.0 + mod(4)) + mod(3)
        for c in range(ACC_ROWS):
            dst = pl.ds(u * TT * ACC_ROWS + c, TT, stride=ACC_ROWS)
            x1_ref[0, dst, :] = x1[:, c * LANES:(c + 1) * LANES]
            hr_ref[0, dst, :] = h2[:, c * LANES:(c + 1) * LANES]
        r = _dot(jnp.concatenate(_split(h2), axis=0), wr_ref[...])
        lg = r[:TT, :LANES] + r[:TT, LANES:] + r[TT:, :LANES]
        lgt = lg.T[0:N_EXPERTS, :]
        e = jnp.exp(lgt - jnp.max(lgt, axis=0, keepdims=True))
        at_ref[0, :, rows] = e / jnp.sum(e, axis=0, keepdims=True)


def _mix_bwd(x, pk, fz, mods, lbl, hgn, w_out_bf, nf, wr_pad, s0_b, wc):
    nb, t, _ = x.shape
    nt = t // STEP_TOK
    cast_spec = pl.BlockSpec((1,) + wc.shape[1:], lambda b, j: (b * nt + j, 0, 0))
    rev = lambda b, j: (b, nt - 1 - j, 0)
    tspec = lambda a: pl.BlockSpec((1, STEP_TOK, a.shape[2]), rev)
    xspec = pl.BlockSpec((1, STEP_TOK, D_MODEL), rev)
    full2 = lambda a: pl.BlockSpec(a.shape, lambda b, j: (0, 0))
    return pl.pallas_call(
        _mix_bwd_body,
        grid=(nb, nt),
        in_specs=[xspec, tspec(pk), tspec(fz),
                  full2(mods), pl.BlockSpec(lbl.shape, lambda b, j: (0, 0, 0)), full2(hgn),
                  full2(w_out_bf), full2(nf), full2(wr_pad),
                  pl.BlockSpec((1, HG_HEADS, HG_VDIM, HG_KDIM), lambda b, j: (b, 0, 0, 0)),
                  cast_spec],
        out_specs=[pl.BlockSpec((1, STEP_TOK * ACC_ROWS, LANES), rev),
                   pl.BlockSpec((1, STEP_TOK * ACC_ROWS, LANES), rev),
                   pl.BlockSpec((1, N_EXPERTS, STEP_TOK), lambda b, j: (b, 0, nt - 1 - j)),
                   cast_spec],
        out_shape=[jax.ShapeDtypeStruct((nb, t * ACC_ROWS, LANES), F32),
                   jax.ShapeDtypeStruct((nb, t * ACC_ROWS, LANES), F32),
                   jax.ShapeDtypeStruct((nb, N_EXPERTS, t), F32),
                   jax.ShapeDtypeStruct(wc.shape, BF16)],
        scratch_shapes=[pltpu.VMEM((HG_HEADS, HG_VDIM, HG_KDIM), F32), pltpu.VMEM((TT, TT), I32),
                        pltpu.VMEM((TT, TT), BF16)],
        compiler_params=pltpu.CompilerParams(dimension_semantics=("arbitrary", "arbitrary"),
                                             vmem_limit_bytes=VMEM_LIMIT),
        name="mix_bwd",
    )(x, pk, fz, mods, lbl, hgn, w_out_bf, nf, wr_pad, s0_b, wc)


def _route_body(at_ref, idx_ref, gate_ref, *, cap):
    ne, t = at_ref.shape[1], at_ref.shape[2]
    blk = 2 * LANES
    aff = at_ref[0]

    def count(m):
        return jnp.sum(jnp.where(m, 1.0, 0.0), axis=1, keepdims=True)

    def enough(cand):
        return count(aff >= lax.bitcast_convert_type(cand, F32)) >= cap

    def bit_step(i, thr):
        hi = jnp.left_shift(jnp.int32(1), 30 - 2 * i)
        lo = jnp.right_shift(hi, 1)
        both, upper, lower = thr | hi | lo, thr | hi, thr | lo
        return jnp.where(enough(both), both, jnp.where(enough(upper), upper, jnp.where(enough(lower), lower, thr)))

    thr = lax.fori_loop(0, 16, bit_step, jnp.zeros((ne, 1), I32))
    gt = aff >= lax.bitcast_convert_type(thr + 1, F32)
    eq = (aff >= lax.bitcast_convert_type(thr, F32)) & jnp.logical_not(gt)
    need = cap - count(gt)

    ii = lax.broadcasted_iota(I32, (blk, blk), 0)
    jj = lax.broadcasted_iota(I32, (blk, blk), 1)
    upper = jnp.where(ii <= jj, 1.0, 0.0).astype(BF16)

    def prefix_blocks(m):
        mb = jnp.where(m, 1.0, 0.0).astype(BF16)
        carry = jnp.zeros((ne, 1), F32)
        out = []
        for kb in range(t // blk):
            p = _dot(mb[:, kb * blk:(kb + 1) * blk], upper) + carry
            out.append(p)
            carry = p[:, blk - 1:blk]
        return out

    eq_rank = jnp.concatenate(prefix_blocks(eq), axis=1)
    sel = gt | (eq & (eq_rank <= need))
    rank = jnp.concatenate(prefix_blocks(sel), axis=1).astype(I32)

    tok = lax.broadcasted_iota(I32, (ne, t), 1)
    x = jnp.where(sel, tok, -1)
    d = jnp.where(sel, tok - rank + 1, 0)
    g = aff
    for k in range(t.bit_length() - 1):
        left = lambda a: pltpu.roll(a, t - (1 << k), axis=1)
        ds = left(d)
        take = ((ds >> k) & 1) == 1
        keep = ((d >> k) & 1) == 0
        x = jnp.where(take, left(x), jnp.where(keep, x, -1))
        g = jnp.where(take, left(g), g)
        d = jnp.where(take, ds, jnp.where(keep, d, 0))
    idx_ref[0] = x[:, :cap]
    gate_ref[0] = g[:, :cap]


def _route(aff_t, cap):
    nb, ne, t = aff_t.shape
    spec = pl.BlockSpec((1, ne, cap), lambda b: (b, 0, 0))
    return pl.pallas_call(
        functools.partial(_route_body, cap=cap),
        grid=(nb,),
        in_specs=[pl.BlockSpec((1, ne, t), lambda b: (b, 0, 0))],
        out_specs=[spec, spec],
        out_shape=[jax.ShapeDtypeStruct((nb, ne, cap), I32), jax.ShapeDtypeStruct((nb, ne, cap), F32)],
        compiler_params=pltpu.CompilerParams(vmem_limit_bytes=VMEM_LIMIT),
        name="route",
    )(aff_t)


def _moe_body(idx_ref, gate_ref, hr_hbm, wg_ref, wu_ref, wd_ref, x1_hbm, mods_ref, nfin_ref,
              out_ref, acc_ref, hr_ref, rowg_ref, rows_ref, xs_ref, ys_ref, seed_sem, hr_sem, *, cap, n_ff, n_epi):
    b = pl.program_id(0)
    s = pl.program_id(1)
    n_work = N_EXPERTS * n_ff
    e = s // n_ff
    ff = s % n_ff
    slot = e % 2
    group = SUBLANES

    def list_base(expert):
        return (b * N_EXPERTS + expert) * cap

    def gather(base, j0, n):
        for u in range(n):
            tkn = idx_ref[base + j0 + u]
            rowg_ref[pl.ds(pl.multiple_of((j0 + u) * ACC_ROWS, ACC_ROWS), ACC_ROWS), :] = (
                hr_ref[pl.ds(pl.multiple_of(tkn * ACC_ROWS, ACC_ROWS), ACC_ROWS), :])

    def scatter(base, j0, n):
        for g0 in range(0, n, group):
            dst = []
            val = []
            for u in range(g0, g0 + group):
                r = pl.multiple_of(idx_ref[base + j0 + u] * ACC_ROWS, ACC_ROWS)
                dst.append(r)
                val.append(acc_ref[pl.ds(r, ACC_ROWS), :] + gate_ref[base + j0 + u]
                           * rows_ref[pl.ds(pl.multiple_of((j0 + u) * ACC_ROWS, ACC_ROWS), ACC_ROWS), :])
            for r, v in zip(dst, val):
                acc_ref[pl.ds(r, ACC_ROWS), :] = v

    def in_groups(fn, base):
        def step(jo, carry):
            fn(base, jo * group, group)
            return carry
        lax.fori_loop(0, cap // group, step, 0)

    def lay_out_input(to_slot):
        for c in range(ACC_ROWS):
            xs_ref[to_slot, :, c * LANES:(c + 1) * LANES] = (
                rowg_ref[pl.ds(c, cap, stride=ACC_ROWS), :].astype(BF16))

    def lay_out_output():
        g2 = mods_ref[pl.ds(b, 1), 5 * D_MODEL:6 * D_MODEL]
        for c in range(ACC_ROWS):
            cols = slice(c * LANES, (c + 1) * LANES)
            rows_ref[pl.ds(c, cap, stride=ACC_ROWS), :] = ys_ref[:, cols] * g2[:, cols]

    chunk_rows = EPI_TOK * ACC_ROWS

    def seed_copy(sample, chunk):
        rows = pl.ds(chunk * chunk_rows, chunk_rows)
        return pltpu.make_async_copy(x1_hbm.at[sample, rows], acc_ref.at[rows], seed_sem.at[chunk])

    def expert_tile():
        xs = xs_ref[slot]
        hid = _silu(_dot(xs, wg_ref[0, 0])) * _dot(xs, wu_ref[0, 0])
        ys_ref[...] = _dot(hid.astype(BF16), wd_ref[0]) + jnp.where(ff == 0, 0.0, ys_ref[...])

    def input_copy(sample):
        return pltpu.make_async_copy(hr_hbm.at[sample], hr_ref, hr_sem.at[0])

    @pl.when((s == 0) & (b == 0))
    def _():
        input_copy(0).start()
        for chunk in range(n_epi):
            seed_copy(0, chunk).start()

    @pl.when(s == 0)
    def _():
        input_copy(b).wait()
        ys_ref[...] = jnp.zeros_like(ys_ref)
        in_groups(gather, list_base(0))
        lay_out_input(0)

    @pl.when(s == n_ff - 1)
    def _():
        for chunk in range(n_epi):
            seed_copy(b, chunk).wait()

    @pl.when((s < n_work) & (ff == 0))
    def _():
        gather(list_base(jnp.minimum(e + 1, N_EXPERTS - 1)), 0, cap)
        lay_out_output()
        expert_tile()

    if n_ff > 2:
        @pl.when((s < n_work) & (ff > 0) & (ff < n_ff - 1))
        def _():
            expert_tile()

    @pl.when((s < n_work) & (ff == n_ff - 1))
    def _():
        lay_out_input(1 - slot)
        scatter(list_base(jnp.maximum(e - 1, 0)), 0, cap)
        expert_tile()

    @pl.when(s == n_work)
    def _():
        @pl.when(b + 1 < pl.num_programs(0))
        def _():
            input_copy(b + 1).start()

        lay_out_output()
        in_groups(scatter, list_base(N_EXPERTS - 1))

    @pl.when(s >= n_work)
    def _():
        chunk = s - n_work
        row0 = chunk * chunk_rows
        tot = jnp.concatenate([acc_ref[pl.ds(row0 + c, EPI_TOK, stride=ACC_ROWS), :] for c in range(ACC_ROWS)], axis=1)
        out_ref[0] = _rms(tot, nfin_ref[...])

        @pl.when(b + 1 < pl.num_programs(0))
        def _():
            seed_copy(b + 1, chunk).start()


def _moe(idx_flat, gate_flat, hr, wg, wu, wd, x1, mods, nfin, cap):
    nb, t = x1.shape[0], x1.shape[1] // ACC_ROWS
    n_ff = wg.shape[1]
    n_work = N_EXPERTS * n_ff
    n_epi = t // EPI_TOK

    def w_idx(s):
        sc = jnp.minimum(s, n_work - 1)
        return sc // n_ff, sc % n_ff

    def epi(b, s, *_):
        return (b, jnp.maximum(s - n_work, 0), 0)

    grid_spec = pltpu.PrefetchScalarGridSpec(
        num_scalar_prefetch=2,
        grid=(nb, n_work + n_epi),
        in_specs=[pl.BlockSpec(memory_space=pl.ANY),
                  pl.BlockSpec((1, 1, D_MODEL, DFF_TILE), lambda b, s, *_: w_idx(s) + (0, 0)),
                  pl.BlockSpec((1, 1, D_MODEL, DFF_TILE), lambda b, s, *_: w_idx(s) + (0, 0)),
                  pl.BlockSpec((1, DFF_TILE, D_MODEL), lambda b, s, *_: (w_idx(s)[0], w_idx(s)[1], 0)),
                  pl.BlockSpec(memory_space=pl.ANY),
                  pl.BlockSpec(mods.shape, lambda b, s, *_: (0, 0)),
                  pl.BlockSpec((1, D_MODEL), lambda b, s, *_: (0, 0))],
        out_specs=pl.BlockSpec((1, EPI_TOK, D_MODEL), epi),
        scratch_shapes=[pltpu.VMEM((t * ACC_ROWS, LANES), F32),
                        pltpu.VMEM((t * ACC_ROWS, LANES), F32),
                        pltpu.VMEM((cap * ACC_ROWS, LANES), F32),
                        pltpu.VMEM((cap * ACC_ROWS, LANES), F32),
                        pltpu.VMEM((2, cap, D_MODEL), BF16),
                        pltpu.VMEM((cap, D_MODEL), F32),
                        pltpu.SemaphoreType.DMA((n_epi,)),
                        pltpu.SemaphoreType.DMA((1,))])
    return pl.pallas_call(
        functools.partial(_moe_body, cap=cap, n_ff=n_ff, n_epi=n_epi),
        grid_spec=grid_spec,
        out_shape=jax.ShapeDtypeStruct((nb, t, D_MODEL), F32),
        compiler_params=pltpu.CompilerParams(dimension_semantics=("arbitrary", "arbitrary"),
                                             vmem_limit_bytes=VMEM_LIMIT),
        name="moe",
    )(idx_flat, gate_flat, hr, wg, wu, wd, x1, mods, nfin)


def kernel(x, c, ctx, c_ctx, w_ada, b_ada, norm_mix, norm_ffn, w_in, lb_logits, hg_norm, conv_w,
           w_out, w_router, w_gate, w_up, w_down, norm_final):
    nb, t, d = x.shape
    assert d == D_MODEL and w_ada.shape[0] == 1 and nb < MOD_ROWS and t % STEP_TOK == 0 and TT % GRID_W == 0
    assert w_in.shape[2] == 3 * HG_KEYS + 2 * HG_WIDTH + 3 * SC_WIDTH and w_router.shape[2] == N_EXPERTS
    cap = EC_CAPACITY * t // N_EXPERTS
    assert cap % LANES == 0 and w_gate.shape[3] % DFF_TILE == 0 and w_gate.shape[3] // DFF_TILE >= 2
    assert (nb * (t // STEP_TOK)) % N_EXPERTS == 0 and (D_MODEL * N_EXPERTS) % (nb * (t // STEP_TOK)) == 0

    cc = jnp.concatenate([c, c_ctx[None, :], jnp.zeros((MOD_ROWS - nb - 1, d), F32)], axis=0)
    w_in_bf = w_in[0].astype(BF16)
    w_out_bf = w_out[0].astype(BF16)
    wr_hi, wr_lo = _split(jnp.pad(w_router[0], ((0, 0), (0, LANES - N_EXPERTS))))
    wr_pad = jnp.concatenate([wr_hi, wr_lo], axis=1)
    nm, nf, nfin = norm_mix[0][None, :], norm_ffn[0][None, :], norm_final[None, :]
    hgn = hg_norm[0][None, :]

    mods = _adaln(cc, w_ada[0], b_ada[0][None, :])
    s0_f, s0_b = _ctx_states(ctx, mods, nm, w_in_bf, lb_logits, ctx_row=nb)
    steps = nb * (t // STEP_TOK)
    sliced = lambda w: w.reshape((steps, w.shape[0] * w.shape[1] // steps) + w.shape[2:])
    pk, fz, wg, wu = _mix_fwd(x, mods, nm, w_in_bf, lb_logits, conv_w[0], s0_f, w_gate[0], w_up[0], DFF_TILE)
    x1, hr, aff_t, wd = _mix_bwd(x, pk, fz, mods, lb_logits, hgn, w_out_bf, nf, wr_pad, s0_b, sliced(w_down[0]))
    wd = wd.reshape(w_down[0].shape)
    idx, gates = _route(aff_t, cap)
    return _moe(idx.reshape(-1), gates.reshape(-1), hr, wg, wu, wd, x1, mods, nfin, cap)
```

```python
import functools

import jax
import jax.numpy as jnp
from jax import lax
from jax.experimental import pallas as pl
from jax.experimental.pallas import tpu as pltpu

F32 = jnp.float32
BF16 = jnp.bfloat16
I32 = jnp.int32

D_MODEL = 1024
N_MOD = 6
HG_HEADS = 4
HG_KDIM = 128
HG_VDIM = 128
HG_KEYS = HG_HEADS * HG_KDIM
HG_WIDTH = HG_HEADS * HG_VDIM
SC_WIDTH = D_MODEL - HG_WIDTH
GRID_W = 64
CHUNK = 32
N_EXPERTS = 16
EC_CAPACITY = 2
EPS = 1e-6

LANES = 128
SUBLANES = 8
VMEM_LIMIT = 62 * 1024 * 1024

TT = 256
STEP_TILES = 2
STEP_TOK = TT * STEP_TILES
NCH = TT // CHUNK
MOD_ROWS = 16
DFF_TILE = 1024
EPI_TOK = 512
ACC_ROWS = D_MODEL // LANES


def _dot(a, b):
    return jnp.dot(a, b, preferred_element_type=F32)


def _dot_nt(a, b):
    return lax.dot_general(a, b, (((1,), (1,)), ((), ())), preferred_element_type=F32)


def _dot_tn(a, b):
    return lax.dot_general(a, b, (((0,), (0,)), ((), ())), preferred_element_type=F32)


def _split(x):
    hi = x.astype(BF16)
    lo = (x - hi.astype(F32)).astype(BF16)
    return hi, lo


def _rms(x, gain):
    return x * lax.rsqrt(jnp.mean(x * x, axis=-1, keepdims=True) + EPS) * gain


def _silu(x):
    return x * jax.nn.sigmoid(x)


def _lower_bounds(lbl_ref):
    rows = [lbl_ref[l] for l in range(lbl_ref.shape[0])]
    m = functools.reduce(jnp.maximum, rows)
    es = [jnp.exp(r - m) for r in rows]
    sm0 = es[0] / functools.reduce(lambda a, b: a + b, es)
    return sm0[0:1, :], sm0[1:2, :]


def _ada_body(c_ref, w_ref, b_ref, o_ref):
    s_hi, s_lo = _split(_silu(c_ref[...]))
    w_hi, w_lo = _split(w_ref[...])
    o_ref[...] = _dot(s_hi, w_hi) + _dot(s_hi, w_lo) + _dot(s_lo, w_hi) + b_ref[...]


def _adaln(cc, w, b):
    n = w.shape[1]
    tn = n // 4
    return pl.pallas_call(
        _ada_body,
        grid=(n // tn,),
        in_specs=[pl.BlockSpec((MOD_ROWS, D_MODEL), lambda i: (0, 0)),
                  pl.BlockSpec((D_MODEL, tn), lambda i: (0, i)),
                  pl.BlockSpec((1, tn), lambda i: (0, i))],
        out_specs=pl.BlockSpec((MOD_ROWS, tn), lambda i: (0, i)),
        out_shape=jax.ShapeDtypeStruct((MOD_ROWS, n), F32),
        compiler_params=pltpu.CompilerParams(vmem_limit_bytes=VMEM_LIMIT),
        name="adaln",
    )(cc, w, b)


def _ctx_body(ctx_ref, mods_ref, nm_ref, wf_ref, wb_ref, wi_ref, lbl_ref, sf_ref, sb_ref, *, ctx_row):
    x = ctx_ref[0]
    n = x.shape[0]
    csh = mods_ref[ctx_row:ctx_row + 1, 0:D_MODEL]
    csc = mods_ref[ctx_row:ctx_row + 1, D_MODEL:2 * D_MODEL]
    hb = (_rms(x, nm_ref[...]) * (1.0 + csc) + csh).astype(BF16)
    zf = _dot(hb, wf_ref[...])
    zb = _dot(hb, wb_ref[...])
    vi = _dot(hb, wi_ref[...]).astype(BF16)
    lb_f, lb_b = _lower_bounds(lbl_ref)
    f_f = lb_f + (1.0 - lb_f) * jax.nn.sigmoid(zf)
    f_b = lb_b + (1.0 - lb_b) * jax.nn.sigmoid(zb)
    ii = lax.broadcasted_iota(I32, (n, n), 0)
    jj = lax.broadcasted_iota(I32, (n, n), 1)
    after = jnp.where(jj > ii, 1.0, 0.0).astype(BF16)
    before = jnp.where(jj < ii, 1.0, 0.0).astype(BF16)
    hi, lo = _split(jnp.log(f_f))
    kd_f = ((1.0 - f_f) * jnp.exp(_dot(after, hi) + _dot(after, lo))).astype(BF16)
    hi, lo = _split(jnp.log(f_b))
    kd_b = ((1.0 - f_b) * jnp.exp(_dot(before, hi) + _dot(before, lo))).astype(BF16)
    for h in range(HG_HEADS):
        ks = slice(h * HG_KDIM, (h + 1) * HG_KDIM)
        vs = slice(h * HG_VDIM, (h + 1) * HG_VDIM)
        sf_ref[0, h] = _dot_tn(vi[:, vs], kd_f[:, ks])
        sb_ref[0, h] = _dot_tn(vi[:, vs], kd_b[:, ks])


def _ctx_states(ctx, mods, nm, w_in_bf, lbl, ctx_row):
    nb, n, _ = ctx.shape
    st = jax.ShapeDtypeStruct((nb, HG_HEADS, HG_VDIM, HG_KDIM), F32)
    st_spec = pl.BlockSpec((1, HG_HEADS, HG_VDIM, HG_KDIM), lambda b: (b, 0, 0, 0))
    wcol = lambda k: pl.BlockSpec((D_MODEL, HG_KEYS), lambda b, k=k: (0, k))
    return pl.pallas_call(
        functools.partial(_ctx_body, ctx_row=ctx_row),
        grid=(nb,),
        in_specs=[pl.BlockSpec((1, n, D_MODEL), lambda b: (b, 0, 0)),
                  pl.BlockSpec(mods.shape, lambda b: (0, 0)),
                  pl.BlockSpec((1, D_MODEL), lambda b: (0, 0)),
                  wcol(1), wcol(2), wcol(3),
                  pl.BlockSpec(lbl.shape, lambda b: (0, 0, 0))],
        out_specs=[st_spec, st_spec],
        out_shape=[st, st],
        compiler_params=pltpu.CompilerParams(vmem_limit_bytes=VMEM_LIMIT),
        name="ctx_states",
    )(ctx, mods, nm, w_in_bf, w_in_bf, w_in_bf, lbl)


def _tile_positions(reverse):
    ii = lax.broadcasted_iota(I32, (TT, TT), 0)
    jj = lax.broadcasted_iota(I32, (TT, TT), 1)
    if reverse:
        ii = TT - 1 - ii
        jj = TT - 1 - jj
    sh0 = CHUNK.bit_length() - 1
    return ii, jj, ii >> sh0, jj >> sh0


def _init_maps(lm_ref, tri_ref, reverse):
    lm_ref[...] = _level_map(reverse)
    ii, jj, ci, cj = _tile_positions(reverse)
    tri_ref[...] = jnp.where((ci == cj) & (jj <= ii), 1.0, 0.0).astype(BF16)


def _level_map(reverse):
    ii, jj, ci, cj = _tile_positions(reverse)
    lm = jnp.where(((ci >> 1) == (cj >> 1)) & (jj <= ii), 1, 0)
    for lvl in range(2, NCH.bit_length()):
        m = ((ci >> lvl) == (cj >> lvl)) & (((ci >> (lvl - 1)) & 1) == 1) & (((cj >> (lvl - 1)) & 1) == 0)
        lm = jnp.where(m, lvl, lm)
    return lm


def _gla_gates(zf, lb, tri):
    f = lb + (1.0 - lb) * jax.nn.sigmoid(zf)
    hi, lo = _split(jnp.log(f))
    return 1.0 - f, _dot(tri, hi) + _dot(tri, lo)


def _gla_tile(q, k, cum, v, st_ref, lm, reverse):
    n_lvl = NCH.bit_length() - 2
    phys = list(range(NCH - 1, -1, -1)) if reverse else list(range(NCH))
    last = 0 if reverse else CHUNK - 1
    tot = [cum[a * CHUNK + last:a * CHUNK + last + 1, :] for a in range(NCH)]
    bnd = [jnp.zeros_like(tot[0])]
    for c in range(NCH):
        bnd.append(bnd[-1] + tot[phys[c]])
    first = [None] * NCH
    for c in range(NCH):
        first[phys[c]] = c % 2 == 0
    ref = jnp.concatenate([cum[a * CHUNK:(a + 1) * CHUNK] - tot[a] if first[a] else cum[a * CHUNK:(a + 1) * CHUNK]
                           for a in range(NCH)], axis=0)
    qp = q * jnp.exp(ref)
    kp = k * jnp.exp(-ref)
    qs = [[None] * NCH for _ in range(n_lvl + 1)]
    ks = [[None] * NCH for _ in range(n_lvl + 1)]
    for c in range(NCH):
        a = phys[c]
        rows = slice(a * CHUNK, (a + 1) * CHUNK)
        to_start = tot[a] if first[a] else 0.0
        to_end = 0.0 if first[a] else tot[a]
        zero = jnp.zeros_like(qp[rows])
        for lvl in range(1, n_lvl + 1):
            s = 1 << lvl
            mid = (c // (2 * s)) * 2 * s + s
            right = (c // s) % 2 == 1
            qs[lvl - 1][a] = qp[rows] * jnp.exp(bnd[c] - bnd[mid] + to_start) if right else zero
            ks[lvl - 1][a] = zero if right else kp[rows] * jnp.exp(bnd[mid] - bnd[c + 1] + to_end)
        qs[n_lvl][a] = qp[rows] * jnp.exp(bnd[c] + to_start)
        ks[n_lvl][a] = kp[rows] * jnp.exp(bnd[NCH] - bnd[c + 1] + to_end)
    qs = [jnp.concatenate(p, axis=0).astype(BF16) for p in qs]
    ks = [jnp.concatenate(p, axis=0).astype(BF16) for p in ks]
    qp = qp.astype(BF16)
    kp = kp.astype(BF16)
    vb = v.astype(BF16)
    dec = jnp.exp(bnd[NCH])
    outs = []
    for h in range(HG_HEADS):
        kk = slice(h * HG_KDIM, (h + 1) * HG_KDIM)
        vv = slice(h * HG_VDIM, (h + 1) * HG_VDIM)
        sc = jnp.where(lm == 1, _dot_nt(qp[:, kk], kp[:, kk]), 0.0)
        for lvl in range(n_lvl):
            sc = jnp.where(lm == lvl + 2, _dot_nt(qs[lvl][:, kk], ks[lvl][:, kk]), sc)
        st = st_ref[h]
        outs.append(_dot(sc.astype(BF16), vb[:, vv]) + _dot_nt(qs[n_lvl][:, kk], st.astype(BF16)))
        st_ref[h] = st * dec[:, kk] + _dot_tn(vb[:, vv], ks[n_lvl][:, kk])
    return jnp.concatenate(outs, axis=1)


def _mix_fwd_body(x_ref, mods_ref, nm_ref, w_ref, lbl_ref, cw_ref, s0_ref, wa_ref, wb_ref,
                  pk_ref, fz_ref, wa_bf_ref, wb_bf_ref, st_ref, lm_ref, tri_ref):
    b = pl.program_id(0)
    j = pl.program_id(1)

    @pl.when((b == 0) & (j == 0))
    def _():
        _init_maps(lm_ref, tri_ref, False)

    @pl.when(j == 0)
    def _():
        st_ref[...] = s0_ref[0]

    sh = mods_ref[pl.ds(b, 1), 0:D_MODEL]
    sc = mods_ref[pl.ds(b, 1), D_MODEL:2 * D_MODEL]
    w = HG_KEYS
    lb_f, _ = _lower_bounds(lbl_ref)
    cw = cw_ref[...]
    col = lax.broadcasted_iota(I32, (TT, 1), 0) & (GRID_W - 1)

    def cast_column_tiles(src_ref, dst_ref):
        tile = dst_ref.shape[3]
        for f in range(dst_ref.shape[1]):
            dst_ref[0, f] = src_ref[0, :, f * tile:(f + 1) * tile].astype(BF16)

    for u in range(STEP_TILES):
        rows = slice(u * TT, (u + 1) * TT)
        hx = (_rms(x_ref[0, rows], nm_ref[...]) * (1.0 + sc) + sh).astype(BF16)
        proj = lambda k: _dot(hx, w_ref[:, k * w:(k + 1) * w])
        gate_b = proj(5)
        if u == 0:
            cast_column_tiles(wa_ref, wa_bf_ref)
        u_conv = proj(6) * proj(7)
        if u == 0:
            cast_column_tiles(wb_ref, wb_bf_ref)
        zff = proj(1)
        up = jnp.where(col != 0, pltpu.roll(u_conv, 1, axis=0), 0.0)
        un = jnp.where(col != GRID_W - 1, pltpu.roll(u_conv, TT - 1, axis=0), 0.0)
        pk_ref[0, rows, 3 * w:4 * w] = (gate_b * (cw[0:1] * up + cw[1:2] * u_conv + cw[2:3] * un)).astype(BF16)
        q = proj(0)
        k, cum = _gla_gates(zff, lb_f, tri_ref[...])
        vi = proj(3)
        fz_ref[0, rows, 0:w] = proj(2)
        pk_ref[0, rows, 2 * w:3 * w] = proj(4).astype(BF16)
        fz_ref[0, rows, w:2 * w] = _gla_tile(q, k, cum, vi, st_ref, lm_ref[...], False)
        pk_ref[0, rows, 0:w] = q.astype(BF16)
        pk_ref[0, rows, w:2 * w] = vi.astype(BF16)


def _mix_fwd(x, mods, nm, w_in_bf, lbl, cw, s0_f, wa, wb, col_tile):
    nb, t, _ = x.shape
    nt = t // STEP_TOK
    ne, rows, cols = wa.shape
    per_expert = nb * nt // ne
    step_rows = rows // per_expert
    tok = lambda n, dt: jax.ShapeDtypeStruct((nb, t, n * HG_WIDTH), dt)
    tspec = lambda n: pl.BlockSpec((1, STEP_TOK, n * HG_WIDTH), lambda b, j: (b, j, 0))
    cast_in = pl.BlockSpec((1, step_rows, cols), lambda b, j: ((b * nt + j) // per_expert, (b * nt + j) % per_expert, 0))
    cast_out = pl.BlockSpec((1, cols // col_tile, step_rows, col_tile),
                            lambda b, j: ((b * nt + j) // per_expert, 0, (b * nt + j) % per_expert, 0))
    cast_shape = jax.ShapeDtypeStruct((ne, cols // col_tile, rows, col_tile), BF16)
    return pl.pallas_call(
        _mix_fwd_body,
        grid=(nb, nt),
        in_specs=[pl.BlockSpec((1, STEP_TOK, D_MODEL), lambda b, j: (b, j, 0)),
                  pl.BlockSpec(mods.shape, lambda b, j: (0, 0)),
                  pl.BlockSpec((1, D_MODEL), lambda b, j: (0, 0)),
                  pl.BlockSpec(w_in_bf.shape, lambda b, j: (0, 0)),
                  pl.BlockSpec(lbl.shape, lambda b, j: (0, 0, 0)),
                  pl.BlockSpec(cw.shape, lambda b, j: (0, 0)),
                  pl.BlockSpec((1, HG_HEADS, HG_VDIM, HG_KDIM), lambda b, j: (b, 0, 0, 0)),
                  cast_in, cast_in],
        out_specs=[tspec(4), tspec(2), cast_out, cast_out],
        out_shape=[tok(4, BF16), tok(2, F32), cast_shape, cast_shape],
        scratch_shapes=[pltpu.VMEM((HG_HEADS, HG_VDIM, HG_KDIM), F32), pltpu.VMEM((TT, TT), I32),
                        pltpu.VMEM((TT, TT), BF16)],
        compiler_params=pltpu.CompilerParams(dimension_semantics=("arbitrary", "arbitrary"),
                                             vmem_limit_bytes=VMEM_LIMIT),
        name="mix_fwd",
    )(x, mods, nm, w_in_bf, lbl, cw, s0_f, wa, wb)


def _mix_bwd_body(x_ref, pk_ref, fz_ref, mods_ref, lbl_ref, hgn_ref,
                  wo_ref, nf_ref, wr_ref, s0_ref, wc_ref,
                  x1_ref, hr_ref, at_ref, wc_bf_ref, st_ref, lm_ref, tri_ref):
    b = pl.program_id(0)
    j = pl.program_id(1)
    wc_bf_ref[...] = wc_ref[...].astype(BF16)

    @pl.when((b == 0) & (j == 0))
    def _():
        _init_maps(lm_ref, tri_ref, True)

    @pl.when(j == 0)
    def _():
        st_ref[...] = s0_ref[0]

    _, lb_b = _lower_bounds(lbl_ref)
    w = HG_WIDTH
    hgn = hgn_ref[...]
    mod = lambda k: mods_ref[pl.ds(b, 1), k * D_MODEL:(k + 1) * D_MODEL]
    scans = {}
    for u in reversed(range(STEP_TILES)):
        rows = slice(u * TT, (u + 1) * TT)
        k, cum = _gla_gates(fz_ref[0, rows, 0:w], lb_b, tri_ref[...])
        scans[u] = _gla_tile(pk_ref[0, rows, 0:w].astype(F32), k, cum, pk_ref[0, rows, w:2 * w], st_ref, lm_ref[...],
                             True)
    for u in reversed(range(STEP_TILES)):
        rows = slice(u * TT, (u + 1) * TT)
        o = fz_ref[0, rows, w:2 * w] + scans[u]
        heads = [_rms(o[:, h * HG_VDIM:(h + 1) * HG_VDIM], hgn) for h in range(HG_HEADS)]
        g = pk_ref[0, rows, 2 * w:3 * w].astype(F32)
        ohg = jnp.concatenate(heads, axis=1) * _silu(g)
        cat = jnp.concatenate([ohg.astype(BF16), pk_ref[0, rows, 3 * w:4 * w]], axis=1)
        mx = _dot(cat, wo_ref[...])
        x1 = x_ref[0, rows] + mod(2) * mx
        h2 = _rms(x1, nf_ref[...]) * (1.0 + mod(4)) + mod(3)
        for c in range(ACC_ROWS):
            dst = pl.ds(u * TT * ACC_ROWS + c, TT, stride=ACC_ROWS)
            x1_ref[0, dst, :] = x1[:, c * LANES:(c + 1) * LANES]
            hr_ref[0, dst, :] = h2[:, c * LANES:(c + 1) * LANES]
        r = _dot(jnp.concatenate(_split(h2), axis=0), wr_ref[...])
        lg = r[:TT, :LANES] + r[:TT, LANES:] + r[TT:, :LANES]
        lgt = lg.T[0:N_EXPERTS, :]
        e = jnp.exp(lgt - jnp.max(lgt, axis=0, keepdims=True))
        at_ref[0, :, rows] = e / jnp.sum(e, axis=0, keepdims=True)


def _mix_bwd(x, pk, fz, mods, lbl, hgn, w_out_bf, nf, wr_pad, s0_b, wc):
    nb, t, _ = x.shape
    nt = t // STEP_TOK
    cast_spec = pl.BlockSpec((1,) + wc.shape[1:], lambda b, j: (b * nt + j, 0, 0))
    rev = lambda b, j: (b, nt - 1 - j, 0)
    tspec = lambda a: pl.BlockSpec((1, STEP_TOK, a.shape[2]), rev)
    xspec = pl.BlockSpec((1, STEP_TOK, D_MODEL), rev)
    full2 = lambda a: pl.BlockSpec(a.shape, lambda b, j: (0, 0))
    return pl.pallas_call(
        _mix_bwd_body,
        grid=(nb, nt),
        in_specs=[xspec, tspec(pk), tspec(fz),
                  full2(mods), pl.BlockSpec(lbl.shape, lambda b, j: (0, 0, 0)), full2(hgn),
                  full2(w_out_bf), full2(nf), full2(wr_pad),
                  pl.BlockSpec((1, HG_HEADS, HG_VDIM, HG_KDIM), lambda b, j: (b, 0, 0, 0)),
                  cast_spec],
        out_specs=[pl.BlockSpec((1, STEP_TOK * ACC_ROWS, LANES), rev),
                   pl.BlockSpec((1, STEP_TOK * ACC_ROWS, LANES), rev),
                   pl.BlockSpec((1, N_EXPERTS, STEP_TOK), lambda b, j: (b, 0, nt - 1 - j)),
                   cast_spec],
        out_shape=[jax.ShapeDtypeStruct((nb, t * ACC_ROWS, LANES), F32),
                   jax.ShapeDtypeStruct((nb, t * ACC_ROWS, LANES), F32),
                   jax.ShapeDtypeStruct((nb, N_EXPERTS, t), F32),
                   jax.ShapeDtypeStruct(wc.shape, BF16)],
        scratch_shapes=[pltpu.VMEM((HG_HEADS, HG_VDIM, HG_KDIM), F32), pltpu.VMEM((TT, TT), I32),
                        pltpu.VMEM((TT, TT), BF16)],
        compiler_params=pltpu.CompilerParams(dimension_semantics=("arbitrary", "arbitrary"),
                                             vmem_limit_bytes=VMEM_LIMIT),
        name="mix_bwd",
    )(x, pk, fz, mods, lbl, hgn, w_out_bf, nf, wr_pad, s0_b, wc)


def _route_body(at_ref, idx_ref, gate_ref, *, cap):
    ne, t = at_ref.shape[1], at_ref.shape[2]
    blk = 2 * LANES
    aff = at_ref[0]

    def count(m):
        return jnp.sum(jnp.where(m, 1.0, 0.0), axis=1, keepdims=True)

    def enough(cand):
        return count(aff >= lax.bitcast_convert_type(cand, F32)) >= cap

    def bit_step(i, thr):
        hi = jnp.left_shift(jnp.int32(1), 30 - 2 * i)
        lo = jnp.right_shift(hi, 1)
        both, upper, lower = thr | hi | lo, thr | hi, thr | lo
        return jnp.where(enough(both), both, jnp.where(enough(upper), upper, jnp.where(enough(lower), lower, thr)))

    thr = lax.fori_loop(0, 16, bit_step, jnp.zeros((ne, 1), I32))
    gt = aff >= lax.bitcast_convert_type(thr + 1, F32)
    eq = (aff >= lax.bitcast_convert_type(thr, F32)) & jnp.logical_not(gt)
    need = cap - count(gt)

    ii = lax.broadcasted_iota(I32, (blk, blk), 0)
    jj = lax.broadcasted_iota(I32, (blk, blk), 1)
    upper = jnp.where(ii <= jj, 1.0, 0.0).astype(BF16)

    def prefix_blocks(m):
        mb = jnp.where(m, 1.0, 0.0).astype(BF16)
        carry = jnp.zeros((ne, 1), F32)
        out = []
        for kb in range(t // blk):
            p = _dot(mb[:, kb * blk:(kb + 1) * blk], upper) + carry
            out.append(p)
            carry = p[:, blk - 1:blk]
        return out

    eq_rank = jnp.concatenate(prefix_blocks(eq), axis=1)
    sel = gt | (eq & (eq_rank <= need))
    rank = jnp.concatenate(prefix_blocks(sel), axis=1).astype(I32)

    tok = lax.broadcasted_iota(I32, (ne, t), 1)
    x = jnp.where(sel, tok, -1)
    d = jnp.where(sel, tok - rank + 1, 0)
    g = aff
    for k in range(t.bit_length() - 1):
        left = lambda a: pltpu.roll(a, t - (1 << k), axis=1)
        ds = left(d)
        take = ((ds >> k) & 1) == 1
        keep = ((d >> k) & 1) == 0
        x = jnp.where(take, left(x), jnp.where(keep, x, -1))
        g = jnp.where(take, left(g), g)
        d = jnp.where(take, ds, jnp.where(keep, d, 0))
    idx_ref[0] = x[:, :cap]
    gate_ref[0] = g[:, :cap]


def _route(aff_t, cap):
    nb, ne, t = aff_t.shape
    spec = pl.BlockSpec((1, ne, cap), lambda b: (b, 0, 0))
    return pl.pallas_call(
        functools.partial(_route_body, cap=cap),
        grid=(nb,),
        in_specs=[pl.BlockSpec((1, ne, t), lambda b: (b, 0, 0))],
        out_specs=[spec, spec],
        out_shape=[jax.ShapeDtypeStruct((nb, ne, cap), I32), jax.ShapeDtypeStruct((nb, ne, cap), F32)],
        compiler_params=pltpu.CompilerParams(vmem_limit_bytes=VMEM_LIMIT),
        name="route",
    )(aff_t)


def _moe_body(idx_ref, gate_ref, hr_hbm, wg_ref, wu_ref, wd_ref, x1_hbm, mods_ref, nfin_ref,
              out_ref, acc_ref, hr_ref, rowg_ref, rows_ref, xs_ref, ys_ref, seed_sem, hr_sem, *, cap, n_ff, n_epi):
    b = pl.program_id(0)
    s = pl.program_id(1)
    n_work = N_EXPERTS * n_ff
    e = s // n_ff
    ff = s % n_ff
    slot = e % 2
    group = SUBLANES

    def list_base(expert):
        return (b * N_EXPERTS + expert) * cap

    def gather(base, j0, n):
        for u in range(n):
            tkn = idx_ref[base + j0 + u]
            rowg_ref[pl.ds(pl.multiple_of((j0 + u) * ACC_ROWS, ACC_ROWS), ACC_ROWS), :] = (
                hr_ref[pl.ds(pl.multiple_of(tkn * ACC_ROWS, ACC_ROWS), ACC_ROWS), :])

    def scatter(base, j0, n):
        for g0 in range(0, n, group):
            dst = []
            val = []
            for u in range(g0, g0 + group):
                r = pl.multiple_of(idx_ref[base + j0 + u] * ACC_ROWS, ACC_ROWS)
                dst.append(r)
                val.append(acc_ref[pl.ds(r, ACC_ROWS), :] + gate_ref[base + j0 + u]
                           * rows_ref[pl.ds(pl.multiple_of((j0 + u) * ACC_ROWS, ACC_ROWS), ACC_ROWS), :])
            for r, v in zip(dst, val):
                acc_ref[pl.ds(r, ACC_ROWS), :] = v

    def in_groups(fn, base):
        def step(jo, carry):
            fn(base, jo * group, group)
            return carry
        lax.fori_loop(0, cap // group, step, 0)

    def lay_out_input(to_slot):
        for c in range(ACC_ROWS):
            xs_ref[to_slot, :, c * LANES:(c + 1) * LANES] = (
                rowg_ref[pl.ds(c, cap, stride=ACC_ROWS), :].astype(BF16))

    def lay_out_output():
        g2 = mods_ref[pl.ds(b, 1), 5 * D_MODEL:6 * D_MODEL]
        for c in range(ACC_ROWS):
            cols = slice(c * LANES, (c + 1) * LANES)
            rows_ref[pl.ds(c, cap, stride=ACC_ROWS), :] = ys_ref[:, cols] * g2[:, cols]

    chunk_rows = EPI_TOK * ACC_ROWS

    def seed_copy(sample, chunk):
        rows = pl.ds(chunk * chunk_rows, chunk_rows)
        return pltpu.make_async_copy(x1_hbm.at[sample, rows], acc_ref.at[rows], seed_sem.at[chunk])

    def expert_tile():
        xs = xs_ref[slot]
        hid = _silu(_dot(xs, wg_ref[0, 0])) * _dot(xs, wu_ref[0, 0])
        ys_ref[...] = _dot(hid.astype(BF16), wd_ref[0]) + jnp.where(ff == 0, 0.0, ys_ref[...])

    def input_copy(sample):
        return pltpu.make_async_copy(hr_hbm.at[sample], hr_ref, hr_sem.at[0])

    @pl.when((s == 0) & (b == 0))
    def _():
        input_copy(0).start()
        for chunk in range(n_epi):
            seed_copy(0, chunk).start()

    @pl.when(s == 0)
    def _():
        input_copy(b).wait()
        ys_ref[...] = jnp.zeros_like(ys_ref)
        in_groups(gather, list_base(0))
        lay_out_input(0)

    @pl.when(s == n_ff - 1)
    def _():
        for chunk in range(n_epi):
            seed_copy(b, chunk).wait()

    @pl.when((s < n_work) & (ff == 0))
    def _():
        gather(list_base(jnp.minimum(e + 1, N_EXPERTS - 1)), 0, cap)
        lay_out_output()
        expert_tile()

    if n_ff > 2:
        @pl.when((s < n_work) & (ff > 0) & (ff < n_ff - 1))
        def _():
            expert_tile()

    @pl.when((s < n_work) & (ff == n_ff - 1))
    def _():
        lay_out_input(1 - slot)
        scatter(list_base(jnp.maximum(e - 1, 0)), 0, cap)
        expert_tile()

    @pl.when(s == n_work)
    def _():
        @pl.when(b + 1 < pl.num_programs(0))
        def _():
            input_copy(b + 1).start()

        lay_out_output()
        in_groups(scatter, list_base(N_EXPERTS - 1))

    @pl.when(s >= n_work)
    def _():
        chunk = s - n_work
        row0 = chunk * chunk_rows
        tot = jnp.concatenate([acc_ref[pl.ds(row0 + c, EPI_TOK, stride=ACC_ROWS), :] for c in range(ACC_ROWS)], axis=1)
        out_ref[0] = _rms(tot, nfin_ref[...])

        @pl.when(b + 1 < pl.num_programs(0))
        def _():
            seed_copy(b + 1, chunk).start()


def _moe(idx_flat, gate_flat, hr, wg, wu, wd, x1, mods, nfin, cap):
    nb, t = x1.shape[0], x1.shape[1] // ACC_ROWS
    n_ff = wg.shape[1]
    n_work = N_EXPERTS * n_ff
    n_epi = t // EPI_TOK

    def w_idx(s):
        sc = jnp.minimum(s, n_work - 1)
        return sc // n_ff, sc % n_ff

    def epi(b, s, *_):
        return (b, jnp.maximum(s - n_work, 0), 0)

    grid_spec = pltpu.PrefetchScalarGridSpec(
        num_scalar_prefetch=2,
        grid=(nb, n_work + n_epi),
        in_specs=[pl.BlockSpec(memory_space=pl.ANY),
                  pl.BlockSpec((1, 1, D_MODEL, DFF_TILE), lambda b, s, *_: w_idx(s) + (0, 0)),
                  pl.BlockSpec((1, 1, D_MODEL, DFF_TILE), lambda b, s, *_: w_idx(s) + (0, 0)),
                  pl.BlockSpec((1, DFF_TILE, D_MODEL), lambda b, s, *_: (w_idx(s)[0], w_idx(s)[1], 0)),
                  pl.BlockSpec(memory_space=pl.ANY),
                  pl.BlockSpec(mods.shape, lambda b, s, *_: (0, 0)),
                  pl.BlockSpec((1, D_MODEL), lambda b, s, *_: (0, 0))],
        out_specs=pl.BlockSpec((1, EPI_TOK, D_MODEL), epi),
        scratch_shapes=[pltpu.VMEM((t * ACC_ROWS, LANES), F32),
                        pltpu.VMEM((t * ACC_ROWS, LANES), F32),
                        pltpu.VMEM((cap * ACC_ROWS, LANES), F32),
                        pltpu.VMEM((cap * ACC_ROWS, LANES), F32),
                        pltpu.VMEM((2, cap, D_MODEL), BF16),
                        pltpu.VMEM((cap, D_MODEL), F32),
                        pltpu.SemaphoreType.DMA((n_epi,)),
                        pltpu.SemaphoreType.DMA((1,))])
    return pl.pallas_call(
        functools.partial(_moe_body, cap=cap, n_ff=n_ff, n_epi=n_epi),
        grid_spec=grid_spec,
        out_shape=jax.ShapeDtypeStruct((nb, t, D_MODEL), F32),
        compiler_params=pltpu.CompilerParams(dimension_semantics=("arbitrary", "arbitrary"),
                                             vmem_limit_bytes=VMEM_LIMIT),
        name="moe",
    )(idx_flat, gate_flat, hr, wg, wu, wd, x1, mods, nfin)


def kernel(x, c, ctx, c_ctx, w_ada, b_ada, norm_mix, norm_ffn, w_in, lb_logits, hg_norm, conv_w,
           w_out, w_router, w_gate, w_up, w_down, norm_final):
    nb, t, d = x.shape
    assert d == D_MODEL and w_ada.shape[0] == 1 and nb < MOD_ROWS and t % STEP_TOK == 0 and TT % GRID_W == 0
    assert w_in.shape[2] == 3 * HG_KEYS + 2 * HG_WIDTH + 3 * SC_WIDTH and w_router.shape[2] == N_EXPERTS
    cap = EC_CAPACITY * t // N_EXPERTS
    assert cap % LANES == 0 and w_gate.shape[3] % DFF_TILE == 0 and w_gate.shape[3] // DFF_TILE >= 2
    assert (nb * (t // STEP_TOK)) % N_EXPERTS == 0 and (D_MODEL * N_EXPERTS) % (nb * (t // STEP_TOK)) == 0

    cc = jnp.concatenate([c, c_ctx[None, :], jnp.zeros((MOD_ROWS - nb - 1, d), F32)], axis=0)
    w_in_bf = w_in[0].astype(BF16)
    w_out_bf = w_out[0].astype(BF16)
    wr_hi, wr_lo = _split(jnp.pad(w_router[0], ((0, 0), (0, LANES - N_EXPERTS))))
    wr_pad = jnp.concatenate([wr_hi, wr_lo], axis=1)
    nm, nf, nfin = norm_mix[0][None, :], norm_ffn[0][None, :], norm_final[None, :]
    hgn = hg_norm[0][None, :]

    mods = _adaln(cc, w_ada[0], b_ada[0][None, :])
    s0_f, s0_b = _ctx_states(ctx, mods, nm, w_in_bf, lb_logits, ctx_row=nb)
    steps = nb * (t // STEP_TOK)
    sliced = lambda w: w.reshape((steps, w.shape[0] * w.shape[1] // steps) + w.shape[2:])
    pk, fz, wg, wu = _mix_fwd(x, mods, nm, w_in_bf, lb_logits, conv_w[0], s0_f, w_gate[0], w_up[0], DFF_TILE)
    x1, hr, aff_t, wd = _mix_bwd(x, pk, fz, mods, lb_logits, hgn, w_out_bf, nf, wr_pad, s0_b, sliced(w_down[0]))
    wd = wd.reshape(w_down[0].shape)
    idx, gates = _route(aff_t, cap)
    return _moe(idx.reshape(-1), gates.reshape(-1), hr, wg, wu, wd, x1, mods, nfin, cap)
```

```python
import functools

import jax
import jax.numpy as jnp
from jax import lax
from jax.experimental import pallas as pl
from jax.experimental.pallas import tpu as pltpu

F32 = jnp.float32
BF16 = jnp.bfloat16
I32 = jnp.int32

D_MODEL = 1024
N_MOD = 6
HG_HEADS = 4
HG_KDIM = 128
HG_VDIM = 128
HG_KEYS = HG_HEADS * HG_KDIM
HG_WIDTH = HG_HEADS * HG_VDIM
SC_WIDTH = D_MODEL - HG_WIDTH
GRID_W = 64
CHUNK = 32
N_EXPERTS = 16
EC_CAPACITY = 2
EPS = 1e-6

LANES = 128
SUBLANES = 8
VMEM_LIMIT = 62 * 1024 * 1024

TT = 256
STEP_TILES = 2
STEP_TOK = TT * STEP_TILES
NCH = TT // CHUNK
MOD_ROWS = 16
DFF_TILE = 1024
EPI_TOK = 512
ACC_ROWS = D_MODEL // LANES


def _dot(a, b):
    return jnp.dot(a, b, preferred_element_type=F32)


def _dot_nt(a, b):
    return lax.dot_general(a, b, (((1,), (1,)), ((), ())), preferred_element_type=F32)


def _dot_tn(a, b):
    return lax.dot_general(a, b, (((0,), (0,)), ((), ())), preferred_element_type=F32)


def _split(x):
    hi = x.astype(BF16)
    lo = (x - hi.astype(F32)).astype(BF16)
    return hi, lo


def _rms(x, gain):
    return x * lax.rsqrt(jnp.mean(x * x, axis=-1, keepdims=True) + EPS) * gain


def _silu(x):
    return x * jax.nn.sigmoid(x)


def _lower_bounds(lbl_ref):
    rows = [lbl_ref[l] for l in range(lbl_ref.shape[0])]
    m = functools.reduce(jnp.maximum, rows)
    es = [jnp.exp(r - m) for r in rows]
    sm0 = es[0] / functools.reduce(lambda a, b: a + b, es)
    return sm0[0:1, :], sm0[1:2, :]


def _ada_body(c_ref, w_ref, b_ref, o_ref):
    s_hi, s_lo = _split(_silu(c_ref[...]))
    w_hi, w_lo = _split(w_ref[...])
    o_ref[...] = _dot(s_hi, w_hi) + _dot(s_hi, w_lo) + _dot(s_lo, w_hi) + b_ref[...]


def _adaln(cc, w, b):
    n = w.shape[1]
    tn = n // 4
    return pl.pallas_call(
        _ada_body,
        grid=(n // tn,),
        in_specs=[pl.BlockSpec((MOD_ROWS, D_MODEL), lambda i: (0, 0)),
                  pl.BlockSpec((D_MODEL, tn), lambda i: (0, i)),
                  pl.BlockSpec((1, tn), lambda i: (0, i))],
        out_specs=pl.BlockSpec((MOD_ROWS, tn), lambda i: (0, i)),
        out_shape=jax.ShapeDtypeStruct((MOD_ROWS, n), F32),
        compiler_params=pltpu.CompilerParams(vmem_limit_bytes=VMEM_LIMIT),
        name="adaln",
    )(cc, w, b)


def _ctx_body(ctx_ref, mods_ref, nm_ref, wf_ref, wb_ref, wi_ref, lbl_ref, sf_ref, sb_ref, *, ctx_row):
    x = ctx_ref[0]
    n = x.shape[0]
    csh = mods_ref[ctx_row:ctx_row + 1, 0:D_MODEL]
    csc = mods_ref[ctx_row:ctx_row + 1, D_MODEL:2 * D_MODEL]
    hb = (_rms(x, nm_ref[...]) * (1.0 + csc) + csh).astype(BF16)
    zf = _dot(hb, wf_ref[...])
    zb = _dot(hb, wb_ref[...])
    vi = _dot(hb, wi_ref[...]).astype(BF16)
    lb_f, lb_b = _lower_bounds(lbl_ref)
    f_f = lb_f + (1.0 - lb_f) * jax.nn.sigmoid(zf)
    f_b = lb_b + (1.0 - lb_b) * jax.nn.sigmoid(zb)
    ii = lax.broadcasted_iota(I32, (n, n), 0)
    jj = lax.broadcasted_iota(I32, (n, n), 1)
    after = jnp.where(jj > ii, 1.0, 0.0).astype(BF16)
    before = jnp.where(jj < ii, 1.0, 0.0).astype(BF16)
    hi, lo = _split(jnp.log(f_f))
    kd_f = ((1.0 - f_f) * jnp.exp(_dot(after, hi) + _dot(after, lo))).astype(BF16)
    hi, lo = _split(jnp.log(f_b))
    kd_b = ((1.0 - f_b) * jnp.exp(_dot(before, hi) + _dot(before, lo))).astype(BF16)
    for h in range(HG_HEADS):
        ks = slice(h * HG_KDIM, (h + 1) * HG_KDIM)
        vs = slice(h * HG_VDIM, (h + 1) * HG_VDIM)
        sf_ref[0, h] = _dot_tn(vi[:, vs], kd_f[:, ks])
        sb_ref[0, h] = _dot_tn(vi[:, vs], kd_b[:, ks])


def _ctx_states(ctx, mods, nm, w_in_bf, lbl, ctx_row):
    nb, n, _ = ctx.shape
    st = jax.ShapeDtypeStruct((nb, HG_HEADS, HG_VDIM, HG_KDIM), F32)
    st_spec = pl.BlockSpec((1, HG_HEADS, HG_VDIM, HG_KDIM), lambda b: (b, 0, 0, 0))
    wcol = lambda k: pl.BlockSpec((D_MODEL, HG_KEYS), lambda b, k=k: (0, k))
    return pl.pallas_call(
        functools.partial(_ctx_body, ctx_row=ctx_row),
        grid=(nb,),
        in_specs=[pl.BlockSpec((1, n, D_MODEL), lambda b: (b, 0, 0)),
                  pl.BlockSpec(mods.shape, lambda b: (0, 0)),
                  pl.BlockSpec((1, D_MODEL), lambda b: (0, 0)),
                  wcol(1), wcol(2), wcol(3),
                  pl.BlockSpec(lbl.shape, lambda b: (0, 0, 0))],
        out_specs=[st_spec, st_spec],
        out_shape=[st, st],
        compiler_params=pltpu.CompilerParams(vmem_limit_bytes=VMEM_LIMIT),
        name="ctx_states",
    )(ctx, mods, nm, w_in_bf, w_in_bf, w_in_bf, lbl)


def _tile_positions(reverse):
    ii = lax.broadcasted_iota(I32, (TT, TT), 0)
    jj = lax.broadcasted_iota(I32, (TT, TT), 1)
    if reverse:
        ii = TT - 1 - ii
        jj = TT - 1 - jj
    sh0 = CHUNK.bit_length() - 1
    return ii, jj, ii >> sh0, jj >> sh0


def _init_maps(lm_ref, tri_ref, reverse):
    lm_ref[...] = _level_map(reverse)
    ii, jj, ci, cj = _tile_positions(reverse)
    tri_ref[...] = jnp.where((ci == cj) & (jj <= ii), 1.0, 0.0).astype(BF16)


def _level_map(reverse):
    ii, jj, ci, cj = _tile_positions(reverse)
    lm = jnp.where(((ci >> 1) == (cj >> 1)) & (jj <= ii), 1, 0)
    for lvl in range(2, NCH.bit_length()):
        m = ((ci >> lvl) == (cj >> lvl)) & (((ci >> (lvl - 1)) & 1) == 1) & (((cj >> (lvl - 1)) & 1) == 0)
        lm = jnp.where(m, lvl, lm)
    return lm


def _gla_gates(zf, lb, tri):
    f = lb + (1.0 - lb) * jax.nn.sigmoid(zf)
    hi, lo = _split(jnp.log(f))
    return 1.0 - f, _dot(tri, hi) + _dot(tri, lo)


def _gla_tile(q, k, cum, v, st_ref, lm, reverse):
    n_lvl = NCH.bit_length() - 2
    phys = list(range(NCH - 1, -1, -1)) if reverse else list(range(NCH))
    last = 0 if reverse else CHUNK - 1
    tot = [cum[a * CHUNK + last:a * CHUNK + last + 1, :] for a in range(NCH)]
    bnd = [jnp.zeros_like(tot[0])]
    for c in range(NCH):
        bnd.append(bnd[-1] + tot[phys[c]])
    first = [None] * NCH
    for c in range(NCH):
        first[phys[c]] = c % 2 == 0
    ref = jnp.concatenate([cum[a * CHUNK:(a + 1) * CHUNK] - tot[a] if first[a] else cum[a * CHUNK:(a + 1) * CHUNK]
                           for a in range(NCH)], axis=0)
    qp = q * jnp.exp(ref)
    kp = k * jnp.exp(-ref)
    qs = [[None] * NCH for _ in range(n_lvl + 1)]
    ks = [[None] * NCH for _ in range(n_lvl + 1)]
    for c in range(NCH):
        a = phys[c]
        rows = slice(a * CHUNK, (a + 1) * CHUNK)
        to_start = tot[a] if first[a] else 0.0
        to_end = 0.0 if first[a] else tot[a]
        zero = jnp.zeros_like(qp[rows])
        for lvl in range(1, n_lvl + 1):
            s = 1 << lvl
            mid = (c // (2 * s)) * 2 * s + s
            right = (c // s) % 2 == 1
            qs[lvl - 1][a] = qp[rows] * jnp.exp(bnd[c] - bnd[mid] + to_start) if right else zero
            ks[lvl - 1][a] = zero if right else kp[rows] * jnp.exp(bnd[mid] - bnd[c + 1] + to_end)
        qs[n_lvl][a] = qp[rows] * jnp.exp(bnd[c] + to_start)
        ks[n_lvl][a] = kp[rows] * jnp.exp(bnd[NCH] - bnd[c + 1] + to_end)
    qs = [jnp.concatenate(p, axis=0).astype(BF16) for p in qs]
    ks = [jnp.concatenate(p, axis=0).astype(BF16) for p in ks]
    qp = qp.astype(BF16)
    kp = kp.astype(BF16)
    vb = v.astype(BF16)
    dec = jnp.exp(bnd[NCH])
    outs = []
    for h in range(HG_HEADS):
        kk = slice(h * HG_KDIM, (h + 1) * HG_KDIM)
        vv = slice(h * HG_VDIM, (h + 1) * HG_VDIM)
        sc = jnp.where(lm == 1, _dot_nt(qp[:, kk], kp[:, kk]), 0.0)
        for lvl in range(n_lvl):
            sc = jnp.where(lm == lvl + 2, _dot_nt(qs[lvl][:, kk], ks[lvl][:, kk]), sc)
        st = st_ref[h]
        outs.append(_dot(sc.astype(BF16), vb[:, vv]) + _dot_nt(qs[n_lvl][:, kk], st.astype(BF16)))
        st_ref[h] = st * dec[:, kk] + _dot_tn(vb[:, vv], ks[n_lvl][:, kk])
    return jnp.concatenate(outs, axis=1)


def _mix_fwd_body(x_ref, mods_ref, nm_ref, w_ref, lbl_ref, cw_ref, s0_ref, wa_ref, wb_ref,
                  pk_ref, fz_ref, wa_bf_ref, wb_bf_ref, st_ref, lm_ref, tri_ref):
    b = pl.program_id(0)
    j = pl.program_id(1)

    @pl.when((b == 0) & (j == 0))
    def _():
        _init_maps(lm_ref, tri_ref, False)

    @pl.when(j == 0)
    def _():
        st_ref[...] = s0_ref[0]

    sh = mods_ref[pl.ds(b, 1), 0:D_MODEL]
    sc = mods_ref[pl.ds(b, 1), D_MODEL:2 * D_MODEL]
    w = HG_KEYS
    lb_f, _ = _lower_bounds(lbl_ref)
    cw = cw_ref[...]
    col = lax.broadcasted_iota(I32, (TT, 1), 0) & (GRID_W - 1)

    def cast_column_tiles(src_ref, dst_ref):
        tile = dst_ref.shape[3]
        for f in range(dst_ref.shape[1]):
            dst_ref[0, f] = src_ref[0, :, f * tile:(f + 1) * tile].astype(BF16)

    for u in range(STEP_TILES):
        rows = slice(u * TT, (u + 1) * TT)
        hx = (_rms(x_ref[0, rows], nm_ref[...]) * (1.0 + sc) + sh).astype(BF16)
        proj = lambda k: _dot(hx, w_ref[:, k * w:(k + 1) * w])
        gate_b = proj(5)
        if u == 0:
            cast_column_tiles(wa_ref, wa_bf_ref)
        u_conv = proj(6) * proj(7)
        if u == 0:
            cast_column_tiles(wb_ref, wb_bf_ref)
        zff = proj(1)
        up = jnp.where(col != 0, pltpu.roll(u_conv, 1, axis=0), 0.0)
        un = jnp.where(col != GRID_W - 1, pltpu.roll(u_conv, TT - 1, axis=0), 0.0)
        pk_ref[0, rows, 3 * w:4 * w] = (gate_b * (cw[0:1] * up + cw[1:2] * u_conv + cw[2:3] * un)).astype(BF16)
        q = proj(0)
        k, cum = _gla_gates(zff, lb_f, tri_ref[...])
        vi = proj(3)
        fz_ref[0, rows, 0:w] = proj(2)
        pk_ref[0, rows, 2 * w:3 * w] = proj(4).astype(BF16)
        fz_ref[0, rows, w:2 * w] = _gla_tile(q, k, cum, vi, st_ref, lm_ref[...], False)
        pk_ref[0, rows, 0:w] = q.astype(BF16)
        pk_ref[0, rows, w:2 * w] = vi.astype(BF16)


def _mix_fwd(x, mods, nm, w_in_bf, lbl, cw, s0_f, wa, wb, col_tile):
    nb, t, _ = x.shape
    nt = t // STEP_TOK
    ne, rows, cols = wa.shape
    per_expert = nb * nt // ne
    step_rows = rows // per_expert
    tok = lambda n, dt: jax.ShapeDtypeStruct((nb, t, n * HG_WIDTH), dt)
    tspec = lambda n: pl.BlockSpec((1, STEP_TOK, n * HG_WIDTH), lambda b, j: (b, j, 0))
    cast_in = pl.BlockSpec((1, step_rows, cols), lambda b, j: ((b * nt + j) // per_expert, (b * nt + j) % per_expert, 0))
    cast_out = pl.BlockSpec((1, cols // col_tile, step_rows, col_tile),
                            lambda b, j: ((b * nt + j) // per_expert, 0, (b * nt + j) % per_expert, 0))
    cast_shape = jax.ShapeDtypeStruct((ne, cols // col_tile, rows, col_tile), BF16)
    return pl.pallas_call(
        _mix_fwd_body,
        grid=(nb, nt),
        in_specs=[pl.BlockSpec((1, STEP_TOK, D_MODEL), lambda b, j: (b, j, 0)),
                  pl.BlockSpec(mods.shape, lambda b, j: (0, 0)),
                  pl.BlockSpec((1, D_MODEL), lambda b, j: (0, 0)),
                  pl.BlockSpec(w_in_bf.shape, lambda b, j: (0, 0)),
                  pl.BlockSpec(lbl.shape, lambda b, j: (0, 0, 0)),
                  pl.BlockSpec(cw.shape, lambda b, j: (0, 0)),
                  pl.BlockSpec((1, HG_HEADS, HG_VDIM, HG_KDIM), lambda b, j: (b, 0, 0, 0)),
                  cast_in, cast_in],
        out_specs=[tspec(4), tspec(2), cast_out, cast_out],
        out_shape=[tok(4, BF16), tok(2, F32), cast_shape, cast_shape],
        scratch_shapes=[pltpu.VMEM((HG_HEADS, HG_VDIM, HG_KDIM), F32), pltpu.VMEM((TT, TT), I32),
                        pltpu.VMEM((TT, TT), BF16)],
        compiler_params=pltpu.CompilerParams(dimension_semantics=("arbitrary", "arbitrary"),
                                             vmem_limit_bytes=VMEM_LIMIT),
        name="mix_fwd",
    )(x, mods, nm, w_in_bf, lbl, cw, s0_f, wa, wb)


def _mix_bwd_body(x_ref, pk_ref, fz_ref, mods_ref, lbl_ref, hgn_ref,
                  wo_ref, nf_ref, wr_ref, s0_ref, wc_ref,
                  x1_ref, hr_ref, at_ref, wc_bf_ref, st_ref, lm_ref, tri_ref):
    b = pl.program_id(0)
    j = pl.program_id(1)
    wc_bf_ref[...] = wc_ref[...].astype(BF16)

    @pl.when((b == 0) & (j == 0))
    def _():
        _init_maps(lm_ref, tri_ref, True)

    @pl.when(j == 0)
    def _():
        st_ref[...] = s0_ref[0]

    _, lb_b = _lower_bounds(lbl_ref)
    w = HG_WIDTH
    hgn = hgn_ref[...]
    mod = lambda k: mods_ref[pl.ds(b, 1), k * D_MODEL:(k + 1) * D_MODEL]
    scans = {}
    for u in reversed(range(STEP_TILES)):
        rows = slice(u * TT, (u + 1) * TT)
        k, cum = _gla_gates(fz_ref[0, rows, 0:w], lb_b, tri_ref[...])
        scans[u] = _gla_tile(pk_ref[0, rows, 0:w].astype(F32), k, cum, pk_ref[0, rows, w:2 * w], st_ref, lm_ref[...],
                             True)
    o = fz_ref[0, :, w:2 * w] + jnp.concatenate([scans[u] for u in range(STEP_TILES)], axis=0)
    heads = [_rms(o[:, h * HG_VDIM:(h + 1) * HG_VDIM], hgn) for h in range(HG_HEADS)]
    g = pk_ref[0, :, 2 * w:3 * w].astype(F32)
    ohg = jnp.concatenate(heads, axis=1) * _silu(g)
    cat = jnp.concatenate([ohg.astype(BF16), pk_ref[0, :, 3 * w:4 * w]], axis=1)
    mx = _dot(cat, wo_ref[...])
    x1 = x_ref[0] + mod(2) * mx
    h2 = _rms(x1, nf_ref[...]) * (1.0 + mod(4)) + mod(3)
    for c in range(ACC_ROWS):
        dst = pl.ds(c, STEP_TOK, stride=ACC_ROWS)
        x1_ref[0, dst, :] = x1[:, c * LANES:(c + 1) * LANES]
        hr_ref[0, dst, :] = h2[:, c * LANES:(c + 1) * LANES]
    r = _dot(jnp.concatenate(_split(h2), axis=0), wr_ref[...])
    lg = r[:STEP_TOK, :LANES] + r[:STEP_TOK, LANES:] + r[STEP_TOK:, :LANES]
    lgt = lg.T[0:N_EXPERTS, :]
    e = jnp.exp(lgt - jnp.max(lgt, axis=0, keepdims=True))
    at_ref[0] = e / jnp.sum(e, axis=0, keepdims=True)


def _mix_bwd(x, pk, fz, mods, lbl, hgn, w_out_bf, nf, wr_pad, s0_b, wc):
    nb, t, _ = x.shape
    nt = t // STEP_TOK
    cast_spec = pl.BlockSpec((1,) + wc.shape[1:], lambda b, j: (b * nt + j, 0, 0))
    rev = lambda b, j: (b, nt - 1 - j, 0)
    tspec = lambda a: pl.BlockSpec((1, STEP_TOK, a.shape[2]), rev)
    xspec = pl.BlockSpec((1, STEP_TOK, D_MODEL), rev)
    full2 = lambda a: pl.BlockSpec(a.shape, lambda b, j: (0, 0))
    return pl.pallas_call(
        _mix_bwd_body,
        grid=(nb, nt),
        in_specs=[xspec, tspec(pk), tspec(fz),
                  full2(mods), pl.BlockSpec(lbl.shape, lambda b, j: (0, 0, 0)), full2(hgn),
                  full2(w_out_bf), full2(nf), full2(wr_pad),
                  pl.BlockSpec((1, HG_HEADS, HG_VDIM, HG_KDIM), lambda b, j: (b, 0, 0, 0)),
                  cast_spec],
        out_specs=[pl.BlockSpec((1, STEP_TOK * ACC_ROWS, LANES), rev),
                   pl.BlockSpec((1, STEP_TOK * ACC_ROWS, LANES), rev),
                   pl.BlockSpec((1, N_EXPERTS, STEP_TOK), lambda b, j: (b, 0, nt - 1 - j)),
                   cast_spec],
        out_shape=[jax.ShapeDtypeStruct((nb, t * ACC_ROWS, LANES), F32),
                   jax.ShapeDtypeStruct((nb, t * ACC_ROWS, LANES), F32),
                   jax.ShapeDtypeStruct((nb, N_EXPERTS, t), F32),
                   jax.ShapeDtypeStruct(wc.shape, BF16)],
        scratch_shapes=[pltpu.VMEM((HG_HEADS, HG_VDIM, HG_KDIM), F32), pltpu.VMEM((TT, TT), I32),
                        pltpu.VMEM((TT, TT), BF16)],
        compiler_params=pltpu.CompilerParams(dimension_semantics=("arbitrary", "arbitrary"),
                                             vmem_limit_bytes=VMEM_LIMIT),
        name="mix_bwd",
    )(x, pk, fz, mods, lbl, hgn, w_out_bf, nf, wr_pad, s0_b, wc)


def _route_body(at_ref, idx_ref, gate_ref, *, cap):
    ne, t = at_ref.shape[1], at_ref.shape[2]
    blk = 2 * LANES
    aff = at_ref[0]

    def count(m):
        return jnp.sum(jnp.where(m, 1.0, 0.0), axis=1, keepdims=True)

    def enough(cand):
        return count(aff >= lax.bitcast_convert_type(cand, F32)) >= cap

    def bit_step(i, thr):
        hi = jnp.left_shift(jnp.int32(1), 30 - 2 * i)
        lo = jnp.right_shift(hi, 1)
        both, upper, lower = thr | hi | lo, thr | hi, thr | lo
        return jnp.where(enough(both), both, jnp.where(enough(upper), upper, jnp.where(enough(lower), lower, thr)))

    thr = lax.fori_loop(0, 16, bit_step, jnp.zeros((ne, 1), I32))
    gt = aff >= lax.bitcast_convert_type(thr + 1, F32)
    eq = (aff >= lax.bitcast_convert_type(thr, F32)) & jnp.logical_not(gt)
    need = cap - count(gt)

    ii = lax.broadcasted_iota(I32, (blk, blk), 0)
    jj = lax.broadcasted_iota(I32, (blk, blk), 1)
    upper = jnp.where(ii <= jj, 1.0, 0.0).astype(BF16)

    def prefix_blocks(m):
        mb = jnp.where(m, 1.0, 0.0).astype(BF16)
        carry = jnp.zeros((ne, 1), F32)
        out = []
        for kb in range(t // blk):
            p = _dot(mb[:, kb * blk:(kb + 1) * blk], upper) + carry
            out.append(p)
            carry = p[:, blk - 1:blk]
        return out

    eq_rank = jnp.concatenate(prefix_blocks(eq), axis=1)
    sel = gt | (eq & (eq_rank <= need))
    rank = jnp.concatenate(prefix_blocks(sel), axis=1).astype(I32)

    tok = lax.broadcasted_iota(I32, (ne, t), 1)
    x = jnp.where(sel, tok, -1)
    d = jnp.where(sel, tok - rank + 1, 0)
    g = aff
    for k in range(t.bit_length() - 1):
        left = lambda a: pltpu.roll(a, t - (1 << k), axis=1)
        ds = left(d)
        take = ((ds >> k) & 1) == 1
        keep = ((d >> k) & 1) == 0
        x = jnp.where(take, left(x), jnp.where(keep, x, -1))
        g = jnp.where(take, left(g), g)
        d = jnp.where(take, ds, jnp.where(keep, d, 0))
    idx_ref[0] = x[:, :cap]
    gate_ref[0] = g[:, :cap]


def _route(aff_t, cap):
    nb, ne, t = aff_t.shape
    spec = pl.BlockSpec((1, ne, cap), lambda b: (b, 0, 0))
    return pl.pallas_call(
        functools.partial(_route_body, cap=cap),
        grid=(nb,),
        in_specs=[pl.BlockSpec((1, ne, t), lambda b: (b, 0, 0))],
        out_specs=[spec, spec],
        out_shape=[jax.ShapeDtypeStruct((nb, ne, cap), I32), jax.ShapeDtypeStruct((nb, ne, cap), F32)],
        compiler_params=pltpu.CompilerParams(vmem_limit_bytes=VMEM_LIMIT),
        name="route",
    )(aff_t)


def _moe_body(idx_ref, gate_ref, hr_hbm, wg_ref, wu_ref, wd_ref, x1_hbm, mods_ref, nfin_ref,
              out_ref, acc_ref, hr_ref, rowg_ref, rows_ref, xs_ref, ys_ref, seed_sem, hr_sem, *, cap, n_ff, n_epi):
    b = pl.program_id(0)
    s = pl.program_id(1)
    n_work = N_EXPERTS * n_ff
    e = s // n_ff
    ff = s % n_ff
    slot = e % 2
    group = SUBLANES

    def list_base(expert):
        return (b * N_EXPERTS + expert) * cap

    def gather(base, j0, n):
        for u in range(n):
            tkn = idx_ref[base + j0 + u]
            rowg_ref[pl.ds(pl.multiple_of((j0 + u) * ACC_ROWS, ACC_ROWS), ACC_ROWS), :] = (
                hr_ref[pl.ds(pl.multiple_of(tkn * ACC_ROWS, ACC_ROWS), ACC_ROWS), :])

    def scatter(base, j0, n):
        for g0 in range(0, n, group):
            dst = []
            val = []
            for u in range(g0, g0 + group):
                r = pl.multiple_of(idx_ref[base + j0 + u] * ACC_ROWS, ACC_ROWS)
                dst.append(r)
                val.append(acc_ref[pl.ds(r, ACC_ROWS), :] + gate_ref[base + j0 + u]
                           * rows_ref[pl.ds(pl.multiple_of((j0 + u) * ACC_ROWS, ACC_ROWS), ACC_ROWS), :])
            for r, v in zip(dst, val):
                acc_ref[pl.ds(r, ACC_ROWS), :] = v

    def in_groups(fn, base):
        def step(jo, carry):
            fn(base, jo * group, group)
            return carry
        lax.fori_loop(0, cap // group, step, 0)

    def lay_out_input(to_slot):
        for c in range(ACC_ROWS):
            xs_ref[to_slot, :, c * LANES:(c + 1) * LANES] = (
                rowg_ref[pl.ds(c, cap, stride=ACC_ROWS), :].astype(BF16))

    def lay_out_output():
        g2 = mods_ref[pl.ds(b, 1), 5 * D_MODEL:6 * D_MODEL]
        for c in range(ACC_ROWS):
            cols = slice(c * LANES, (c + 1) * LANES)
            rows_ref[pl.ds(c, cap, stride=ACC_ROWS), :] = ys_ref[:, cols] * g2[:, cols]

    chunk_rows = EPI_TOK * ACC_ROWS

    def seed_copy(sample, chunk):
        rows = pl.ds(chunk * chunk_rows, chunk_rows)
        return pltpu.make_async_copy(x1_hbm.at[sample, rows], acc_ref.at[rows], seed_sem.at[chunk])

    def expert_tile():
        xs = xs_ref[slot]
        hid = _silu(_dot(xs, wg_ref[0, 0])) * _dot(xs, wu_ref[0, 0])
        ys_ref[...] = _dot(hid.astype(BF16), wd_ref[0]) + jnp.where(ff == 0, 0.0, ys_ref[...])

    def input_copy(sample):
        return pltpu.make_async_copy(hr_hbm.at[sample], hr_ref, hr_sem.at[0])

    @pl.when((s == 0) & (b == 0))
    def _():
        input_copy(0).start()
        for chunk in range(n_epi):
            seed_copy(0, chunk).start()

    @pl.when(s == 0)
    def _():
        input_copy(b).wait()
        ys_ref[...] = jnp.zeros_like(ys_ref)
        in_groups(gather, list_base(0))
        lay_out_input(0)

    @pl.when(s == n_ff - 1)
    def _():
        for chunk in range(n_epi):
            seed_copy(b, chunk).wait()

    @pl.when((s < n_work) & (ff == 0))
    def _():
        gather(list_base(jnp.minimum(e + 1, N_EXPERTS - 1)), 0, cap)
        lay_out_output()
        expert_tile()

    if n_ff > 2:
        @pl.when((s < n_work) & (ff > 0) & (ff < n_ff - 1))
        def _():
            expert_tile()

    @pl.when((s < n_work) & (ff == n_ff - 1))
    def _():
        lay_out_input(1 - slot)
        scatter(list_base(jnp.maximum(e - 1, 0)), 0, cap)
        expert_tile()

    @pl.when(s == n_work)
    def _():
        @pl.when(b + 1 < pl.num_programs(0))
        def _():
            input_copy(b + 1).start()

        lay_out_output()
        in_groups(scatter, list_base(N_EXPERTS - 1))

    @pl.when(s >= n_work)
    def _():
        chunk = s - n_work
        row0 = chunk * chunk_rows
        tot = jnp.concatenate([acc_ref[pl.ds(row0 + c, EPI_TOK, stride=ACC_ROWS), :] for c in range(ACC_ROWS)], axis=1)
        out_ref[0] = _rms(tot, nfin_ref[...])

        @pl.when(b + 1 < pl.num_programs(0))
        def _():
            seed_copy(b + 1, chunk).start()


def _moe(idx_flat, gate_flat, hr, wg, wu, wd, x1, mods, nfin, cap):
    nb, t = x1.shape[0], x1.shape[1] // ACC_ROWS
    n_ff = wg.shape[1]
    n_work = N_EXPERTS * n_ff
    n_epi = t // EPI_TOK

    def w_idx(s):
        sc = jnp.minimum(s, n_work - 1)
        return sc // n_ff, sc % n_ff

    def epi(b, s, *_):
        return (b, jnp.maximum(s - n_work, 0), 0)

    grid_spec = pltpu.PrefetchScalarGridSpec(
        num_scalar_prefetch=2,
        grid=(nb, n_work + n_epi),
        in_specs=[pl.BlockSpec(memory_space=pl.ANY),
                  pl.BlockSpec((1, 1, D_MODEL, DFF_TILE), lambda b, s, *_: w_idx(s) + (0, 0)),
                  pl.BlockSpec((1, 1, D_MODEL, DFF_TILE), lambda b, s, *_: w_idx(s) + (0, 0)),
                  pl.BlockSpec((1, DFF_TILE, D_MODEL), lambda b, s, *_: (w_idx(s)[0], w_idx(s)[1], 0)),
                  pl.BlockSpec(memory_space=pl.ANY),
                  pl.BlockSpec(mods.shape, lambda b, s, *_: (0, 0)),
                  pl.BlockSpec((1, D_MODEL), lambda b, s, *_: (0, 0))],
        out_specs=pl.BlockSpec((1, EPI_TOK, D_MODEL), epi),
        scratch_shapes=[pltpu.VMEM((t * ACC_ROWS, LANES), F32),
                        pltpu.VMEM((t * ACC_ROWS, LANES), F32),
                        pltpu.VMEM((cap * ACC_ROWS, LANES), F32),
                        pltpu.VMEM((cap * ACC_ROWS, LANES), F32),
                        pltpu.VMEM((2, cap, D_MODEL), BF16),
                        pltpu.VMEM((cap, D_MODEL), F32),
                        pltpu.SemaphoreType.DMA((n_epi,)),
                        pltpu.SemaphoreType.DMA((1,))])
    return pl.pallas_call(
        functools.partial(_moe_body, cap=cap, n_ff=n_ff, n_epi=n_epi),
        grid_spec=grid_spec,
        out_shape=jax.ShapeDtypeStruct((nb, t, D_MODEL), F32),
        compiler_params=pltpu.CompilerParams(dimension_semantics=("arbitrary", "arbitrary"),
                                             vmem_limit_bytes=VMEM_LIMIT),
        name="moe",
    )(idx_flat, gate_flat, hr, wg, wu, wd, x1, mods, nfin)


def kernel(x, c, ctx, c_ctx, w_ada, b_ada, norm_mix, norm_ffn, w_in, lb_logits, hg_norm, conv_w,
           w_out, w_router, w_gate, w_up, w_down, norm_final):
    nb, t, d = x.shape
    assert d == D_MODEL and w_ada.shape[0] == 1 and nb < MOD_ROWS and t % STEP_TOK == 0 and TT % GRID_W == 0
    assert w_in.shape[2] == 3 * HG_KEYS + 2 * HG_WIDTH + 3 * SC_WIDTH and w_router.shape[2] == N_EXPERTS
    cap = EC_CAPACITY * t // N_EXPERTS
    assert cap % LANES == 0 and w_gate.shape[3] % DFF_TILE == 0 and w_gate.shape[3] // DFF_TILE >= 2
    assert (nb * (t // STEP_TOK)) % N_EXPERTS == 0 and (D_MODEL * N_EXPERTS) % (nb * (t // STEP_TOK)) == 0

    cc = jnp.concatenate([c, c_ctx[None, :], jnp.zeros((MOD_ROWS - nb - 1, d), F32)], axis=0)
    w_in_bf = w_in[0].astype(BF16)
    w_out_bf = w_out[0].astype(BF16)
    wr_hi, wr_lo = _split(jnp.pad(w_router[0], ((0, 0), (0, LANES - N_EXPERTS))))
    wr_pad = jnp.concatenate([wr_hi, wr_lo], axis=1)
    nm, nf, nfin = norm_mix[0][None, :], norm_ffn[0][None, :], norm_final[None, :]
    hgn = hg_norm[0][None, :]

    mods = _adaln(cc, w_ada[0], b_ada[0][None, :])
    s0_f, s0_b = _ctx_states(ctx, mods, nm, w_in_bf, lb_logits, ctx_row=nb)
    steps = nb * (t // STEP_TOK)
    sliced = lambda w: w.reshape((steps, w.shape[0] * w.shape[1] // steps) + w.shape[2:])
    pk, fz, wg, wu = _mix_fwd(x, mods, nm, w_in_bf, lb_logits, conv_w[0], s0_f, w_gate[0], w_up[0], DFF_TILE)
    x1, hr, aff_t, wd = _mix_bwd(x, pk, fz, mods, lb_logits, hgn, w_out_bf, nf, wr_pad, s0_b, sliced(w_down[0]))
    wd = wd.reshape(w_down[0].shape)
    idx, gates = _route(aff_t, cap)
    return _moe(idx.reshape(-1), gates.reshape(-1), hr, wg, wu, wd, x1, mods, nfin, cap)
```

```python
import functools

import jax
import jax.numpy as jnp
from jax import lax
from jax.experimental import pallas as pl
from jax.experimental.pallas import tpu as pltpu

F32 = jnp.float32
BF16 = jnp.bfloat16
I32 = jnp.int32

D_MODEL = 1024
N_MOD = 6
HG_HEADS = 4
HG_KDIM = 128
HG_VDIM = 128
HG_KEYS = HG_HEADS * HG_KDIM
HG_WIDTH = HG_HEADS * HG_VDIM
SC_WIDTH = D_MODEL - HG_WIDTH
GRID_W = 64
CHUNK = 32
N_EXPERTS = 16
EC_CAPACITY = 2
EPS = 1e-6

LANES = 128
SUBLANES = 8
VMEM_LIMIT = 62 * 1024 * 1024

TT = 256
STEP_TILES = 2
STEP_TOK = TT * STEP_TILES
NCH = TT // CHUNK
MOD_ROWS = 16
DFF_TILE = 1024
EPI_TOK = 512
ACC_ROWS = D_MODEL // LANES


def _dot(a, b):
    return jnp.dot(a, b, preferred_element_type=F32)


def _dot_nt(a, b):
    return lax.dot_general(a, b, (((1,), (1,)), ((), ())), preferred_element_type=F32)


def _dot_tn(a, b):
    return lax.dot_general(a, b, (((0,), (0,)), ((), ())), preferred_element_type=F32)


def _split(x):
    hi = x.astype(BF16)
    lo = (x - hi.astype(F32)).astype(BF16)
    return hi, lo


def _rms(x, gain):
    return x * lax.rsqrt(jnp.mean(x * x, axis=-1, keepdims=True) + EPS) * gain


def _silu(x):
    return x * jax.nn.sigmoid(x)


def _lower_bounds(lbl_ref):
    rows = [lbl_ref[l] for l in range(lbl_ref.shape[0])]
    m = functools.reduce(jnp.maximum, rows)
    es = [jnp.exp(r - m) for r in rows]
    sm0 = es[0] / functools.reduce(lambda a, b: a + b, es)
    return sm0[0:1, :], sm0[1:2, :]


def _ada_body(c_ref, w_ref, b_ref, o_ref):
    s_hi, s_lo = _split(_silu(c_ref[...]))
    w_hi, w_lo = _split(w_ref[...])
    o_ref[...] = _dot(s_hi, w_hi) + _dot(s_hi, w_lo) + _dot(s_lo, w_hi) + b_ref[...]


def _adaln(cc, w, b):
    n = w.shape[1]
    tn = n // 4
    return pl.pallas_call(
        _ada_body,
        grid=(n // tn,),
        in_specs=[pl.BlockSpec((MOD_ROWS, D_MODEL), lambda i: (0, 0)),
                  pl.BlockSpec((D_MODEL, tn), lambda i: (0, i)),
                  pl.BlockSpec((1, tn), lambda i: (0, i))],
        out_specs=pl.BlockSpec((MOD_ROWS, tn), lambda i: (0, i)),
        out_shape=jax.ShapeDtypeStruct((MOD_ROWS, n), F32),
        compiler_params=pltpu.CompilerParams(vmem_limit_bytes=VMEM_LIMIT),
        name="adaln",
    )(cc, w, b)


def _ctx_body(ctx_ref, mods_ref, nm_ref, wf_ref, wb_ref, wi_ref, lbl_ref, sf_ref, sb_ref, *, ctx_row):
    x = ctx_ref[0]
    n = x.shape[0]
    csh = mods_ref[ctx_row:ctx_row + 1, 0:D_MODEL]
    csc = mods_ref[ctx_row:ctx_row + 1, D_MODEL:2 * D_MODEL]
    hb = (_rms(x, nm_ref[...]) * (1.0 + csc) + csh).astype(BF16)
    zf = _dot(hb, wf_ref[...])
    zb = _dot(hb, wb_ref[...])
    vi = _dot(hb, wi_ref[...]).astype(BF16)
    lb_f, lb_b = _lower_bounds(lbl_ref)
    f_f = lb_f + (1.0 - lb_f) * jax.nn.sigmoid(zf)
    f_b = lb_b + (1.0 - lb_b) * jax.nn.sigmoid(zb)
    ii = lax.broadcasted_iota(I32, (n, n), 0)
    jj = lax.broadcasted_iota(I32, (n, n), 1)
    after = jnp.where(jj > ii, 1.0, 0.0).astype(BF16)
    before = jnp.where(jj < ii, 1.0, 0.0).astype(BF16)
    hi, lo = _split(jnp.log(f_f))
    kd_f = ((1.0 - f_f) * jnp.exp(_dot(after, hi) + _dot(after, lo))).astype(BF16)
    hi, lo = _split(jnp.log(f_b))
    kd_b = ((1.0 - f_b) * jnp.exp(_dot(before, hi) + _dot(before, lo))).astype(BF16)
    for h in range(HG_HEADS):
        ks = slice(h * HG_KDIM, (h + 1) * HG_KDIM)
        vs = slice(h * HG_VDIM, (h + 1) * HG_VDIM)
        sf_ref[0, h] = _dot_tn(vi[:, vs], kd_f[:, ks])
        sb_ref[0, h] = _dot_tn(vi[:, vs], kd_b[:, ks])


def _ctx_states(ctx, mods, nm, w_in_bf, lbl, ctx_row):
    nb, n, _ = ctx.shape
    st = jax.ShapeDtypeStruct((nb, HG_HEADS, HG_VDIM, HG_KDIM), F32)
    st_spec = pl.BlockSpec((1, HG_HEADS, HG_VDIM, HG_KDIM), lambda b: (b, 0, 0, 0))
    wcol = lambda k: pl.BlockSpec((D_MODEL, HG_KEYS), lambda b, k=k: (0, k))
    return pl.pallas_call(
        functools.partial(_ctx_body, ctx_row=ctx_row),
        grid=(nb,),
        in_specs=[pl.BlockSpec((1, n, D_MODEL), lambda b: (b, 0, 0)),
                  pl.BlockSpec(mods.shape, lambda b: (0, 0)),
                  pl.BlockSpec((1, D_MODEL), lambda b: (0, 0)),
                  wcol(1), wcol(2), wcol(3),
                  pl.BlockSpec(lbl.shape, lambda b: (0, 0, 0))],
        out_specs=[st_spec, st_spec],
        out_shape=[st, st],
        compiler_params=pltpu.CompilerParams(vmem_limit_bytes=VMEM_LIMIT),
        name="ctx_states",
    )(ctx, mods, nm, w_in_bf, w_in_bf, w_in_bf, lbl)


def _tile_positions(reverse):
    ii = lax.broadcasted_iota(I32, (TT, TT), 0)
    jj = lax.broadcasted_iota(I32, (TT, TT), 1)
    if reverse:
        ii = TT - 1 - ii
        jj = TT - 1 - jj
    sh0 = CHUNK.bit_length() - 1
    return ii, jj, ii >> sh0, jj >> sh0


def _init_maps(lm_ref, tri_ref, reverse):
    lm_ref[...] = _level_map(reverse)
    ii, jj, ci, cj = _tile_positions(reverse)
    tri_ref[...] = jnp.where((ci == cj) & (jj <= ii), 1.0, 0.0).astype(BF16)


def _level_map(reverse):
    ii, jj, ci, cj = _tile_positions(reverse)
    lm = jnp.where(((ci >> 1) == (cj >> 1)) & (jj <= ii), 1, 0)
    for lvl in range(2, NCH.bit_length()):
        m = ((ci >> lvl) == (cj >> lvl)) & (((ci >> (lvl - 1)) & 1) == 1) & (((cj >> (lvl - 1)) & 1) == 0)
        lm = jnp.where(m, lvl, lm)
    return lm


def _gla_gates(zf, lb, tri):
    f = lb + (1.0 - lb) * jax.nn.sigmoid(zf)
    hi, lo = _split(jnp.log(f))
    return 1.0 - f, _dot(tri, hi) + _dot(tri, lo)


def _gla_tile(q, k, cum, v, st_ref, lm, reverse):
    n_lvl = NCH.bit_length() - 2
    phys = list(range(NCH - 1, -1, -1)) if reverse else list(range(NCH))
    last = 0 if reverse else CHUNK - 1
    tot = [cum[a * CHUNK + last:a * CHUNK + last + 1, :] for a in range(NCH)]
    bnd = [jnp.zeros_like(tot[0])]
    for c in range(NCH):
        bnd.append(bnd[-1] + tot[phys[c]])
    first = [None] * NCH
    for c in range(NCH):
        first[phys[c]] = c % 2 == 0
    ref = jnp.concatenate([cum[a * CHUNK:(a + 1) * CHUNK] - tot[a] if first[a] else cum[a * CHUNK:(a + 1) * CHUNK]
                           for a in range(NCH)], axis=0)
    qp = q * jnp.exp(ref)
    kp = k * jnp.exp(-ref)
    qs = [[None] * NCH for _ in range(n_lvl + 1)]
    ks = [[None] * NCH for _ in range(n_lvl + 1)]
    for c in range(NCH):
        a = phys[c]
        rows = slice(a * CHUNK, (a + 1) * CHUNK)
        to_start = tot[a] if first[a] else 0.0
        to_end = 0.0 if first[a] else tot[a]
        zero = jnp.zeros_like(qp[rows])
        for lvl in range(1, n_lvl + 1):
            s = 1 << lvl
            mid = (c // (2 * s)) * 2 * s + s
            right = (c // s) % 2 == 1
            qs[lvl - 1][a] = qp[rows] * jnp.exp(bnd[c] - bnd[mid] + to_start) if right else zero
            ks[lvl - 1][a] = zero if right else kp[rows] * jnp.exp(bnd[mid] - bnd[c + 1] + to_end)
        qs[n_lvl][a] = qp[rows] * jnp.exp(bnd[c] + to_start)
        ks[n_lvl][a] = kp[rows] * jnp.exp(bnd[NCH] - bnd[c + 1] + to_end)
    qs = [jnp.concatenate(p, axis=0).astype(BF16) for p in qs]
    ks = [jnp.concatenate(p, axis=0).astype(BF16) for p in ks]
    qp = qp.astype(BF16)
    kp = kp.astype(BF16)
    vb = v.astype(BF16)
    dec = jnp.exp(bnd[NCH])
    outs = []
    for h in range(HG_HEADS):
        kk = slice(h * HG_KDIM, (h + 1) * HG_KDIM)
        vv = slice(h * HG_VDIM, (h + 1) * HG_VDIM)
        sc = jnp.where(lm == 1, _dot_nt(qp[:, kk], kp[:, kk]), 0.0)
        for lvl in range(n_lvl):
            sc = jnp.where(lm == lvl + 2, _dot_nt(qs[lvl][:, kk], ks[lvl][:, kk]), sc)
        st = st_ref[h]
        outs.append(_dot(sc.astype(BF16), vb[:, vv]) + _dot_nt(qs[n_lvl][:, kk], st.astype(BF16)))
        st_ref[h] = st * dec[:, kk] + _dot_tn(vb[:, vv], ks[n_lvl][:, kk])
    return jnp.concatenate(outs, axis=1)


def _mix_fwd_body(x_ref, mods_ref, nm_ref, w_ref, lbl_ref, cw_ref, s0_ref, wa_ref, wb_ref,
                  pk_ref, fz_ref, wa_bf_ref, wb_bf_ref, st_ref, lm_ref, tri_ref):
    b = pl.program_id(0)
    j = pl.program_id(1)

    @pl.when((b == 0) & (j == 0))
    def _():
        _init_maps(lm_ref, tri_ref, False)

    @pl.when(j == 0)
    def _():
        st_ref[...] = s0_ref[0]

    sh = mods_ref[pl.ds(b, 1), 0:D_MODEL]
    sc = mods_ref[pl.ds(b, 1), D_MODEL:2 * D_MODEL]
    w = HG_KEYS
    lb_f, _ = _lower_bounds(lbl_ref)
    cw = cw_ref[...]
    col = lax.broadcasted_iota(I32, (TT, 1), 0) & (GRID_W - 1)

    def cast_column_tiles(src_ref, dst_ref):
        tile = dst_ref.shape[3]
        for f in range(dst_ref.shape[1]):
            dst_ref[0, f] = src_ref[0, :, f * tile:(f + 1) * tile].astype(BF16)

    for u in range(STEP_TILES):
        rows = slice(u * TT, (u + 1) * TT)
        hx = (_rms(x_ref[0, rows], nm_ref[...]) * (1.0 + sc) + sh).astype(BF16)
        proj = lambda k: _dot(hx, w_ref[:, k * w:(k + 1) * w])
        gate_b = proj(5)
        if u == 0:
            cast_column_tiles(wa_ref, wa_bf_ref)
        u_conv = proj(6) * proj(7)
        if u == 0:
            cast_column_tiles(wb_ref, wb_bf_ref)
        zff = proj(1)
        up = jnp.where(col != 0, pltpu.roll(u_conv, 1, axis=0), 0.0)
        un = jnp.where(col != GRID_W - 1, pltpu.roll(u_conv, TT - 1, axis=0), 0.0)
        pk_ref[0, rows, 3 * w:4 * w] = (gate_b * (cw[0:1] * up + cw[1:2] * u_conv + cw[2:3] * un)).astype(BF16)
        q = proj(0)
        k, cum = _gla_gates(zff, lb_f, tri_ref[...])
        vi = proj(3)
        fz_ref[0, rows, 0:w] = proj(2)
        pk_ref[0, rows, 2 * w:3 * w] = proj(4).astype(BF16)
        fz_ref[0, rows, w:2 * w] = _gla_tile(q, k, cum, vi, st_ref, lm_ref[...], False)
        pk_ref[0, rows, 0:w] = q.astype(BF16)
        pk_ref[0, rows, w:2 * w] = vi.astype(BF16)


def _mix_fwd(x, mods, nm, w_in_bf, lbl, cw, s0_f, wa, wb, col_tile):
    nb, t, _ = x.shape
    nt = t // STEP_TOK
    ne, rows, cols = wa.shape
    per_expert = nb * nt // ne
    step_rows = rows // per_expert
    tok = lambda n, dt: jax.ShapeDtypeStruct((nb, t, n * HG_WIDTH), dt)
    tspec = lambda n: pl.BlockSpec((1, STEP_TOK, n * HG_WIDTH), lambda b, j: (b, j, 0))
    cast_in = pl.BlockSpec((1, step_rows, cols), lambda b, j: ((b * nt + j) // per_expert, (b * nt + j) % per_expert, 0))
    cast_out = pl.BlockSpec((1, cols // col_tile, step_rows, col_tile),
                            lambda b, j: ((b * nt + j) // per_expert, 0, (b * nt + j) % per_expert, 0))
    cast_shape = jax.ShapeDtypeStruct((ne, cols // col_tile, rows, col_tile), BF16)
    return pl.pallas_call(
        _mix_fwd_body,
        grid=(nb, nt),
        in_specs=[pl.BlockSpec((1, STEP_TOK, D_MODEL), lambda b, j: (b, j, 0)),
                  pl.BlockSpec(mods.shape, lambda b, j: (0, 0)),
                  pl.BlockSpec((1, D_MODEL), lambda b, j: (0, 0)),
                  pl.BlockSpec(w_in_bf.shape, lambda b, j: (0, 0)),
                  pl.BlockSpec(lbl.shape, lambda b, j: (0, 0, 0)),
                  pl.BlockSpec(cw.shape, lambda b, j: (0, 0)),
                  pl.BlockSpec((1, HG_HEADS, HG_VDIM, HG_KDIM), lambda b, j: (b, 0, 0, 0)),
                  cast_in, cast_in],
        out_specs=[tspec(4), tspec(2), cast_out, cast_out],
        out_shape=[tok(4, BF16), tok(2, F32), cast_shape, cast_shape],
        scratch_shapes=[pltpu.VMEM((HG_HEADS, HG_VDIM, HG_KDIM), F32), pltpu.VMEM((TT, TT), I32),
                        pltpu.VMEM((TT, TT), BF16)],
        compiler_params=pltpu.CompilerParams(dimension_semantics=("arbitrary", "arbitrary"),
                                             vmem_limit_bytes=VMEM_LIMIT),
        name="mix_fwd",
    )(x, mods, nm, w_in_bf, lbl, cw, s0_f, wa, wb)


def _mix_bwd_body(x_ref, pk_ref, fz_ref, mods_ref, lbl_ref, hgn_ref,
                  wo_ref, nf_ref, wr_ref, s0_ref, wc_ref,
                  x1_ref, hr_ref, at_ref, wc_bf_ref, st_ref, lm_ref, tri_ref):
    b = pl.program_id(0)
    j = pl.program_id(1)
    wc_bf_ref[...] = wc_ref[...].astype(BF16)

    @pl.when((b == 0) & (j == 0))
    def _():
        _init_maps(lm_ref, tri_ref, True)

    @pl.when(j == 0)
    def _():
        st_ref[...] = s0_ref[0]

    _, lb_b = _lower_bounds(lbl_ref)
    w = HG_WIDTH
    hgn = hgn_ref[...]
    mod = lambda k: mods_ref[pl.ds(b, 1), k * D_MODEL:(k + 1) * D_MODEL]
    scans = {}
    for u in reversed(range(STEP_TILES)):
        rows = slice(u * TT, (u + 1) * TT)
        k, cum = _gla_gates(fz_ref[0, rows, 0:w], lb_b, tri_ref[...])
        scans[u] = _gla_tile(pk_ref[0, rows, 0:w].astype(F32), k, cum, pk_ref[0, rows, w:2 * w], st_ref, lm_ref[...],
                             True)
    o = fz_ref[0, :, w:2 * w] + jnp.concatenate([scans[u] for u in range(STEP_TILES)], axis=0)
    heads = [_rms(o[:, h * HG_VDIM:(h + 1) * HG_VDIM], hgn) for h in range(HG_HEADS)]
    g = pk_ref[0, :, 2 * w:3 * w].astype(F32)
    ohg = jnp.concatenate(heads, axis=1) * _silu(g)
    cat = jnp.concatenate([ohg.astype(BF16), pk_ref[0, :, 3 * w:4 * w]], axis=1)
    mx = _dot(cat, wo_ref[...])
    x1 = x_ref[0] + mod(2) * mx
    h2 = _rms(x1, nf_ref[...]) * (1.0 + mod(4)) + mod(3)
    for c in range(ACC_ROWS):
        dst = pl.ds(c, STEP_TOK, stride=ACC_ROWS)
        x1_ref[0, dst, :] = x1[:, c * LANES:(c + 1) * LANES]
        hr_ref[0, dst, :] = h2[:, c * LANES:(c + 1) * LANES]
    r = _dot(jnp.concatenate(_split(h2), axis=0), wr_ref[...])
    lg = r[:STEP_TOK, :LANES] + r[:STEP_TOK, LANES:] + r[STEP_TOK:, :LANES]
    lgt = lg.T[0:N_EXPERTS, :]
    e = jnp.exp(lgt - jnp.max(lgt, axis=0, keepdims=True))
    at_ref[0] = e / jnp.sum(e, axis=0, keepdims=True)


def _mix_bwd(x, pk, fz, mods, lbl, hgn, w_out_bf, nf, wr_pad, s0_b, wc):
    nb, t, _ = x.shape
    nt = t // STEP_TOK
    cast_spec = pl.BlockSpec((1,) + wc.shape[1:], lambda b, j: (b * nt + j, 0, 0))
    rev = lambda b, j: (b, nt - 1 - j, 0)
    tspec = lambda a: pl.BlockSpec((1, STEP_TOK, a.shape[2]), rev)
    xspec = pl.BlockSpec((1, STEP_TOK, D_MODEL), rev)
    full2 = lambda a: pl.BlockSpec(a.shape, lambda b, j: (0, 0))
    return pl.pallas_call(
        _mix_bwd_body,
        grid=(nb, nt),
        in_specs=[xspec, tspec(pk), tspec(fz),
                  full2(mods), pl.BlockSpec(lbl.shape, lambda b, j: (0, 0, 0)), full2(hgn),
                  full2(w_out_bf), full2(nf), full2(wr_pad),
                  pl.BlockSpec((1, HG_HEADS, HG_VDIM, HG_KDIM), lambda b, j: (b, 0, 0, 0)),
                  cast_spec],
        out_specs=[pl.BlockSpec((1, STEP_TOK * ACC_ROWS, LANES), rev),
                   pl.BlockSpec((1, STEP_TOK * ACC_ROWS, LANES), rev),
                   pl.BlockSpec((1, N_EXPERTS, STEP_TOK), lambda b, j: (b, 0, nt - 1 - j)),
                   cast_spec],
        out_shape=[jax.ShapeDtypeStruct((nb, t * ACC_ROWS, LANES), F32),
                   jax.ShapeDtypeStruct((nb, t * ACC_ROWS, LANES), F32),
                   jax.ShapeDtypeStruct((nb, N_EXPERTS, t), F32),
                   jax.ShapeDtypeStruct(wc.shape, BF16)],
        scratch_shapes=[pltpu.VMEM((HG_HEADS, HG_VDIM, HG_KDIM), F32), pltpu.VMEM((TT, TT), I32),
                        pltpu.VMEM((TT, TT), BF16)],
        compiler_params=pltpu.CompilerParams(dimension_semantics=("arbitrary", "arbitrary"),
                                             vmem_limit_bytes=VMEM_LIMIT),
        name="mix_bwd",
    )(x, pk, fz, mods, lbl, hgn, w_out_bf, nf, wr_pad, s0_b, wc)


def _route_body(at_ref, idx_ref, gate_ref, *, cap):
    ne, t = at_ref.shape[1], at_ref.shape[2]
    blk = 2 * LANES
    aff = at_ref[0]

    def count(m):
        return jnp.sum(jnp.where(m, 1.0, 0.0), axis=1, keepdims=True)

    def enough(cand):
        return count(aff >= lax.bitcast_convert_type(cand, F32)) >= cap

    def bit_step(i, thr):
        hi = jnp.left_shift(jnp.int32(1), 30 - 2 * i)
        lo = jnp.right_shift(hi, 1)
        both, upper, lower = thr | hi | lo, thr | hi, thr | lo
        return jnp.where(enough(both), both, jnp.where(enough(upper), upper, jnp.where(enough(lower), lower, thr)))

    thr = lax.fori_loop(0, 16, bit_step, jnp.zeros((ne, 1), I32))
    gt = aff >= lax.bitcast_convert_type(thr + 1, F32)
    eq = (aff >= lax.bitcast_convert_type(thr, F32)) & jnp.logical_not(gt)
    need = cap - count(gt)

    ii = lax.broadcasted_iota(I32, (blk, blk), 0)
    jj = lax.broadcasted_iota(I32, (blk, blk), 1)
    upper = jnp.where(ii <= jj, 1.0, 0.0).astype(BF16)

    def prefix_blocks(m):
        mb = jnp.where(m, 1.0, 0.0).astype(BF16)
        carry = jnp.zeros((ne, 1), F32)
        out = []
        for kb in range(t // blk):
            p = _dot(mb[:, kb * blk:(kb + 1) * blk], upper) + carry
            out.append(p)
            carry = p[:, blk - 1:blk]
        return out

    eq_rank = jnp.concatenate(prefix_blocks(eq), axis=1)
    sel = gt | (eq & (eq_rank <= need))
    rank = jnp.concatenate(prefix_blocks(sel), axis=1).astype(I32)

    tok = lax.broadcasted_iota(I32, (ne, t), 1)
    x = jnp.where(sel, tok, -1)
    d = jnp.where(sel, tok - rank + 1, 0)
    g = aff
    for k in range(t.bit_length() - 1):
        left = lambda a: pltpu.roll(a, t - (1 << k), axis=1)
        ds = left(d)
        take = ((ds >> k) & 1) == 1
        keep = ((d >> k) & 1) == 0
        x = jnp.where(take, left(x), jnp.where(keep, x, -1))
        g = jnp.where(take, left(g), g)
        d = jnp.where(take, ds, jnp.where(keep, d, 0))
    idx_ref[0] = x[:, :cap]
    gate_ref[0] = g[:, :cap]


def _route(aff_t, cap):
    nb, ne, t = aff_t.shape
    spec = pl.BlockSpec((1, ne, cap), lambda b: (b, 0, 0))
    return pl.pallas_call(
        functools.partial(_route_body, cap=cap),
        grid=(nb,),
        in_specs=[pl.BlockSpec((1, ne, t), lambda b: (b, 0, 0))],
        out_specs=[spec, spec],
        out_shape=[jax.ShapeDtypeStruct((nb, ne, cap), I32), jax.ShapeDtypeStruct((nb, ne, cap), F32)],
        compiler_params=pltpu.CompilerParams(vmem_limit_bytes=VMEM_LIMIT),
        name="route",
    )(aff_t)


def _moe_body(idx_ref, gate_ref, hr_hbm, wg_ref, wu_ref, wd_ref, x1_hbm, mods_ref, nfin_ref,
              out_ref, acc_ref, hr_ref, rowg_ref, rows_ref, xs_ref, ys_ref, seed_sem, hr_sem, *, cap, n_ff, n_epi):
    b = pl.program_id(0)
    s = pl.program_id(1)
    n_work = N_EXPERTS * n_ff
    e = s // n_ff
    ff = s % n_ff
    slot = e % 2
    group = SUBLANES

    def list_base(expert):
        return (b * N_EXPERTS + expert) * cap

    def gather(base, j0, n):
        for u in range(n):
            tkn = idx_ref[base + j0 + u]
            rowg_ref[pl.ds(pl.multiple_of((j0 + u) * ACC_ROWS, ACC_ROWS), ACC_ROWS), :] = (
                hr_ref[pl.ds(pl.multiple_of(tkn * ACC_ROWS, ACC_ROWS), ACC_ROWS), :])

    def scatter(base, j0, n):
        for g0 in range(0, n, group):
            dst = []
            val = []
            for u in range(g0, g0 + group):
                r = pl.multiple_of(idx_ref[base + j0 + u] * ACC_ROWS, ACC_ROWS)
                dst.append(r)
                val.append(acc_ref[pl.ds(r, ACC_ROWS), :] + gate_ref[base + j0 + u]
                           * rows_ref[pl.ds(pl.multiple_of((j0 + u) * ACC_ROWS, ACC_ROWS), ACC_ROWS), :])
            for r, v in zip(dst, val):
                acc_ref[pl.ds(r, ACC_ROWS), :] = v

    def in_groups(fn, base):
        def step(jo, carry):
            fn(base, jo * group, group)
            return carry
        lax.fori_loop(0, cap // group, step, 0)

    def lay_out_input(to_slot):
        for c in range(ACC_ROWS):
            xs_ref[to_slot, :, c * LANES:(c + 1) * LANES] = (
                rowg_ref[pl.ds(c, cap, stride=ACC_ROWS), :].astype(BF16))

    def lay_out_output():
        g2 = mods_ref[pl.ds(b, 1), 5 * D_MODEL:6 * D_MODEL]
        for c in range(ACC_ROWS):
            cols = slice(c * LANES, (c + 1) * LANES)
            rows_ref[pl.ds(c, cap, stride=ACC_ROWS), :] = ys_ref[:, cols] * g2[:, cols]

    chunk_rows = EPI_TOK * ACC_ROWS

    def seed_copy(sample, chunk):
        rows = pl.ds(chunk * chunk_rows, chunk_rows)
        return pltpu.make_async_copy(x1_hbm.at[sample, rows], acc_ref.at[rows], seed_sem.at[chunk])

    def expert_tile():
        half = cap // 2
        for m in range(2):
            rows = slice(m * half, (m + 1) * half)
            xs = xs_ref[slot, rows, :]
            hid = _silu(_dot(xs, wg_ref[0, 0])) * _dot(xs, wu_ref[0, 0])
            ys_ref[rows, :] = _dot(hid.astype(BF16), wd_ref[0]) + jnp.where(ff == 0, 0.0, ys_ref[rows, :])

    def input_copy(sample):
        return pltpu.make_async_copy(hr_hbm.at[sample], hr_ref, hr_sem.at[0])

    @pl.when((s == 0) & (b == 0))
    def _():
        input_copy(0).start()
        for chunk in range(n_epi):
            seed_copy(0, chunk).start()

    @pl.when(s == 0)
    def _():
        input_copy(b).wait()
        ys_ref[...] = jnp.zeros_like(ys_ref)
        in_groups(gather, list_base(0))
        lay_out_input(0)

    @pl.when(s == n_ff - 1)
    def _():
        for chunk in range(n_epi):
            seed_copy(b, chunk).wait()

    @pl.when((s < n_work) & (ff == 0))
    def _():
        gather(list_base(jnp.minimum(e + 1, N_EXPERTS - 1)), 0, cap)
        lay_out_output()
        expert_tile()

    if n_ff > 2:
        @pl.when((s < n_work) & (ff > 0) & (ff < n_ff - 1))
        def _():
            expert_tile()

    @pl.when((s < n_work) & (ff == n_ff - 1))
    def _():
        lay_out_input(1 - slot)
        scatter(list_base(jnp.maximum(e - 1, 0)), 0, cap)
        expert_tile()

    @pl.when(s == n_work)
    def _():
        @pl.when(b + 1 < pl.num_programs(0))
        def _():
            input_copy(b + 1).start()

        lay_out_output()
        in_groups(scatter, list_base(N_EXPERTS - 1))

    @pl.when(s >= n_work)
    def _():
        chunk = s - n_work
        row0 = chunk * chunk_rows
        tot = jnp.concatenate([acc_ref[pl.ds(row0 + c, EPI_TOK, stride=ACC_ROWS), :] for c in range(ACC_ROWS)], axis=1)
        out_ref[0] = _rms(tot, nfin_ref[...])

        @pl.when(b + 1 < pl.num_programs(0))
        def _():
            seed_copy(b + 1, chunk).start()


def _moe(idx_flat, gate_flat, hr, wg, wu, wd, x1, mods, nfin, cap):
    nb, t = x1.shape[0], x1.shape[1] // ACC_ROWS
    n_ff = wg.shape[1]
    n_work = N_EXPERTS * n_ff
    n_epi = t // EPI_TOK

    def w_idx(s):
        sc = jnp.minimum(s, n_work - 1)
        return sc // n_ff, sc % n_ff

    def epi(b, s, *_):
        return (b, jnp.maximum(s - n_work, 0), 0)

    grid_spec = pltpu.PrefetchScalarGridSpec(
        num_scalar_prefetch=2,
        grid=(nb, n_work + n_epi),
        in_specs=[pl.BlockSpec(memory_space=pl.ANY),
                  pl.BlockSpec((1, 1, D_MODEL, DFF_TILE), lambda b, s, *_: w_idx(s) + (0, 0)),
                  pl.BlockSpec((1, 1, D_MODEL, DFF_TILE), lambda b, s, *_: w_idx(s) + (0, 0)),
                  pl.BlockSpec((1, DFF_TILE, D_MODEL), lambda b, s, *_: (w_idx(s)[0], w_idx(s)[1], 0)),
                  pl.BlockSpec(memory_space=pl.ANY),
                  pl.BlockSpec(mods.shape, lambda b, s, *_: (0, 0)),
                  pl.BlockSpec((1, D_MODEL), lambda b, s, *_: (0, 0))],
        out_specs=pl.BlockSpec((1, EPI_TOK, D_MODEL), epi),
        scratch_shapes=[pltpu.VMEM((t * ACC_ROWS, LANES), F32),
                        pltpu.VMEM((t * ACC_ROWS, LANES), F32),
                        pltpu.VMEM((cap * ACC_ROWS, LANES), F32),
                        pltpu.VMEM((cap * ACC_ROWS, LANES), F32),
                        pltpu.VMEM((2, cap, D_MODEL), BF16),
                        pltpu.VMEM((cap, D_MODEL), F32),
                        pltpu.SemaphoreType.DMA((n_epi,)),
                        pltpu.SemaphoreType.DMA((1,))])
    return pl.pallas_call(
        functools.partial(_moe_body, cap=cap, n_ff=n_ff, n_epi=n_epi),
        grid_spec=grid_spec,
        out_shape=jax.ShapeDtypeStruct((nb, t, D_MODEL), F32),
        compiler_params=pltpu.CompilerParams(dimension_semantics=("arbitrary", "arbitrary"),
                                             vmem_limit_bytes=VMEM_LIMIT),
        name="moe",
    )(idx_flat, gate_flat, hr, wg, wu, wd, x1, mods, nfin)


def kernel(x, c, ctx, c_ctx, w_ada, b_ada, norm_mix, norm_ffn, w_in, lb_logits, hg_norm, conv_w,
           w_out, w_router, w_gate, w_up, w_down, norm_final):
    nb, t, d = x.shape
    assert d == D_MODEL and w_ada.shape[0] == 1 and nb < MOD_ROWS and t % STEP_TOK == 0 and TT % GRID_W == 0
    assert w_in.shape[2] == 3 * HG_KEYS + 2 * HG_WIDTH + 3 * SC_WIDTH and w_router.shape[2] == N_EXPERTS
    cap = EC_CAPACITY * t // N_EXPERTS
    assert cap % LANES == 0 and w_gate.shape[3] % DFF_TILE == 0 and w_gate.shape[3] // DFF_TILE >= 2
    assert (nb * (t // STEP_TOK)) % N_EXPERTS == 0 and (D_MODEL * N_EXPERTS) % (nb * (t // STEP_TOK)) == 0

    cc = jnp.concatenate([c, c_ctx[None, :], jnp.zeros((MOD_ROWS - nb - 1, d), F32)], axis=0)
    w_in_bf = w_in[0].astype(BF16)
    w_out_bf = w_out[0].astype(BF16)
    wr_hi, wr_lo = _split(jnp.pad(w_router[0], ((0, 0), (0, LANES - N_EXPERTS))))
    wr_pad = jnp.concatenate([wr_hi, wr_lo], axis=1)
    nm, nf, nfin = norm_mix[0][None, :], norm_ffn[0][None, :], norm_final[None, :]
    hgn = hg_norm[0][None, :]

    mods = _adaln(cc, w_ada[0], b_ada[0][None, :])
    s0_f, s0_b = _ctx_states(ctx, mods, nm, w_in_bf, lb_logits, ctx_row=nb)
    steps = nb * (t // STEP_TOK)
    sliced = lambda w: w.reshape((steps, w.shape[0] * w.shape[1] // steps) + w.shape[2:])
    pk, fz, wg, wu = _mix_fwd(x, mods, nm, w_in_bf, lb_logits, conv_w[0], s0_f, w_gate[0], w_up[0], DFF_TILE)
    x1, hr, aff_t, wd = _mix_bwd(x, pk, fz, mods, lb_logits, hgn, w_out_bf, nf, wr_pad, s0_b, sliced(w_down[0]))
    wd = wd.reshape(w_down[0].shape)
    idx, gates = _route(aff_t, cap)
    return _moe(idx.reshape(-1), gates.reshape(-1), hr, wg, wu, wd, x1, mods, nfin, cap)
```

```python
import functools

import jax
import jax.numpy as jnp
from jax import lax
from jax.experimental import pallas as pl
from jax.experimental.pallas import tpu as pltpu

F32 = jnp.float32
BF16 = jnp.bfloat16
I32 = jnp.int32

D_MODEL = 1024
N_MOD = 6
HG_HEADS = 4
HG_KDIM = 128
HG_VDIM = 128
HG_KEYS = HG_HEADS * HG_KDIM
HG_WIDTH = HG_HEADS * HG_VDIM
SC_WIDTH = D_MODEL - HG_WIDTH
GRID_W = 64
CHUNK = 32
N_EXPERTS = 16
EC_CAPACITY = 2
EPS = 1e-6

LANES = 128
SUBLANES = 8
VMEM_LIMIT = 62 * 1024 * 1024

TT = 256
STEP_TILES = 2
STEP_TOK = TT * STEP_TILES
FWD_TILES = 4
FWD_TOK = TT * FWD_TILES
NCH = TT // CHUNK
MOD_ROWS = 16
DFF_TILE = 1024
EPI_TOK = 512
ACC_ROWS = D_MODEL // LANES


def _dot(a, b):
    return jnp.dot(a, b, preferred_element_type=F32)


def _dot_nt(a, b):
    return lax.dot_general(a, b, (((1,), (1,)), ((), ())), preferred_element_type=F32)


def _dot_tn(a, b):
    return lax.dot_general(a, b, (((0,), (0,)), ((), ())), preferred_element_type=F32)


def _split(x):
    hi = x.astype(BF16)
    lo = (x - hi.astype(F32)).astype(BF16)
    return hi, lo


def _rms(x, gain):
    return x * lax.rsqrt(jnp.mean(x * x, axis=-1, keepdims=True) + EPS) * gain


def _silu(x):
    return x * jax.nn.sigmoid(x)


def _lower_bounds(lbl_ref):
    rows = [lbl_ref[l] for l in range(lbl_ref.shape[0])]
    m = functools.reduce(jnp.maximum, rows)
    es = [jnp.exp(r - m) for r in rows]
    sm0 = es[0] / functools.reduce(lambda a, b: a + b, es)
    return sm0[0:1, :], sm0[1:2, :]


def _ada_body(c_ref, w_ref, b_ref, o_ref):
    s_hi, s_lo = _split(_silu(c_ref[...]))
    w_hi, w_lo = _split(w_ref[...])
    o_ref[...] = _dot(s_hi, w_hi) + _dot(s_hi, w_lo) + _dot(s_lo, w_hi) + b_ref[...]


def _adaln(cc, w, b):
    n = w.shape[1]
    tn = n // 4
    return pl.pallas_call(
        _ada_body,
        grid=(n // tn,),
        in_specs=[pl.BlockSpec((MOD_ROWS, D_MODEL), lambda i: (0, 0)),
                  pl.BlockSpec((D_MODEL, tn), lambda i: (0, i)),
                  pl.BlockSpec((1, tn), lambda i: (0, i))],
        out_specs=pl.BlockSpec((MOD_ROWS, tn), lambda i: (0, i)),
        out_shape=jax.ShapeDtypeStruct((MOD_ROWS, n), F32),
        compiler_params=pltpu.CompilerParams(vmem_limit_bytes=VMEM_LIMIT),
        name="adaln",
    )(cc, w, b)


def _ctx_body(ctx_ref, mods_ref, nm_ref, wf_ref, wb_ref, wi_ref, lbl_ref, sf_ref, sb_ref, *, ctx_row):
    x = ctx_ref[0]
    n = x.shape[0]
    csh = mods_ref[ctx_row:ctx_row + 1, 0:D_MODEL]
    csc = mods_ref[ctx_row:ctx_row + 1, D_MODEL:2 * D_MODEL]
    hb = (_rms(x, nm_ref[...]) * (1.0 + csc) + csh).astype(BF16)
    zf = _dot(hb, wf_ref[...])
    zb = _dot(hb, wb_ref[...])
    vi = _dot(hb, wi_ref[...]).astype(BF16)
    lb_f, lb_b = _lower_bounds(lbl_ref)
    f_f = lb_f + (1.0 - lb_f) * jax.nn.sigmoid(zf)
    f_b = lb_b + (1.0 - lb_b) * jax.nn.sigmoid(zb)
    ii = lax.broadcasted_iota(I32, (n, n), 0)
    jj = lax.broadcasted_iota(I32, (n, n), 1)
    after = jnp.where(jj > ii, 1.0, 0.0).astype(BF16)
    before = jnp.where(jj < ii, 1.0, 0.0).astype(BF16)
    hi, lo = _split(jnp.log(f_f))
    kd_f = ((1.0 - f_f) * jnp.exp(_dot(after, hi) + _dot(after, lo))).astype(BF16)
    hi, lo = _split(jnp.log(f_b))
    kd_b = ((1.0 - f_b) * jnp.exp(_dot(before, hi) + _dot(before, lo))).astype(BF16)
    for h in range(HG_HEADS):
        ks = slice(h * HG_KDIM, (h + 1) * HG_KDIM)
        vs = slice(h * HG_VDIM, (h + 1) * HG_VDIM)
        sf_ref[0, h] = _dot_tn(vi[:, vs], kd_f[:, ks])
        sb_ref[0, h] = _dot_tn(vi[:, vs], kd_b[:, ks])


def _ctx_states(ctx, mods, nm, w_in_bf, lbl, ctx_row):
    nb, n, _ = ctx.shape
    st = jax.ShapeDtypeStruct((nb, HG_HEADS, HG_VDIM, HG_KDIM), F32)
    st_spec = pl.BlockSpec((1, HG_HEADS, HG_VDIM, HG_KDIM), lambda b: (b, 0, 0, 0))
    wcol = lambda k: pl.BlockSpec((D_MODEL, HG_KEYS), lambda b, k=k: (0, k))
    return pl.pallas_call(
        functools.partial(_ctx_body, ctx_row=ctx_row),
        grid=(nb,),
        in_specs=[pl.BlockSpec((1, n, D_MODEL), lambda b: (b, 0, 0)),
                  pl.BlockSpec(mods.shape, lambda b: (0, 0)),
                  pl.BlockSpec((1, D_MODEL), lambda b: (0, 0)),
                  wcol(1), wcol(2), wcol(3),
                  pl.BlockSpec(lbl.shape, lambda b: (0, 0, 0))],
        out_specs=[st_spec, st_spec],
        out_shape=[st, st],
        compiler_params=pltpu.CompilerParams(vmem_limit_bytes=VMEM_LIMIT),
        name="ctx_states",
    )(ctx, mods, nm, w_in_bf, w_in_bf, w_in_bf, lbl)


def _tile_positions(reverse):
    ii = lax.broadcasted_iota(I32, (TT, TT), 0)
    jj = lax.broadcasted_iota(I32, (TT, TT), 1)
    if reverse:
        ii = TT - 1 - ii
        jj = TT - 1 - jj
    sh0 = CHUNK.bit_length() - 1
    return ii, jj, ii >> sh0, jj >> sh0


def _init_maps(lm_ref, tri_ref, reverse):
    lm_ref[...] = _level_map(reverse)
    ii, jj, ci, cj = _tile_positions(reverse)
    tri_ref[...] = jnp.where((ci == cj) & (jj <= ii), 1.0, 0.0).astype(BF16)


def _level_map(reverse):
    ii, jj, ci, cj = _tile_positions(reverse)
    lm = jnp.where(((ci >> 1) == (cj >> 1)) & (jj <= ii), 1, 0)
    for lvl in range(2, NCH.bit_length()):
        m = ((ci >> lvl) == (cj >> lvl)) & (((ci >> (lvl - 1)) & 1) == 1) & (((cj >> (lvl - 1)) & 1) == 0)
        lm = jnp.where(m, lvl, lm)
    return lm


def _gla_gates(zf, lb, tri):
    f = lb + (1.0 - lb) * jax.nn.sigmoid(zf)
    hi, lo = _split(jnp.log(f))
    return 1.0 - f, _dot(tri, hi) + _dot(tri, lo)


def _gla_tile(q, k, cum, v, st_ref, lm, reverse):
    n_lvl = NCH.bit_length() - 2
    phys = list(range(NCH - 1, -1, -1)) if reverse else list(range(NCH))
    last = 0 if reverse else CHUNK - 1
    tot = [cum[a * CHUNK + last:a * CHUNK + last + 1, :] for a in range(NCH)]
    bnd = [jnp.zeros_like(tot[0])]
    for c in range(NCH):
        bnd.append(bnd[-1] + tot[phys[c]])
    first = [None] * NCH
    for c in range(NCH):
        first[phys[c]] = c % 2 == 0
    ref = jnp.concatenate([cum[a * CHUNK:(a + 1) * CHUNK] - tot[a] if first[a] else cum[a * CHUNK:(a + 1) * CHUNK]
                           for a in range(NCH)], axis=0)
    qp = q * jnp.exp(ref)
    kp = k * jnp.exp(-ref)
    qs = [[None] * NCH for _ in range(n_lvl + 1)]
    ks = [[None] * NCH for _ in range(n_lvl + 1)]
    for c in range(NCH):
        a = phys[c]
        rows = slice(a * CHUNK, (a + 1) * CHUNK)
        to_start = tot[a] if first[a] else 0.0
        to_end = 0.0 if first[a] else tot[a]
        zero = jnp.zeros_like(qp[rows])
        for lvl in range(1, n_lvl + 1):
            s = 1 << lvl
            mid = (c // (2 * s)) * 2 * s + s
            right = (c // s) % 2 == 1
            qs[lvl - 1][a] = qp[rows] * jnp.exp(bnd[c] - bnd[mid] + to_start) if right else zero
            ks[lvl - 1][a] = zero if right else kp[rows] * jnp.exp(bnd[mid] - bnd[c + 1] + to_end)
        qs[n_lvl][a] = qp[rows] * jnp.exp(bnd[c] + to_start)
        ks[n_lvl][a] = kp[rows] * jnp.exp(bnd[NCH] - bnd[c + 1] + to_end)
    qs = [jnp.concatenate(p, axis=0).astype(BF16) for p in qs]
    ks = [jnp.concatenate(p, axis=0).astype(BF16) for p in ks]
    qp = qp.astype(BF16)
    kp = kp.astype(BF16)
    vb = v.astype(BF16)
    dec = jnp.exp(bnd[NCH])
    outs = []
    for h in range(HG_HEADS):
        kk = slice(h * HG_KDIM, (h + 1) * HG_KDIM)
        vv = slice(h * HG_VDIM, (h + 1) * HG_VDIM)
        sc = jnp.where(lm == 1, _dot_nt(qp[:, kk], kp[:, kk]), 0.0)
        for lvl in range(n_lvl):
            sc = jnp.where(lm == lvl + 2, _dot_nt(qs[lvl][:, kk], ks[lvl][:, kk]), sc)
        st = st_ref[h]
        outs.append(_dot(sc.astype(BF16), vb[:, vv]) + _dot_nt(qs[n_lvl][:, kk], st.astype(BF16)))
        st_ref[h] = st * dec[:, kk] + _dot_tn(vb[:, vv], ks[n_lvl][:, kk])
    return jnp.concatenate(outs, axis=1)


def _cast_column_tiles(src_ref, dst_ref):
    tile = dst_ref.shape[3]
    for f in range(dst_ref.shape[1]):
        dst_ref[0, f] = src_ref[0, :, f * tile:(f + 1) * tile].astype(BF16)


def _mix_fwd_body(x_ref, mods_ref, nm_ref, w_ref, lbl_ref, cw_ref, s0_ref,
                  pk_ref, fz_ref, st_ref, lm_ref, tri_ref):
    b = pl.program_id(0)
    j = pl.program_id(1)

    @pl.when((b == 0) & (j == 0))
    def _():
        _init_maps(lm_ref, tri_ref, False)

    @pl.when(j == 0)
    def _():
        st_ref[...] = s0_ref[0]

    sh = mods_ref[pl.ds(b, 1), 0:D_MODEL]
    sc = mods_ref[pl.ds(b, 1), D_MODEL:2 * D_MODEL]
    w = HG_KEYS
    lb_f, _ = _lower_bounds(lbl_ref)
    cw = cw_ref[...]
    col = lax.broadcasted_iota(I32, (TT, 1), 0) & (GRID_W - 1)

    for u in range(FWD_TILES):
        rows = slice(u * TT, (u + 1) * TT)
        hx = (_rms(x_ref[0, rows], nm_ref[...]) * (1.0 + sc) + sh).astype(BF16)
        proj = lambda k: _dot(hx, w_ref[:, k * w:(k + 1) * w])
        gate_b = proj(5)
        u_conv = proj(6) * proj(7)
        zff = proj(1)
        up = jnp.where(col != 0, pltpu.roll(u_conv, 1, axis=0), 0.0)
        un = jnp.where(col != GRID_W - 1, pltpu.roll(u_conv, TT - 1, axis=0), 0.0)
        pk_ref[0, rows, 3 * w:4 * w] = (gate_b * (cw[0:1] * up + cw[1:2] * u_conv + cw[2:3] * un)).astype(BF16)
        q = proj(0)
        k, cum = _gla_gates(zff, lb_f, tri_ref[...])
        vi = proj(3)
        fz_ref[0, rows, 0:w] = proj(2)
        pk_ref[0, rows, 2 * w:3 * w] = proj(4).astype(BF16)
        fz_ref[0, rows, w:2 * w] = _gla_tile(q, k, cum, vi, st_ref, lm_ref[...], False)
        pk_ref[0, rows, 0:w] = q.astype(BF16)
        pk_ref[0, rows, w:2 * w] = vi.astype(BF16)


def _mix_fwd(x, mods, nm, w_in_bf, lbl, cw, s0_f):
    nb, t, _ = x.shape
    nt = t // FWD_TOK
    tok = lambda n, dt: jax.ShapeDtypeStruct((nb, t, n * HG_WIDTH), dt)
    tspec = lambda n: pl.BlockSpec((1, FWD_TOK, n * HG_WIDTH), lambda b, j: (b, j, 0))
    return pl.pallas_call(
        _mix_fwd_body,
        grid=(nb, nt),
        in_specs=[pl.BlockSpec((1, FWD_TOK, D_MODEL), lambda b, j: (b, j, 0)),
                  pl.BlockSpec(mods.shape, lambda b, j: (0, 0)),
                  pl.BlockSpec((1, D_MODEL), lambda b, j: (0, 0)),
                  pl.BlockSpec(w_in_bf.shape, lambda b, j: (0, 0)),
                  pl.BlockSpec(lbl.shape, lambda b, j: (0, 0, 0)),
                  pl.BlockSpec(cw.shape, lambda b, j: (0, 0)),
                  pl.BlockSpec((1, HG_HEADS, HG_VDIM, HG_KDIM), lambda b, j: (b, 0, 0, 0))],
        out_specs=[tspec(4), tspec(2)],
        out_shape=[tok(4, BF16), tok(2, F32)],
        scratch_shapes=[pltpu.VMEM((HG_HEADS, HG_VDIM, HG_KDIM), F32), pltpu.VMEM((TT, TT), I32),
                        pltpu.VMEM((TT, TT), BF16)],
        compiler_params=pltpu.CompilerParams(dimension_semantics=("arbitrary", "arbitrary"),
                                             vmem_limit_bytes=VMEM_LIMIT),
        name="mix_fwd",
    )(x, mods, nm, w_in_bf, lbl, cw, s0_f)


def _mix_bwd_body(x_ref, pk_ref, fz_ref, mods_ref, lbl_ref, hgn_ref,
                  wo_ref, nf_ref, wr_ref, s0_ref, wa_ref, wb_ref, wc_ref,
                  x1_ref, hr_ref, at_ref, wa_bf_ref, wb_bf_ref, wc_bf_ref, st_ref, lm_ref, tri_ref):
    b = pl.program_id(0)
    j = pl.program_id(1)
    _cast_column_tiles(wa_ref, wa_bf_ref)
    _cast_column_tiles(wb_ref, wb_bf_ref)
    wc_bf_ref[...] = wc_ref[...].astype(BF16)

    @pl.when((b == 0) & (j == 0))
    def _():
        _init_maps(lm_ref, tri_ref, True)

    @pl.when(j == 0)
    def _():
        st_ref[...] = s0_ref[0]

    _, lb_b = _lower_bounds(lbl_ref)
    w = HG_WIDTH
    hgn = hgn_ref[...]
    mod = lambda k: mods_ref[pl.ds(b, 1), k * D_MODEL:(k + 1) * D_MODEL]
    scans = {}
    for u in reversed(range(STEP_TILES)):
        rows = slice(u * TT, (u + 1) * TT)
        k, cum = _gla_gates(fz_ref[0, rows, 0:w], lb_b, tri_ref[...])
        scans[u] = _gla_tile(pk_ref[0, rows, 0:w].astype(F32), k, cum, pk_ref[0, rows, w:2 * w], st_ref, lm_ref[...],
                             True)
    o = fz_ref[0, :, w:2 * w] + jnp.concatenate([scans[u] for u in range(STEP_TILES)], axis=0)
    heads = [_rms(o[:, h * HG_VDIM:(h + 1) * HG_VDIM], hgn) for h in range(HG_HEADS)]
    g = pk_ref[0, :, 2 * w:3 * w].astype(F32)
    ohg = jnp.concatenate(heads, axis=1) * _silu(g)
    cat = jnp.concatenate([ohg.astype(BF16), pk_ref[0, :, 3 * w:4 * w]], axis=1)
    mx = _dot(cat, wo_ref[...])
    x1 = x_ref[0] + mod(2) * mx
    h2 = _rms(x1, nf_ref[...]) * (1.0 + mod(4)) + mod(3)
    for c in range(ACC_ROWS):
        dst = pl.ds(c, STEP_TOK, stride=ACC_ROWS)
        x1_ref[0, dst, :] = x1[:, c * LANES:(c + 1) * LANES]
        hr_ref[0, dst, :] = h2[:, c * LANES:(c + 1) * LANES]
    r = _dot(jnp.concatenate(_split(h2), axis=0), wr_ref[...])
    lg = r[:STEP_TOK, :LANES] + r[:STEP_TOK, LANES:] + r[STEP_TOK:, :LANES]
    lgt = lg.T[0:N_EXPERTS, :]
    e = jnp.exp(lgt - jnp.max(lgt, axis=0, keepdims=True))
    at_ref[0] = e / jnp.sum(e, axis=0, keepdims=True)


def _mix_bwd(x, pk, fz, mods, lbl, hgn, w_out_bf, nf, wr_pad, s0_b, wa, wb, wc, col_tile):
    nb, t, _ = x.shape
    nt = t // STEP_TOK
    cast_spec = pl.BlockSpec((1,) + wc.shape[1:], lambda b, j: (b * nt + j, 0, 0))
    ne, rows, cols = wa.shape
    per_expert = nb * nt // ne
    step_rows = rows // per_expert
    cast_in = pl.BlockSpec((1, step_rows, cols), lambda b, j: ((b * nt + j) // per_expert, (b * nt + j) % per_expert, 0))
    cast_out = pl.BlockSpec((1, cols // col_tile, step_rows, col_tile),
                            lambda b, j: ((b * nt + j) // per_expert, 0, (b * nt + j) % per_expert, 0))
    cast_shape = jax.ShapeDtypeStruct((ne, cols // col_tile, rows, col_tile), BF16)
    rev = lambda b, j: (b, nt - 1 - j, 0)
    tspec = lambda a: pl.BlockSpec((1, STEP_TOK, a.shape[2]), rev)
    xspec = pl.BlockSpec((1, STEP_TOK, D_MODEL), rev)
    full2 = lambda a: pl.BlockSpec(a.shape, lambda b, j: (0, 0))
    return pl.pallas_call(
        _mix_bwd_body,
        grid=(nb, nt),
        in_specs=[xspec, tspec(pk), tspec(fz),
                  full2(mods), pl.BlockSpec(lbl.shape, lambda b, j: (0, 0, 0)), full2(hgn),
                  full2(w_out_bf), full2(nf), full2(wr_pad),
                  pl.BlockSpec((1, HG_HEADS, HG_VDIM, HG_KDIM), lambda b, j: (b, 0, 0, 0)),
                  cast_in, cast_in, cast_spec],
        out_specs=[pl.BlockSpec((1, STEP_TOK * ACC_ROWS, LANES), rev),
                   pl.BlockSpec((1, STEP_TOK * ACC_ROWS, LANES), rev),
                   pl.BlockSpec((1, N_EXPERTS, STEP_TOK), lambda b, j: (b, 0, nt - 1 - j)),
                   cast_out, cast_out, cast_spec],
        out_shape=[jax.ShapeDtypeStruct((nb, t * ACC_ROWS, LANES), F32),
                   jax.ShapeDtypeStruct((nb, t * ACC_ROWS, LANES), F32),
                   jax.ShapeDtypeStruct((nb, N_EXPERTS, t), F32),
                   cast_shape, cast_shape, jax.ShapeDtypeStruct(wc.shape, BF16)],
        scratch_shapes=[pltpu.VMEM((HG_HEADS, HG_VDIM, HG_KDIM), F32), pltpu.VMEM((TT, TT), I32),
                        pltpu.VMEM((TT, TT), BF16)],
        compiler_params=pltpu.CompilerParams(dimension_semantics=("arbitrary", "arbitrary"),
                                             vmem_limit_bytes=VMEM_LIMIT),
        name="mix_bwd",
    )(x, pk, fz, mods, lbl, hgn, w_out_bf, nf, wr_pad, s0_b, wa, wb, wc)


def _route_body(at_ref, idx_ref, gate_ref, *, cap):
    ne, t = at_ref.shape[1], at_ref.shape[2]
    blk = 2 * LANES
    aff = at_ref[0]

    def count(m):
        return jnp.sum(jnp.where(m, 1.0, 0.0), axis=1, keepdims=True)

    def enough(cand):
        return count(aff >= lax.bitcast_convert_type(cand, F32)) >= cap

    def bit_step(i, thr):
        hi = jnp.left_shift(jnp.int32(1), 30 - 2 * i)
        lo = jnp.right_shift(hi, 1)
        both, upper, lower = thr | hi | lo, thr | hi, thr | lo
        return jnp.where(enough(both), both, jnp.where(enough(upper), upper, jnp.where(enough(lower), lower, thr)))

    thr = lax.fori_loop(0, 16, bit_step, jnp.zeros((ne, 1), I32))
    gt = aff >= lax.bitcast_convert_type(thr + 1, F32)
    eq = (aff >= lax.bitcast_convert_type(thr, F32)) & jnp.logical_not(gt)
    need = cap - count(gt)

    ii = lax.broadcasted_iota(I32, (blk, blk), 0)
    jj = lax.broadcasted_iota(I32, (blk, blk), 1)
    upper = jnp.where(ii <= jj, 1.0, 0.0).astype(BF16)

    def prefix_blocks(m):
        mb = jnp.where(m, 1.0, 0.0).astype(BF16)
        carry = jnp.zeros((ne, 1), F32)
        out = []
        for kb in range(t // blk):
            p = _dot(mb[:, kb * blk:(kb + 1) * blk], upper) + carry
            out.append(p)
            carry = p[:, blk - 1:blk]
        return out

    eq_rank = jnp.concatenate(prefix_blocks(eq), axis=1)
    sel = gt | (eq & (eq_rank <= need))
    rank = jnp.concatenate(prefix_blocks(sel), axis=1).astype(I32)

    tok = lax.broadcasted_iota(I32, (ne, t), 1)
    x = jnp.where(sel, tok, -1)
    d = jnp.where(sel, tok - rank + 1, 0)
    g = aff
    for k in range(t.bit_length() - 1):
        left = lambda a: pltpu.roll(a, t - (1 << k), axis=1)
        ds = left(d)
        take = ((ds >> k) & 1) == 1
        keep = ((d >> k) & 1) == 0
        x = jnp.where(take, left(x), jnp.where(keep, x, -1))
        g = jnp.where(take, left(g), g)
        d = jnp.where(take, ds, jnp.where(keep, d, 0))
    idx_ref[0] = x[:, :cap]
    gate_ref[0] = g[:, :cap]


def _route(aff_t, cap):
    nb, ne, t = aff_t.shape
    spec = pl.BlockSpec((1, ne, cap), lambda b: (b, 0, 0))
    return pl.pallas_call(
        functools.partial(_route_body, cap=cap),
        grid=(nb,),
        in_specs=[pl.BlockSpec((1, ne, t), lambda b: (b, 0, 0))],
        out_specs=[spec, spec],
        out_shape=[jax.ShapeDtypeStruct((nb, ne, cap), I32), jax.ShapeDtypeStruct((nb, ne, cap), F32)],
        compiler_params=pltpu.CompilerParams(vmem_limit_bytes=VMEM_LIMIT),
        name="route",
    )(aff_t)


def _moe_body(idx_ref, gate_ref, hr_hbm, wg_ref, wu_ref, wd_ref, x1_hbm, mods_ref, nfin_ref,
              out_ref, acc_ref, hr_ref, rowg_ref, rows_ref, xs_ref, ys_ref, seed_sem, hr_sem, *, cap, n_ff, n_epi):
    b = pl.program_id(0)
    s = pl.program_id(1)
    n_work = N_EXPERTS * n_ff
    e = s // n_ff
    ff = s % n_ff
    slot = e % 2
    group = SUBLANES

    def list_base(expert):
        return (b * N_EXPERTS + expert) * cap

    def gather(base, j0, n):
        for u in range(n):
            tkn = idx_ref[base + j0 + u]
            rowg_ref[pl.ds(pl.multiple_of((j0 + u) * ACC_ROWS, ACC_ROWS), ACC_ROWS), :] = (
                hr_ref[pl.ds(pl.multiple_of(tkn * ACC_ROWS, ACC_ROWS), ACC_ROWS), :])

    def scatter(base, j0, n):
        for g0 in range(0, n, group):
            dst = []
            val = []
            for u in range(g0, g0 + group):
                r = pl.multiple_of(idx_ref[base + j0 + u] * ACC_ROWS, ACC_ROWS)
                dst.append(r)
                val.append(acc_ref[pl.ds(r, ACC_ROWS), :] + gate_ref[base + j0 + u]
                           * rows_ref[pl.ds(pl.multiple_of((j0 + u) * ACC_ROWS, ACC_ROWS), ACC_ROWS), :])
            for r, v in zip(dst, val):
                acc_ref[pl.ds(r, ACC_ROWS), :] = v

    def in_groups(fn, base):
        def step(jo, carry):
            fn(base, jo * group, group)
            return carry
        lax.fori_loop(0, cap // group, step, 0)

    def lay_out_input(to_slot):
        for c in range(ACC_ROWS):
            xs_ref[to_slot, :, c * LANES:(c + 1) * LANES] = (
                rowg_ref[pl.ds(c, cap, stride=ACC_ROWS), :].astype(BF16))

    def lay_out_output():
        g2 = mods_ref[pl.ds(b, 1), 5 * D_MODEL:6 * D_MODEL]
        for c in range(ACC_ROWS):
            cols = slice(c * LANES, (c + 1) * LANES)
            rows_ref[pl.ds(c, cap, stride=ACC_ROWS), :] = ys_ref[:, cols] * g2[:, cols]

    chunk_rows = EPI_TOK * ACC_ROWS

    def seed_copy(sample, chunk):
        rows = pl.ds(chunk * chunk_rows, chunk_rows)
        return pltpu.make_async_copy(x1_hbm.at[sample, rows], acc_ref.at[rows], seed_sem.at[chunk])

    def expert_tile():
        half = cap // 2
        for m in range(2):
            rows = slice(m * half, (m + 1) * half)
            xs = xs_ref[slot, rows, :]
            hid = _silu(_dot(xs, wg_ref[0, 0])) * _dot(xs, wu_ref[0, 0])
            ys_ref[rows, :] = _dot(hid.astype(BF16), wd_ref[0]) + jnp.where(ff == 0, 0.0, ys_ref[rows, :])

    def input_copy(sample):
        return pltpu.make_async_copy(hr_hbm.at[sample], hr_ref, hr_sem.at[0])

    @pl.when((s == 0) & (b == 0))
    def _():
        input_copy(0).start()
        for chunk in range(n_epi):
            seed_copy(0, chunk).start()

    @pl.when(s == 0)
    def _():
        input_copy(b).wait()
        ys_ref[...] = jnp.zeros_like(ys_ref)
        in_groups(gather, list_base(0))
        lay_out_input(0)

    @pl.when(s == n_ff - 1)
    def _():
        for chunk in range(n_epi):
            seed_copy(b, chunk).wait()

    @pl.when((s < n_work) & (ff == 0))
    def _():
        gather(list_base(jnp.minimum(e + 1, N_EXPERTS - 1)), 0, cap)
        lay_out_output()
        expert_tile()

    if n_ff > 2:
        @pl.when((s < n_work) & (ff > 0) & (ff < n_ff - 1))
        def _():
            expert_tile()

    @pl.when((s < n_work) & (ff == n_ff - 1))
    def _():
        lay_out_input(1 - slot)
        scatter(list_base(jnp.maximum(e - 1, 0)), 0, cap)
        expert_tile()

    @pl.when(s == n_work)
    def _():
        @pl.when(b + 1 < pl.num_programs(0))
        def _():
            input_copy(b + 1).start()

        lay_out_output()
        in_groups(scatter, list_base(N_EXPERTS - 1))

    @pl.when(s >= n_work)
    def _():
        chunk = s - n_work
        row0 = chunk * chunk_rows
        tot = jnp.concatenate([acc_ref[pl.ds(row0 + c, EPI_TOK, stride=ACC_ROWS), :] for c in range(ACC_ROWS)], axis=1)
        out_ref[0] = _rms(tot, nfin_ref[...])

        @pl.when(b + 1 < pl.num_programs(0))
        def _():
            seed_copy(b + 1, chunk).start()


def _moe(idx_flat, gate_flat, hr, wg, wu, wd, x1, mods, nfin, cap):
    nb, t = x1.shape[0], x1.shape[1] // ACC_ROWS
    n_ff = wg.shape[1]
    n_work = N_EXPERTS * n_ff
    n_epi = t // EPI_TOK

    def w_idx(s):
        sc = jnp.minimum(s, n_work - 1)
        return sc // n_ff, sc % n_ff

    def epi(b, s, *_):
        return (b, jnp.maximum(s - n_work, 0), 0)

    grid_spec = pltpu.PrefetchScalarGridSpec(
        num_scalar_prefetch=2,
        grid=(nb, n_work + n_epi),
        in_specs=[pl.BlockSpec(memory_space=pl.ANY),
                  pl.BlockSpec((1, 1, D_MODEL, DFF_TILE), lambda b, s, *_: w_idx(s) + (0, 0)),
                  pl.BlockSpec((1, 1, D_MODEL, DFF_TILE), lambda b, s, *_: w_idx(s) + (0, 0)),
                  pl.BlockSpec((1, DFF_TILE, D_MODEL), lambda b, s, *_: (w_idx(s)[0], w_idx(s)[1], 0)),
                  pl.BlockSpec(memory_space=pl.ANY),
                  pl.BlockSpec(mods.shape, lambda b, s, *_: (0, 0)),
                  pl.BlockSpec((1, D_MODEL), lambda b, s, *_: (0, 0))],
        out_specs=pl.BlockSpec((1, EPI_TOK, D_MODEL), epi),
        scratch_shapes=[pltpu.VMEM((t * ACC_ROWS, LANES), F32),
                        pltpu.VMEM((t * ACC_ROWS, LANES), F32),
                        pltpu.VMEM((cap * ACC_ROWS, LANES), F32),
                        pltpu.VMEM((cap * ACC_ROWS, LANES), F32),
                        pltpu.VMEM((2, cap, D_MODEL), BF16),
                        pltpu.VMEM((cap, D_MODEL), F32),
                        pltpu.SemaphoreType.DMA((n_epi,)),
                        pltpu.SemaphoreType.DMA((1,))])
    return pl.pallas_call(
        functools.partial(_moe_body, cap=cap, n_ff=n_ff, n_epi=n_epi),
        grid_spec=grid_spec,
        out_shape=jax.ShapeDtypeStruct((nb, t, D_MODEL), F32),
        compiler_params=pltpu.CompilerParams(dimension_semantics=("arbitrary", "arbitrary"),
                                             vmem_limit_bytes=VMEM_LIMIT),
        name="moe",
    )(idx_flat, gate_flat, hr, wg, wu, wd, x1, mods, nfin)


def kernel(x, c, ctx, c_ctx, w_ada, b_ada, norm_mix, norm_ffn, w_in, lb_logits, hg_norm, conv_w,
           w_out, w_router, w_gate, w_up, w_down, norm_final):
    nb, t, d = x.shape
    assert d == D_MODEL and w_ada.shape[0] == 1 and nb < MOD_ROWS and t % STEP_TOK == 0 and t % FWD_TOK == 0 and TT % GRID_W == 0
    assert w_in.shape[2] == 3 * HG_KEYS + 2 * HG_WIDTH + 3 * SC_WIDTH and w_router.shape[2] == N_EXPERTS
    cap = EC_CAPACITY * t // N_EXPERTS
    assert cap % LANES == 0 and w_gate.shape[3] % DFF_TILE == 0 and w_gate.shape[3] // DFF_TILE >= 2
    assert (nb * (t // STEP_TOK)) % N_EXPERTS == 0 and (D_MODEL * N_EXPERTS) % (nb * (t // STEP_TOK)) == 0

    cc = jnp.concatenate([c, c_ctx[None, :], jnp.zeros((MOD_ROWS - nb - 1, d), F32)], axis=0)
    w_in_bf = w_in[0].astype(BF16)
    w_out_bf = w_out[0].astype(BF16)
    wr_hi, wr_lo = _split(jnp.pad(w_router[0], ((0, 0), (0, LANES - N_EXPERTS))))
    wr_pad = jnp.concatenate([wr_hi, wr_lo], axis=1)
    nm, nf, nfin = norm_mix[0][None, :], norm_ffn[0][None, :], norm_final[None, :]
    hgn = hg_norm[0][None, :]

    mods = _adaln(cc, w_ada[0], b_ada[0][None, :])
    s0_f, s0_b = _ctx_states(ctx, mods, nm, w_in_bf, lb_logits, ctx_row=nb)
    steps = nb * (t // STEP_TOK)
    sliced = lambda w: w.reshape((steps, w.shape[0] * w.shape[1] // steps) + w.shape[2:])
    pk, fz = _mix_fwd(x, mods, nm, w_in_bf, lb_logits, conv_w[0], s0_f)
    x1, hr, aff_t, wg, wu, wd = _mix_bwd(x, pk, fz, mods, lb_logits, hgn, w_out_bf, nf, wr_pad, s0_b,
                                         w_gate[0], w_up[0], sliced(w_down[0]), DFF_TILE)
    wd = wd.reshape(w_down[0].shape)
    idx, gates = _route(aff_t, cap)
    return _moe(idx.reshape(-1), gates.reshape(-1), hr, wg, wu, wd, x1, mods, nfin, cap)
```

```python
import functools

import jax
import jax.numpy as jnp
from jax import lax
from jax.experimental import pallas as pl
from jax.experimental.pallas import tpu as pltpu

F32 = jnp.float32
BF16 = jnp.bfloat16
I32 = jnp.int32

D_MODEL = 1024
N_MOD = 6
HG_HEADS = 4
HG_KDIM = 128
HG_VDIM = 128
HG_KEYS = HG_HEADS * HG_KDIM
HG_WIDTH = HG_HEADS * HG_VDIM
SC_WIDTH = D_MODEL - HG_WIDTH
GRID_W = 64
CHUNK = 32
N_EXPERTS = 16
EC_CAPACITY = 2
EPS = 1e-6

LANES = 128
SUBLANES = 8
VMEM_LIMIT = 62 * 1024 * 1024

TT = 256
STEP_TILES = 2
STEP_TOK = TT * STEP_TILES
NCH = TT // CHUNK
MOD_ROWS = 16
DFF_TILE = 1024
EPI_TOK = 512
ACC_ROWS = D_MODEL // LANES


def _dot(a, b):
    return jnp.dot(a, b, preferred_element_type=F32)


def _dot_nt(a, b):
    return lax.dot_general(a, b, (((1,), (1,)), ((), ())), preferred_element_type=F32)


def _dot_tn(a, b):
    return lax.dot_general(a, b, (((0,), (0,)), ((), ())), preferred_element_type=F32)


def _split(x):
    hi = x.astype(BF16)
    lo = (x - hi.astype(F32)).astype(BF16)
    return hi, lo


def _rms(x, gain):
    return x * lax.rsqrt(jnp.mean(x * x, axis=-1, keepdims=True) + EPS) * gain


def _silu(x):
    return x * jax.nn.sigmoid(x)


def _lower_bounds(lbl_ref):
    rows = [lbl_ref[l] for l in range(lbl_ref.shape[0])]
    m = functools.reduce(jnp.maximum, rows)
    es = [jnp.exp(r - m) for r in rows]
    sm0 = es[0] / functools.reduce(lambda a, b: a + b, es)
    return sm0[0:1, :], sm0[1:2, :]


def _ada_body(c_ref, w_ref, b_ref, o_ref):
    s_hi, s_lo = _split(_silu(c_ref[...]))
    w_hi, w_lo = _split(w_ref[...])
    o_ref[...] = _dot(s_hi, w_hi) + _dot(s_hi, w_lo) + _dot(s_lo, w_hi) + b_ref[...]


def _adaln(cc, w, b):
    n = w.shape[1]
    tn = n // 4
    return pl.pallas_call(
        _ada_body,
        grid=(n // tn,),
        in_specs=[pl.BlockSpec((MOD_ROWS, D_MODEL), lambda i: (0, 0)),
                  pl.BlockSpec((D_MODEL, tn), lambda i: (0, i)),
                  pl.BlockSpec((1, tn), lambda i: (0, i))],
        out_specs=pl.BlockSpec((MOD_ROWS, tn), lambda i: (0, i)),
        out_shape=jax.ShapeDtypeStruct((MOD_ROWS, n), F32),
        compiler_params=pltpu.CompilerParams(vmem_limit_bytes=VMEM_LIMIT),
        name="adaln",
    )(cc, w, b)


def _ctx_body(ctx_ref, mods_ref, nm_ref, wf_ref, wb_ref, wi_ref, lbl_ref, sf_ref, sb_ref, *, ctx_row):
    x = ctx_ref[0]
    n = x.shape[0]
    csh = mods_ref[ctx_row:ctx_row + 1, 0:D_MODEL]
    csc = mods_ref[ctx_row:ctx_row + 1, D_MODEL:2 * D_MODEL]
    hb = (_rms(x, nm_ref[...]) * (1.0 + csc) + csh).astype(BF16)
    zf = _dot(hb, wf_ref[...])
    zb = _dot(hb, wb_ref[...])
    vi = _dot(hb, wi_ref[...]).astype(BF16)
    lb_f, lb_b = _lower_bounds(lbl_ref)
    f_f = lb_f + (1.0 - lb_f) * jax.nn.sigmoid(zf)
    f_b = lb_b + (1.0 - lb_b) * jax.nn.sigmoid(zb)
    ii = lax.broadcasted_iota(I32, (n, n), 0)
    jj = lax.broadcasted_iota(I32, (n, n), 1)
    after = jnp.where(jj > ii, 1.0, 0.0).astype(BF16)
    before = jnp.where(jj < ii, 1.0, 0.0).astype(BF16)
    hi, lo = _split(jnp.log(f_f))
    kd_f = ((1.0 - f_f) * jnp.exp(_dot(after, hi) + _dot(after, lo))).astype(BF16)
    hi, lo = _split(jnp.log(f_b))
    kd_b = ((1.0 - f_b) * jnp.exp(_dot(before, hi) + _dot(before, lo))).astype(BF16)
    for h in range(HG_HEADS):
        ks = slice(h * HG_KDIM, (h + 1) * HG_KDIM)
        vs = slice(h * HG_VDIM, (h + 1) * HG_VDIM)
        sf_ref[0, h] = _dot_tn(vi[:, vs], kd_f[:, ks])
        sb_ref[0, h] = _dot_tn(vi[:, vs], kd_b[:, ks])


def _ctx_states(ctx, mods, nm, w_in_bf, lbl, ctx_row):
    nb, n, _ = ctx.shape
    st = jax.ShapeDtypeStruct((nb, HG_HEADS, HG_VDIM, HG_KDIM), F32)
    st_spec = pl.BlockSpec((1, HG_HEADS, HG_VDIM, HG_KDIM), lambda b: (b, 0, 0, 0))
    wcol = lambda k: pl.BlockSpec((D_MODEL, HG_KEYS), lambda b, k=k: (0, k))
    return pl.pallas_call(
        functools.partial(_ctx_body, ctx_row=ctx_row),
        grid=(nb,),
        in_specs=[pl.BlockSpec((1, n, D_MODEL), lambda b: (b, 0, 0)),
                  pl.BlockSpec(mods.shape, lambda b: (0, 0)),
                  pl.BlockSpec((1, D_MODEL), lambda b: (0, 0)),
                  wcol(1), wcol(2), wcol(3),
                  pl.BlockSpec(lbl.shape, lambda b: (0, 0, 0))],
        out_specs=[st_spec, st_spec],
        out_shape=[st, st],
        compiler_params=pltpu.CompilerParams(vmem_limit_bytes=VMEM_LIMIT),
        name="ctx_states",
    )(ctx, mods, nm, w_in_bf, w_in_bf, w_in_bf, lbl)


def _tile_positions(reverse):
    ii = lax.broadcasted_iota(I32, (TT, TT), 0)
    jj = lax.broadcasted_iota(I32, (TT, TT), 1)
    if reverse:
        ii = TT - 1 - ii
        jj = TT - 1 - jj
    sh0 = CHUNK.bit_length() - 1
    return ii, jj, ii >> sh0, jj >> sh0


def _init_maps(lm_ref, tri_ref, reverse):
    lm_ref[...] = _level_map(reverse)
    ii, jj, ci, cj = _tile_positions(reverse)
    tri_ref[...] = jnp.where((ci == cj) & (jj <= ii), 1.0, 0.0).astype(BF16)


def _level_map(reverse):
    ii, jj, ci, cj = _tile_positions(reverse)
    lm = jnp.where(((ci >> 1) == (cj >> 1)) & (jj <= ii), 1, 0)
    for lvl in range(2, NCH.bit_length()):
        m = ((ci >> lvl) == (cj >> lvl)) & (((ci >> (lvl - 1)) & 1) == 1) & (((cj >> (lvl - 1)) & 1) == 0)
        lm = jnp.where(m, lvl, lm)
    return lm


def _gla_gates(zf, lb, tri):
    f = lb + (1.0 - lb) * jax.nn.sigmoid(zf)
    hi, lo = _split(jnp.log(f))
    return 1.0 - f, _dot(tri, hi) + _dot(tri, lo)


def _gla_tile(q, k, cum, v, st_ref, lm, reverse):
    n_lvl = NCH.bit_length() - 2
    phys = list(range(NCH - 1, -1, -1)) if reverse else list(range(NCH))
    last = 0 if reverse else CHUNK - 1
    tot = [cum[a * CHUNK + last:a * CHUNK + last + 1, :] for a in range(NCH)]
    bnd = [jnp.zeros_like(tot[0])]
    for c in range(NCH):
        bnd.append(bnd[-1] + tot[phys[c]])
    first = [None] * NCH
    for c in range(NCH):
        first[phys[c]] = c % 2 == 0
    ref = jnp.concatenate([cum[a * CHUNK:(a + 1) * CHUNK] - tot[a] if first[a] else cum[a * CHUNK:(a + 1) * CHUNK]
                           for a in range(NCH)], axis=0)
    qp = q * jnp.exp(ref)
    kp = k * jnp.exp(-ref)
    qs = [[None] * NCH for _ in range(n_lvl + 1)]
    ks = [[None] * NCH for _ in range(n_lvl + 1)]
    for c in range(NCH):
        a = phys[c]
        rows = slice(a * CHUNK, (a + 1) * CHUNK)
        to_start = tot[a] if first[a] else 0.0
        to_end = 0.0 if first[a] else tot[a]
        zero = jnp.zeros_like(qp[rows])
        for lvl in range(1, n_lvl + 1):
            s = 1 << lvl
            mid = (c // (2 * s)) * 2 * s + s
            right = (c // s) % 2 == 1
            qs[lvl - 1][a] = qp[rows] * jnp.exp(bnd[c] - bnd[mid] + to_start) if right else zero
            ks[lvl - 1][a] = zero if right else kp[rows] * jnp.exp(bnd[mid] - bnd[c + 1] + to_end)
        qs[n_lvl][a] = qp[rows] * jnp.exp(bnd[c] + to_start)
        ks[n_lvl][a] = kp[rows] * jnp.exp(bnd[NCH] - bnd[c + 1] + to_end)
    qs = [jnp.concatenate(p, axis=0).astype(BF16) for p in qs]
    ks = [jnp.concatenate(p, axis=0).astype(BF16) for p in ks]
    qp = qp.astype(BF16)
    kp = kp.astype(BF16)
    vb = v.astype(BF16)
    dec = jnp.exp(bnd[NCH])
    outs = []
    for h in range(HG_HEADS):
        kk = slice(h * HG_KDIM, (h + 1) * HG_KDIM)
        vv = slice(h * HG_VDIM, (h + 1) * HG_VDIM)
        sc = jnp.where(lm == 1, _dot_nt(qp[:, kk], kp[:, kk]), 0.0)
        for lvl in range(n_lvl):
            sc = jnp.where(lm == lvl + 2, _dot_nt(qs[lvl][:, kk], ks[lvl][:, kk]), sc)
        st = st_ref[h]
        outs.append(_dot(sc.astype(BF16), vb[:, vv]) + _dot_nt(qs[n_lvl][:, kk], st.astype(BF16)))
        st_ref[h] = st * dec[:, kk] + _dot_tn(vb[:, vv], ks[n_lvl][:, kk])
    return jnp.concatenate(outs, axis=1)


def _mix_fwd_body(x_ref, mods_ref, nm_ref, w_ref, lbl_ref, cw_ref, s0_ref, wa_ref, wb_ref, wc_ref,
                  pk_ref, fz_ref, wa_bf_ref, wb_bf_ref, wc_bf_ref, st_ref, lm_ref, tri_ref):
    b = pl.program_id(0)
    j = pl.program_id(1)

    @pl.when((b == 0) & (j == 0))
    def _():
        _init_maps(lm_ref, tri_ref, False)

    @pl.when(j == 0)
    def _():
        st_ref[...] = s0_ref[0]

    sh = mods_ref[pl.ds(b, 1), 0:D_MODEL]
    sc = mods_ref[pl.ds(b, 1), D_MODEL:2 * D_MODEL]
    w = HG_KEYS
    lb_f, _ = _lower_bounds(lbl_ref)
    cw = cw_ref[...]
    col = lax.broadcasted_iota(I32, (TT, 1), 0) & (GRID_W - 1)

    def cast_column_tiles(src_ref, dst_ref):
        tile = dst_ref.shape[3]
        for f in range(dst_ref.shape[1]):
            dst_ref[0, f] = src_ref[0, :, f * tile:(f + 1) * tile].astype(BF16)

    for u in range(STEP_TILES):
        rows = slice(u * TT, (u + 1) * TT)
        hx = (_rms(x_ref[0, rows], nm_ref[...]) * (1.0 + sc) + sh).astype(BF16)
        proj = lambda k: _dot(hx, w_ref[:, k * w:(k + 1) * w])
        gate_b = proj(5)
        if u == 0:
            cast_column_tiles(wa_ref, wa_bf_ref)
        u_conv = proj(6) * proj(7)
        if u == 0:
            cast_column_tiles(wb_ref, wb_bf_ref)
        zff = proj(1)
        up = jnp.where(col != 0, pltpu.roll(u_conv, 1, axis=0), 0.0)
        un = jnp.where(col != GRID_W - 1, pltpu.roll(u_conv, TT - 1, axis=0), 0.0)
        pk_ref[0, rows, 3 * w:4 * w] = (gate_b * (cw[0:1] * up + cw[1:2] * u_conv + cw[2:3] * un)).astype(BF16)
        q = proj(0)
        if u == STEP_TILES - 1:
            wc_bf_ref[...] = wc_ref[...].astype(BF16)
        k, cum = _gla_gates(zff, lb_f, tri_ref[...])
        vi = proj(3)
        fz_ref[0, rows, 0:w] = proj(2)
        pk_ref[0, rows, 2 * w:3 * w] = proj(4).astype(BF16)
        fz_ref[0, rows, w:2 * w] = _gla_tile(q, k, cum, vi, st_ref, lm_ref[...], False)
        pk_ref[0, rows, 0:w] = q.astype(BF16)
        pk_ref[0, rows, w:2 * w] = vi.astype(BF16)


def _mix_fwd(x, mods, nm, w_in_bf, lbl, cw, s0_f, wa, wb, wc, col_tile):
    nb, t, _ = x.shape
    nt = t // STEP_TOK
    ne, rows, cols = wa.shape
    per_expert = nb * nt // ne
    step_rows = rows // per_expert
    tok = lambda n, dt: jax.ShapeDtypeStruct((nb, t, n * HG_WIDTH), dt)
    tspec = lambda n: pl.BlockSpec((1, STEP_TOK, n * HG_WIDTH), lambda b, j: (b, j, 0))
    cast_in = pl.BlockSpec((1, step_rows, cols), lambda b, j: ((b * nt + j) // per_expert, (b * nt + j) % per_expert, 0))
    cast_out = pl.BlockSpec((1, cols // col_tile, step_rows, col_tile),
                            lambda b, j: ((b * nt + j) // per_expert, 0, (b * nt + j) % per_expert, 0))
    cast_shape = jax.ShapeDtypeStruct((ne, cols // col_tile, rows, col_tile), BF16)
    plain = pl.BlockSpec((1,) + wc.shape[1:], lambda b, j: (b * nt + j, 0, 0))
    return pl.pallas_call(
        _mix_fwd_body,
        grid=(nb, nt),
        in_specs=[pl.BlockSpec((1, STEP_TOK, D_MODEL), lambda b, j: (b, j, 0)),
                  pl.BlockSpec(mods.shape, lambda b, j: (0, 0)),
                  pl.BlockSpec((1, D_MODEL), lambda b, j: (0, 0)),
                  pl.BlockSpec(w_in_bf.shape, lambda b, j: (0, 0)),
                  pl.BlockSpec(lbl.shape, lambda b, j: (0, 0, 0)),
                  pl.BlockSpec(cw.shape, lambda b, j: (0, 0)),
                  pl.BlockSpec((1, HG_HEADS, HG_VDIM, HG_KDIM), lambda b, j: (b, 0, 0, 0)),
                  cast_in, cast_in, plain],
        out_specs=[tspec(4), tspec(2), cast_out, cast_out, plain],
        out_shape=[tok(4, BF16), tok(2, F32), cast_shape, cast_shape, jax.ShapeDtypeStruct(wc.shape, BF16)],
        scratch_shapes=[pltpu.VMEM((HG_HEADS, HG_VDIM, HG_KDIM), F32), pltpu.VMEM((TT, TT), I32),
                        pltpu.VMEM((TT, TT), BF16)],
        compiler_params=pltpu.CompilerParams(dimension_semantics=("arbitrary", "arbitrary"),
                                             vmem_limit_bytes=VMEM_LIMIT),
        name="mix_fwd",
    )(x, mods, nm, w_in_bf, lbl, cw, s0_f, wa, wb, wc)


def _mix_bwd_body(x_ref, pk_ref, fz_ref, mods_ref, lbl_ref, hgn_ref,
                  wo_ref, nf_ref, wr_ref, s0_ref,
                  x1_ref, hr_ref, at_ref, st_ref, lm_ref, tri_ref):
    b = pl.program_id(0)
    j = pl.program_id(1)

    @pl.when((b == 0) & (j == 0))
    def _():
        _init_maps(lm_ref, tri_ref, True)

    @pl.when(j == 0)
    def _():
        st_ref[...] = s0_ref[0]

    _, lb_b = _lower_bounds(lbl_ref)
    w = HG_WIDTH
    hgn = hgn_ref[...]
    mod = lambda k: mods_ref[pl.ds(b, 1), k * D_MODEL:(k + 1) * D_MODEL]
    scans = {}
    for u in reversed(range(STEP_TILES)):
        rows = slice(u * TT, (u + 1) * TT)
        k, cum = _gla_gates(fz_ref[0, rows, 0:w], lb_b, tri_ref[...])
        scans[u] = _gla_tile(pk_ref[0, rows, 0:w].astype(F32), k, cum, pk_ref[0, rows, w:2 * w], st_ref, lm_ref[...],
                             True)
    o = fz_ref[0, :, w:2 * w] + jnp.concatenate([scans[u] for u in range(STEP_TILES)], axis=0)
    heads = [_rms(o[:, h * HG_VDIM:(h + 1) * HG_VDIM], hgn) for h in range(HG_HEADS)]
    g = pk_ref[0, :, 2 * w:3 * w].astype(F32)
    ohg = jnp.concatenate(heads, axis=1) * _silu(g)
    cat = jnp.concatenate([ohg.astype(BF16), pk_ref[0, :, 3 * w:4 * w]], axis=1)
    mx = _dot(cat, wo_ref[...])
    x1 = x_ref[0] + mod(2) * mx
    h2 = _rms(x1, nf_ref[...]) * (1.0 + mod(4)) + mod(3)
    for c in range(ACC_ROWS):
        dst = pl.ds(c, STEP_TOK, stride=ACC_ROWS)
        x1_ref[0, dst, :] = x1[:, c * LANES:(c + 1) * LANES]
        hr_ref[0, dst, :] = h2[:, c * LANES:(c + 1) * LANES]
    r = _dot(jnp.concatenate(_split(h2), axis=0), wr_ref[...])
    lg = r[:STEP_TOK, :LANES] + r[:STEP_TOK, LANES:] + r[STEP_TOK:, :LANES]
    lgt = lg.T[0:N_EXPERTS, :]
    e = jnp.exp(lgt - jnp.max(lgt, axis=0, keepdims=True))
    at_ref[0] = e / jnp.sum(e, axis=0, keepdims=True)


def _mix_bwd(x, pk, fz, mods, lbl, hgn, w_out_bf, nf, wr_pad, s0_b):
    nb, t, _ = x.shape
    nt = t // STEP_TOK
    rev = lambda b, j: (b, nt - 1 - j, 0)
    tspec = lambda a: pl.BlockSpec((1, STEP_TOK, a.shape[2]), rev)
    xspec = pl.BlockSpec((1, STEP_TOK, D_MODEL), rev)
    full2 = lambda a: pl.BlockSpec(a.shape, lambda b, j: (0, 0))
    return pl.pallas_call(
        _mix_bwd_body,
        grid=(nb, nt),
        in_specs=[xspec, tspec(pk), tspec(fz),
                  full2(mods), pl.BlockSpec(lbl.shape, lambda b, j: (0, 0, 0)), full2(hgn),
                  full2(w_out_bf), full2(nf), full2(wr_pad),
                  pl.BlockSpec((1, HG_HEADS, HG_VDIM, HG_KDIM), lambda b, j: (b, 0, 0, 0))],
        out_specs=[pl.BlockSpec((1, STEP_TOK * ACC_ROWS, LANES), rev),
                   pl.BlockSpec((1, STEP_TOK * ACC_ROWS, LANES), rev),
                   pl.BlockSpec((1, N_EXPERTS, STEP_TOK), lambda b, j: (b, 0, nt - 1 - j))],
        out_shape=[jax.ShapeDtypeStruct((nb, t * ACC_ROWS, LANES), F32),
                   jax.ShapeDtypeStruct((nb, t * ACC_ROWS, LANES), F32),
                   jax.ShapeDtypeStruct((nb, N_EXPERTS, t), F32)],
        scratch_shapes=[pltpu.VMEM((HG_HEADS, HG_VDIM, HG_KDIM), F32), pltpu.VMEM((TT, TT), I32),
                        pltpu.VMEM((TT, TT), BF16)],
        compiler_params=pltpu.CompilerParams(dimension_semantics=("arbitrary", "arbitrary"),
                                             vmem_limit_bytes=VMEM_LIMIT),
        name="mix_bwd",
    )(x, pk, fz, mods, lbl, hgn, w_out_bf, nf, wr_pad, s0_b)


def _route_body(at_ref, idx_ref, gate_ref, *, cap):
    ne, t = at_ref.shape[1], at_ref.shape[2]
    blk = 2 * LANES
    aff = at_ref[0]

    def count(m):
        return jnp.sum(jnp.where(m, 1.0, 0.0), axis=1, keepdims=True)

    def enough(cand):
        return count(aff >= lax.bitcast_convert_type(cand, F32)) >= cap

    def bit_step(i, thr):
        hi = jnp.left_shift(jnp.int32(1), 30 - 2 * i)
        lo = jnp.right_shift(hi, 1)
        both, upper, lower = thr | hi | lo, thr | hi, thr | lo
        return jnp.where(enough(both), both, jnp.where(enough(upper), upper, jnp.where(enough(lower), lower, thr)))

    thr = lax.fori_loop(0, 16, bit_step, jnp.zeros((ne, 1), I32))
    gt = aff >= lax.bitcast_convert_type(thr + 1, F32)
    eq = (aff >= lax.bitcast_convert_type(thr, F32)) & jnp.logical_not(gt)
    need = cap - count(gt)

    ii = lax.broadcasted_iota(I32, (blk, blk), 0)
    jj = lax.broadcasted_iota(I32, (blk, blk), 1)
    upper = jnp.where(ii <= jj, 1.0, 0.0).astype(BF16)

    def prefix_blocks(m):
        mb = jnp.where(m, 1.0, 0.0).astype(BF16)
        carry = jnp.zeros((ne, 1), F32)
        out = []
        for kb in range(t // blk):
            p = _dot(mb[:, kb * blk:(kb + 1) * blk], upper) + carry
            out.append(p)
            carry = p[:, blk - 1:blk]
        return out

    eq_rank = jnp.concatenate(prefix_blocks(eq), axis=1)
    sel = gt | (eq & (eq_rank <= need))
    rank = jnp.concatenate(prefix_blocks(sel), axis=1).astype(I32)

    tok = lax.broadcasted_iota(I32, (ne, t), 1)
    x = jnp.where(sel, tok, -1)
    d = jnp.where(sel, tok - rank + 1, 0)
    g = aff
    for k in range(t.bit_length() - 1):
        left = lambda a: pltpu.roll(a, t - (1 << k), axis=1)
        ds = left(d)
        take = ((ds >> k) & 1) == 1
        keep = ((d >> k) & 1) == 0
        x = jnp.where(take, left(x), jnp.where(keep, x, -1))
        g = jnp.where(take, left(g), g)
        d = jnp.where(take, ds, jnp.where(keep, d, 0))
    idx_ref[0] = x[:, :cap]
    gate_ref[0] = g[:, :cap]


def _route(aff_t, cap):
    nb, ne, t = aff_t.shape
    spec = pl.BlockSpec((1, ne, cap), lambda b: (b, 0, 0))
    return pl.pallas_call(
        functools.partial(_route_body, cap=cap),
        grid=(nb,),
        in_specs=[pl.BlockSpec((1, ne, t), lambda b: (b, 0, 0))],
        out_specs=[spec, spec],
        out_shape=[jax.ShapeDtypeStruct((nb, ne, cap), I32), jax.ShapeDtypeStruct((nb, ne, cap), F32)],
        compiler_params=pltpu.CompilerParams(vmem_limit_bytes=VMEM_LIMIT),
        name="route",
    )(aff_t)


def _moe_body(idx_ref, gate_ref, hr_hbm, wg_ref, wu_ref, wd_ref, x1_hbm, mods_ref, nfin_ref,
              out_ref, acc_ref, hr_ref, rowg_ref, rows_ref, xs_ref, ys_ref, seed_sem, hr_sem, *, cap, n_ff, n_epi):
    b = pl.program_id(0)
    s = pl.program_id(1)
    n_work = N_EXPERTS * n_ff
    e = s // n_ff
    ff = s % n_ff
    slot = e % 2
    group = SUBLANES

    def list_base(expert):
        return (b * N_EXPERTS + expert) * cap

    def gather(base, j0, n):
        for u in range(n):
            tkn = idx_ref[base + j0 + u]
            rowg_ref[pl.ds(pl.multiple_of((j0 + u) * ACC_ROWS, ACC_ROWS), ACC_ROWS), :] = (
                hr_ref[pl.ds(pl.multiple_of(tkn * ACC_ROWS, ACC_ROWS), ACC_ROWS), :])

    def scatter(base, j0, n):
        for g0 in range(0, n, group):
            dst = []
            val = []
            for u in range(g0, g0 + group):
                r = pl.multiple_of(idx_ref[base + j0 + u] * ACC_ROWS, ACC_ROWS)
                dst.append(r)
                val.append(acc_ref[pl.ds(r, ACC_ROWS), :] + gate_ref[base + j0 + u]
                           * rows_ref[pl.ds(pl.multiple_of((j0 + u) * ACC_ROWS, ACC_ROWS), ACC_ROWS), :])
            for r, v in zip(dst, val):
                acc_ref[pl.ds(r, ACC_ROWS), :] = v

    def in_groups(fn, base):
        def step(jo, carry):
            fn(base, jo * group, group)
            return carry
        lax.fori_loop(0, cap // group, step, 0)

    def lay_out_input(to_slot):
        for c in range(ACC_ROWS):
            xs_ref[to_slot, :, c * LANES:(c + 1) * LANES] = (
                rowg_ref[pl.ds(c, cap, stride=ACC_ROWS), :].astype(BF16))

    def lay_out_output():
        g2 = mods_ref[pl.ds(b, 1), 5 * D_MODEL:6 * D_MODEL]
        for c in range(ACC_ROWS):
            cols = slice(c * LANES, (c + 1) * LANES)
            rows_ref[pl.ds(c, cap, stride=ACC_ROWS), :] = ys_ref[:, cols] * g2[:, cols]

    chunk_rows = EPI_TOK * ACC_ROWS

    def seed_copy(sample, chunk):
        rows = pl.ds(chunk * chunk_rows, chunk_rows)
        return pltpu.make_async_copy(x1_hbm.at[sample, rows], acc_ref.at[rows], seed_sem.at[chunk])

    def expert_tile():
        half = cap // 2
        for m in range(2):
            rows = slice(m * half, (m + 1) * half)
            xs = xs_ref[slot, rows, :]
            hid = _silu(_dot(xs, wg_ref[0, 0])) * _dot(xs, wu_ref[0, 0])
            ys_ref[rows, :] = _dot(hid.astype(BF16), wd_ref[0]) + jnp.where(ff == 0, 0.0, ys_ref[rows, :])

    def input_copy(sample):
        return pltpu.make_async_copy(hr_hbm.at[sample], hr_ref, hr_sem.at[0])

    @pl.when((s == 0) & (b == 0))
    def _():
        input_copy(0).start()
        for chunk in range(n_epi):
            seed_copy(0, chunk).start()

    @pl.when(s == 0)
    def _():
        input_copy(b).wait()
        ys_ref[...] = jnp.zeros_like(ys_ref)
        in_groups(gather, list_base(0))
        lay_out_input(0)

    @pl.when(s == n_ff - 1)
    def _():
        for chunk in range(n_epi):
            seed_copy(b, chunk).wait()

    @pl.when((s < n_work) & (ff == 0))
    def _():
        gather(list_base(jnp.minimum(e + 1, N_EXPERTS - 1)), 0, cap)
        lay_out_output()
        expert_tile()

    if n_ff > 2:
        @pl.when((s < n_work) & (ff > 0) & (ff < n_ff - 1))
        def _():
            expert_tile()

    @pl.when((s < n_work) & (ff == n_ff - 1))
    def _():
        lay_out_input(1 - slot)
        scatter(list_base(jnp.maximum(e - 1, 0)), 0, cap)
        expert_tile()

    @pl.when(s == n_work)
    def _():
        @pl.when(b + 1 < pl.num_programs(0))
        def _():
            input_copy(b + 1).start()

        lay_out_output()
        in_groups(scatter, list_base(N_EXPERTS - 1))

    @pl.when(s >= n_work)
    def _():
        chunk = s - n_work
        row0 = chunk * chunk_rows
        tot = jnp.concatenate([acc_ref[pl.ds(row0 + c, EPI_TOK, stride=ACC_ROWS), :] for c in range(ACC_ROWS)], axis=1)
        out_ref[0] = _rms(tot, nfin_ref[...])

        @pl.when(b + 1 < pl.num_programs(0))
        def _():
            seed_copy(b + 1, chunk).start()


def _moe(idx_flat, gate_flat, hr, wg, wu, wd, x1, mods, nfin, cap):
    nb, t = x1.shape[0], x1.shape[1] // ACC_ROWS
    n_ff = wg.shape[1]
    n_work = N_EXPERTS * n_ff
    n_epi = t // EPI_TOK

    def w_idx(s):
        sc = jnp.minimum(s, n_work - 1)
        return sc // n_ff, sc % n_ff

    def epi(b, s, *_):
        return (b, jnp.maximum(s - n_work, 0), 0)

    grid_spec = pltpu.PrefetchScalarGridSpec(
        num_scalar_prefetch=2,
        grid=(nb, n_work + n_epi),
        in_specs=[pl.BlockSpec(memory_space=pl.ANY),
                  pl.BlockSpec((1, 1, D_MODEL, DFF_TILE), lambda b, s, *_: w_idx(s) + (0, 0)),
                  pl.BlockSpec((1, 1, D_MODEL, DFF_TILE), lambda b, s, *_: w_idx(s) + (0, 0)),
                  pl.BlockSpec((1, DFF_TILE, D_MODEL), lambda b, s, *_: (w_idx(s)[0], w_idx(s)[1], 0)),
                  pl.BlockSpec(memory_space=pl.ANY),
                  pl.BlockSpec(mods.shape, lambda b, s, *_: (0, 0)),
                  pl.BlockSpec((1, D_MODEL), lambda b, s, *_: (0, 0))],
        out_specs=pl.BlockSpec((1, EPI_TOK, D_MODEL), epi),
        scratch_shapes=[pltpu.VMEM((t * ACC_ROWS, LANES), F32),
                        pltpu.VMEM((t * ACC_ROWS, LANES), F32),
                        pltpu.VMEM((cap * ACC_ROWS, LANES), F32),
                        pltpu.VMEM((cap * ACC_ROWS, LANES), F32),
                        pltpu.VMEM((2, cap, D_MODEL), BF16),
                        pltpu.VMEM((cap, D_MODEL), F32),
                        pltpu.SemaphoreType.DMA((n_epi,)),
                        pltpu.SemaphoreType.DMA((1,))])
    return pl.pallas_call(
        functools.partial(_moe_body, cap=cap, n_ff=n_ff, n_epi=n_epi),
        grid_spec=grid_spec,
        out_shape=jax.ShapeDtypeStruct((nb, t, D_MODEL), F32),
        compiler_params=pltpu.CompilerParams(dimension_semantics=("arbitrary", "arbitrary"),
                                             vmem_limit_bytes=VMEM_LIMIT),
        name="moe",
    )(idx_flat, gate_flat, hr, wg, wu, wd, x1, mods, nfin)


def kernel(x, c, ctx, c_ctx, w_ada, b_ada, norm_mix, norm_ffn, w_in, lb_logits, hg_norm, conv_w,
           w_out, w_router, w_gate, w_up, w_down, norm_final):
    nb, t, d = x.shape
    assert d == D_MODEL and w_ada.shape[0] == 1 and nb < MOD_ROWS and t % STEP_TOK == 0 and TT % GRID_W == 0
    assert w_in.shape[2] == 3 * HG_KEYS + 2 * HG_WIDTH + 3 * SC_WIDTH and w_router.shape[2] == N_EXPERTS
    cap = EC_CAPACITY * t // N_EXPERTS
    assert cap % LANES == 0 and w_gate.shape[3] % DFF_TILE == 0 and w_gate.shape[3] // DFF_TILE >= 2
    assert (nb * (t // STEP_TOK)) % N_EXPERTS == 0 and (D_MODEL * N_EXPERTS) % (nb * (t // STEP_TOK)) == 0

    cc = jnp.concatenate([c, c_ctx[None, :], jnp.zeros((MOD_ROWS - nb - 1, d), F32)], axis=0)
    w_in_bf = w_in[0].astype(BF16)
    w_out_bf = w_out[0].astype(BF16)
    wr_hi, wr_lo = _split(jnp.pad(w_router[0], ((0, 0), (0, LANES - N_EXPERTS))))
    wr_pad = jnp.concatenate([wr_hi, wr_lo], axis=1)
    nm, nf, nfin = norm_mix[0][None, :], norm_ffn[0][None, :], norm_final[None, :]
    hgn = hg_norm[0][None, :]

    mods = _adaln(cc, w_ada[0], b_ada[0][None, :])
    s0_f, s0_b = _ctx_states(ctx, mods, nm, w_in_bf, lb_logits, ctx_row=nb)
    steps = nb * (t // STEP_TOK)
    sliced = lambda w: w.reshape((steps, w.shape[0] * w.shape[1] // steps) + w.shape[2:])
    pk, fz, wg, wu, wd = _mix_fwd(x, mods, nm, w_in_bf, lb_logits, conv_w[0], s0_f, w_gate[0], w_up[0],
                                  sliced(w_down[0]), DFF_TILE)
    x1, hr, aff_t = _mix_bwd(x, pk, fz, mods, lb_logits, hgn, w_out_bf, nf, wr_pad, s0_b)
    wd = wd.reshape(w_down[0].shape)
    idx, gates = _route(aff_t, cap)
    return _moe(idx.reshape(-1), gates.reshape(-1), hr, wg, wu, wd, x1, mods, nfin, cap)
```

```python
import functools

import jax
import jax.numpy as jnp
from jax import lax
from jax.experimental import pallas as pl
from jax.experimental.pallas import tpu as pltpu

F32 = jnp.float32
BF16 = jnp.bfloat16
I32 = jnp.int32

D_MODEL = 1024
N_MOD = 6
HG_HEADS = 4
HG_KDIM = 128
HG_VDIM = 128
HG_KEYS = HG_HEADS * HG_KDIM
HG_WIDTH = HG_HEADS * HG_VDIM
SC_WIDTH = D_MODEL - HG_WIDTH
GRID_W = 64
CHUNK = 32
N_EXPERTS = 16
EC_CAPACITY = 2
EPS = 1e-6

LANES = 128
SUBLANES = 8
VMEM_LIMIT = 62 * 1024 * 1024

TT = 256
STEP_TILES = 2
STEP_TOK = TT * STEP_TILES
NCH = TT // CHUNK
MOD_ROWS = 16
DFF_TILE = 1024
EPI_TOK = 512
ACC_ROWS = D_MODEL // LANES


def _dot(a, b):
    return jnp.dot(a, b, preferred_element_type=F32)


def _dot_nt(a, b):
    return lax.dot_general(a, b, (((1,), (1,)), ((), ())), preferred_element_type=F32)


def _dot_tn(a, b):
    return lax.dot_general(a, b, (((0,), (0,)), ((), ())), preferred_element_type=F32)


def _split(x):
    hi = x.astype(BF16)
    lo = (x - hi.astype(F32)).astype(BF16)
    return hi, lo


def _rms(x, gain):
    return x * lax.rsqrt(jnp.mean(x * x, axis=-1, keepdims=True) + EPS) * gain


def _silu(x):
    return x * jax.nn.sigmoid(x)


def _lower_bounds(lbl_ref):
    rows = [lbl_ref[l] for l in range(lbl_ref.shape[0])]
    m = functools.reduce(jnp.maximum, rows)
    es = [jnp.exp(r - m) for r in rows]
    sm0 = es[0] / functools.reduce(lambda a, b: a + b, es)
    return sm0[0:1, :], sm0[1:2, :]


def _ada_body(c_ref, w_ref, b_ref, o_ref):
    s_hi, s_lo = _split(_silu(c_ref[...]))
    w_hi, w_lo = _split(w_ref[...])
    o_ref[...] = _dot(s_hi, w_hi) + _dot(s_hi, w_lo) + _dot(s_lo, w_hi) + b_ref[...]


def _adaln(cc, w, b):
    n = w.shape[1]
    tn = n // 4
    return pl.pallas_call(
        _ada_body,
        grid=(n // tn,),
        in_specs=[pl.BlockSpec((MOD_ROWS, D_MODEL), lambda i: (0, 0)),
                  pl.BlockSpec((D_MODEL, tn), lambda i: (0, i)),
                  pl.BlockSpec((1, tn), lambda i: (0, i))],
        out_specs=pl.BlockSpec((MOD_ROWS, tn), lambda i: (0, i)),
        out_shape=jax.ShapeDtypeStruct((MOD_ROWS, n), F32),
        compiler_params=pltpu.CompilerParams(vmem_limit_bytes=VMEM_LIMIT),
        name="adaln",
    )(cc, w, b)


def _ctx_body(ctx_ref, mods_ref, nm_ref, wf_ref, wb_ref, wi_ref, lbl_ref, sf_ref, sb_ref, *, ctx_row):
    x = ctx_ref[0]
    n = x.shape[0]
    csh = mods_ref[ctx_row:ctx_row + 1, 0:D_MODEL]
    csc = mods_ref[ctx_row:ctx_row + 1, D_MODEL:2 * D_MODEL]
    hb = (_rms(x, nm_ref[...]) * (1.0 + csc) + csh).astype(BF16)
    zf = _dot(hb, wf_ref[...])
    zb = _dot(hb, wb_ref[...])
    vi = _dot(hb, wi_ref[...]).astype(BF16)
    lb_f, lb_b = _lower_bounds(lbl_ref)
    f_f = lb_f + (1.0 - lb_f) * jax.nn.sigmoid(zf)
    f_b = lb_b + (1.0 - lb_b) * jax.nn.sigmoid(zb)
    ii = lax.broadcasted_iota(I32, (n, n), 0)
    jj = lax.broadcasted_iota(I32, (n, n), 1)
    after = jnp.where(jj > ii, 1.0, 0.0).astype(BF16)
    before = jnp.where(jj < ii, 1.0, 0.0).astype(BF16)
    hi, lo = _split(jnp.log(f_f))
    kd_f = ((1.0 - f_f) * jnp.exp(_dot(after, hi) + _dot(after, lo))).astype(BF16)
    hi, lo = _split(jnp.log(f_b))
    kd_b = ((1.0 - f_b) * jnp.exp(_dot(before, hi) + _dot(before, lo))).astype(BF16)
    for h in range(HG_HEADS):
        ks = slice(h * HG_KDIM, (h + 1) * HG_KDIM)
        vs = slice(h * HG_VDIM, (h + 1) * HG_VDIM)
        sf_ref[0, h] = _dot_tn(vi[:, vs], kd_f[:, ks])
        sb_ref[0, h] = _dot_tn(vi[:, vs], kd_b[:, ks])


def _ctx_states(ctx, mods, nm, w_in_bf, lbl, ctx_row):
    nb, n, _ = ctx.shape
    st = jax.ShapeDtypeStruct((nb, HG_HEADS, HG_VDIM, HG_KDIM), F32)
    st_spec = pl.BlockSpec((1, HG_HEADS, HG_VDIM, HG_KDIM), lambda b: (b, 0, 0, 0))
    wcol = lambda k: pl.BlockSpec((D_MODEL, HG_KEYS), lambda b, k=k: (0, k))
    return pl.pallas_call(
        functools.partial(_ctx_body, ctx_row=ctx_row),
        grid=(nb,),
        in_specs=[pl.BlockSpec((1, n, D_MODEL), lambda b: (b, 0, 0)),
                  pl.BlockSpec(mods.shape, lambda b: (0, 0)),
                  pl.BlockSpec((1, D_MODEL), lambda b: (0, 0)),
                  wcol(1), wcol(2), wcol(3),
                  pl.BlockSpec(lbl.shape, lambda b: (0, 0, 0))],
        out_specs=[st_spec, st_spec],
        out_shape=[st, st],
        compiler_params=pltpu.CompilerParams(vmem_limit_bytes=VMEM_LIMIT),
        name="ctx_states",
    )(ctx, mods, nm, w_in_bf, w_in_bf, w_in_bf, lbl)


def _tile_positions(reverse):
    ii = lax.broadcasted_iota(I32, (TT, TT), 0)
    jj = lax.broadcasted_iota(I32, (TT, TT), 1)
    if reverse:
        ii = TT - 1 - ii
        jj = TT - 1 - jj
    sh0 = CHUNK.bit_length() - 1
    return ii, jj, ii >> sh0, jj >> sh0


def _init_maps(lm_ref, tri_ref, reverse):
    lm_ref[...] = _level_map(reverse)
    ii, jj, ci, cj = _tile_positions(reverse)
    tri_ref[...] = jnp.where((ci == cj) & (jj <= ii), 1.0, 0.0).astype(BF16)


def _level_map(reverse):
    ii, jj, ci, cj = _tile_positions(reverse)
    lm = jnp.where(((ci >> 1) == (cj >> 1)) & (jj <= ii), 1, 0)
    for lvl in range(2, NCH.bit_length()):
        m = ((ci >> lvl) == (cj >> lvl)) & (((ci >> (lvl - 1)) & 1) == 1) & (((cj >> (lvl - 1)) & 1) == 0)
        lm = jnp.where(m, lvl, lm)
    return lm


def _gla_gates(zf, lb, tri):
    f = lb + (1.0 - lb) * jax.nn.sigmoid(zf)
    hi, lo = _split(jnp.log(f))
    return 1.0 - f, _dot(tri, hi) + _dot(tri, lo)


def _gla_tile(q, k, cum, v, st_ref, lm, reverse):
    n_lvl = NCH.bit_length() - 2
    phys = list(range(NCH - 1, -1, -1)) if reverse else list(range(NCH))
    last = 0 if reverse else CHUNK - 1
    tot = [cum[a * CHUNK + last:a * CHUNK + last + 1, :] for a in range(NCH)]
    bnd = [jnp.zeros_like(tot[0])]
    for c in range(NCH):
        bnd.append(bnd[-1] + tot[phys[c]])
    first = [None] * NCH
    for c in range(NCH):
        first[phys[c]] = c % 2 == 0
    ref = jnp.concatenate([cum[a * CHUNK:(a + 1) * CHUNK] - tot[a] if first[a] else cum[a * CHUNK:(a + 1) * CHUNK]
                           for a in range(NCH)], axis=0)
    qp = q * jnp.exp(ref)
    kp = k * jnp.exp(-ref)
    qs = [[None] * NCH for _ in range(n_lvl + 1)]
    ks = [[None] * NCH for _ in range(n_lvl + 1)]
    for c in range(NCH):
        a = phys[c]
        rows = slice(a * CHUNK, (a + 1) * CHUNK)
        to_start = tot[a] if first[a] else 0.0
        to_end = 0.0 if first[a] else tot[a]
        zero = jnp.zeros_like(qp[rows])
        for lvl in range(1, n_lvl + 1):
            s = 1 << lvl
            mid = (c // (2 * s)) * 2 * s + s
            right = (c // s) % 2 == 1
            qs[lvl - 1][a] = qp[rows] * jnp.exp(bnd[c] - bnd[mid] + to_start) if right else zero
            ks[lvl - 1][a] = zero if right else kp[rows] * jnp.exp(bnd[mid] - bnd[c + 1] + to_end)
        qs[n_lvl][a] = qp[rows] * jnp.exp(bnd[c] + to_start)
        ks[n_lvl][a] = kp[rows] * jnp.exp(bnd[NCH] - bnd[c + 1] + to_end)
    qs = [jnp.concatenate(p, axis=0).astype(BF16) for p in qs]
    ks = [jnp.concatenate(p, axis=0).astype(BF16) for p in ks]
    qp = qp.astype(BF16)
    kp = kp.astype(BF16)
    vb = v.astype(BF16)
    dec = jnp.exp(bnd[NCH])
    outs = []
    for h in range(HG_HEADS):
        kk = slice(h * HG_KDIM, (h + 1) * HG_KDIM)
        vv = slice(h * HG_VDIM, (h + 1) * HG_VDIM)
        sc = jnp.where(lm == 1, _dot_nt(qp[:, kk], kp[:, kk]), 0.0)
        for lvl in range(n_lvl):
            sc = jnp.where(lm == lvl + 2, _dot_nt(qs[lvl][:, kk], ks[lvl][:, kk]), sc)
        st = st_ref[h]
        outs.append(_dot(sc.astype(BF16), vb[:, vv]) + _dot_nt(qs[n_lvl][:, kk], st.astype(BF16)))
        st_ref[h] = st * dec[:, kk] + _dot_tn(vb[:, vv], ks[n_lvl][:, kk])
    return jnp.concatenate(outs, axis=1)


def _mix_fwd_body(x_ref, mods_ref, nm_ref, w_ref, lbl_ref, cw_ref, s0_ref, wa_ref, wb_ref, wc_ref,
                  pk_ref, fz_ref, wa_bf_ref, wb_bf_ref, wc_bf_ref, st_ref, lm_ref, tri_ref):
    b = pl.program_id(0)
    j = pl.program_id(1)

    @pl.when((b == 0) & (j == 0))
    def _():
        _init_maps(lm_ref, tri_ref, False)

    @pl.when(j == 0)
    def _():
        st_ref[...] = s0_ref[0]

    sh = mods_ref[pl.ds(b, 1), 0:D_MODEL]
    sc = mods_ref[pl.ds(b, 1), D_MODEL:2 * D_MODEL]
    w = HG_KEYS
    lb_f, _ = _lower_bounds(lbl_ref)
    cw = cw_ref[...]
    col = lax.broadcasted_iota(I32, (TT, 1), 0) & (GRID_W - 1)

    def cast_column_tiles(src_ref, dst_ref):
        tile = dst_ref.shape[3]
        for f in range(dst_ref.shape[1]):
            dst_ref[0, f] = src_ref[0, :, f * tile:(f + 1) * tile].astype(BF16)

    for u in range(STEP_TILES):
        rows = slice(u * TT, (u + 1) * TT)
        hx = (_rms(x_ref[0, rows], nm_ref[...]) * (1.0 + sc) + sh).astype(BF16)
        proj = lambda k: _dot(hx, w_ref[:, k * w:(k + 1) * w])
        gate_b = proj(5)
        if u == STEP_TILES - 1:
            wc_bf_ref[...] = wc_ref[...].astype(BF16)
        if u == 0:
            cast_column_tiles(wa_ref, wa_bf_ref)
        u_conv = proj(6) * proj(7)
        if u == 0:
            cast_column_tiles(wb_ref, wb_bf_ref)
        zff = proj(1)
        up = jnp.where(col != 0, pltpu.roll(u_conv, 1, axis=0), 0.0)
        un = jnp.where(col != GRID_W - 1, pltpu.roll(u_conv, TT - 1, axis=0), 0.0)
        pk_ref[0, rows, 3 * w:4 * w] = (gate_b * (cw[0:1] * up + cw[1:2] * u_conv + cw[2:3] * un)).astype(BF16)
        q = proj(0)
        k, cum = _gla_gates(zff, lb_f, tri_ref[...])
        vi = proj(3)
        fz_ref[0, rows, 0:w] = proj(2)
        pk_ref[0, rows, 2 * w:3 * w] = proj(4).astype(BF16)
        fz_ref[0, rows, w:2 * w] = _gla_tile(q, k, cum, vi, st_ref, lm_ref[...], False)
        pk_ref[0, rows, 0:w] = q.astype(BF16)
        pk_ref[0, rows, w:2 * w] = vi.astype(BF16)


def _mix_fwd(x, mods, nm, w_in_bf, lbl, cw, s0_f, wa, wb, wc, col_tile):
    nb, t, _ = x.shape
    nt = t // STEP_TOK
    ne, rows, cols = wa.shape
    per_expert = nb * nt // ne
    step_rows = rows // per_expert
    tok = lambda n, dt: jax.ShapeDtypeStruct((nb, t, n * HG_WIDTH), dt)
    tspec = lambda n: pl.BlockSpec((1, STEP_TOK, n * HG_WIDTH), lambda b, j: (b, j, 0))
    cast_in = pl.BlockSpec((1, step_rows, cols), lambda b, j: ((b * nt + j) // per_expert, (b * nt + j) % per_expert, 0))
    cast_out = pl.BlockSpec((1, cols // col_tile, step_rows, col_tile),
                            lambda b, j: ((b * nt + j) // per_expert, 0, (b * nt + j) % per_expert, 0))
    cast_shape = jax.ShapeDtypeStruct((ne, cols // col_tile, rows, col_tile), BF16)
    plain = pl.BlockSpec((1,) + wc.shape[1:], lambda b, j: (b * nt + j, 0, 0))
    return pl.pallas_call(
        _mix_fwd_body,
        grid=(nb, nt),
        in_specs=[pl.BlockSpec((1, STEP_TOK, D_MODEL), lambda b, j: (b, j, 0)),
                  pl.BlockSpec(mods.shape, lambda b, j: (0, 0)),
                  pl.BlockSpec((1, D_MODEL), lambda b, j: (0, 0)),
                  pl.BlockSpec(w_in_bf.shape, lambda b, j: (0, 0)),
                  pl.BlockSpec(lbl.shape, lambda b, j: (0, 0, 0)),
                  pl.BlockSpec(cw.shape, lambda b, j: (0, 0)),
                  pl.BlockSpec((1, HG_HEADS, HG_VDIM, HG_KDIM), lambda b, j: (b, 0, 0, 0)),
                  cast_in, cast_in, plain],
        out_specs=[tspec(4), tspec(2), cast_out, cast_out, plain],
        out_shape=[tok(4, BF16), tok(2, F32), cast_shape, cast_shape, jax.ShapeDtypeStruct(wc.shape, BF16)],
        scratch_shapes=[pltpu.VMEM((HG_HEADS, HG_VDIM, HG_KDIM), F32), pltpu.VMEM((TT, TT), I32),
                        pltpu.VMEM((TT, TT), BF16)],
        compiler_params=pltpu.CompilerParams(dimension_semantics=("arbitrary", "arbitrary"),
                                             vmem_limit_bytes=VMEM_LIMIT),
        name="mix_fwd",
    )(x, mods, nm, w_in_bf, lbl, cw, s0_f, wa, wb, wc)


def _mix_bwd_body(x_ref, pk_ref, fz_ref, mods_ref, lbl_ref, hgn_ref,
                  wo_ref, nf_ref, wr_ref, s0_ref,
                  x1_ref, hr_ref, at_ref, st_ref, lm_ref, tri_ref):
    b = pl.program_id(0)
    j = pl.program_id(1)

    @pl.when((b == 0) & (j == 0))
    def _():
        _init_maps(lm_ref, tri_ref, True)

    @pl.when(j == 0)
    def _():
        st_ref[...] = s0_ref[0]

    _, lb_b = _lower_bounds(lbl_ref)
    w = HG_WIDTH
    hgn = hgn_ref[...]
    mod = lambda k: mods_ref[pl.ds(b, 1), k * D_MODEL:(k + 1) * D_MODEL]
    scans = {}
    for u in reversed(range(STEP_TILES)):
        rows = slice(u * TT, (u + 1) * TT)
        k, cum = _gla_gates(fz_ref[0, rows, 0:w], lb_b, tri_ref[...])
        scans[u] = _gla_tile(pk_ref[0, rows, 0:w].astype(F32), k, cum, pk_ref[0, rows, w:2 * w], st_ref, lm_ref[...],
                             True)
    o = fz_ref[0, :, w:2 * w] + jnp.concatenate([scans[u] for u in range(STEP_TILES)], axis=0)
    heads = [_rms(o[:, h * HG_VDIM:(h + 1) * HG_VDIM], hgn) for h in range(HG_HEADS)]
    g = pk_ref[0, :, 2 * w:3 * w].astype(F32)
    ohg = jnp.concatenate(heads, axis=1) * _silu(g)
    cat = jnp.concatenate([ohg.astype(BF16), pk_ref[0, :, 3 * w:4 * w]], axis=1)
    mx = _dot(cat, wo_ref[...])
    x1 = x_ref[0] + mod(2) * mx
    h2 = _rms(x1, nf_ref[...]) * (1.0 + mod(4)) + mod(3)
    for c in range(ACC_ROWS):
        dst = pl.ds(c, STEP_TOK, stride=ACC_ROWS)
        x1_ref[0, dst, :] = x1[:, c * LANES:(c + 1) * LANES]
        hr_ref[0, dst, :] = h2[:, c * LANES:(c + 1) * LANES]
    r = _dot(jnp.concatenate(_split(h2), axis=0), wr_ref[...])
    lg = r[:STEP_TOK, :LANES] + r[:STEP_TOK, LANES:] + r[STEP_TOK:, :LANES]
    lgt = lg.T[0:N_EXPERTS, :]
    e = jnp.exp(lgt - jnp.max(lgt, axis=0, keepdims=True))
    at_ref[0] = e / jnp.sum(e, axis=0, keepdims=True)


def _mix_bwd(x, pk, fz, mods, lbl, hgn, w_out_bf, nf, wr_pad, s0_b):
    nb, t, _ = x.shape
    nt = t // STEP_TOK
    rev = lambda b, j: (b, nt - 1 - j, 0)
    tspec = lambda a: pl.BlockSpec((1, STEP_TOK, a.shape[2]), rev)
    xspec = pl.BlockSpec((1, STEP_TOK, D_MODEL), rev)
    full2 = lambda a: pl.BlockSpec(a.shape, lambda b, j: (0, 0))
    return pl.pallas_call(
        _mix_bwd_body,
        grid=(nb, nt),
        in_specs=[xspec, tspec(pk), tspec(fz),
                  full2(mods), pl.BlockSpec(lbl.shape, lambda b, j: (0, 0, 0)), full2(hgn),
                  full2(w_out_bf), full2(nf), full2(wr_pad),
                  pl.BlockSpec((1, HG_HEADS, HG_VDIM, HG_KDIM), lambda b, j: (b, 0, 0, 0))],
        out_specs=[pl.BlockSpec((1, STEP_TOK * ACC_ROWS, LANES), rev),
                   pl.BlockSpec((1, STEP_TOK * ACC_ROWS, LANES), rev),
                   pl.BlockSpec((1, N_EXPERTS, STEP_TOK), lambda b, j: (b, 0, nt - 1 - j))],
        out_shape=[jax.ShapeDtypeStruct((nb, t * ACC_ROWS, LANES), F32),
                   jax.ShapeDtypeStruct((nb, t * ACC_ROWS, LANES), F32),
                   jax.ShapeDtypeStruct((nb, N_EXPERTS, t), F32)],
        scratch_shapes=[pltpu.VMEM((HG_HEADS, HG_VDIM, HG_KDIM), F32), pltpu.VMEM((TT, TT), I32),
                        pltpu.VMEM((TT, TT), BF16)],
        compiler_params=pltpu.CompilerParams(dimension_semantics=("arbitrary", "arbitrary"),
                                             vmem_limit_bytes=VMEM_LIMIT),
        name="mix_bwd",
    )(x, pk, fz, mods, lbl, hgn, w_out_bf, nf, wr_pad, s0_b)


def _route_body(at_ref, idx_ref, gate_ref, *, cap):
    ne, t = at_ref.shape[1], at_ref.shape[2]
    blk = 2 * LANES
    aff = at_ref[0]

    def count(m):
        return jnp.sum(jnp.where(m, 1.0, 0.0), axis=1, keepdims=True)

    def enough(cand):
        return count(aff >= lax.bitcast_convert_type(cand, F32)) >= cap

    def bit_step(i, thr):
        hi = jnp.left_shift(jnp.int32(1), 30 - 2 * i)
        lo = jnp.right_shift(hi, 1)
        both, upper, lower = thr | hi | lo, thr | hi, thr | lo
        return jnp.where(enough(both), both, jnp.where(enough(upper), upper, jnp.where(enough(lower), lower, thr)))

    thr = lax.fori_loop(0, 16, bit_step, jnp.zeros((ne, 1), I32))
    gt = aff >= lax.bitcast_convert_type(thr + 1, F32)
    eq = (aff >= lax.bitcast_convert_type(thr, F32)) & jnp.logical_not(gt)
    need = cap - count(gt)

    ii = lax.broadcasted_iota(I32, (blk, blk), 0)
    jj = lax.broadcasted_iota(I32, (blk, blk), 1)
    upper = jnp.where(ii <= jj, 1.0, 0.0).astype(BF16)

    def prefix_blocks(m):
        mb = jnp.where(m, 1.0, 0.0).astype(BF16)
        carry = jnp.zeros((ne, 1), F32)
        out = []
        for kb in range(t // blk):
            p = _dot(mb[:, kb * blk:(kb + 1) * blk], upper) + carry
            out.append(p)
            carry = p[:, blk - 1:blk]
        return out

    eq_rank = jnp.concatenate(prefix_blocks(eq), axis=1)
    sel = gt | (eq & (eq_rank <= need))
    rank = jnp.concatenate(prefix_blocks(sel), axis=1).astype(I32)

    tok = lax.broadcasted_iota(I32, (ne, t), 1)
    x = jnp.where(sel, tok, -1)
    d = jnp.where(sel, tok - rank + 1, 0)
    g = aff
    for k in range(t.bit_length() - 1):
        left = lambda a: pltpu.roll(a, t - (1 << k), axis=1)
        ds = left(d)
        take = ((ds >> k) & 1) == 1
        keep = ((d >> k) & 1) == 0
        x = jnp.where(take, left(x), jnp.where(keep, x, -1))
        g = jnp.where(take, left(g), g)
        d = jnp.where(take, ds, jnp.where(keep, d, 0))
    idx_ref[0] = x[:, :cap]
    gate_ref[0] = g[:, :cap]


def _route(aff_t, cap):
    nb, ne, t = aff_t.shape
    spec = pl.BlockSpec((1, ne, cap), lambda b: (b, 0, 0))
    return pl.pallas_call(
        functools.partial(_route_body, cap=cap),
        grid=(nb,),
        in_specs=[pl.BlockSpec((1, ne, t), lambda b: (b, 0, 0))],
        out_specs=[spec, spec],
        out_shape=[jax.ShapeDtypeStruct((nb, ne, cap), I32), jax.ShapeDtypeStruct((nb, ne, cap), F32)],
        compiler_params=pltpu.CompilerParams(vmem_limit_bytes=VMEM_LIMIT),
        name="route",
    )(aff_t)


def _moe_body(idx_ref, gate_ref, hr_hbm, wg_ref, wu_ref, wd_ref, x1_hbm, mods_ref, nfin_ref,
              out_ref, acc_ref, hr_ref, rowg_ref, rows_ref, xs_ref, ys_ref, seed_sem, hr_sem, *, cap, n_ff, n_epi):
    b = pl.program_id(0)
    s = pl.program_id(1)
    n_work = N_EXPERTS * n_ff
    e = s // n_ff
    ff = s % n_ff
    slot = e % 2
    group = SUBLANES

    def list_base(expert):
        return (b * N_EXPERTS + expert) * cap

    def gather(base, j0, n):
        for u in range(n):
            tkn = idx_ref[base + j0 + u]
            rowg_ref[pl.ds(pl.multiple_of((j0 + u) * ACC_ROWS, ACC_ROWS), ACC_ROWS), :] = (
                hr_ref[pl.ds(pl.multiple_of(tkn * ACC_ROWS, ACC_ROWS), ACC_ROWS), :])

    def scatter(base, j0, n):
        for g0 in range(0, n, group):
            dst = []
            val = []
            for u in range(g0, g0 + group):
                r = pl.multiple_of(idx_ref[base + j0 + u] * ACC_ROWS, ACC_ROWS)
                dst.append(r)
                val.append(acc_ref[pl.ds(r, ACC_ROWS), :] + gate_ref[base + j0 + u]
                           * rows_ref[pl.ds(pl.multiple_of((j0 + u) * ACC_ROWS, ACC_ROWS), ACC_ROWS), :])
            for r, v in zip(dst, val):
                acc_ref[pl.ds(r, ACC_ROWS), :] = v

    def in_groups(fn, base):
        def step(jo, carry):
            fn(base, jo * group, group)
            return carry
        lax.fori_loop(0, cap // group, step, 0)

    def lay_out_input(to_slot):
        for c in range(ACC_ROWS):
            xs_ref[to_slot, :, c * LANES:(c + 1) * LANES] = (
                rowg_ref[pl.ds(c, cap, stride=ACC_ROWS), :].astype(BF16))

    def lay_out_output():
        g2 = mods_ref[pl.ds(b, 1), 5 * D_MODEL:6 * D_MODEL]
        for c in range(ACC_ROWS):
            cols = slice(c * LANES, (c + 1) * LANES)
            rows_ref[pl.ds(c, cap, stride=ACC_ROWS), :] = ys_ref[:, cols] * g2[:, cols]

    chunk_rows = EPI_TOK * ACC_ROWS

    def seed_copy(sample, chunk):
        rows = pl.ds(chunk * chunk_rows, chunk_rows)
        return pltpu.make_async_copy(x1_hbm.at[sample, rows], acc_ref.at[rows], seed_sem.at[chunk])

    def expert_tile():
        half = cap // 2
        for m in range(2):
            rows = slice(m * half, (m + 1) * half)
            xs = xs_ref[slot, rows, :]
            hid = _silu(_dot(xs, wg_ref[0, 0])) * _dot(xs, wu_ref[0, 0])
            ys_ref[rows, :] = _dot(hid.astype(BF16), wd_ref[0]) + jnp.where(ff == 0, 0.0, ys_ref[rows, :])

    def input_copy(sample):
        return pltpu.make_async_copy(hr_hbm.at[sample], hr_ref, hr_sem.at[0])

    @pl.when((s == 0) & (b == 0))
    def _():
        input_copy(0).start()
        for chunk in range(n_epi):
            seed_copy(0, chunk).start()

    @pl.when(s == 0)
    def _():
        input_copy(b).wait()
        ys_ref[...] = jnp.zeros_like(ys_ref)
        in_groups(gather, list_base(0))
        lay_out_input(0)

    @pl.when(s == n_ff - 1)
    def _():
        for chunk in range(n_epi):
            seed_copy(b, chunk).wait()

    @pl.when((s < n_work) & (ff == 0))
    def _():
        gather(list_base(jnp.minimum(e + 1, N_EXPERTS - 1)), 0, cap)
        lay_out_output()
        expert_tile()

    if n_ff > 2:
        @pl.when((s < n_work) & (ff > 0) & (ff < n_ff - 1))
        def _():
            expert_tile()

    @pl.when((s < n_work) & (ff == n_ff - 1))
    def _():
        lay_out_input(1 - slot)
        scatter(list_base(jnp.maximum(e - 1, 0)), 0, cap)
        expert_tile()

    @pl.when(s == n_work)
    def _():
        @pl.when(b + 1 < pl.num_programs(0))
        def _():
            input_copy(b + 1).start()

        lay_out_output()
        in_groups(scatter, list_base(N_EXPERTS - 1))

    @pl.when(s >= n_work)
    def _():
        chunk = s - n_work
        row0 = chunk * chunk_rows
        tot = jnp.concatenate([acc_ref[pl.ds(row0 + c, EPI_TOK, stride=ACC_ROWS), :] for c in range(ACC_ROWS)], axis=1)
        out_ref[0] = _rms(tot, nfin_ref[...])

        @pl.when(b + 1 < pl.num_programs(0))
        def _():
            seed_copy(b + 1, chunk).start()


def _moe(idx_flat, gate_flat, hr, wg, wu, wd, x1, mods, nfin, cap):
    nb, t = x1.shape[0], x1.shape[1] // ACC_ROWS
    n_ff = wg.shape[1]
    n_work = N_EXPERTS * n_ff
    n_epi = t // EPI_TOK

    def w_idx(s):
        sc = jnp.minimum(s, n_work - 1)
        return sc // n_ff, sc % n_ff

    def epi(b, s, *_):
        return (b, jnp.maximum(s - n_work, 0), 0)

    grid_spec = pltpu.PrefetchScalarGridSpec(
        num_scalar_prefetch=2,
        grid=(nb, n_work + n_epi),
        in_specs=[pl.BlockSpec(memory_space=pl.ANY),
                  pl.BlockSpec((1, 1, D_MODEL, DFF_TILE), lambda b, s, *_: w_idx(s) + (0, 0)),
                  pl.BlockSpec((1, 1, D_MODEL, DFF_TILE), lambda b, s, *_: w_idx(s) + (0, 0)),
                  pl.BlockSpec((1, DFF_TILE, D_MODEL), lambda b, s, *_: (w_idx(s)[0], w_idx(s)[1], 0)),
                  pl.BlockSpec(memory_space=pl.ANY),
                  pl.BlockSpec(mods.shape, lambda b, s, *_: (0, 0)),
                  pl.BlockSpec((1, D_MODEL), lambda b, s, *_: (0, 0))],
        out_specs=pl.BlockSpec((1, EPI_TOK, D_MODEL), epi),
        scratch_shapes=[pltpu.VMEM((t * ACC_ROWS, LANES), F32),
                        pltpu.VMEM((t * ACC_ROWS, LANES), F32),
                        pltpu.VMEM((cap * ACC_ROWS, LANES), F32),
                        pltpu.VMEM((cap * ACC_ROWS, LANES), F32),
                        pltpu.VMEM((2, cap, D_MODEL), BF16),
                        pltpu.VMEM((cap, D_MODEL), F32),
                        pltpu.SemaphoreType.DMA((n_epi,)),
                        pltpu.SemaphoreType.DMA((1,))])
    return pl.pallas_call(
        functools.partial(_moe_body, cap=cap, n_ff=n_ff, n_epi=n_epi),
        grid_spec=grid_spec,
        out_shape=jax.ShapeDtypeStruct((nb, t, D_MODEL), F32),
        compiler_params=pltpu.CompilerParams(dimension_semantics=("arbitrary", "arbitrary"),
                                             vmem_limit_bytes=VMEM_LIMIT),
        name="moe",
    )(idx_flat, gate_flat, hr, wg, wu, wd, x1, mods, nfin)


def kernel(x, c, ctx, c_ctx, w_ada, b_ada, norm_mix, norm_ffn, w_in, lb_logits, hg_norm, conv_w,
           w_out, w_router, w_gate, w_up, w_down, norm_final):
    nb, t, d = x.shape
    assert d == D_MODEL and w_ada.shape[0] == 1 and nb < MOD_ROWS and t % STEP_TOK == 0 and TT % GRID_W == 0
    assert w_in.shape[2] == 3 * HG_KEYS + 2 * HG_WIDTH + 3 * SC_WIDTH and w_router.shape[2] == N_EXPERTS
    cap = EC_CAPACITY * t // N_EXPERTS
    assert cap % LANES == 0 and w_gate.shape[3] % DFF_TILE == 0 and w_gate.shape[3] // DFF_TILE >= 2
    assert (nb * (t // STEP_TOK)) % N_EXPERTS == 0 and (D_MODEL * N_EXPERTS) % (nb * (t // STEP_TOK)) == 0

    cc = jnp.concatenate([c, c_ctx[None, :], jnp.zeros((MOD_ROWS - nb - 1, d), F32)], axis=0)
    w_in_bf = w_in[0].astype(BF16)
    w_out_bf = w_out[0].astype(BF16)
    wr_hi, wr_lo = _split(jnp.pad(w_router[0], ((0, 0), (0, LANES - N_EXPERTS))))
    wr_pad = jnp.concatenate([wr_hi, wr_lo], axis=1)
    nm, nf, nfin = norm_mix[0][None, :], norm_ffn[0][None, :], norm_final[None, :]
    hgn = hg_norm[0][None, :]

    mods = _adaln(cc, w_ada[0], b_ada[0][None, :])
    s0_f, s0_b = _ctx_states(ctx, mods, nm, w_in_bf, lb_logits, ctx_row=nb)
    steps = nb * (t // STEP_TOK)
    sliced = lambda w: w.reshape((steps, w.shape[0] * w.shape[1] // steps) + w.shape[2:])
    pk, fz, wg, wu, wd = _mix_fwd(x, mods, nm, w_in_bf, lb_logits, conv_w[0], s0_f, w_gate[0], w_up[0],
                                  sliced(w_down[0]), DFF_TILE)
    x1, hr, aff_t = _mix_bwd(x, pk, fz, mods, lb_logits, hgn, w_out_bf, nf, wr_pad, s0_b)
    wd = wd.reshape(w_down[0].shape)
    idx, gates = _route(aff_t, cap)
    return _moe(idx.reshape(-1), gates.reshape(-1), hr, wg, wu, wd, x1, mods, nfin, cap)
```
